```python
import math
import jax, jax.numpy as jnp
from jax import lax
import numpy as np

D_MODEL = 1024
BATCH = 8
SEQ = 2048
DEPTH = 1
DEC_BATCH = 32
DEC_SEQ = 64
PAST_LEN = 2048

CHUNK = 64
N_HEADS = 16
HEAD_DIM = 64
N_KV_HEADS = 4
Q_PER_KV = N_HEADS // N_KV_HEADS
IDX_HEADS = 8
IDX_DIM = 64
TOPK_MAX = 256
QBLK = 128
ROPE_THETA = 10000.0
D_INNER = 2 * D_MODEL
SSM_HEAD_DIM = 64
SSM_HEADS = D_INNER // SSM_HEAD_DIM
SSM_GROUPS = 8
D_STATE = 128
CONV_W = 4
CONV_CH = D_INNER + 2 * SSM_GROUPS * D_STATE
SSD_CHUNK = CHUNK
D_FF = -(-8 * D_MODEL // (3 * 256)) * 256
EPS = 1e-6
IN_SIZES = (N_HEADS * HEAD_DIM, N_KV_HEADS * HEAD_DIM, N_KV_HEADS * HEAD_DIM, IDX_HEADS * IDX_DIM, IDX_DIM, IDX_HEADS, D_INNER, CONV_CH, SSM_HEADS, 2 * D_MODEL)
IN_DIM = sum(IN_SIZES)

kernel_name = 'dsa_ssd_hybrid_stream_step'


def rms_normalize(x):
    x32 = x.astype(jnp.float32)
    return (x32 * lax.rsqrt(jnp.mean(x32 * x32, axis=-1, keepdims=True) + EPS)).astype(x.dtype)


def rope(x, pos):
    half = x.shape[-1] // 2
    inv = ROPE_THETA ** (-jnp.arange(half, dtype=jnp.float32) / half)
    ang = pos.astype(jnp.float32)[:, None] * inv[None, :]
    cos = jnp.cos(ang)[:, None, :].astype(x.dtype)
    sin = jnp.sin(ang)[:, None, :].astype(x.dtype)
    x1, x2 = x[..., :half], x[..., half:]
    return jnp.concatenate([x1 * cos - x2 * sin, x2 * cos + x1 * sin], axis=-1)


def dsa_attention(q, qi, wi, qpos, k_all, v_all, ki_all):
    t = q.shape[1]
    n_keys = k_all.shape[1]
    topk = min(TOPK_MAX, n_keys // 4)
    qb = min(QBLK, t)
    nb = t // qb
    kpos = jnp.arange(n_keys)
    qpos_blocks = qpos.reshape(nb, qb)

    def one_seq(args):
        q_s, qi_s, wi_s, k_s, v_s, ki_s = args

        def one_block(bargs):
            qq, qqi, ww, pp = bargs
            logits = jnp.einsum('thd,sd->ths', qqi, ki_s).astype(jnp.float32) * (IDX_DIM ** -0.5)
            score = jnp.einsum('th,ths->ts', ww.astype(jnp.float32), jax.nn.relu(logits))
            limit = (pp // CHUNK + 1) * CHUNK
            score = jnp.where(kpos[None, :] < limit[:, None], score, -jnp.inf)
            _, idx = lax.top_k(score, topk)
            valid = idx < limit[:, None]
            ks = k_s[idx]
            vs = v_s[idx]
            qg = qq.reshape(qb, N_KV_HEADS, Q_PER_KV, HEAD_DIM)
            s = jnp.einsum('tkgd,tjkd->tkgj', qg, ks).astype(jnp.float32) * (HEAD_DIM ** -0.5)
            s = jnp.where(valid[:, None, None, :], s, -jnp.inf)
            p = jax.nn.softmax(s, axis=-1).astype(vs.dtype)
            return jnp.einsum('tkgj,tjkd->tkgd', p, vs).reshape(qb, N_HEADS * HEAD_DIM)

        out = lax.map(one_block, (q_s.reshape(nb, qb, N_HEADS, HEAD_DIM), qi_s.reshape(nb, qb, IDX_HEADS, IDX_DIM), wi_s.reshape(nb, qb, IDX_HEADS), qpos_blocks))
        return out.reshape(t, N_HEADS * HEAD_DIM)

    return lax.map(one_seq, (q, qi, wi, k_all, v_all, ki_all))


def causal_conv(full, w_conv, b_conv, t):
    out = b_conv
    for j in range(CONV_W):
        out = out + full[:, j:j + t] * w_conv[j]
    return out


def ssd_scan(x, dt, a, bm, cm, h0, chunk):
    b, l, nh, hp = x.shape
    g, n = bm.shape[2], bm.shape[3]
    r = nh // g
    nc = l // chunk
    xc = x.reshape(b, nc, chunk, g, r, hp)
    dtc = dt.reshape(b, nc, chunk, g, r)
    bc = bm.reshape(b, nc, chunk, g, n)
    cc = cm.reshape(b, nc, chunk, g, n)
    acs = jnp.cumsum(dtc * a.reshape(g, r), axis=2)
    diff = acs[:, :, :, None] - acs[:, :, None, :]
    causal = jnp.tril(jnp.ones((chunk, chunk), dtype=bool))[:, :, None, None]
    decay = jnp.exp(jnp.where(causal, diff, -jnp.inf))
    cb = jnp.einsum('bcign,bcjgn->bcijg', cc, bc)
    y_diag = jnp.einsum('bcijg,bcijgr,bcjgr,bcjgrp->bcigrp', cb, decay, dtc, xc)
    w_state = jnp.exp(acs[:, :, -1:] - acs) * dtc
    states = jnp.einsum('bcjgn,bcjgr,bcjgrp->bcgrpn', bc, w_state, xc)
    chunk_decay = jnp.exp(acs[:, :, -1])

    def step(h, inp):
        dec, st = inp
        return dec[..., None, None] * h + st, h

    h_last, h_prev = lax.scan(step, h0.reshape(b, g, r, hp, n), (jnp.transpose(chunk_decay, (1, 0, 2, 3)), jnp.transpose(states, (1, 0, 2, 3, 4, 5))))
    h_prev = jnp.transpose(h_prev, (1, 0, 2, 3, 4, 5))
    y_off = jnp.einsum('bcign,bcigr,bcgrpn->bcigrp', cc, jnp.exp(acs), h_prev)
    y = (y_diag + y_off).reshape(b, l, nh, hp)
    return y, h_last.reshape(b, nh, hp, n)


def layer(x, c, cache_k, cache_v, cache_ki, state_conv, state_ssm, w_ada, b_ada, g_norm_mix, g_norm_ffn, w_in, g_q, g_k, w_conv, b_conv, dt_bias, a_log, d_skip, g_ssm_norm, w_branch_attn, w_branch_ssm, w_out, w_gate_up, w_down):
    b, t, _ = x.shape
    pos = cache_k.shape[1] + jnp.arange(t)
    mod = (jax.nn.silu(c) @ w_ada + b_ada)[:, None, :]
    sh1, sc1, gt1, sh2, sc2, gt2 = jnp.split(mod, 6, axis=-1)
    h = rms_normalize(x) * g_norm_mix * (1.0 + sc1) + sh1
    offs = np.cumsum(IN_SIZES)[:-1].tolist()
    q, k, v, qi, ki, wi, z, xbc, dt, gate_logits = jnp.split(h @ w_in, offs, axis=-1)
    q = rope(rms_normalize(q.reshape(b, t, N_HEADS, HEAD_DIM)) * g_q, pos)
    k = rope(rms_normalize(k.reshape(b, t, N_KV_HEADS, HEAD_DIM)) * g_k, pos)
    v = v.reshape(b, t, N_KV_HEADS, HEAD_DIM)
    qi = rope(qi.reshape(b, t, IDX_HEADS, IDX_DIM), pos)
    ki = rope(ki[:, :, None, :], pos)[:, :, 0, :]
    wi = wi * (IDX_HEADS ** -0.5)
    o_attn = dsa_attention(q, qi, wi, pos, jnp.concatenate([cache_k, k], axis=1), jnp.concatenate([cache_v, v], axis=1), jnp.concatenate([cache_ki, ki], axis=1))
    full = jnp.concatenate([state_conv, xbc], axis=1)
    xbc_c = jax.nn.silu(causal_conv(full, w_conv, b_conv, t))
    xs, bm, cm = jnp.split(xbc_c, [D_INNER, D_INNER + SSM_GROUPS * D_STATE], axis=-1)
    xh = xs.reshape(b, t, SSM_HEADS, SSM_HEAD_DIM)
    dtp = jax.nn.softplus((dt + dt_bias).astype(jnp.float32))
    y, h_last = ssd_scan(xh.astype(jnp.float32), dtp, -jnp.exp(a_log.astype(jnp.float32)), bm.reshape(b, t, SSM_GROUPS, D_STATE).astype(jnp.float32), cm.reshape(b, t, SSM_GROUPS, D_STATE).astype(jnp.float32), state_ssm.astype(jnp.float32), min(SSD_CHUNK, t))
    y = (y.astype(x.dtype) + d_skip[:, None] * xh).reshape(b, t, D_INNER) * jax.nn.silu(z)
    y = rms_normalize(y.reshape(b, t, SSM_GROUPS, D_INNER // SSM_GROUPS)).reshape(b, t, D_INNER) * g_ssm_norm
    g_attn, g_ssm = jnp.split(jax.nn.sigmoid(gate_logits), 2, axis=-1)
    mixed = (g_attn * (o_attn @ w_branch_attn) + g_ssm * (y @ w_branch_ssm)) @ w_out
    x = x + gt1 * mixed
    h2 = rms_normalize(x) * g_norm_ffn * (1.0 + sc2) + sh2
    gate, up = jnp.split(h2 @ w_gate_up, 2, axis=-1)
    x = x + gt2 * ((jax.nn.silu(gate) * up) @ w_down)
    return x, k, v, ki, full[:, -(CONV_W - 1):], h_last.astype(x.dtype)


def setup_inputs(seed: int = 0) -> dict:
    key = jax.random.key(seed)
    ks = iter(jax.random.split(key, 40))
    f32 = jnp.float32

    def nrm(shape, scale):
        return jax.random.normal(next(ks), shape, f32) * scale

    def gain(shape):
        return 1.0 + nrm(shape, 0.02)

    dt0 = jnp.exp(jax.random.uniform(next(ks), (DEPTH, SSM_HEADS), f32, math.log(1e-3), math.log(1e-1)))
    dt_bias = dt0 + jnp.log(-jnp.expm1(-dt0))
    a_log = jnp.log(jax.random.uniform(next(ks), (DEPTH, SSM_HEADS), f32, 1.0, 16.0))
    return {
        'x_prompt': nrm((BATCH, SEQ, D_MODEL), 1.0),
        'x_sample': nrm((DEC_BATCH, DEC_SEQ, D_MODEL), 1.0),
        'cache_k': nrm((DEPTH, DEC_BATCH, PAST_LEN, N_KV_HEADS, HEAD_DIM), 1.0),
        'cache_v': nrm((DEPTH, DEC_BATCH, PAST_LEN, N_KV_HEADS, HEAD_DIM), 1.0),
        'cache_ki': nrm((DEPTH, DEC_BATCH, PAST_LEN, IDX_DIM), 1.0),
        'state_conv': nrm((DEPTH, DEC_BATCH, CONV_W - 1, CONV_CH), 1.0),
        'state_ssm': nrm((DEPTH, DEC_BATCH, SSM_HEADS, SSM_HEAD_DIM, D_STATE), 0.1),
        'c_prompt': nrm((BATCH, D_MODEL), 1.0),
        'c_sample': nrm((DEC_BATCH, D_MODEL), 1.0),
        'w_ada': nrm((DEPTH, D_MODEL, 6 * D_MODEL), 0.5 * D_MODEL ** -0.5),
        'b_ada': nrm((DEPTH, 6 * D_MODEL), 0.02),
        'g_norm_mix': gain((DEPTH, D_MODEL)),
        'g_norm_ffn': gain((DEPTH, D_MODEL)),
        'w_in': nrm((DEPTH, D_MODEL, IN_DIM), D_MODEL ** -0.5),
        'g_q': gain((DEPTH, HEAD_DIM)),
        'g_k': gain((DEPTH, HEAD_DIM)),
        'w_conv': nrm((DEPTH, CONV_W, CONV_CH), CONV_W ** -0.5),
        'b_conv': nrm((DEPTH, CONV_CH), 0.02),
        'dt_bias': dt_bias,
        'a_log': a_log,
        'd_skip': gain((DEPTH, SSM_HEADS)),
        'g_ssm_norm': gain((DEPTH, D_INNER)),
        'w_branch_attn': nrm((DEPTH, N_HEADS * HEAD_DIM, D_MODEL), (N_HEADS * HEAD_DIM) ** -0.5),
        'w_branch_ssm': nrm((DEPTH, D_INNER, D_MODEL), D_INNER ** -0.5),
        'w_out': nrm((DEPTH, D_MODEL, D_MODEL), D_MODEL ** -0.5),
        'w_gate_up': nrm((DEPTH, D_MODEL, 2 * D_FF), D_MODEL ** -0.5),
        'w_down': nrm((DEPTH, D_FF, D_MODEL), D_FF ** -0.5),
    }


def reference(x_prompt, x_sample, cache_k, cache_v, cache_ki, state_conv, state_ssm, c_prompt, c_sample, w_ada, b_ada, g_norm_mix, g_norm_ffn, w_in, g_q, g_k, w_conv, b_conv, dt_bias, a_log, d_skip, g_ssm_norm, w_branch_attn, w_branch_ssm, w_out, w_gate_up, w_down):
    bp = x_prompt.shape[0]
    dtype = x_prompt.dtype
    y_p, y_s = x_prompt, x_sample
    new_p = [[], [], [], [], []]
    new_s = [[], [], [], [], []]
    for l in range(DEPTH):
        lw = (w_ada[l], b_ada[l], g_norm_mix[l], g_norm_ffn[l], w_in[l], g_q[l], g_k[l], w_conv[l], b_conv[l], dt_bias[l], a_log[l], d_skip[l], g_ssm_norm[l], w_branch_attn[l], w_branch_ssm[l], w_out[l], w_gate_up[l], w_down[l])
        y_p, *st_p = layer(y_p, c_prompt, jnp.zeros((bp, 0, N_KV_HEADS, HEAD_DIM), dtype), jnp.zeros((bp, 0, N_KV_HEADS, HEAD_DIM), dtype), jnp.zeros((bp, 0, IDX_DIM), dtype), jnp.zeros((bp, CONV_W - 1, CONV_CH), dtype), jnp.zeros((bp, SSM_HEADS, SSM_HEAD_DIM, D_STATE), dtype), *lw)
        y_s, *st_s = layer(y_s, c_sample, cache_k[l], cache_v[l], cache_ki[l], state_conv[l], state_ssm[l], *lw)
        for acc, a in zip(new_p, st_p):
            acc.append(a)
        for acc, a in zip(new_s, st_s):
            acc.append(a)
    return (y_p, y_s, jnp.stack(new_p[0]), jnp.stack(new_p[1]), jnp.stack(new_p[2]), jnp.stack(new_p[3]), jnp.stack(new_p[4]), jnp.stack(new_s[0]), jnp.stack(new_s[1]), jnp.stack(new_s[2]), jnp.stack(new_s[3]), jnp.stack(new_s[4]))
```

```python
import functools

import jax
import jax.numpy as jnp
from jax import lax
from jax.experimental import pallas as pl
from jax.experimental.pallas import tpu as pltpu

F32, BF16, I32 = jnp.float32, jnp.bfloat16, jnp.int32

D_MODEL = 1024
CHUNK = 64
N_HEADS = 16
HEAD_DIM = 64
N_KV_HEADS = 4
Q_PER_KV = N_HEADS // N_KV_HEADS
IDX_HEADS = 8
IDX_DIM = 64
TOPK_MAX = 256
ROPE_THETA = 10000.0
D_INNER = 2 * D_MODEL
SSM_HEAD_DIM = 64
SSM_HEADS = D_INNER // SSM_HEAD_DIM
SSM_GROUPS = 8
HEADS_PER_GROUP = SSM_HEADS // SSM_GROUPS
GROUP_W = HEADS_PER_GROUP * SSM_HEAD_DIM
D_STATE = 128
CONV_W = 4
CONV_CH = D_INNER + 2 * SSM_GROUPS * D_STATE
D_FF = -(-8 * D_MODEL // (3 * 256)) * 256
EPS = 1e-6
QK_W = (N_HEADS + N_KV_HEADS) * HEAD_DIM
KV_W = N_KV_HEADS * HEAD_DIM
QI_W = IDX_HEADS * IDX_DIM
SSM_HEADS_N = SSM_HEADS

LANES = 128
MISC_WI = IDX_DIM
MISC_DT = IDX_DIM + IDX_HEADS
WI_SCALE = (IDX_HEADS ** -0.5) * (IDX_DIM ** -0.5)
INT_MIN = -(2 ** 31)
NEG_BIG = -1e30
VMEM_LIMIT = 56 * 1024 * 1024


def _cparams(*sem):
    return pltpu.CompilerParams(dimension_semantics=sem, vmem_limit_bytes=VMEM_LIMIT)


def _silu(x):
    return x * jax.nn.sigmoid(x)


def _split_bf16(x, n):
    pieces = []
    r = x
    for _ in range(n):
        p = r.astype(BF16)
        pieces.append(p)
        r = r - p.astype(F32)
    return pieces


def _dot(a, b):
    return jnp.dot(a, b, preferred_element_type=F32)


def _dot_nt(a, b):
    return lax.dot_general(a, b, (((1,), (1,)), ((), ())), preferred_element_type=F32)


def _dot_tn(a, b):
    return lax.dot_general(a, b, (((0,), (0,)), ((), ())), preferred_element_type=F32)


def _exact_dot(x, m, n):
    out = None
    for p in _split_bf16(x, n):
        t = _dot(p, m)
        out = t if out is None else out + t
    return out


def _rotate_half(x):
    w = x.shape[-1]
    lane = lax.broadcasted_iota(I32, x.shape, x.ndim - 1)
    first = (lane % HEAD_DIM) < (HEAD_DIM // 2)
    return jnp.where(first, pltpu.roll(x, w - HEAD_DIM // 2, x.ndim - 1), pltpu.roll(x, HEAD_DIM // 2, x.ndim - 1))


def _rope(x, cos, sin):
    reps = x.shape[-1] // LANES
    if reps > 1:
        cos = jnp.tile(cos, (1, reps))
        sin = jnp.tile(sin, (1, reps))
    return x * cos + _rotate_half(x) * sin


def _mod_kernel(c_ref, w_ref, b_ref, o_ref):
    s = _silu(c_ref[...])
    o_ref[...] = _dot(s.astype(BF16), w_ref[...].astype(BF16)) + b_ref[...]


def _ada_mod(c_all, w_ada, b_ada):
    bt = c_all.shape[0]
    n = w_ada.shape[1]
    tn = D_MODEL
    return pl.pallas_call(
        _mod_kernel,
        grid=(n // tn,),
        in_specs=[
            pl.BlockSpec((bt, D_MODEL), lambda j: (0, 0)),
            pl.BlockSpec((D_MODEL, tn), lambda j: (0, j)),
            pl.BlockSpec((1, tn), lambda j: (0, j)),
        ],
        out_specs=pl.BlockSpec((bt, tn), lambda j: (0, j)),
        out_shape=jax.ShapeDtypeStruct((bt, n), F32),
        compiler_params=_cparams("parallel"),
        name="ada_mod",
    )(c_all, w_ada, b_ada.reshape(1, n))


def _modulated_norm(x, mod, gain, shift_idx, scale_idx):
    ms = jnp.mean(x * x, axis=-1, keepdims=True)
    xn = x * lax.rsqrt(ms + EPS)
    sh = mod[:, shift_idx:shift_idx + 1, :]
    sc = mod[:, scale_idx:scale_idx + 1, :]
    return xn * gain * (1.0 + sc) + sh


def _hnorm_kernel(x_ref, mod_ref, g_ref, o_ref):
    o_ref[...] = _modulated_norm(x_ref[...], mod_ref[...], g_ref[...], 0, 1).astype(o_ref.dtype)


def _row_blocking(b, t):
    tt = min(t, 512)
    bb = max(1, min(b, 512 // tt))
    return bb, tt


def _hnorm(x, mod3, mod_off, gain):
    b, t, d = x.shape
    bb, tt = _row_blocking(b, t)
    off = mod_off // bb
    return pl.pallas_call(
        _hnorm_kernel,
        grid=(b // bb, t // tt),
        in_specs=[
            pl.BlockSpec((bb, tt, d), lambda i, j: (i, j, 0)),
            pl.BlockSpec((bb, 6, d), lambda i, j: (i + off, 0, 0)),
            pl.BlockSpec((1, 1, d), lambda i, j: (0, 0, 0)),
        ],
        out_specs=pl.BlockSpec((bb, tt, d), lambda i, j: (i, j, 0)),
        out_shape=jax.ShapeDtypeStruct((b, t, d), BF16),
        compiler_params=_cparams("parallel", "parallel"),
        name="hnorm",
    )(x, mod3, gain.reshape(1, 1, d))


def _proj_act_kernel(h_ref, w_ref, o_ref, *, act):
    acc = _dot(h_ref[...], w_ref[...])
    if act == "silu":
        acc = _silu(acc)
    elif act == "sigmoid":
        acc = jax.nn.sigmoid(acc)
    o_ref[...] = acc.astype(o_ref.dtype)


def _proj_act(h2d, w, act, out_dtype, tm, tn):
    m, k = h2d.shape
    n = w.shape[1]
    return pl.pallas_call(
        functools.partial(_proj_act_kernel, act=act),
        grid=(n // tn, m // tm),
        in_specs=[
            pl.BlockSpec((tm, k), lambda j, i: (i, 0)),
            pl.BlockSpec((k, tn), lambda j, i: (0, j)),
        ],
        out_specs=pl.BlockSpec((tm, tn), lambda j, i: (i, j)),
        out_shape=jax.ShapeDtypeStruct((m, n), out_dtype),
        compiler_params=_cparams("parallel", "parallel"),
        name="proj_" + act,
    )(h2d, w)


def _qk_kernel(h_ref, w_ref, gsum_ref, gexp_ref, gain_ref, cos_ref, sin_ref, q_ref, k_ref):
    acc = _dot(h_ref[...], w_ref[...])
    ss = _exact_dot(acc * acc, gsum_ref[...], 2)
    rs = lax.rsqrt(ss * (1.0 / HEAD_DIM) + EPS)
    rs_full = _exact_dot(rs, gexp_ref[...], 2)
    xn = acc * rs_full * gain_ref[...]
    out = _rope(xn, cos_ref[...], sin_ref[...])
    nq = N_HEADS * HEAD_DIM
    q_ref[...] = (out[:, :nq] * (HEAD_DIM ** -0.5)).astype(q_ref.dtype)
    k_ref[...] = out[:, nq:]


def _vm_kernel(h_ref, w_ref, cos_ref, sin_ref, dtb_ref, v_ref, qi_ref, misc_ref):
    acc = _dot(h_ref[...], w_ref[...])
    cos, sin = cos_ref[...], sin_ref[...]
    v_ref[...] = acc[:, :KV_W]
    qi_ref[...] = _rope(acc[:, KV_W:KV_W + QI_W], cos, sin).astype(qi_ref.dtype)
    m = acc[:, KV_W + QI_W:]
    lane = lax.broadcasted_iota(I32, m.shape, 1)
    roped = _rope(m, cos, sin)
    dt = jax.nn.softplus(m + dtb_ref[...])
    misc_ref[...] = jnp.where(lane < MISC_WI, roped,
                              jnp.where(lane < MISC_DT, m * WI_SCALE,
                                        jnp.where(lane < MISC_DT + SSM_HEADS, dt, 0.0)))


def _table_spec(tab_rows, tm):
    nblk = tab_rows // tm
    return pl.BlockSpec((tm, LANES), lambda i: (i % nblk, 0))


def _proj_qk(h2d, w_qk, gsum, gexp, gain, cos_tab, sin_tab, tm):
    m, k = h2d.shape
    nq = N_HEADS * HEAD_DIM
    const = lambda i: (0, 0)
    return pl.pallas_call(
        _qk_kernel,
        grid=(m // tm,),
        in_specs=[
            pl.BlockSpec((tm, k), lambda i: (i, 0)),
            pl.BlockSpec((k, QK_W), const),
            pl.BlockSpec((QK_W, LANES), const),
            pl.BlockSpec((LANES, QK_W), const),
            pl.BlockSpec((1, QK_W), const),
            _table_spec(cos_tab.shape[0], tm),
            _table_spec(sin_tab.shape[0], tm),
        ],
        out_specs=[pl.BlockSpec((tm, nq), lambda i: (i, 0)), pl.BlockSpec((tm, KV_W), lambda i: (i, 0))],
        out_shape=[jax.ShapeDtypeStruct((m, nq), BF16), jax.ShapeDtypeStruct((m, KV_W), F32)],
        compiler_params=_cparams("parallel"),
        name="proj_qk",
    )(h2d, w_qk, gsum, gexp, gain, cos_tab, sin_tab)


def _proj_vm(h2d, w_vm, cos_tab, sin_tab, dtb, tm):
    m, k = h2d.shape
    wn = w_vm.shape[1]
    const = lambda i: (0, 0)
    return pl.pallas_call(
        _vm_kernel,
        grid=(m // tm,),
        in_specs=[
            pl.BlockSpec((tm, k), lambda i: (i, 0)),
            pl.BlockSpec((k, wn), const),
            _table_spec(cos_tab.shape[0], tm),
            _table_spec(sin_tab.shape[0], tm),
            pl.BlockSpec((1, LANES), const),
        ],
        out_specs=[pl.BlockSpec((tm, KV_W), lambda i: (i, 0)), pl.BlockSpec((tm, QI_W), lambda i: (i, 0)),
                   pl.BlockSpec((tm, LANES), lambda i: (i, 0))],
        out_shape=[jax.ShapeDtypeStruct((m, KV_W), F32), jax.ShapeDtypeStruct((m, QI_W), BF16),
                   jax.ShapeDtypeStruct((m, LANES), F32)],
        compiler_params=_cparams("parallel"),
        name="proj_vm",
    )(h2d, w_vm, cos_tab, sin_tab, dtb)


def _attn_kernel(*refs, tq, kb, topk, past, t_new, has_cache):
    if has_cache:
        (q_ref, qi_ref, wi_ref, k_ref, v_ref, kim_ref, ck_ref, cv_ref, cki_ref,
         o_ref, kbf, vbf, kibf, key_scr, bias_scr, qis, qs, m_scr, l_scr, acc_scr) = refs
    else:
        (q_ref, qi_ref, wi_ref, k_ref, v_ref, kim_ref,
         o_ref, kbf, vbf, kibf, key_scr, bias_scr, qis, qs, m_scr, l_scr, acc_scr) = refs
    j = pl.program_id(1)
    kbn = min(kb, t_new)
    n_cache_blocks = past // kb

    @pl.when(j == 0)
    def _():
        for g in range(N_KV_HEADS):
            kbf[g] = k_ref[0, :, g * HEAD_DIM:(g + 1) * HEAD_DIM].astype(BF16)
            vbf[g] = v_ref[0, :, g * HEAD_DIM:(g + 1) * HEAD_DIM].astype(BF16)
        kibf[...] = kim_ref[0, :, 0:IDX_DIM].astype(BF16)

    for h in range(IDX_HEADS):
        qis[h * tq:(h + 1) * tq, :] = qi_ref[0, :, h * IDX_DIM:(h + 1) * IDX_DIM]
    for h in range(N_HEADS):
        qs[h * tq:(h + 1) * tq, :] = q_ref[0, :, h * HEAD_DIM:(h + 1) * HEAD_DIM]
    wi = wi_ref[0, :, MISC_WI:MISC_WI + IDX_HEADS]

    qpos = past + j * tq + lax.broadcasted_iota(I32, (tq, 1), 0)
    limit = (qpos // CHUNK + 1) * CHUNK
    n_new_blocks = (j * tq + tq + kbn - 1) // kbn

    def over_cache(fn, init):
        if not has_cache:
            return init
        return lax.fori_loop(0, n_cache_blocks, lambda i, c: fn(pl.multiple_of(i * kb, kb), c), init)

    def over_new(fn, init):
        if t_new <= kb:
            return fn(0, init)
        return lax.fori_loop(0, n_new_blocks, lambda i, c: fn(pl.multiple_of(i * kbn, kbn), c), init)

    def lanes_at(off):
        return past + off if isinstance(off, int) else pl.multiple_of(past + off, LANES)

    def score_block(ki_blk, kpos0, width, lane_off):
        lg = _dot_nt(qis[...], ki_blk)
        sc = jnp.zeros((tq, width), F32)
        for h in range(IDX_HEADS):
            sc = sc + wi[:, h:h + 1] * jnp.maximum(lg[h * tq:(h + 1) * tq], 0.0)
        bits = pltpu.bitcast(sc, I32)
        key = bits ^ ((bits >> 31) & 0x7FFFFFFF)
        kpos = kpos0 + lax.broadcasted_iota(I32, (1, width), 1)
        key = jnp.where(kpos < limit, key, INT_MIN)
        key_scr[:, pl.ds(lane_off, width)] = key

    def p1c(off, c):
        score_block(cki_ref[0, pl.ds(off, kb), :].astype(BF16), off, kb, off)
        return c

    def p1n(off, c):
        score_block(kibf[pl.ds(off, kbn), :], past + off, kbn, lanes_at(off))
        return c

    over_cache(p1c, 0)
    over_new(p1n, 0)

    def count_ge(cand):
        def cnt(lane_off, width, acc):
            kk = key_scr[:, pl.ds(lane_off, width)]
            hit = jnp.where(kk >= cand, 1.0, 0.0)
            if width % LANES:
                lane0 = lax.broadcasted_iota(I32, (tq, LANES), 1) == 0
                return acc + jnp.where(lane0, jnp.sum(hit, axis=1, keepdims=True), 0.0)
            for s in range(width // LANES):
                acc = acc + hit[:, s * LANES:(s + 1) * LANES]
            return acc
        acc = jnp.zeros((tq, LANES), F32)
        acc = over_cache(lambda off, a: cnt(off, kb, a), acc)
        acc = over_new(lambda off, a: cnt(lanes_at(off), kbn, a), acc)
        return jnp.sum(acc, axis=1, keepdims=True)

    def bit_step(it, prefix):
        bit = jnp.left_shift(jnp.int32(1), 31 - it)
        cand = (prefix | bit) ^ INT_MIN
        return jnp.where(count_ge(cand) >= float(topk), prefix | bit, prefix)

    prefix = lax.fori_loop(0, 32, bit_step, jnp.zeros((tq, 1), I32))
    thr = jnp.maximum(prefix ^ INT_MIN, INT_MIN + 1)

    n_gt = count_ge(thr + 1)
    need = float(topk) - n_gt

    def bias_block(lane_off, width, seen):
        kk = key_scr[:, pl.ds(lane_off, width)]
        tie = kk == thr
        tie_f = jnp.where(tie, 1.0, 0.0)
        tri = (lax.broadcasted_iota(I32, (width, width), 0) <= lax.broadcasted_iota(I32, (width, width), 1))
        rank = seen + _dot(tie_f.astype(BF16), jnp.where(tri, 1.0, 0.0).astype(BF16))
        keep = (kk > thr) | (tie & (rank <= need))
        bias_scr[:, pl.ds(lane_off, width)] = jnp.where(keep, 0.0, NEG_BIG)
        return seen + jnp.sum(tie_f, axis=1, keepdims=True)

    seen = over_cache(lambda off, s: bias_block(off, kb, s), jnp.zeros((tq, 1), F32))
    over_new(lambda off, s: bias_block(lanes_at(off), kbn, s), seen)

    rows = Q_PER_KV * tq
    for g in range(N_KV_HEADS):
        qg = qs[g * rows:(g + 1) * rows, :]
        m_scr[...] = jnp.full((rows, 1), NEG_BIG, F32)
        l_scr[...] = jnp.zeros((rows, 1), F32)
        acc_scr[...] = jnp.zeros((rows, HEAD_DIM), F32)

        def attend(k_blk, v_blk, lane_off, width):
            b = bias_scr[:, pl.ds(lane_off, width)]
            s = _dot_nt(qg, k_blk) + jnp.tile(b, (Q_PER_KV, 1))
            m_old = m_scr[...]
            m_new = jnp.maximum(m_old, jnp.max(s, axis=1, keepdims=True))
            alpha = jnp.exp(m_old - m_new)
            p = jnp.exp(s - m_new)
            l_scr[...] = alpha * l_scr[...] + jnp.sum(p, axis=1, keepdims=True)
            acc_scr[...] = alpha * acc_scr[...] + _dot(p.astype(BF16), v_blk)
            m_scr[...] = m_new

        hs = slice(g * HEAD_DIM, (g + 1) * HEAD_DIM)

        def p3c(off, c):
            attend(ck_ref[0, pl.ds(off, kb), hs].astype(BF16), cv_ref[0, pl.ds(off, kb), hs].astype(BF16), off, kb)
            return c

        def p3n(off, c):
            attend(kbf[g, pl.ds(off, kbn), :], vbf[g, pl.ds(off, kbn), :], lanes_at(off), kbn)
            return c

        over_cache(p3c, 0)
        over_new(p3n, 0)

        out = acc_scr[...] / l_scr[...]
        for r in range(Q_PER_KV):
            hh = g * Q_PER_KV + r
            o_ref[0, :, hh * HEAD_DIM:(hh + 1) * HEAD_DIM] = out[r * tq:(r + 1) * tq].astype(o_ref.dtype)


def _attention(q, qi, misc, k, v, cache=None):
    b, t, _ = q.shape
    has_cache = cache is not None
    past = cache[0].shape[1] if has_cache else 0
    n_keys = past + t
    topk = min(TOPK_MAX, n_keys // 4)
    tq = min(t, 128)
    kb = 256
    key_w = past + -(-t // LANES) * LANES
    rows = Q_PER_KV * tq
    qtile = lambda w: pl.BlockSpec((1, tq, w), lambda i, j: (i, j, 0))
    whole = lambda n, w: pl.BlockSpec((1, n, w), lambda i, j: (i, 0, 0))
    in_specs = [qtile(N_HEADS * HEAD_DIM), qtile(QI_W), qtile(LANES), whole(t, KV_W), whole(t, KV_W), whole(t, LANES)]
    args = [q, qi, misc, k, v, misc]
    if has_cache:
        in_specs += [whole(past, KV_W), whole(past, KV_W), whole(past, IDX_DIM)]
        args += list(cache)
    kern = functools.partial(_attn_kernel, tq=tq, kb=kb, topk=topk, past=past, t_new=t, has_cache=has_cache)
    return pl.pallas_call(
        kern,
        grid=(b, t // tq),
        in_specs=in_specs,
        out_specs=qtile(N_HEADS * HEAD_DIM),
        out_shape=jax.ShapeDtypeStruct((b, t, N_HEADS * HEAD_DIM), BF16),
        scratch_shapes=[
            pltpu.VMEM((N_KV_HEADS, t, HEAD_DIM), BF16),
            pltpu.VMEM((N_KV_HEADS, t, HEAD_DIM), BF16),
            pltpu.VMEM((t, IDX_DIM), BF16),
            pltpu.VMEM((tq, key_w), I32),
            pltpu.VMEM((tq, key_w), F32),
            pltpu.VMEM((IDX_HEADS * tq, IDX_DIM), BF16),
            pltpu.VMEM((N_HEADS * tq, HEAD_DIM), BF16),
            pltpu.VMEM((rows, 1), F32),
            pltpu.VMEM((rows, 1), F32),
            pltpu.VMEM((rows, HEAD_DIM), F32),
        ],
        compiler_params=_cparams("parallel", "arbitrary"),
        name="attn_cache" if has_cache else "attn_prompt",
    )(*args)


def _ssd_kernel(xbc_ref, misc_ref, zs_ref, h0_ref, cst_ref, wconv_ref, bconv_ref, alog_ref, dskip_ref, gnorm_ref,
                expand_ref, y_ref, hout_ref, buf, xs_scr, bm_scr, cm_scr):
    c = pl.program_id(1)
    L = CHUNK
    tail = CONV_W - 1

    @pl.when(c == 0)
    def _():
        buf[0:8, :] = jnp.zeros((8, CONV_CH), F32)
        buf[8 - tail:8, :] = cst_ref[0]
        hout_ref[...] = h0_ref[...]

    buf[8:8 + L, :] = xbc_ref[0]
    slab = 512
    for s in range(CONV_CH // slab):
        cs = slice(s * slab, (s + 1) * slab)
        xc = bconv_ref[:, cs]
        for jj in range(CONV_W):
            xc = xc + wconv_ref[jj:jj + 1, cs] * buf[8 - tail + jj:8 - tail + jj + L, cs]
        xa = _silu(xc)
        if s * slab < D_INNER:
            xs_scr[:, cs] = xa
        elif s * slab < D_INNER + SSM_GROUPS * D_STATE:
            bm_scr[:, s * slab - D_INNER:(s + 1) * slab - D_INNER] = xa.astype(BF16)
        else:
            o = D_INNER + SSM_GROUPS * D_STATE
            cm_scr[:, s * slab - o:(s + 1) * slab - o] = xa.astype(BF16)
    buf[0:8, :] = buf[L:L + 8, :]

    dt = misc_ref[0, :, MISC_DT:MISC_DT + SSM_HEADS]
    a = -jnp.exp(alog_ref[...])
    ri = lax.broadcasted_iota(I32, (L, L), 0)
    ci = lax.broadcasted_iota(I32, (L, L), 1)
    tri = jnp.where(ri >= ci, 1.0, 0.0).astype(BF16)
    ones = jnp.ones((L, L), BF16)
    acs = _exact_dot_left(tri, dt * a, 3)
    expand = expand_ref[...]
    col_acs = _exact_dot(acs, expand, 3)
    col_dt = _exact_dot(dt, expand, 2)
    row_i = lax.broadcasted_iota(I32, (L, D_INNER), 0)
    lane_j = lax.broadcasted_iota(I32, (L, D_INNER), 1) % SSM_HEAD_DIM
    eye = row_i == lane_j
    causal = row_i >= lane_j
    row_acs = _exact_dot_left(ones, jnp.where(eye, col_acs, 0.0), 3)
    row_dt = _exact_dot_left(ones, jnp.where(eye, col_dt, 0.0), 2)
    a_last = acs[L - 1:L, :]

    blk_r = lax.broadcasted_iota(I32, (GROUP_W, GROUP_W), 0) // SSM_HEAD_DIM
    blk_c = lax.broadcasted_iota(I32, (GROUP_W, GROUP_W), 1) // SSM_HEAD_DIM
    same_head = blk_r == blk_c

    for g in range(SSM_GROUPS):
        gs = slice(g * GROUP_W, (g + 1) * GROUP_W)
        ns = slice(g * D_STATE, (g + 1) * D_STATE)
        ce, re = col_acs[:, gs], row_acs[:, gs]
        bg, cg = bm_scr[:, ns], cm_scr[:, ns]
        xg = xs_scr[:, gs]
        cb = _dot_nt(cg, jnp.tile(bg, (HEADS_PER_GROUP, 1)))
        mm = cb * jnp.exp(jnp.where(causal[:, gs], ce - re, -jnp.inf)) * row_dt[:, gs]
        xbd = jnp.where(same_head, jnp.tile(xg.astype(BF16), (HEADS_PER_GROUP, 1)), 0.0).astype(BF16)
        y_diag = _dot(mm.astype(BF16), xbd)
        hprev = hout_ref[0, g * HEADS_PER_GROUP:(g + 1) * HEADS_PER_GROUP].reshape(GROUP_W, D_STATE)
        y_off = jnp.exp(ce) * _dot_nt(cg, hprev.astype(BF16))
        w_state = jnp.exp(ce[L - 1:L, :] - ce) * col_dt[:, gs]
        st = _dot_tn((xg * w_state).astype(BF16), bg)
        for r in range(HEADS_PER_GROUP):
            hh = g * HEADS_PER_GROUP + r
            decay = jnp.exp(a_last[:, hh:hh + 1])
            rs = slice(r * SSM_HEAD_DIM, (r + 1) * SSM_HEAD_DIM)
            hout_ref[0, hh] = decay * hprev[rs] + st[rs]
        yt = (y_diag + y_off + dskip_ref[:, gs] * xg) * zs_ref[0, :, gs].astype(F32)
        ms = jnp.mean(yt * yt, axis=-1, keepdims=True)
        y_ref[0, :, gs] = (yt * lax.rsqrt(ms + EPS) * gnorm_ref[:, gs]).astype(y_ref.dtype)


def _exact_dot_left(m, x, n):
    out = None
    for p in _split_bf16(x, n):
        t = _dot(m, p)
        out = t if out is None else out + t
    return out


def _ssd(xbc, misc, zs, h0, conv_state, w_conv, b_conv, a_log, dskip_full, g_norm, expand):
    b, t, _ = xbc.shape
    nc = t // CHUNK
    chunk = lambda w: pl.BlockSpec((1, CHUNK, w), lambda i, c: (i, c, 0))
    const2 = lambda r, w: pl.BlockSpec((r, w), lambda i, c: (0, 0))
    state = pl.BlockSpec((1, SSM_HEADS, SSM_HEAD_DIM, D_STATE), lambda i, c: (i, 0, 0, 0))
    return pl.pallas_call(
        _ssd_kernel,
        grid=(b, nc),
        in_specs=[
            chunk(CONV_CH), chunk(LANES), chunk(D_INNER), state,
            pl.BlockSpec((1, CONV_W - 1, CONV_CH), lambda i, c: (i, 0, 0)),
            const2(CONV_W, CONV_CH), const2(1, CONV_CH), const2(1, SSM_HEADS), const2(1, D_INNER), const2(1, D_INNER),
            const2(SSM_HEADS, D_INNER),
        ],
        out_specs=[chunk(D_INNER), state],
        out_shape=[jax.ShapeDtypeStruct((b, t, D_INNER), BF16),
                   jax.ShapeDtypeStruct((b, SSM_HEADS, SSM_HEAD_DIM, D_STATE), F32)],
        scratch_shapes=[
            pltpu.VMEM((CHUNK + 8, CONV_CH), F32),
            pltpu.VMEM((CHUNK, D_INNER), F32),
            pltpu.VMEM((CHUNK, SSM_GROUPS * D_STATE), BF16),
            pltpu.VMEM((CHUNK, SSM_GROUPS * D_STATE), BF16),
        ],
        compiler_params=_cparams("parallel", "arbitrary"),
        name="ssd",
    )(xbc, misc, zs, h0, conv_state, w_conv, b_conv, a_log, dskip_full, g_norm, expand)


def _merge_kernel(o_ref, y_ref, g_ref, x_ref, mod_ref, gain_ref, wa_ref, ws_ref, wo_ref, x1_ref, h2_ref):
    bb, tt, d = x_ref.shape
    rows = bb * tt
    o = o_ref[...].reshape(rows, -1)
    y = y_ref[...].reshape(rows, -1)
    gates = g_ref[...].reshape(rows, -1).astype(F32)
    mixed = gates[:, :d] * _dot(o, wa_ref[...]) + gates[:, d:] * _dot(y, ws_ref[...])
    out = _dot(mixed.astype(BF16), wo_ref[...]).reshape(bb, tt, d)
    mod = mod_ref[...]
    x1 = x_ref[...] + mod[:, 2:3, :] * out
    x1_ref[...] = x1
    h2_ref[...] = _modulated_norm(x1, mod, gain_ref[...], 3, 4).astype(h2_ref.dtype)


def _merge(o_attn, y, gates, x, mod3, mod_off, gain, wa, ws, wo):
    b, t, d = x.shape
    bb, tt = _row_blocking(b, t)
    off = mod_off // bb
    tile = lambda w: pl.BlockSpec((bb, tt, w), lambda i, j: (i, j, 0))
    const = lambda r, w: pl.BlockSpec((r, w), lambda i, j: (0, 0))
    return pl.pallas_call(
        _merge_kernel,
        grid=(b // bb, t // tt),
        in_specs=[
            tile(N_HEADS * HEAD_DIM), tile(D_INNER), tile(2 * D_MODEL), tile(d),
            pl.BlockSpec((bb, 6, d), lambda i, j: (i + off, 0, 0)),
            pl.BlockSpec((1, 1, d), lambda i, j: (0, 0, 0)),
            const(N_HEADS * HEAD_DIM, d), const(D_INNER, d), const(d, d),
        ],
        out_specs=[tile(d), tile(d)],
        out_shape=[jax.ShapeDtypeStruct((b, t, d), F32), jax.ShapeDtypeStruct((b, t, d), BF16)],
        compiler_params=_cparams("parallel", "parallel"),
        name="merge",
    )(o_attn, y, gates, x, mod3, gain.reshape(1, 1, d), wa, ws, wo)


def _ffn_kernel(h2_ref, x1_ref, mod_ref, wg_ref, wu_ref, wd_ref, o_ref, acc_ref):
    f = pl.program_id(2)
    bb, tt, d = x1_ref.shape
    h2 = h2_ref[...].reshape(bb * tt, d)
    act = _silu(_dot(h2, wg_ref[...])) * _dot(h2, wu_ref[...])
    part = _dot(act.astype(BF16), wd_ref[...])

    @pl.when(f == 0)
    def _():
        acc_ref[...] = part

    @pl.when(f > 0)
    def _():
        acc_ref[...] += part

    @pl.when(f == pl.num_programs(2) - 1)
    def _():
        o_ref[...] = x1_ref[...] + mod_ref[...][:, 5:6, :] * acc_ref[...].reshape(bb, tt, d)


def _ffn(h2, x1, mod3, mod_off, w_gu, w_down):
    b, t, d = x1.shape
    bb, tt = _row_blocking(b, t)
    off = mod_off // bb
    nf = 2
    tf = D_FF // nf
    tile = pl.BlockSpec((bb, tt, d), lambda i, j, f: (i, j, 0))
    return pl.pallas_call(
        _ffn_kernel,
        grid=(b // bb, t // tt, nf),
        in_specs=[
            tile, tile,
            pl.BlockSpec((bb, 6, d), lambda i, j, f: (i + off, 0, 0)),
            pl.BlockSpec((d, tf), lambda i, j, f: (0, f)),
            pl.BlockSpec((d, tf), lambda i, j, f: (0, nf + f)),
            pl.BlockSpec((tf, d), lambda i, j, f: (f, 0)),
        ],
        out_specs=tile,
        out_shape=jax.ShapeDtypeStruct((b, t, d), F32),
        scratch_shapes=[pltpu.VMEM((bb * tt, d), F32)],
        compiler_params=_cparams("parallel", "parallel", "arbitrary"),
        name="ffn",
    )(h2, x1, mod3, w_gu, w_gu, w_down)


def _rope_tables(t, past, tm):
    half = HEAD_DIM // 2
    inv = ROPE_THETA ** (-jnp.arange(half, dtype=F32) / half)
    ang = (past + jnp.arange(t)).astype(F32)[:, None] * inv[None, :]
    cos, sin = jnp.cos(ang), jnp.sin(ang)
    cos_t = jnp.concatenate([cos, cos, cos, cos], axis=1)
    sin_t = jnp.concatenate([-sin, sin, -sin, sin], axis=1)
    if t < tm:
        cos_t, sin_t = jnp.tile(cos_t, (tm // t, 1)), jnp.tile(sin_t, (tm // t, 1))
    return cos_t, sin_t


def _group_step(x, mod3, mod_off, cache, conv_state, ssm_state, p):
    b, t, d = x.shape
    m = b * t
    tm = min(512, m)
    past = cache[0].shape[1] if cache is not None else 0
    h = _hnorm(x, mod3, mod_off, p["g_norm_mix"]).reshape(m, d)
    cos_t, sin_t = _rope_tables(t, past, tm)
    q, k = _proj_qk(h, p["w_qk"], p["gsum"], p["gexp"], p["qk_gain"], cos_t, sin_t, tm)
    v, qi, misc = _proj_vm(h, p["w_vm"], cos_t, sin_t, p["dtb"], tm)
    zs = _proj_act(h, p["w_z"], "silu", BF16, tm, D_INNER)
    xbc = _proj_act(h, p["w_x"], "none", F32, tm, CONV_CH // 2)
    gates = _proj_act(h, p["w_g"], "sigmoid", BF16, tm, 2 * D_MODEL)

    r3 = lambda a: a.reshape(b, t, a.shape[-1])
    o_attn = _attention(r3(q), r3(qi), r3(misc), r3(k), r3(v), cache)
    xbc3 = r3(xbc)
    y, h_last = _ssd(xbc3, r3(misc), r3(zs), ssm_state, conv_state, p["w_conv"], p["b_conv"], p["a_log"],
                     p["dskip_full"], p["g_ssm_norm"], p["expand"])
    x1, h2 = _merge(o_attn, y, r3(gates), x, mod3, mod_off, p["g_norm_ffn"], p["w_ba"], p["w_bs"], p["w_out"])
    out = _ffn(h2, x1, mod3, mod_off, p["w_gu"], p["w_down"])
    conv_new = jnp.concatenate([conv_state, xbc3], axis=1)[:, -(CONV_W - 1):] if t < CONV_W - 1 else xbc3[:, -(CONV_W - 1):]
    return (out, k.reshape(b, t, N_KV_HEADS, HEAD_DIM), v.reshape(b, t, N_KV_HEADS, HEAD_DIM),
            r3(misc)[:, :, :IDX_DIM], conv_new, h_last)


def _layer_params(l, w_in, g_q, g_k, g_norm_mix, g_norm_ffn, w_conv, b_conv, dt_bias, a_log, d_skip, g_ssm_norm,
                  w_branch_attn, w_branch_ssm, w_out, w_gate_up, w_down):
    sizes = (N_HEADS * HEAD_DIM, KV_W, KV_W, QI_W, IDX_DIM, IDX_HEADS, D_INNER, CONV_CH, SSM_HEADS, 2 * D_MODEL)
    offs = [0]
    for s in sizes:
        offs.append(offs[-1] + s)
    col = lambda i: w_in[l][:, offs[i]:offs[i + 1]]
    pad = LANES - (IDX_DIM + IDX_HEADS + SSM_HEADS)
    w_vm = jnp.concatenate([col(2), col(3), col(4), col(5), col(8), jnp.zeros((D_MODEL, pad), F32)], axis=1)
    head_of = jnp.arange(QK_W) // HEAD_DIM
    gsum = (head_of[:, None] == jnp.arange(LANES)[None, :]).astype(BF16)
    dtb = jnp.zeros((1, LANES), F32).at[0, MISC_DT:MISC_DT + SSM_HEADS].set(dt_bias[l])
    expand = (jnp.arange(SSM_HEADS)[:, None] == (jnp.arange(D_INNER) // SSM_HEAD_DIM)[None, :]).astype(BF16)
    return dict(
        w_qk=w_in[l][:, :QK_W].astype(BF16), w_vm=w_vm.astype(BF16), w_z=col(6).astype(BF16),
        w_x=col(7).astype(BF16), w_g=col(9).astype(BF16),
        gsum=gsum, gexp=gsum.T,
        qk_gain=jnp.concatenate([jnp.tile(g_q[l], N_HEADS), jnp.tile(g_k[l], N_KV_HEADS)]).reshape(1, QK_W),
        dtb=dtb, expand=expand,
        g_norm_mix=g_norm_mix[l], g_norm_ffn=g_norm_ffn[l],
        w_conv=w_conv[l], b_conv=b_conv[l].reshape(1, CONV_CH), a_log=a_log[l].reshape(1, SSM_HEADS),
        dskip_full=jnp.repeat(d_skip[l], SSM_HEAD_DIM).reshape(1, D_INNER), g_ssm_norm=g_ssm_norm[l].reshape(1, D_INNER),
        w_ba=w_branch_attn[l].astype(BF16), w_bs=w_branch_ssm[l].astype(BF16), w_out=w_out[l].astype(BF16),
        w_gu=w_gate_up[l].astype(BF16), w_down=w_down[l].astype(BF16),
    )


def kernel(x_prompt, x_sample, cache_k, cache_v, cache_ki, state_conv, state_ssm, c_prompt, c_sample, w_ada, b_ada, g_norm_mix, g_norm_ffn, w_in, g_q, g_k, w_conv, b_conv, dt_bias, a_log, d_skip, g_ssm_norm, w_branch_attn, w_branch_ssm, w_out, w_gate_up, w_down):
    depth = w_in.shape[0]
    bp, bs = x_prompt.shape[0], x_sample.shape[0]
    past = cache_k.shape[2]
    y_p, y_s = x_prompt, x_sample
    c_all = jnp.concatenate([c_prompt, c_sample], axis=0)
    new_p = [[] for _ in range(5)]
    new_s = [[] for _ in range(5)]
    for l in range(depth):
        p = _layer_params(l, w_in, g_q, g_k, g_norm_mix, g_norm_ffn, w_conv, b_conv, dt_bias, a_log, d_skip,
                          g_ssm_norm, w_branch_attn, w_branch_ssm, w_out, w_gate_up, w_down)
        mod3 = _ada_mod(c_all, w_ada[l], b_ada[l]).reshape(bp + bs, 6, D_MODEL)
        zero_conv = jnp.zeros((bp, CONV_W - 1, CONV_CH), F32)
        zero_ssm = jnp.zeros((bp, SSM_HEADS, SSM_HEAD_DIM, D_STATE), F32)
        y_p, *st_p = _group_step(y_p, mod3, 0, None, zero_conv, zero_ssm, p)
        cache = (cache_k[l].reshape(bs, past, KV_W), cache_v[l].reshape(bs, past, KV_W), cache_ki[l])
        y_s, *st_s = _group_step(y_s, mod3, bp, cache, state_conv[l], state_ssm[l], p)
        for acc, a in zip(new_p, st_p):
            acc.append(a)
        for acc, a in zip(new_s, st_s):
            acc.append(a)
    return (y_p, y_s, *[jnp.stack(a) for a in new_p], *[jnp.stack(a) for a in new_s])
```

```python
import functools

import jax
import jax.numpy as jnp
from jax import lax
from jax.experimental import pallas as pl
from jax.experimental.pallas import tpu as pltpu

F32, BF16, I32 = jnp.float32, jnp.bfloat16, jnp.int32

D_MODEL = 1024
CHUNK = 64
N_HEADS = 16
HEAD_DIM = 64
N_KV_HEADS = 4
Q_PER_KV = N_HEADS // N_KV_HEADS
IDX_HEADS = 8
IDX_DIM = 64
TOPK_MAX = 256
ROPE_THETA = 10000.0
D_INNER = 2 * D_MODEL
SSM_HEAD_DIM = 64
SSM_HEADS = D_INNER // SSM_HEAD_DIM
SSM_GROUPS = 8
HEADS_PER_GROUP = SSM_HEADS // SSM_GROUPS
GROUP_W = HEADS_PER_GROUP * SSM_HEAD_DIM
D_STATE = 128
CONV_W = 4
CONV_CH = D_INNER + 2 * SSM_GROUPS * D_STATE
D_FF = -(-8 * D_MODEL // (3 * 256)) * 256
EPS = 1e-6
QK_W = (N_HEADS + N_KV_HEADS) * HEAD_DIM
KV_W = N_KV_HEADS * HEAD_DIM
QI_W = IDX_HEADS * IDX_DIM
SSM_HEADS_N = SSM_HEADS

LANES = 128
MISC_WI = IDX_DIM
MISC_DT = IDX_DIM + IDX_HEADS
WI_SCALE = (IDX_HEADS ** -0.5) * (IDX_DIM ** -0.5)
INT_MIN = -(2 ** 31)
NEG_BIG = -1e30
VMEM_LIMIT = 56 * 1024 * 1024


def _cparams(*sem):
    return pltpu.CompilerParams(dimension_semantics=sem, vmem_limit_bytes=VMEM_LIMIT)


def _silu(x):
    return x * jax.nn.sigmoid(x)


def _split_bf16(x, n):
    pieces = []
    r = x
    for _ in range(n):
        p = r.astype(BF16)
        pieces.append(p)
        r = r - p.astype(F32)
    return pieces


def _dot(a, b):
    return jnp.dot(a, b, preferred_element_type=F32)


def _dot_nt(a, b):
    return lax.dot_general(a, b, (((1,), (1,)), ((), ())), preferred_element_type=F32)


def _dot_tn(a, b):
    return lax.dot_general(a, b, (((0,), (0,)), ((), ())), preferred_element_type=F32)


def _exact_dot(x, m, n):
    out = None
    for p in _split_bf16(x, n):
        t = _dot(p, m)
        out = t if out is None else out + t
    return out


def _rotate_half(x):
    w = x.shape[-1]
    lane = lax.broadcasted_iota(I32, x.shape, x.ndim - 1)
    first = (lane % HEAD_DIM) < (HEAD_DIM // 2)
    return jnp.where(first, pltpu.roll(x, w - HEAD_DIM // 2, x.ndim - 1), pltpu.roll(x, HEAD_DIM // 2, x.ndim - 1))


def _rope(x, cos, sin):
    reps = x.shape[-1] // LANES
    if reps > 1:
        cos = jnp.tile(cos, (1, reps))
        sin = jnp.tile(sin, (1, reps))
    return x * cos + _rotate_half(x) * sin


def _mod_kernel(c_ref, w_ref, b_ref, o_ref):
    s = _silu(c_ref[...])
    o_ref[...] = _dot(s.astype(BF16), w_ref[...].astype(BF16)) + b_ref[...]


def _ada_mod(c_all, w_ada, b_ada):
    bt = c_all.shape[0]
    n = w_ada.shape[1]
    tn = D_MODEL
    return pl.pallas_call(
        _mod_kernel,
        grid=(n // tn,),
        in_specs=[
            pl.BlockSpec((bt, D_MODEL), lambda j: (0, 0)),
            pl.BlockSpec((D_MODEL, tn), lambda j: (0, j)),
            pl.BlockSpec((1, tn), lambda j: (0, j)),
        ],
        out_specs=pl.BlockSpec((bt, tn), lambda j: (0, j)),
        out_shape=jax.ShapeDtypeStruct((bt, n), F32),
        compiler_params=_cparams("parallel"),
        name="ada_mod",
    )(c_all, w_ada, b_ada.reshape(1, n))


def _modulated_norm(x, mod, gain, shift_idx, scale_idx):
    ms = jnp.mean(x * x, axis=-1, keepdims=True)
    xn = x * lax.rsqrt(ms + EPS)
    sh = mod[:, shift_idx:shift_idx + 1, :]
    sc = mod[:, scale_idx:scale_idx + 1, :]
    return xn * gain * (1.0 + sc) + sh


def _hnorm_kernel(x_ref, mod_ref, g_ref, o_ref):
    o_ref[...] = _modulated_norm(x_ref[...], mod_ref[...], g_ref[...], 0, 1).astype(o_ref.dtype)


def _row_blocking(b, t):
    tt = min(t, 512)
    bb = max(1, min(b, 512 // tt))
    return bb, tt


def _hnorm(x, mod3, mod_off, gain):
    b, t, d = x.shape
    bb, tt = _row_blocking(b, t)
    off = mod_off // bb
    return pl.pallas_call(
        _hnorm_kernel,
        grid=(b // bb, t // tt),
        in_specs=[
            pl.BlockSpec((bb, tt, d), lambda i, j: (i, j, 0)),
            pl.BlockSpec((bb, 6, d), lambda i, j: (i + off, 0, 0)),
            pl.BlockSpec((1, 1, d), lambda i, j: (0, 0, 0)),
        ],
        out_specs=pl.BlockSpec((bb, tt, d), lambda i, j: (i, j, 0)),
        out_shape=jax.ShapeDtypeStruct((b, t, d), BF16),
        compiler_params=_cparams("parallel", "parallel"),
        name="hnorm",
    )(x, mod3, gain.reshape(1, 1, d))


def _proj_act_kernel(h_ref, w_ref, o_ref, *, act):
    acc = _dot(h_ref[...], w_ref[...])
    if act == "silu":
        acc = _silu(acc)
    elif act == "sigmoid":
        acc = jax.nn.sigmoid(acc)
    o_ref[...] = acc.astype(o_ref.dtype)


def _proj_act(h2d, w, act, out_dtype, tm, tn):
    m, k = h2d.shape
    n = w.shape[1]
    return pl.pallas_call(
        functools.partial(_proj_act_kernel, act=act),
        grid=(n // tn, m // tm),
        in_specs=[
            pl.BlockSpec((tm, k), lambda j, i: (i, 0)),
            pl.BlockSpec((k, tn), lambda j, i: (0, j)),
        ],
        out_specs=pl.BlockSpec((tm, tn), lambda j, i: (i, j)),
        out_shape=jax.ShapeDtypeStruct((m, n), out_dtype),
        compiler_params=_cparams("parallel", "parallel"),
        name="proj_" + act,
    )(h2d, w)


def _qk_kernel(h_ref, w_ref, gsum_ref, gexp_ref, gain_ref, cos_ref, sin_ref, q_ref, k_ref):
    acc = _dot(h_ref[...], w_ref[...])
    ss = _exact_dot(acc * acc, gsum_ref[...], 2)
    rs = lax.rsqrt(ss * (1.0 / HEAD_DIM) + EPS)
    rs_full = _exact_dot(rs, gexp_ref[...], 2)
    xn = acc * rs_full * gain_ref[...]
    out = _rope(xn, cos_ref[...], sin_ref[...])
    nq = N_HEADS * HEAD_DIM
    q_ref[...] = (out[:, :nq] * (HEAD_DIM ** -0.5)).astype(q_ref.dtype)
    k_ref[...] = out[:, nq:]


def _vm_kernel(h_ref, w_ref, cos_ref, sin_ref, dtb_ref, v_ref, qi_ref, misc_ref):
    acc = _dot(h_ref[...], w_ref[...])
    cos, sin = cos_ref[...], sin_ref[...]
    v_ref[...] = acc[:, :KV_W]
    qi_ref[...] = _rope(acc[:, KV_W:KV_W + QI_W], cos, sin).astype(qi_ref.dtype)
    m = acc[:, KV_W + QI_W:]
    lane = lax.broadcasted_iota(I32, m.shape, 1)
    roped = _rope(m, cos, sin)
    dt = jax.nn.softplus(m + dtb_ref[...])
    misc_ref[...] = jnp.where(lane < MISC_WI, roped,
                              jnp.where(lane < MISC_DT, m * WI_SCALE,
                                        jnp.where(lane < MISC_DT + SSM_HEADS, dt, 0.0)))


def _table_spec(tab_rows, tm):
    nblk = tab_rows // tm
    return pl.BlockSpec((tm, LANES), lambda i: (i % nblk, 0))


def _proj_qk(h2d, w_qk, gsum, gexp, gain, cos_tab, sin_tab, tm):
    m, k = h2d.shape
    nq = N_HEADS * HEAD_DIM
    const = lambda i: (0, 0)
    return pl.pallas_call(
        _qk_kernel,
        grid=(m // tm,),
        in_specs=[
            pl.BlockSpec((tm, k), lambda i: (i, 0)),
            pl.BlockSpec((k, QK_W), const),
            pl.BlockSpec((QK_W, LANES), const),
            pl.BlockSpec((LANES, QK_W), const),
            pl.BlockSpec((1, QK_W), const),
            _table_spec(cos_tab.shape[0], tm),
            _table_spec(sin_tab.shape[0], tm),
        ],
        out_specs=[pl.BlockSpec((tm, nq), lambda i: (i, 0)), pl.BlockSpec((tm, KV_W), lambda i: (i, 0))],
        out_shape=[jax.ShapeDtypeStruct((m, nq), BF16), jax.ShapeDtypeStruct((m, KV_W), F32)],
        compiler_params=_cparams("parallel"),
        name="proj_qk",
    )(h2d, w_qk, gsum, gexp, gain, cos_tab, sin_tab)


def _proj_vm(h2d, w_vm, cos_tab, sin_tab, dtb, tm):
    m, k = h2d.shape
    wn = w_vm.shape[1]
    const = lambda i: (0, 0)
    return pl.pallas_call(
        _vm_kernel,
        grid=(m // tm,),
        in_specs=[
            pl.BlockSpec((tm, k), lambda i: (i, 0)),
            pl.BlockSpec((k, wn), const),
            _table_spec(cos_tab.shape[0], tm),
            _table_spec(sin_tab.shape[0], tm),
            pl.BlockSpec((1, LANES), const),
        ],
        out_specs=[pl.BlockSpec((tm, KV_W), lambda i: (i, 0)), pl.BlockSpec((tm, QI_W), lambda i: (i, 0)),
                   pl.BlockSpec((tm, LANES), lambda i: (i, 0))],
        out_shape=[jax.ShapeDtypeStruct((m, KV_W), F32), jax.ShapeDtypeStruct((m, QI_W), BF16),
                   jax.ShapeDtypeStruct((m, LANES), F32)],
        compiler_params=_cparams("parallel"),
        name="proj_vm",
    )(h2d, w_vm, cos_tab, sin_tab, dtb)


def _attn_kernel(*refs, tq, kb, topk, past, t_new, has_cache):
    n_in = 9 if has_cache else 6
    q_ref, qi_ref, wi_ref, k_ref, v_ref, kim_ref = refs[:6]
    o_ref = refs[n_in]
    kbf, vbf, kibf, sc_scr, bias_scr, qis, qs, mb_scr, acc_scr = refs[n_in + 1:n_in + 10]
    if has_cache:
        ck_ref, cv_ref, cki_ref = refs[6:9]
        ckb, cvb = refs[n_in + 10:]
    j = pl.program_id(1)
    kbn = min(kb, t_new)
    n_cache_blocks = past // kb

    def ones_column(n):
        return jnp.where(lax.broadcasted_iota(I32, (n, HEAD_DIM), 1) == 0, 1.0, 0.0).astype(BF16)

    @pl.when(j == 0)
    def _():
        for g in range(N_KV_HEADS):
            hs = slice(g * HEAD_DIM, (g + 1) * HEAD_DIM)
            kbf[g] = k_ref[0, :, hs].astype(BF16)
            vbf[g, :, 0:HEAD_DIM] = v_ref[0, :, hs].astype(BF16)
            vbf[g, :, HEAD_DIM:2 * HEAD_DIM] = ones_column(t_new)
        kibf[...] = kim_ref[0, :, 0:IDX_DIM].astype(BF16)
        if has_cache:
            def fill(i, c):
                rs = pl.ds(pl.multiple_of(i * kb, kb), kb)
                for g in range(N_KV_HEADS):
                    hs = slice(g * HEAD_DIM, (g + 1) * HEAD_DIM)
                    ckb[g, rs, :] = ck_ref[0, rs, hs].astype(BF16)
                    cvb[g, rs, 0:HEAD_DIM] = cv_ref[0, rs, hs].astype(BF16)
                    cvb[g, rs, HEAD_DIM:2 * HEAD_DIM] = ones_column(kb)
                return c
            lax.fori_loop(0, n_cache_blocks, fill, 0)

    for h in range(IDX_HEADS):
        qis[h * tq:(h + 1) * tq, :] = qi_ref[0, :, h * IDX_DIM:(h + 1) * IDX_DIM]
    for h in range(N_HEADS):
        qs[h * tq:(h + 1) * tq, :] = q_ref[0, :, h * HEAD_DIM:(h + 1) * HEAD_DIM]
    wi = wi_ref[0, :, MISC_WI:MISC_WI + IDX_HEADS]

    qpos = past + j * tq + lax.broadcasted_iota(I32, (tq, 1), 0)
    limit = (qpos // CHUNK + 1) * CHUNK
    n_new_blocks = (j * tq + tq + kbn - 1) // kbn

    def over_cache(fn, init):
        if not has_cache:
            return init
        return lax.fori_loop(0, n_cache_blocks, lambda i, c: fn(pl.multiple_of(i * kb, kb), c), init)

    def over_new(fn, init):
        if t_new <= kb:
            return fn(0, init)
        return lax.fori_loop(0, n_new_blocks, lambda i, c: fn(pl.multiple_of(i * kbn, kbn), c), init)

    def lanes_at(off):
        return past + off if isinstance(off, int) else pl.multiple_of(past + off, LANES)

    def score_block(ki_blk, kpos0, width, lane_off):
        lg = _dot_nt(qis[...], ki_blk)
        sc = jnp.zeros((tq, width), F32)
        for h in range(IDX_HEADS):
            sc = sc + wi[:, h:h + 1] * jnp.maximum(lg[h * tq:(h + 1) * tq], 0.0)
        kpos = kpos0 + lax.broadcasted_iota(I32, (1, width), 1)
        sc_scr[:, pl.ds(lane_off, width)] = jnp.where(kpos < limit, sc, -jnp.inf)

    def p1c(off, c):
        score_block(cki_ref[0, pl.ds(off, kb), :].astype(BF16), off, kb, off)
        return c

    def p1n(off, c):
        score_block(kibf[pl.ds(off, kbn), :], past + off, kbn, lanes_at(off))
        return c

    over_cache(p1c, 0)
    over_new(p1n, 0)

    def key_to_float(c):
        return pltpu.bitcast(jnp.where(c >= 0, c, c ^ 0x7FFFFFFF), F32)

    def count(cmp, cand):
        def cnt(lane_off, width, acc):
            hit = jnp.where(cmp(sc_scr[:, pl.ds(lane_off, width)], cand), 1.0, 0.0)
            if width % LANES:
                lane0 = lax.broadcasted_iota(I32, (tq, LANES), 1) == 0
                return acc + jnp.where(lane0, jnp.sum(hit, axis=1, keepdims=True), 0.0)
            for s in range(width // LANES):
                acc = acc + hit[:, s * LANES:(s + 1) * LANES]
            return acc
        acc = jnp.zeros((tq, LANES), F32)
        acc = over_cache(lambda off, a: cnt(off, kb, a), acc)
        acc = over_new(lambda off, a: cnt(lanes_at(off), kbn, a), acc)
        return jnp.sum(acc, axis=1, keepdims=True)

    def bit_step(it, prefix):
        bit = jnp.left_shift(jnp.int32(1), 31 - it)
        cand = key_to_float((prefix | bit) ^ INT_MIN)
        return jnp.where(count(jnp.greater_equal, cand) >= float(topk), prefix | bit, prefix)

    prefix = lax.fori_loop(0, 32, bit_step, jnp.zeros((tq, 1), I32))
    thr = key_to_float(prefix ^ INT_MIN)
    flt_max = float(jnp.finfo(F32).max)
    thr = jnp.where(thr >= -flt_max, thr, -flt_max)

    need = float(topk) - count(jnp.greater, thr)

    def bias_block(lane_off, width, seen):
        sc = sc_scr[:, pl.ds(lane_off, width)]
        tie = sc == thr
        tie_f = jnp.where(tie, 1.0, 0.0)
        tri = (lax.broadcasted_iota(I32, (width, width), 0) <= lax.broadcasted_iota(I32, (width, width), 1))
        rank = seen + _dot(tie_f.astype(BF16), jnp.where(tri, 1.0, 0.0).astype(BF16))
        keep = (sc > thr) | (tie & (rank <= need))
        bias_scr[:, pl.ds(lane_off, width)] = jnp.where(keep, 0.0, NEG_BIG)
        return seen + jnp.sum(tie_f, axis=1, keepdims=True)

    seen = over_cache(lambda off, s: bias_block(off, kb, s), jnp.zeros((tq, 1), F32))
    over_new(lambda off, s: bias_block(lanes_at(off), kbn, s), seen)

    rows = Q_PER_KV * tq

    def logits(g, k_blk, lane_off, width):
        b = bias_scr[:, pl.ds(lane_off, width)]
        return _dot_nt(qs[g * rows:(g + 1) * rows, :], k_blk) + jnp.tile(b, (Q_PER_KV, 1))

    def pass_a(k_of, lane_off, width):
        for g in range(N_KV_HEADS):
            s = logits(g, k_of(g), lane_off, width)
            m = mb_scr[g]
            if width % LANES:
                m = jnp.maximum(m, jnp.max(s, axis=1, keepdims=True))
            else:
                for c in range(width // LANES):
                    m = jnp.maximum(m, s[:, c * LANES:(c + 1) * LANES])
            mb_scr[g] = m

    def pass_b(k_of, v_of, lane_off, width):
        for g in range(N_KV_HEADS):
            s = logits(g, k_of(g), lane_off, width)
            m = mb_scr[g]
            m = m[:, :width] if width < LANES else jnp.tile(m, (1, width // LANES))
            p = jnp.exp(s - m)
            acc_scr[g] += _dot(p.astype(BF16), v_of(g))

    def cache_k(off):
        return lambda g: ckb[g, pl.ds(off, kb), :]

    def cache_v(off):
        return lambda g: cvb[g, pl.ds(off, kb), :]

    def new_k(off):
        return lambda g: kbf[g, pl.ds(off, kbn), :]

    def new_v(off):
        return lambda g: vbf[g, pl.ds(off, kbn), :]

    def run(fn, cache_args, new_args):
        def on_cache(off, c):
            fn(*[a(off) for a in cache_args], off, kb)
            return c

        def on_new(off, c):
            fn(*[a(off) for a in new_args], lanes_at(off), kbn)
            return c

        over_cache(on_cache, 0)
        over_new(on_new, 0)

    mb_scr[...] = jnp.full(mb_scr.shape, NEG_BIG, F32)
    run(pass_a, [cache_k], [new_k])
    for g in range(N_KV_HEADS):
        mb_scr[g] = jnp.broadcast_to(jnp.max(mb_scr[g], axis=1, keepdims=True), (rows, LANES))
    acc_scr[...] = jnp.zeros(acc_scr.shape, F32)
    run(pass_b, [cache_k, cache_v], [new_k, new_v])

    for g in range(N_KV_HEADS):
        a = acc_scr[g]
        out = a[:, :HEAD_DIM] / a[:, HEAD_DIM:HEAD_DIM + 1]
        for r in range(Q_PER_KV):
            hh = g * Q_PER_KV + r
            o_ref[0, :, hh * HEAD_DIM:(hh + 1) * HEAD_DIM] = out[r * tq:(r + 1) * tq].astype(o_ref.dtype)


def _attention(q, qi, misc, k, v, cache=None):
    b, t, _ = q.shape
    has_cache = cache is not None
    past = cache[0].shape[1] if has_cache else 0
    n_keys = past + t
    topk = min(TOPK_MAX, n_keys // 4)
    tq = min(t, 128)
    kb = 256
    key_w = past + -(-t // LANES) * LANES
    rows = Q_PER_KV * tq
    qtile = lambda w: pl.BlockSpec((1, tq, w), lambda i, j: (i, j, 0))
    whole = lambda n, w: pl.BlockSpec((1, n, w), lambda i, j: (i, 0, 0))
    in_specs = [qtile(N_HEADS * HEAD_DIM), qtile(QI_W), qtile(LANES), whole(t, KV_W), whole(t, KV_W), whole(t, LANES)]
    args = [q, qi, misc, k, v, misc]
    if has_cache:
        in_specs += [whole(past, KV_W), whole(past, KV_W), whole(past, IDX_DIM)]
        args += list(cache)
    kern = functools.partial(_attn_kernel, tq=tq, kb=kb, topk=topk, past=past, t_new=t, has_cache=has_cache)
    return pl.pallas_call(
        kern,
        grid=(b, t // tq),
        in_specs=in_specs,
        out_specs=qtile(N_HEADS * HEAD_DIM),
        out_shape=jax.ShapeDtypeStruct((b, t, N_HEADS * HEAD_DIM), BF16),
        scratch_shapes=[
            pltpu.VMEM((N_KV_HEADS, t, HEAD_DIM), BF16),
            pltpu.VMEM((N_KV_HEADS, t, 2 * HEAD_DIM), BF16),
            pltpu.VMEM((t, IDX_DIM), BF16),
            pltpu.VMEM((tq, key_w), F32),
            pltpu.VMEM((tq, key_w), F32),
            pltpu.VMEM((IDX_HEADS * tq, IDX_DIM), BF16),
            pltpu.VMEM((N_HEADS * tq, HEAD_DIM), BF16),
            pltpu.VMEM((N_KV_HEADS, rows, LANES), F32),
            pltpu.VMEM((N_KV_HEADS, rows, 2 * HEAD_DIM), F32),
        ] + ([
            pltpu.VMEM((N_KV_HEADS, past, HEAD_DIM), BF16),
            pltpu.VMEM((N_KV_HEADS, past, 2 * HEAD_DIM), BF16),
        ] if has_cache else []),
        compiler_params=_cparams("parallel", "arbitrary"),
        name="attn_cache" if has_cache else "attn_prompt",
    )(*args)


def _ssd_kernel(xbc_ref, misc_ref, zs_ref, h0_ref, cst_ref, wconv_ref, bconv_ref, alog_ref, dskip_ref, gnorm_ref,
                expand_ref, y_ref, hout_ref, buf, xs_scr, bm_scr, cm_scr):
    c = pl.program_id(1)
    L = CHUNK
    tail = CONV_W - 1

    @pl.when(c == 0)
    def _():
        buf[0:8, :] = jnp.zeros((8, CONV_CH), F32)
        buf[8 - tail:8, :] = cst_ref[0]
        hout_ref[...] = h0_ref[...]

    buf[8:8 + L, :] = xbc_ref[0]
    slab = 512
    for s in range(CONV_CH // slab):
        cs = slice(s * slab, (s + 1) * slab)
        xc = bconv_ref[:, cs]
        for jj in range(CONV_W):
            xc = xc + wconv_ref[jj:jj + 1, cs] * buf[8 - tail + jj:8 - tail + jj + L, cs]
        xa = _silu(xc)
        if s * slab < D_INNER:
            xs_scr[:, cs] = xa
        elif s * slab < D_INNER + SSM_GROUPS * D_STATE:
            bm_scr[:, s * slab - D_INNER:(s + 1) * slab - D_INNER] = xa.astype(BF16)
        else:
            o = D_INNER + SSM_GROUPS * D_STATE
            cm_scr[:, s * slab - o:(s + 1) * slab - o] = xa.astype(BF16)
    buf[0:8, :] = buf[L:L + 8, :]

    dt = misc_ref[0, :, MISC_DT:MISC_DT + SSM_HEADS]
    a = -jnp.exp(alog_ref[...])
    ri = lax.broadcasted_iota(I32, (L, L), 0)
    ci = lax.broadcasted_iota(I32, (L, L), 1)
    tri = jnp.where(ri >= ci, 1.0, 0.0).astype(BF16)
    ones = jnp.ones((L, L), BF16)
    acs = _exact_dot_left(tri, dt * a, 3)
    expand = expand_ref[...]
    col_acs = _exact_dot(acs, expand, 3)
    col_dt = _exact_dot(dt, expand, 2)
    row_i = lax.broadcasted_iota(I32, (L, D_INNER), 0)
    lane_j = lax.broadcasted_iota(I32, (L, D_INNER), 1) % SSM_HEAD_DIM
    eye = row_i == lane_j
    causal = row_i >= lane_j
    row_acs = _exact_dot_left(ones, jnp.where(eye, col_acs, 0.0), 3)
    row_dt = _exact_dot_left(ones, jnp.where(eye, col_dt, 0.0), 2)
    a_last = acs[L - 1:L, :]

    blk_r = lax.broadcasted_iota(I32, (GROUP_W, GROUP_W), 0) // SSM_HEAD_DIM
    blk_c = lax.broadcasted_iota(I32, (GROUP_W, GROUP_W), 1) // SSM_HEAD_DIM
    same_head = blk_r == blk_c

    for g in range(SSM_GROUPS):
        gs = slice(g * GROUP_W, (g + 1) * GROUP_W)
        ns = slice(g * D_STATE, (g + 1) * D_STATE)
        ce, re = col_acs[:, gs], row_acs[:, gs]
        bg, cg = bm_scr[:, ns], cm_scr[:, ns]
        xg = xs_scr[:, gs]
        cb = _dot_nt(cg, jnp.tile(bg, (HEADS_PER_GROUP, 1)))
        mm = cb * jnp.exp(jnp.where(causal[:, gs], ce - re, -jnp.inf)) * row_dt[:, gs]
        xbd = jnp.where(same_head, jnp.tile(xg.astype(BF16), (HEADS_PER_GROUP, 1)), 0.0).astype(BF16)
        y_diag = _dot(mm.astype(BF16), xbd)
        hprev = hout_ref[0, g * HEADS_PER_GROUP:(g + 1) * HEADS_PER_GROUP].reshape(GROUP_W, D_STATE)
        y_off = jnp.exp(ce) * _dot_nt(cg, hprev.astype(BF16))
        w_state = jnp.exp(ce[L - 1:L, :] - ce) * col_dt[:, gs]
        st = _dot_tn((xg * w_state).astype(BF16), bg)
        for r in range(HEADS_PER_GROUP):
            hh = g * HEADS_PER_GROUP + r
            decay = jnp.exp(a_last[:, hh:hh + 1])
            rs = slice(r * SSM_HEAD_DIM, (r + 1) * SSM_HEAD_DIM)
            hout_ref[0, hh] = decay * hprev[rs] + st[rs]
        yt = (y_diag + y_off + dskip_ref[:, gs] * xg) * zs_ref[0, :, gs].astype(F32)
        ms = jnp.mean(yt * yt, axis=-1, keepdims=True)
        y_ref[0, :, gs] = (yt * lax.rsqrt(ms + EPS) * gnorm_ref[:, gs]).astype(y_ref.dtype)


def _exact_dot_left(m, x, n):
    out = None
    for p in _split_bf16(x, n):
        t = _dot(m, p)
        out = t if out is None else out + t
    return out


def _ssd(xbc, misc, zs, h0, conv_state, w_conv, b_conv, a_log, dskip_full, g_norm, expand):
    b, t, _ = xbc.shape
    nc = t // CHUNK
    chunk = lambda w: pl.BlockSpec((1, CHUNK, w), lambda i, c: (i, c, 0))
    const2 = lambda r, w: pl.BlockSpec((r, w), lambda i, c: (0, 0))
    state = pl.BlockSpec((1, SSM_HEADS, SSM_HEAD_DIM, D_STATE), lambda i, c: (i, 0, 0, 0))
    return pl.pallas_call(
        _ssd_kernel,
        grid=(b, nc),
        in_specs=[
            chunk(CONV_CH), chunk(LANES), chunk(D_INNER), state,
            pl.BlockSpec((1, CONV_W - 1, CONV_CH), lambda i, c: (i, 0, 0)),
            const2(CONV_W, CONV_CH), const2(1, CONV_CH), const2(1, SSM_HEADS), const2(1, D_INNER), const2(1, D_INNER),
            const2(SSM_HEADS, D_INNER),
        ],
        out_specs=[chunk(D_INNER), state],
        out_shape=[jax.ShapeDtypeStruct((b, t, D_INNER), BF16),
                   jax.ShapeDtypeStruct((b, SSM_HEADS, SSM_HEAD_DIM, D_STATE), F32)],
        scratch_shapes=[
            pltpu.VMEM((CHUNK + 8, CONV_CH), F32),
            pltpu.VMEM((CHUNK, D_INNER), F32),
            pltpu.VMEM((CHUNK, SSM_GROUPS * D_STATE), BF16),
            pltpu.VMEM((CHUNK, SSM_GROUPS * D_STATE), BF16),
        ],
        compiler_params=_cparams("parallel", "arbitrary"),
        name="ssd",
    )(xbc, misc, zs, h0, conv_state, w_conv, b_conv, a_log, dskip_full, g_norm, expand)


def _merge_kernel(o_ref, y_ref, g_ref, x_ref, mod_ref, gain_ref, wa_ref, ws_ref, wo_ref, x1_ref, h2_ref):
    bb, tt, d = x_ref.shape
    rows = bb * tt
    o = o_ref[...].reshape(rows, -1)
    y = y_ref[...].reshape(rows, -1)
    gates = g_ref[...].reshape(rows, -1).astype(F32)
    mixed = gates[:, :d] * _dot(o, wa_ref[...]) + gates[:, d:] * _dot(y, ws_ref[...])
    out = _dot(mixed.astype(BF16), wo_ref[...]).reshape(bb, tt, d)
    mod = mod_ref[...]
    x1 = x_ref[...] + mod[:, 2:3, :] * out
    x1_ref[...] = x1
    h2_ref[...] = _modulated_norm(x1, mod, gain_ref[...], 3, 4).astype(h2_ref.dtype)


def _merge(o_attn, y, gates, x, mod3, mod_off, gain, wa, ws, wo):
    b, t, d = x.shape
    bb, tt = _row_blocking(b, t)
    off = mod_off // bb
    tile = lambda w: pl.BlockSpec((bb, tt, w), lambda i, j: (i, j, 0))
    const = lambda r, w: pl.BlockSpec((r, w), lambda i, j: (0, 0))
    return pl.pallas_call(
        _merge_kernel,
        grid=(b // bb, t // tt),
        in_specs=[
            tile(N_HEADS * HEAD_DIM), tile(D_INNER), tile(2 * D_MODEL), tile(d),
            pl.BlockSpec((bb, 6, d), lambda i, j: (i + off, 0, 0)),
            pl.BlockSpec((1, 1, d), lambda i, j: (0, 0, 0)),
            const(N_HEADS * HEAD_DIM, d), const(D_INNER, d), const(d, d),
        ],
        out_specs=[tile(d), tile(d)],
        out_shape=[jax.ShapeDtypeStruct((b, t, d), F32), jax.ShapeDtypeStruct((b, t, d), BF16)],
        compiler_params=_cparams("parallel", "parallel"),
        name="merge",
    )(o_attn, y, gates, x, mod3, gain.reshape(1, 1, d), wa, ws, wo)


def _ffn_kernel(h2_ref, x1_ref, mod_ref, wg_ref, wu_ref, wd_ref, o_ref, acc_ref):
    f = pl.program_id(2)
    bb, tt, d = x1_ref.shape
    h2 = h2_ref[...].reshape(bb * tt, d)
    act = _silu(_dot(h2, wg_ref[...])) * _dot(h2, wu_ref[...])
    part = _dot(act.astype(BF16), wd_ref[...])

    @pl.when(f == 0)
    def _():
        acc_ref[...] = part

    @pl.when(f > 0)
    def _():
        acc_ref[...] += part

    @pl.when(f == pl.num_programs(2) - 1)
    def _():
        o_ref[...] = x1_ref[...] + mod_ref[...][:, 5:6, :] * acc_ref[...].reshape(bb, tt, d)


def _ffn(h2, x1, mod3, mod_off, w_gu, w_down):
    b, t, d = x1.shape
    bb, tt = _row_blocking(b, t)
    off = mod_off // bb
    nf = 2
    tf = D_FF // nf
    tile = pl.BlockSpec((bb, tt, d), lambda i, j, f: (i, j, 0))
    return pl.pallas_call(
        _ffn_kernel,
        grid=(b // bb, t // tt, nf),
        in_specs=[
            tile, tile,
            pl.BlockSpec((bb, 6, d), lambda i, j, f: (i + off, 0, 0)),
            pl.BlockSpec((d, tf), lambda i, j, f: (0, f)),
            pl.BlockSpec((d, tf), lambda i, j, f: (0, nf + f)),
            pl.BlockSpec((tf, d), lambda i, j, f: (f, 0)),
        ],
        out_specs=tile,
        out_shape=jax.ShapeDtypeStruct((b, t, d), F32),
        scratch_shapes=[pltpu.VMEM((bb * tt, d), F32)],
        compiler_params=_cparams("parallel", "parallel", "arbitrary"),
        name="ffn",
    )(h2, x1, mod3, w_gu, w_gu, w_down)


def _rope_tables(t, past, tm):
    half = HEAD_DIM // 2
    inv = ROPE_THETA ** (-jnp.arange(half, dtype=F32) / half)
    ang = (past + jnp.arange(t)).astype(F32)[:, None] * inv[None, :]
    cos, sin = jnp.cos(ang), jnp.sin(ang)
    cos_t = jnp.concatenate([cos, cos, cos, cos], axis=1)
    sin_t = jnp.concatenate([-sin, sin, -sin, sin], axis=1)
    if t < tm:
        cos_t, sin_t = jnp.tile(cos_t, (tm // t, 1)), jnp.tile(sin_t, (tm // t, 1))
    return cos_t, sin_t


def _group_step(x, mod3, mod_off, cache, conv_state, ssm_state, p):
    b, t, d = x.shape
    m = b * t
    tm = min(512, m)
    past = cache[0].shape[1] if cache is not None else 0
    h = _hnorm(x, mod3, mod_off, p["g_norm_mix"]).reshape(m, d)
    cos_t, sin_t = _rope_tables(t, past, tm)
    q, k = _proj_qk(h, p["w_qk"], p["gsum"], p["gexp"], p["qk_gain"], cos_t, sin_t, tm)
    v, qi, misc = _proj_vm(h, p["w_vm"], cos_t, sin_t, p["dtb"], tm)
    zs = _proj_act(h, p["w_z"], "silu", BF16, tm, D_INNER)
    xbc = _proj_act(h, p["w_x"], "none", F32, tm, CONV_CH // 2)
    gates = _proj_act(h, p["w_g"], "sigmoid", BF16, tm, 2 * D_MODEL)

    r3 = lambda a: a.reshape(b, t, a.shape[-1])
    o_attn = _attention(r3(q), r3(qi), r3(misc), r3(k), r3(v), cache)
    xbc3 = r3(xbc)
    y, h_last = _ssd(xbc3, r3(misc), r3(zs), ssm_state, conv_state, p["w_conv"], p["b_conv"], p["a_log"],
                     p["dskip_full"], p["g_ssm_norm"], p["expand"])
    x1, h2 = _merge(o_attn, y, r3(gates), x, mod3, mod_off, p["g_norm_ffn"], p["w_ba"], p["w_bs"], p["w_out"])
    out = _ffn(h2, x1, mod3, mod_off, p["w_gu"], p["w_down"])
    conv_new = jnp.concatenate([conv_state, xbc3], axis=1)[:, -(CONV_W - 1):] if t < CONV_W - 1 else xbc3[:, -(CONV_W - 1):]
    return (out, k.reshape(b, t, N_KV_HEADS, HEAD_DIM), v.reshape(b, t, N_KV_HEADS, HEAD_DIM),
            r3(misc)[:, :, :IDX_DIM], conv_new, h_last)


def _layer_params(l, w_in, g_q, g_k, g_norm_mix, g_norm_ffn, w_conv, b_conv, dt_bias, a_log, d_skip, g_ssm_norm,
                  w_branch_attn, w_branch_ssm, w_out, w_gate_up, w_down):
    sizes = (N_HEADS * HEAD_DIM, KV_W, KV_W, QI_W, IDX_DIM, IDX_HEADS, D_INNER, CONV_CH, SSM_HEADS, 2 * D_MODEL)
    offs = [0]
    for s in sizes:
        offs.append(offs[-1] + s)
    col = lambda i: w_in[l][:, offs[i]:offs[i + 1]]
    pad = LANES - (IDX_DIM + IDX_HEADS + SSM_HEADS)
    w_vm = jnp.concatenate([col(2), col(3), col(4), col(5), col(8), jnp.zeros((D_MODEL, pad), F32)], axis=1)
    head_of = jnp.arange(QK_W) // HEAD_DIM
    gsum = (head_of[:, None] == jnp.arange(LANES)[None, :]).astype(BF16)
    dtb = jnp.zeros((1, LANES), F32).at[0, MISC_DT:MISC_DT + SSM_HEADS].set(dt_bias[l])
    expand = (jnp.arange(SSM_HEADS)[:, None] == (jnp.arange(D_INNER) // SSM_HEAD_DIM)[None, :]).astype(BF16)
    return dict(
        w_qk=w_in[l][:, :QK_W].astype(BF16), w_vm=w_vm.astype(BF16), w_z=col(6).astype(BF16),
        w_x=col(7).astype(BF16), w_g=col(9).astype(BF16),
        gsum=gsum, gexp=gsum.T,
        qk_gain=jnp.concatenate([jnp.tile(g_q[l], N_HEADS), jnp.tile(g_k[l], N_KV_HEADS)]).reshape(1, QK_W),
        dtb=dtb, expand=expand,
        g_norm_mix=g_norm_mix[l], g_norm_ffn=g_norm_ffn[l],
        w_conv=w_conv[l], b_conv=b_conv[l].reshape(1, CONV_CH), a_log=a_log[l].reshape(1, SSM_HEADS),
        dskip_full=jnp.repeat(d_skip[l], SSM_HEAD_DIM).reshape(1, D_INNER), g_ssm_norm=g_ssm_norm[l].reshape(1, D_INNER),
        w_ba=w_branch_attn[l].astype(BF16), w_bs=w_branch_ssm[l].astype(BF16), w_out=w_out[l].astype(BF16),
        w_gu=w_gate_up[l].astype(BF16), w_down=w_down[l].astype(BF16),
    )


def kernel(x_prompt, x_sample, cache_k, cache_v, cache_ki, state_conv, state_ssm, c_prompt, c_sample, w_ada, b_ada, g_norm_mix, g_norm_ffn, w_in, g_q, g_k, w_conv, b_conv, dt_bias, a_log, d_skip, g_ssm_norm, w_branch_attn, w_branch_ssm, w_out, w_gate_up, w_down):
    depth = w_in.shape[0]
    bp, bs = x_prompt.shape[0], x_sample.shape[0]
    past = cache_k.shape[2]
    y_p, y_s = x_prompt, x_sample
    c_all = jnp.concatenate([c_prompt, c_sample], axis=0)
    new_p = [[] for _ in range(5)]
    new_s = [[] for _ in range(5)]
    for l in range(depth):
        p = _layer_params(l, w_in, g_q, g_k, g_norm_mix, g_norm_ffn, w_conv, b_conv, dt_bias, a_log, d_skip,
                          g_ssm_norm, w_branch_attn, w_branch_ssm, w_out, w_gate_up, w_down)
        mod3 = _ada_mod(c_all, w_ada[l], b_ada[l]).reshape(bp + bs, 6, D_MODEL)
        zero_conv = jnp.zeros((bp, CONV_W - 1, CONV_CH), F32)
        zero_ssm = jnp.zeros((bp, SSM_HEADS, SSM_HEAD_DIM, D_STATE), F32)
        y_p, *st_p = _group_step(y_p, mod3, 0, None, zero_conv, zero_ssm, p)
        cache = (cache_k[l].reshape(bs, past, KV_W), cache_v[l].reshape(bs, past, KV_W), cache_ki[l])
        y_s, *st_s = _group_step(y_s, mod3, bp, cache, state_conv[l], state_ssm[l], p)
        for acc, a in zip(new_p, st_p):
            acc.append(a)
        for acc, a in zip(new_s, st_s):
            acc.append(a)
    return (y_p, y_s, *[jnp.stack(a) for a in new_p], *[jnp.stack(a) for a in new_s])
```

```python
import functools

import jax
import jax.numpy as jnp
from jax import lax
from jax.experimental import pallas as pl
from jax.experimental.pallas import tpu as pltpu

F32, BF16, I32 = jnp.float32, jnp.bfloat16, jnp.int32

D_MODEL = 1024
CHUNK = 64
N_HEADS = 16
HEAD_DIM = 64
N_KV_HEADS = 4
Q_PER_KV = N_HEADS // N_KV_HEADS
IDX_HEADS = 8
IDX_DIM = 64
TOPK_MAX = 256
ROPE_THETA = 10000.0
D_INNER = 2 * D_MODEL
SSM_HEAD_DIM = 64
SSM_HEADS = D_INNER // SSM_HEAD_DIM
SSM_GROUPS = 8
HEADS_PER_GROUP = SSM_HEADS // SSM_GROUPS
GROUP_W = HEADS_PER_GROUP * SSM_HEAD_DIM
D_STATE = 128
CONV_W = 4
CONV_CH = D_INNER + 2 * SSM_GROUPS * D_STATE
D_FF = -(-8 * D_MODEL // (3 * 256)) * 256
EPS = 1e-6
QK_W = (N_HEADS + N_KV_HEADS) * HEAD_DIM
KV_W = N_KV_HEADS * HEAD_DIM
QI_W = IDX_HEADS * IDX_DIM
SSM_HEADS_N = SSM_HEADS

LANES = 128
MISC_WI = IDX_DIM
MISC_DT = IDX_DIM + IDX_HEADS
WI_SCALE = (IDX_HEADS ** -0.5) * (IDX_DIM ** -0.5)
INT_MIN = -(2 ** 31)
NEG_BIG = -1e30
COUNT_ROWS = 64
VMEM_LIMIT = 56 * 1024 * 1024


def _cparams(*sem):
    return pltpu.CompilerParams(dimension_semantics=sem, vmem_limit_bytes=VMEM_LIMIT)


def _silu(x):
    h = 0.5 * x
    return h + h * jnp.tanh(h)


def _split_bf16(x, n):
    pieces = []
    r = x
    for _ in range(n):
        p = r.astype(BF16)
        pieces.append(p)
        r = r - p.astype(F32)
    return pieces


def _dot(a, b):
    return jnp.dot(a, b, preferred_element_type=F32)


def _dot_nt(a, b):
    return lax.dot_general(a, b, (((1,), (1,)), ((), ())), preferred_element_type=F32)


def _dot_tn(a, b):
    return lax.dot_general(a, b, (((0,), (0,)), ((), ())), preferred_element_type=F32)


def _exact_dot(x, m_stacked, n):
    return _dot(jnp.concatenate(_split_bf16(x, n), axis=1), m_stacked)


def _exact_dot_left(m_tiled, x, n):
    return _dot(m_tiled, jnp.concatenate(_split_bf16(x, n), axis=0))


def _rotate_half(x):
    w = x.shape[-1]
    lane = lax.broadcasted_iota(I32, x.shape, x.ndim - 1)
    first = (lane % HEAD_DIM) < (HEAD_DIM // 2)
    return jnp.where(first, pltpu.roll(x, w - HEAD_DIM // 2, x.ndim - 1), pltpu.roll(x, HEAD_DIM // 2, x.ndim - 1))


def _rope(x, cos, sin):
    reps = x.shape[-1] // LANES
    if reps > 1:
        cos = jnp.tile(cos, (1, reps))
        sin = jnp.tile(sin, (1, reps))
    return x * cos + _rotate_half(x) * sin


def _mod_kernel(c_ref, w_ref, b_ref, o_ref):
    s = _silu(c_ref[...])
    o_ref[...] = _dot(s.astype(BF16), w_ref[...].astype(BF16)) + b_ref[...]


def _ada_mod(c_all, w_ada, b_ada):
    bt = c_all.shape[0]
    n = w_ada.shape[1]
    tn = D_MODEL
    return pl.pallas_call(
        _mod_kernel,
        grid=(n // tn,),
        in_specs=[
            pl.BlockSpec((bt, D_MODEL), lambda j: (0, 0)),
            pl.BlockSpec((D_MODEL, tn), lambda j: (0, j)),
            pl.BlockSpec((1, tn), lambda j: (0, j)),
        ],
        out_specs=pl.BlockSpec((bt, tn), lambda j: (0, j)),
        out_shape=jax.ShapeDtypeStruct((bt, n), F32),
        compiler_params=_cparams("parallel"),
        name="ada_mod",
    )(c_all, w_ada, b_ada.reshape(1, n))


def _modulated_norm(x, mod, gain, shift_idx, scale_idx):
    ms = jnp.mean(x * x, axis=-1, keepdims=True)
    xn = x * lax.rsqrt(ms + EPS)
    sh = mod[:, shift_idx:shift_idx + 1, :]
    sc = mod[:, scale_idx:scale_idx + 1, :]
    return xn * gain * (1.0 + sc) + sh


def _hnorm_kernel(x_ref, mod_ref, g_ref, o_ref):
    o_ref[...] = _modulated_norm(x_ref[...], mod_ref[...], g_ref[...], 0, 1).astype(o_ref.dtype)


def _row_blocking(b, t):
    tt = min(t, 512)
    bb = max(1, min(b, 512 // tt))
    return bb, tt


def _hnorm(x, mod3, mod_off, gain):
    b, t, d = x.shape
    bb, tt = _row_blocking(b, t)
    off = mod_off // bb
    return pl.pallas_call(
        _hnorm_kernel,
        grid=(b // bb, t // tt),
        in_specs=[
            pl.BlockSpec((bb, tt, d), lambda i, j: (i, j, 0)),
            pl.BlockSpec((bb, 6, d), lambda i, j: (i + off, 0, 0)),
            pl.BlockSpec((1, 1, d), lambda i, j: (0, 0, 0)),
        ],
        out_specs=pl.BlockSpec((bb, tt, d), lambda i, j: (i, j, 0)),
        out_shape=jax.ShapeDtypeStruct((b, t, d), BF16),
        compiler_params=_cparams("parallel", "parallel"),
        name="hnorm",
    )(x, mod3, gain.reshape(1, 1, d))


def _proj_act_kernel(h_ref, w_ref, o_ref, *, act):
    acc = _dot(h_ref[...], w_ref[...])
    if act == "silu":
        acc = _silu(acc)
    elif act == "sigmoid":
        acc = jax.nn.sigmoid(acc)
    o_ref[...] = acc.astype(o_ref.dtype)


def _proj_act(h2d, w, act, out_dtype, tm, tn):
    m, k = h2d.shape
    n = w.shape[1]
    return pl.pallas_call(
        functools.partial(_proj_act_kernel, act=act),
        grid=(n // tn, m // tm),
        in_specs=[
            pl.BlockSpec((tm, k), lambda j, i: (i, 0)),
            pl.BlockSpec((k, tn), lambda j, i: (0, j)),
        ],
        out_specs=pl.BlockSpec((tm, tn), lambda j, i: (i, j)),
        out_shape=jax.ShapeDtypeStruct((m, n), out_dtype),
        compiler_params=_cparams("parallel", "parallel"),
        name="proj_" + act,
    )(h2d, w)


def _qk_kernel(h_ref, w_ref, gsum_ref, gexp_ref, gain_ref, cos_ref, sin_ref, q_ref, k_ref):
    acc = _dot(h_ref[...], w_ref[...])
    ss = _exact_dot(acc * acc, gsum_ref[...], 2)
    rs = lax.rsqrt(ss * (1.0 / HEAD_DIM) + EPS)
    rs_full = _exact_dot(rs, gexp_ref[...], 2)
    xn = acc * rs_full * gain_ref[...]
    out = _rope(xn, cos_ref[...], sin_ref[...])
    nq = N_HEADS * HEAD_DIM
    q_ref[...] = (out[:, :nq] * (HEAD_DIM ** -0.5)).astype(q_ref.dtype)
    k_ref[...] = out[:, nq:]


def _vm_kernel(h_ref, w_ref, cos_ref, sin_ref, dtb_ref, v_ref, qi_ref, misc_ref):
    acc = _dot(h_ref[...], w_ref[...])
    cos, sin = cos_ref[...], sin_ref[...]
    v_ref[...] = acc[:, :KV_W]
    qi_ref[...] = _rope(acc[:, KV_W:KV_W + QI_W], cos, sin).astype(qi_ref.dtype)
    m = acc[:, KV_W + QI_W:]
    lane = lax.broadcasted_iota(I32, m.shape, 1)
    roped = _rope(m, cos, sin)
    dt = jax.nn.softplus(m + dtb_ref[...])
    misc_ref[...] = jnp.where(lane < MISC_WI, roped,
                              jnp.where(lane < MISC_DT, m * WI_SCALE,
                                        jnp.where(lane < MISC_DT + SSM_HEADS, dt, 0.0)))


def _table_spec(tab_rows, tm):
    nblk = tab_rows // tm
    return pl.BlockSpec((tm, LANES), lambda i: (i % nblk, 0))


def _proj_qk(h2d, w_qk, gsum, gexp, gain, cos_tab, sin_tab, tm):
    m, k = h2d.shape
    nq = N_HEADS * HEAD_DIM
    const = lambda i: (0, 0)
    return pl.pallas_call(
        _qk_kernel,
        grid=(m // tm,),
        in_specs=[
            pl.BlockSpec((tm, k), lambda i: (i, 0)),
            pl.BlockSpec((k, QK_W), const),
            pl.BlockSpec((2 * QK_W, LANES), const),
            pl.BlockSpec((2 * LANES, QK_W), const),
            pl.BlockSpec((1, QK_W), const),
            _table_spec(cos_tab.shape[0], tm),
            _table_spec(sin_tab.shape[0], tm),
        ],
        out_specs=[pl.BlockSpec((tm, nq), lambda i: (i, 0)), pl.BlockSpec((tm, KV_W), lambda i: (i, 0))],
        out_shape=[jax.ShapeDtypeStruct((m, nq), BF16), jax.ShapeDtypeStruct((m, KV_W), F32)],
        compiler_params=_cparams("parallel"),
        name="proj_qk",
    )(h2d, w_qk, gsum, gexp, gain, cos_tab, sin_tab)


def _proj_vm(h2d, w_vm, cos_tab, sin_tab, dtb, tm):
    m, k = h2d.shape
    wn = w_vm.shape[1]
    const = lambda i: (0, 0)
    return pl.pallas_call(
        _vm_kernel,
        grid=(m // tm,),
        in_specs=[
            pl.BlockSpec((tm, k), lambda i: (i, 0)),
            pl.BlockSpec((k, wn), const),
            _table_spec(cos_tab.shape[0], tm),
            _table_spec(sin_tab.shape[0], tm),
            pl.BlockSpec((1, LANES), const),
        ],
        out_specs=[pl.BlockSpec((tm, KV_W), lambda i: (i, 0)), pl.BlockSpec((tm, QI_W), lambda i: (i, 0)),
                   pl.BlockSpec((tm, LANES), lambda i: (i, 0))],
        out_shape=[jax.ShapeDtypeStruct((m, KV_W), F32), jax.ShapeDtypeStruct((m, QI_W), BF16),
                   jax.ShapeDtypeStruct((m, LANES), F32)],
        compiler_params=_cparams("parallel"),
        name="proj_vm",
    )(h2d, w_vm, cos_tab, sin_tab, dtb)


def _attn_kernel(*refs, tq, kb, topk, past, t_new, has_cache):
    n_in = 9 if has_cache else 6
    q_ref, qi_ref, wi_ref, k_ref, v_ref, kim_ref = refs[:6]
    o_ref = refs[n_in]
    kbf, vbf, kibf, sc_scr, keep_scr, qis, qs, mb_scr, acc_scr, s_scr = refs[n_in + 1:n_in + 11]
    if has_cache:
        ck_ref, cv_ref, cki_ref = refs[6:9]
        ckb, cvb = refs[n_in + 11:]
    j = pl.program_id(1)
    kbn = min(kb, t_new)
    n_cache_blocks = past // kb

    def ones_column(n):
        return jnp.where(lax.broadcasted_iota(I32, (n, HEAD_DIM), 1) == 0, 1.0, 0.0).astype(BF16)

    @pl.when(j == 0)
    def _():
        for g in range(N_KV_HEADS):
            hs = slice(g * HEAD_DIM, (g + 1) * HEAD_DIM)
            kbf[g] = k_ref[0, :, hs].astype(BF16)
            vbf[g, :, 0:HEAD_DIM] = v_ref[0, :, hs].astype(BF16)
            vbf[g, :, HEAD_DIM:2 * HEAD_DIM] = ones_column(t_new)
        kibf[...] = kim_ref[0, :, 0:IDX_DIM].astype(BF16)
        if has_cache:
            def fill(i, c):
                rs = pl.ds(pl.multiple_of(i * kb, kb), kb)
                for g in range(N_KV_HEADS):
                    hs = slice(g * HEAD_DIM, (g + 1) * HEAD_DIM)
                    ckb[g, rs, :] = ck_ref[0, rs, hs].astype(BF16)
                    cvb[g, rs, 0:HEAD_DIM] = cv_ref[0, rs, hs].astype(BF16)
                    cvb[g, rs, HEAD_DIM:2 * HEAD_DIM] = ones_column(kb)
                return c
            lax.fori_loop(0, n_cache_blocks, fill, 0)

    for h in range(IDX_HEADS):
        qis[h * tq:(h + 1) * tq, :] = qi_ref[0, :, h * IDX_DIM:(h + 1) * IDX_DIM]
    for h in range(N_HEADS):
        qs[h * tq:(h + 1) * tq, :] = q_ref[0, :, h * HEAD_DIM:(h + 1) * HEAD_DIM]
    wi_t = wi_ref[0, :, MISC_WI:MISC_WI + IDX_HEADS].T

    qpos = past + j * tq + lax.broadcasted_iota(I32, (1, tq), 1)
    limit = (qpos // CHUNK + 1) * CHUNK
    n_new_blocks = (j * tq + tq + kbn - 1) // kbn

    def over_cache(fn, init):
        if not has_cache:
            return init
        return lax.fori_loop(0, n_cache_blocks, lambda i, c: fn(pl.multiple_of(i * kb, kb), c), init)

    def over_new(fn, init):
        if t_new <= kb:
            return fn(0, init)
        return lax.fori_loop(0, n_new_blocks, lambda i, c: fn(pl.multiple_of(i * kbn, kbn), c), init)

    def lanes_at(off):
        return past + off if isinstance(off, int) else pl.multiple_of(past + off, LANES)

    def score_block(ki_blk, kpos0, width):
        lg = _dot_nt(ki_blk, qis[...])
        sc = jnp.zeros((width, tq), F32)
        for h in range(IDX_HEADS):
            sc = sc + wi_t[h:h + 1, :] * jnp.maximum(lg[:, h * tq:(h + 1) * tq], 0.0)
        kpos = kpos0 + lax.broadcasted_iota(I32, (width, 1), 0)
        sc_scr[pl.ds(kpos0, width), :] = jnp.where(kpos < limit, sc, -jnp.inf)

    def p1c(off, c):
        score_block(cki_ref[0, pl.ds(off, kb), :].astype(BF16), off, kb)
        return c

    def p1n(off, c):
        score_block(kibf[pl.ds(off, kbn), :], lanes_at(off), kbn)
        return c

    over_cache(p1c, 0)
    over_new(p1n, 0)

    def key_to_float(c):
        return pltpu.bitcast(jnp.where(c >= 0, c, c ^ 0x7FFFFFFF), F32)

    def count(cmp, cand):
        def cnt(row_off, width, acc):
            hit = jnp.where(cmp(sc_scr[pl.ds(row_off, width), :], cand), 1.0, 0.0)
            for r in range(width // COUNT_ROWS):
                acc = acc + hit[r * COUNT_ROWS:(r + 1) * COUNT_ROWS]
            return acc
        acc = jnp.zeros((COUNT_ROWS, tq), F32)
        acc = over_cache(lambda off, a: cnt(off, kb, a), acc)
        acc = over_new(lambda off, a: cnt(lanes_at(off), kbn, a), acc)
        return jnp.sum(acc, axis=0, keepdims=True)

    def bit_step(it, prefix):
        bit = jnp.left_shift(jnp.int32(1), 31 - it)
        cand = key_to_float((prefix | bit) ^ INT_MIN)
        return jnp.where(count(jnp.greater_equal, cand) >= float(topk), prefix | bit, prefix)

    prefix = lax.fori_loop(0, 32, bit_step, jnp.zeros((1, tq), I32))
    thr = key_to_float(prefix ^ INT_MIN)
    flt_max = float(jnp.finfo(F32).max)
    thr = jnp.where(thr >= -flt_max, thr, -flt_max)

    n_ge = count(jnp.greater_equal, thr)
    has_ties = jnp.max(n_ge) > float(topk)

    def keep_ranked(row_off, width, need, seen):
        sc = sc_scr[pl.ds(row_off, width), :]
        tie = sc == thr
        tie_f = jnp.where(tie, 1.0, 0.0)
        below = lax.broadcasted_iota(I32, (width, width), 1) <= lax.broadcasted_iota(I32, (width, width), 0)
        rank = seen + _dot(jnp.where(below, 1.0, 0.0).astype(BF16), tie_f.astype(BF16))
        keep = (sc > thr) | (tie & (rank <= need))
        keep_scr[pl.ds(row_off, width), :] = jnp.where(keep, 1.0, 0.0).astype(BF16)
        return seen + jnp.sum(tie_f, axis=0, keepdims=True)

    def keep_all_ties(row_off, width, c):
        sc = sc_scr[pl.ds(row_off, width), :]
        keep_scr[pl.ds(row_off, width), :] = jnp.where(sc >= thr, 1.0, 0.0).astype(BF16)
        return c

    def with_ties():
        need = float(topk) - count(jnp.greater, thr)
        seen = over_cache(lambda off, s: keep_ranked(off, kb, need, s), jnp.zeros((1, tq), F32))
        over_new(lambda off, s: keep_ranked(lanes_at(off), kbn, need, s), seen)

    def without_ties():
        over_cache(lambda off, c: keep_all_ties(off, kb, c), 0)
        over_new(lambda off, c: keep_all_ties(lanes_at(off), kbn, c), 0)

    lax.cond(has_ties, with_ties, without_ties)

    rows = Q_PER_KV * tq

    eye = jnp.where(lax.broadcasted_iota(I32, (tq, tq), 0) == lax.broadcasted_iota(I32, (tq, tq), 1), 1.0, 0.0).astype(BF16)

    def pass_a(k_of, lane_off, width):
        keep_q = _dot_nt(eye, keep_scr[pl.ds(lane_off, width), :])
        b = jnp.tile(jnp.where(keep_q > 0.5, 0.0, NEG_BIG), (Q_PER_KV, 1))
        for g in range(N_KV_HEADS):
            s = _dot_nt(qs[g * rows:(g + 1) * rows, :], k_of(g)) + b
            s_scr[g, :, pl.ds(lane_off, width)] = s
            m = mb_scr[g]
            if width % LANES:
                m = jnp.maximum(m, jnp.max(s, axis=1, keepdims=True))
            else:
                for c in range(width // LANES):
                    m = jnp.maximum(m, s[:, c * LANES:(c + 1) * LANES])
            mb_scr[g] = m

    def pass_b(v_of, lane_off, width):
        for g in range(N_KV_HEADS):
            m = mb_scr[g]
            m = m[:, :width] if width < LANES else jnp.tile(m, (1, width // LANES))
            p = jnp.exp(s_scr[g, :, pl.ds(lane_off, width)] - m)
            acc_scr[g] += _dot(p.astype(BF16), v_of(g))

    def cache_k(off):
        return lambda g: ckb[g, pl.ds(off, kb), :]

    def cache_v(off):
        return lambda g: cvb[g, pl.ds(off, kb), :]

    def new_k(off):
        return lambda g: kbf[g, pl.ds(off, kbn), :]

    def new_v(off):
        return lambda g: vbf[g, pl.ds(off, kbn), :]

    def run(fn, cache_args, new_args):
        def on_cache(off, c):
            fn(*[a(off) for a in cache_args], off, kb)
            return c

        def on_new(off, c):
            fn(*[a(off) for a in new_args], lanes_at(off), kbn)
            return c

        over_cache(on_cache, 0)
        over_new(on_new, 0)

    mb_scr[...] = jnp.full(mb_scr.shape, NEG_BIG, F32)
    run(pass_a, [cache_k], [new_k])
    for g in range(N_KV_HEADS):
        mb_scr[g] = jnp.broadcast_to(jnp.max(mb_scr[g], axis=1, keepdims=True), (rows, LANES))
    acc_scr[...] = jnp.zeros(acc_scr.shape, F32)
    run(pass_b, [cache_v], [new_v])

    for g in range(N_KV_HEADS):
        a = acc_scr[g]
        out = a[:, :HEAD_DIM] / a[:, HEAD_DIM:HEAD_DIM + 1]
        for r in range(Q_PER_KV):
            hh = g * Q_PER_KV + r
            o_ref[0, :, hh * HEAD_DIM:(hh + 1) * HEAD_DIM] = out[r * tq:(r + 1) * tq].astype(o_ref.dtype)


def _attention(q, qi, misc, k, v, cache=None):
    b, t, _ = q.shape
    has_cache = cache is not None
    past = cache[0].shape[1] if has_cache else 0
    n_keys = past + t
    topk = min(TOPK_MAX, n_keys // 4)
    tq = min(t, 128)
    kb = 256
    key_w = past + -(-t // LANES) * LANES
    rows = Q_PER_KV * tq
    qtile = lambda w: pl.BlockSpec((1, tq, w), lambda i, j: (i, j, 0))
    whole = lambda n, w: pl.BlockSpec((1, n, w), lambda i, j: (i, 0, 0))
    in_specs = [qtile(N_HEADS * HEAD_DIM), qtile(QI_W), qtile(LANES), whole(t, KV_W), whole(t, KV_W), whole(t, LANES)]
    args = [q, qi, misc, k, v, misc]
    if has_cache:
        in_specs += [whole(past, KV_W), whole(past, KV_W), whole(past, IDX_DIM)]
        args += list(cache)
    kern = functools.partial(_attn_kernel, tq=tq, kb=kb, topk=topk, past=past, t_new=t, has_cache=has_cache)
    return pl.pallas_call(
        kern,
        grid=(b, t // tq),
        in_specs=in_specs,
        out_specs=qtile(N_HEADS * HEAD_DIM),
        out_shape=jax.ShapeDtypeStruct((b, t, N_HEADS * HEAD_DIM), BF16),
        scratch_shapes=[
            pltpu.VMEM((N_KV_HEADS, t, HEAD_DIM), BF16),
            pltpu.VMEM((N_KV_HEADS, t, 2 * HEAD_DIM), BF16),
            pltpu.VMEM((t, IDX_DIM), BF16),
            pltpu.VMEM((n_keys, tq), F32),
            pltpu.VMEM((n_keys, tq), BF16),
            pltpu.VMEM((IDX_HEADS * tq, IDX_DIM), BF16),
            pltpu.VMEM((N_HEADS * tq, HEAD_DIM), BF16),
            pltpu.VMEM((N_KV_HEADS, rows, LANES), F32),
            pltpu.VMEM((N_KV_HEADS, rows, 2 * HEAD_DIM), F32),
            pltpu.VMEM((N_KV_HEADS, rows, key_w), F32),
        ] + ([
            pltpu.VMEM((N_KV_HEADS, past, HEAD_DIM), BF16),
            pltpu.VMEM((N_KV_HEADS, past, 2 * HEAD_DIM), BF16),
        ] if has_cache else []),
        compiler_params=_cparams("parallel", "arbitrary"),
        name="attn_cache" if has_cache else "attn_prompt",
    )(*args)


def _ssd_kernel(xbc_ref, misc_ref, zs_ref, h0_ref, cst_ref, wconv_ref, bconv_ref, alog_ref, dskip_ref, gnorm_ref,
                expand_ref, y_ref, hout_ref, buf, xs_scr, bm_scr, cm_scr):
    c = pl.program_id(1)
    L = CHUNK
    tail = CONV_W - 1

    @pl.when(c == 0)
    def _():
        buf[0:8, :] = jnp.zeros((8, CONV_CH), F32)
        buf[8 - tail:8, :] = cst_ref[0]
        hout_ref[...] = h0_ref[...]

    buf[8:8 + L, :] = xbc_ref[0]
    slab = 512
    for s in range(CONV_CH // slab):
        cs = slice(s * slab, (s + 1) * slab)
        xc = bconv_ref[:, cs]
        for jj in range(CONV_W):
            xc = xc + wconv_ref[jj:jj + 1, cs] * buf[8 - tail + jj:8 - tail + jj + L, cs]
        xa = _silu(xc)
        if s * slab < D_INNER:
            xs_scr[:, cs] = xa
        elif s * slab < D_INNER + SSM_GROUPS * D_STATE:
            bm_scr[:, s * slab - D_INNER:(s + 1) * slab - D_INNER] = xa.astype(BF16)
        else:
            o = D_INNER + SSM_GROUPS * D_STATE
            cm_scr[:, s * slab - o:(s + 1) * slab - o] = xa.astype(BF16)
    buf[0:8, :] = buf[L:L + 8, :]

    dt = misc_ref[0]
    a = -jnp.exp(alog_ref[...])
    ri = lax.broadcasted_iota(I32, (L, 3 * L), 0)
    ci = lax.broadcasted_iota(I32, (L, 3 * L), 1) % L
    tri3 = jnp.where(ri >= ci, 1.0, 0.0).astype(BF16)
    ones3 = jnp.ones((L, 3 * L), BF16)
    acs = _exact_dot_left(tri3, dt * a, 3)
    expand3 = expand_ref[...]
    col_acs = _exact_dot(acs, expand3, 3)
    col_dt = _exact_dot(dt, expand3[:2 * LANES], 2)
    row_i = lax.broadcasted_iota(I32, (L, D_INNER), 0)
    lane_j = lax.broadcasted_iota(I32, (L, D_INNER), 1) % SSM_HEAD_DIM
    eye = row_i == lane_j
    causal = row_i >= lane_j
    row_acs = _exact_dot_left(ones3, jnp.where(eye, col_acs, 0.0), 3)
    row_dt = _exact_dot_left(ones3[:, :2 * L], jnp.where(eye, col_dt, 0.0), 2)
    a_last = acs[L - 1:L, MISC_DT:MISC_DT + SSM_HEADS]

    blk_r = lax.broadcasted_iota(I32, (GROUP_W, GROUP_W), 0) // SSM_HEAD_DIM
    blk_c = lax.broadcasted_iota(I32, (GROUP_W, GROUP_W), 1) // SSM_HEAD_DIM
    same_head = blk_r == blk_c

    for g in range(SSM_GROUPS):
        gs = slice(g * GROUP_W, (g + 1) * GROUP_W)
        ns = slice(g * D_STATE, (g + 1) * D_STATE)
        ce, re = col_acs[:, gs], row_acs[:, gs]
        bg, cg = bm_scr[:, ns], cm_scr[:, ns]
        xg = xs_scr[:, gs]
        cb = _dot_nt(cg, jnp.tile(bg, (HEADS_PER_GROUP, 1)))
        mm = cb * jnp.exp(jnp.where(causal[:, gs], ce - re, -jnp.inf)) * row_dt[:, gs]
        xbd = jnp.where(same_head, jnp.tile(xg.astype(BF16), (HEADS_PER_GROUP, 1)), 0.0).astype(BF16)
        y_diag = _dot(mm.astype(BF16), xbd)
        hprev = hout_ref[0, g * HEADS_PER_GROUP:(g + 1) * HEADS_PER_GROUP].reshape(GROUP_W, D_STATE)
        y_off = jnp.exp(ce) * _dot_nt(cg, hprev.astype(BF16))
        w_state = jnp.exp(ce[L - 1:L, :] - ce) * col_dt[:, gs]
        st = _dot_tn((xg * w_state).astype(BF16), bg)
        for r in range(HEADS_PER_GROUP):
            hh = g * HEADS_PER_GROUP + r
            decay = jnp.exp(a_last[:, hh:hh + 1])
            rs = slice(r * SSM_HEAD_DIM, (r + 1) * SSM_HEAD_DIM)
            hout_ref[0, hh] = decay * hprev[rs] + st[rs]
        yt = (y_diag + y_off + dskip_ref[:, gs] * xg) * zs_ref[0, :, gs].astype(F32)
        ms = jnp.mean(yt * yt, axis=-1, keepdims=True)
        y_ref[0, :, gs] = (yt * lax.rsqrt(ms + EPS) * gnorm_ref[:, gs]).astype(y_ref.dtype)


def _ssd(xbc, misc, zs, h0, conv_state, w_conv, b_conv, a_log, dskip_full, g_norm, expand):
    b, t, _ = xbc.shape
    nc = t // CHUNK
    chunk = lambda w: pl.BlockSpec((1, CHUNK, w), lambda i, c: (i, c, 0))
    const2 = lambda r, w: pl.BlockSpec((r, w), lambda i, c: (0, 0))
    state = pl.BlockSpec((1, SSM_HEADS, SSM_HEAD_DIM, D_STATE), lambda i, c: (i, 0, 0, 0))
    return pl.pallas_call(
        _ssd_kernel,
        grid=(b, nc),
        in_specs=[
            chunk(CONV_CH), chunk(LANES), chunk(D_INNER), state,
            pl.BlockSpec((1, CONV_W - 1, CONV_CH), lambda i, c: (i, 0, 0)),
            const2(CONV_W, CONV_CH), const2(1, CONV_CH), const2(1, LANES), const2(1, D_INNER), const2(1, D_INNER),
            const2(3 * LANES, D_INNER),
        ],
        out_specs=[chunk(D_INNER), state],
        out_shape=[jax.ShapeDtypeStruct((b, t, D_INNER), BF16),
                   jax.ShapeDtypeStruct((b, SSM_HEADS, SSM_HEAD_DIM, D_STATE), F32)],
        scratch_shapes=[
            pltpu.VMEM((CHUNK + 8, CONV_CH), F32),
            pltpu.VMEM((CHUNK, D_INNER), F32),
            pltpu.VMEM((CHUNK, SSM_GROUPS * D_STATE), BF16),
            pltpu.VMEM((CHUNK, SSM_GROUPS * D_STATE), BF16),
        ],
        compiler_params=_cparams("parallel", "arbitrary"),
        name="ssd",
    )(xbc, misc, zs, h0, conv_state, w_conv, b_conv, a_log, dskip_full, g_norm, expand)


def _merge_kernel(o_ref, y_ref, g_ref, x_ref, mod_ref, gain_ref, wa_ref, ws_ref, wo_ref, x1_ref, h2_ref):
    bb, tt, d = x_ref.shape
    rows = bb * tt
    o = o_ref[...].reshape(rows, -1)
    y = y_ref[...].reshape(rows, -1)
    gates = g_ref[...].reshape(rows, -1).astype(F32)
    mixed = gates[:, :d] * _dot(o, wa_ref[...]) + gates[:, d:] * _dot(y, ws_ref[...])
    out = _dot(mixed.astype(BF16), wo_ref[...]).reshape(bb, tt, d)
    mod = mod_ref[...]
    x1 = x_ref[...] + mod[:, 2:3, :] * out
    x1_ref[...] = x1
    h2_ref[...] = _modulated_norm(x1, mod, gain_ref[...], 3, 4).astype(h2_ref.dtype)


def _merge(o_attn, y, gates, x, mod3, mod_off, gain, wa, ws, wo):
    b, t, d = x.shape
    bb, tt = _row_blocking(b, t)
    off = mod_off // bb
    tile = lambda w: pl.BlockSpec((bb, tt, w), lambda i, j: (i, j, 0))
    const = lambda r, w: pl.BlockSpec((r, w), lambda i, j: (0, 0))
    return pl.pallas_call(
        _merge_kernel,
        grid=(b // bb, t // tt),
        in_specs=[
            tile(N_HEADS * HEAD_DIM), tile(D_INNER), tile(2 * D_MODEL), tile(d),
            pl.BlockSpec((bb, 6, d), lambda i, j: (i + off, 0, 0)),
            pl.BlockSpec((1, 1, d), lambda i, j: (0, 0, 0)),
            const(N_HEADS * HEAD_DIM, d), const(D_INNER, d), const(d, d),
        ],
        out_specs=[tile(d), tile(d)],
        out_shape=[jax.ShapeDtypeStruct((b, t, d), F32), jax.ShapeDtypeStruct((b, t, d), BF16)],
        compiler_params=_cparams("parallel", "parallel"),
        name="merge",
    )(o_attn, y, gates, x, mod3, gain.reshape(1, 1, d), wa, ws, wo)


def _ffn_kernel(h2_ref, x1_ref, mod_ref, wg_ref, wu_ref, wd_ref, o_ref, acc_ref):
    f = pl.program_id(2)
    bb, tt, d = x1_ref.shape
    h2 = h2_ref[...].reshape(bb * tt, d)
    act = _silu(_dot(h2, wg_ref[...])) * _dot(h2, wu_ref[...])
    part = _dot(act.astype(BF16), wd_ref[...])

    @pl.when(f == 0)
    def _():
        acc_ref[...] = part

    @pl.when(f > 0)
    def _():
        acc_ref[...] += part

    @pl.when(f == pl.num_programs(2) - 1)
    def _():
        o_ref[...] = x1_ref[...] + mod_ref[...][:, 5:6, :] * acc_ref[...].reshape(bb, tt, d)


def _ffn(h2, x1, mod3, mod_off, w_gu, w_down):
    b, t, d = x1.shape
    bb, tt = _row_blocking(b, t)
    off = mod_off // bb
    nf = 2
    tf = D_FF // nf
    tile = pl.BlockSpec((bb, tt, d), lambda i, j, f: (i, j, 0))
    return pl.pallas_call(
        _ffn_kernel,
        grid=(b // bb, t // tt, nf),
        in_specs=[
            tile, tile,
            pl.BlockSpec((bb, 6, d), lambda i, j, f: (i + off, 0, 0)),
            pl.BlockSpec((d, tf), lambda i, j, f: (0, f)),
            pl.BlockSpec((d, tf), lambda i, j, f: (0, nf + f)),
            pl.BlockSpec((tf, d), lambda i, j, f: (f, 0)),
        ],
        out_specs=tile,
        out_shape=jax.ShapeDtypeStruct((b, t, d), F32),
        scratch_shapes=[pltpu.VMEM((bb * tt, d), F32)],
        compiler_params=_cparams("parallel", "parallel", "arbitrary"),
        name="ffn",
    )(h2, x1, mod3, w_gu, w_gu, w_down)


def _rope_tables(t, past, tm):
    half = HEAD_DIM // 2
    inv = ROPE_THETA ** (-jnp.arange(half, dtype=F32) / half)
    ang = (past + jnp.arange(t)).astype(F32)[:, None] * inv[None, :]
    cos, sin = jnp.cos(ang), jnp.sin(ang)
    cos_t = jnp.concatenate([cos, cos, cos, cos], axis=1)
    sin_t = jnp.concatenate([-sin, sin, -sin, sin], axis=1)
    if t < tm:
        cos_t, sin_t = jnp.tile(cos_t, (tm // t, 1)), jnp.tile(sin_t, (tm // t, 1))
    return cos_t, sin_t


def _group_step(x, mod3, mod_off, cache, conv_state, ssm_state, p):
    b, t, d = x.shape
    m = b * t
    tm = min(512, m)
    past = cache[0].shape[1] if cache is not None else 0
    h = _hnorm(x, mod3, mod_off, p["g_norm_mix"]).reshape(m, d)
    cos_t, sin_t = _rope_tables(t, past, tm)
    q, k = _proj_qk(h, p["w_qk"], p["gsum"], p["gexp"], p["qk_gain"], cos_t, sin_t, tm)
    v, qi, misc = _proj_vm(h, p["w_vm"], cos_t, sin_t, p["dtb"], tm)
    zs = _proj_act(h, p["w_z"], "silu", BF16, tm, D_INNER)
    xbc = _proj_act(h, p["w_x"], "none", F32, tm, CONV_CH // 2)
    gates = _proj_act(h, p["w_g"], "sigmoid", BF16, tm, 2 * D_MODEL)

    r3 = lambda a: a.reshape(b, t, a.shape[-1])
    o_attn = _attention(r3(q), r3(qi), r3(misc), r3(k), r3(v), cache)
    xbc3 = r3(xbc)
    y, h_last = _ssd(xbc3, r3(misc), r3(zs), ssm_state, conv_state, p["w_conv"], p["b_conv"], p["a_log"],
                     p["dskip_full"], p["g_ssm_norm"], p["expand"])
    x1, h2 = _merge(o_attn, y, r3(gates), x, mod3, mod_off, p["g_norm_ffn"], p["w_ba"], p["w_bs"], p["w_out"])
    out = _ffn(h2, x1, mod3, mod_off, p["w_gu"], p["w_down"])
    conv_new = jnp.concatenate([conv_state, xbc3], axis=1)[:, -(CONV_W - 1):] if t < CONV_W - 1 else xbc3[:, -(CONV_W - 1):]
    return (out, k.reshape(b, t, N_KV_HEADS, HEAD_DIM), v.reshape(b, t, N_KV_HEADS, HEAD_DIM),
            r3(misc)[:, :, :IDX_DIM], conv_new, h_last)


def _layer_params(l, w_in, g_q, g_k, g_norm_mix, g_norm_ffn, w_conv, b_conv, dt_bias, a_log, d_skip, g_ssm_norm,
                  w_branch_attn, w_branch_ssm, w_out, w_gate_up, w_down):
    sizes = (N_HEADS * HEAD_DIM, KV_W, KV_W, QI_W, IDX_DIM, IDX_HEADS, D_INNER, CONV_CH, SSM_HEADS, 2 * D_MODEL)
    offs = [0]
    for s in sizes:
        offs.append(offs[-1] + s)
    col = lambda i: w_in[l][:, offs[i]:offs[i + 1]]
    pad = LANES - (IDX_DIM + IDX_HEADS + SSM_HEADS)
    w_vm = jnp.concatenate([col(2), col(3), col(4), col(5), col(8), jnp.zeros((D_MODEL, pad), F32)], axis=1)
    head_of = jnp.arange(QK_W) // HEAD_DIM
    gsum = (head_of[:, None] == jnp.arange(LANES)[None, :]).astype(BF16)
    dtb = jnp.zeros((1, LANES), F32).at[0, MISC_DT:MISC_DT + SSM_HEADS].set(dt_bias[l])
    expand = ((jnp.arange(LANES) - MISC_DT)[:, None] == (jnp.arange(D_INNER) // SSM_HEAD_DIM)[None, :]).astype(BF16)
    a_log_lanes = jnp.zeros((1, LANES), F32).at[0, MISC_DT:MISC_DT + SSM_HEADS].set(a_log[l])
    return dict(
        w_qk=w_in[l][:, :QK_W].astype(BF16), w_vm=w_vm.astype(BF16), w_z=col(6).astype(BF16),
        w_x=col(7).astype(BF16), w_g=col(9).astype(BF16),
        gsum=jnp.tile(gsum, (2, 1)), gexp=jnp.tile(gsum.T, (2, 1)),
        qk_gain=jnp.concatenate([jnp.tile(g_q[l], N_HEADS), jnp.tile(g_k[l], N_KV_HEADS)]).reshape(1, QK_W),
        dtb=dtb, expand=jnp.tile(expand, (3, 1)),
        g_norm_mix=g_norm_mix[l], g_norm_ffn=g_norm_ffn[l],
        w_conv=w_conv[l], b_conv=b_conv[l].reshape(1, CONV_CH), a_log=a_log_lanes,
        dskip_full=jnp.repeat(d_skip[l], SSM_HEAD_DIM).reshape(1, D_INNER), g_ssm_norm=g_ssm_norm[l].reshape(1, D_INNER),
        w_ba=w_branch_attn[l].astype(BF16), w_bs=w_branch_ssm[l].astype(BF16), w_out=w_out[l].astype(BF16),
        w_gu=w_gate_up[l].astype(BF16), w_down=w_down[l].astype(BF16),
    )


def kernel(x_prompt, x_sample, cache_k, cache_v, cache_ki, state_conv, state_ssm, c_prompt, c_sample, w_ada, b_ada, g_norm_mix, g_norm_ffn, w_in, g_q, g_k, w_conv, b_conv, dt_bias, a_log, d_skip, g_ssm_norm, w_branch_attn, w_branch_ssm, w_out, w_gate_up, w_down):
    depth = w_in.shape[0]
    bp, bs = x_prompt.shape[0], x_sample.shape[0]
    past = cache_k.shape[2]
    y_p, y_s = x_prompt, x_sample
    c_all = jnp.concatenate([c_prompt, c_sample], axis=0)
    new_p = [[] for _ in range(5)]
    new_s = [[] for _ in range(5)]
    for l in range(depth):
        p = _layer_params(l, w_in, g_q, g_k, g_norm_mix, g_norm_ffn, w_conv, b_conv, dt_bias, a_log, d_skip,
                          g_ssm_norm, w_branch_attn, w_branch_ssm, w_out, w_gate_up, w_down)
        mod3 = _ada_mod(c_all, w_ada[l], b_ada[l]).reshape(bp + bs, 6, D_MODEL)
        zero_conv = jnp.zeros((bp, CONV_W - 1, CONV_CH), F32)
        zero_ssm = jnp.zeros((bp, SSM_HEADS, SSM_HEAD_DIM, D_STATE), F32)
        y_p, *st_p = _group_step(y_p, mod3, 0, None, zero_conv, zero_ssm, p)
        cache = (cache_k[l].reshape(bs, past, KV_W), cache_v[l].reshape(bs, past, KV_W), cache_ki[l])
        y_s, *st_s = _group_step(y_s, mod3, bp, cache, state_conv[l], state_ssm[l], p)
        for acc, a in zip(new_p, st_p):
            acc.append(a)
        for acc, a in zip(new_s, st_s):
            acc.append(a)
    return (y_p, y_s, *[jnp.stack(a) for a in new_p], *[jnp.stack(a) for a in new_s])
```

```python
import functools

import jax
import jax.numpy as jnp
from jax import lax
from jax.experimental import pallas as pl
from jax.experimental.pallas import tpu as pltpu

F32, BF16, I32 = jnp.float32, jnp.bfloat16, jnp.int32

D_MODEL = 1024
CHUNK = 64
N_HEADS = 16
HEAD_DIM = 64
N_KV_HEADS = 4
Q_PER_KV = N_HEADS // N_KV_HEADS
IDX_HEADS = 8
IDX_DIM = 64
TOPK_MAX = 256
ROPE_THETA = 10000.0
D_INNER = 2 * D_MODEL
SSM_HEAD_DIM = 64
SSM_HEADS = D_INNER // SSM_HEAD_DIM
SSM_GROUPS = 8
HEADS_PER_GROUP = SSM_HEADS // SSM_GROUPS
GROUP_W = HEADS_PER_GROUP * SSM_HEAD_DIM
D_STATE = 128
CONV_W = 4
CONV_CH = D_INNER + 2 * SSM_GROUPS * D_STATE
D_FF = -(-8 * D_MODEL // (3 * 256)) * 256
EPS = 1e-6
QK_W = (N_HEADS + N_KV_HEADS) * HEAD_DIM
KV_W = N_KV_HEADS * HEAD_DIM
QI_W = IDX_HEADS * IDX_DIM
SSM_HEADS_N = SSM_HEADS

LANES = 128
MISC_WI = IDX_DIM
MISC_DT = IDX_DIM + IDX_HEADS
WI_SCALE = (IDX_HEADS ** -0.5) * (IDX_DIM ** -0.5)
INT_MIN = -(2 ** 31)
NEG_BIG = -1e30
COUNT_ROWS = 64
VMEM_LIMIT = 56 * 1024 * 1024


def _cparams(*sem):
    return pltpu.CompilerParams(dimension_semantics=sem, vmem_limit_bytes=VMEM_LIMIT)


def _silu(x):
    h = 0.5 * x
    return h + h * jnp.tanh(h)


def _split_bf16(x, n):
    pieces = []
    r = x
    for _ in range(n):
        p = r.astype(BF16)
        pieces.append(p)
        r = r - p.astype(F32)
    return pieces


def _dot(a, b):
    return jnp.dot(a, b, preferred_element_type=F32)


def _dot_nt(a, b):
    return lax.dot_general(a, b, (((1,), (1,)), ((), ())), preferred_element_type=F32)


def _dot_tn(a, b):
    return lax.dot_general(a, b, (((0,), (0,)), ((), ())), preferred_element_type=F32)


def _exact_dot(x, m_stacked, n):
    return _dot(jnp.concatenate(_split_bf16(x, n), axis=1), m_stacked)


def _exact_dot_left(m_tiled, x, n):
    return _dot(m_tiled, jnp.concatenate(_split_bf16(x, n), axis=0))


def _rotate_half(x):
    w = x.shape[-1]
    lane = lax.broadcasted_iota(I32, x.shape, x.ndim - 1)
    first = (lane % HEAD_DIM) < (HEAD_DIM // 2)
    return jnp.where(first, pltpu.roll(x, w - HEAD_DIM // 2, x.ndim - 1), pltpu.roll(x, HEAD_DIM // 2, x.ndim - 1))


def _rope(x, cos, sin):
    reps = x.shape[-1] // LANES
    if reps > 1:
        cos = jnp.tile(cos, (1, reps))
        sin = jnp.tile(sin, (1, reps))
    return x * cos + _rotate_half(x) * sin


def _mod_kernel(c_ref, w_ref, b_ref, o_ref):
    s = _silu(c_ref[...])
    o_ref[...] = _dot(s.astype(BF16), w_ref[...].astype(BF16)) + b_ref[...]


def _ada_mod(c_all, w_ada, b_ada):
    bt = c_all.shape[0]
    n = w_ada.shape[1]
    tn = D_MODEL
    return pl.pallas_call(
        _mod_kernel,
        grid=(n // tn,),
        in_specs=[
            pl.BlockSpec((bt, D_MODEL), lambda j: (0, 0)),
            pl.BlockSpec((D_MODEL, tn), lambda j: (0, j)),
            pl.BlockSpec((1, tn), lambda j: (0, j)),
        ],
        out_specs=pl.BlockSpec((bt, tn), lambda j: (0, j)),
        out_shape=jax.ShapeDtypeStruct((bt, n), F32),
        compiler_params=_cparams("parallel"),
        name="ada_mod",
    )(c_all, w_ada, b_ada.reshape(1, n))


def _modulated_norm(x, mod, gain, shift_idx, scale_idx):
    ms = jnp.mean(x * x, axis=-1, keepdims=True)
    xn = x * lax.rsqrt(ms + EPS)
    sh = mod[:, shift_idx:shift_idx + 1, :]
    sc = mod[:, scale_idx:scale_idx + 1, :]
    return xn * gain * (1.0 + sc) + sh


def _hnorm_kernel(x_ref, mod_ref, g_ref, o_ref):
    o_ref[...] = _modulated_norm(x_ref[...], mod_ref[...], g_ref[...], 0, 1).astype(o_ref.dtype)


def _row_blocking(b, t):
    tt = min(t, 512)
    bb = max(1, min(b, 512 // tt))
    assert t % tt == 0 and b % bb == 0 and tt % CHUNK == 0, (b, t)
    return bb, tt


def _hnorm(x, mod3, mod_off, gain):
    b, t, d = x.shape
    bb, tt = _row_blocking(b, t)
    assert mod_off % bb == 0, (mod_off, bb)
    off = mod_off // bb
    return pl.pallas_call(
        _hnorm_kernel,
        grid=(b // bb, t // tt),
        in_specs=[
            pl.BlockSpec((bb, tt, d), lambda i, j: (i, j, 0)),
            pl.BlockSpec((bb, 6, d), lambda i, j: (i + off, 0, 0)),
            pl.BlockSpec((1, 1, d), lambda i, j: (0, 0, 0)),
        ],
        out_specs=pl.BlockSpec((bb, tt, d), lambda i, j: (i, j, 0)),
        out_shape=jax.ShapeDtypeStruct((b, t, d), BF16),
        compiler_params=_cparams("parallel", "parallel"),
        name="hnorm",
    )(x, mod3, gain.reshape(1, 1, d))


def _proj_act_kernel(h_ref, w_ref, o_ref, *, act):
    acc = _dot(h_ref[...], w_ref[...])
    if act == "silu":
        acc = _silu(acc)
    elif act == "sigmoid":
        acc = jax.nn.sigmoid(acc)
    o_ref[...] = acc.astype(o_ref.dtype)


def _proj_act(h2d, w, act, out_dtype, tm, tn):
    m, k = h2d.shape
    n = w.shape[1]
    return pl.pallas_call(
        functools.partial(_proj_act_kernel, act=act),
        grid=(n // tn, m // tm),
        in_specs=[
            pl.BlockSpec((tm, k), lambda j, i: (i, 0)),
            pl.BlockSpec((k, tn), lambda j, i: (0, j)),
        ],
        out_specs=pl.BlockSpec((tm, tn), lambda j, i: (i, j)),
        out_shape=jax.ShapeDtypeStruct((m, n), out_dtype),
        compiler_params=_cparams("parallel", "parallel"),
        name="proj_" + act,
    )(h2d, w)


SUBLANES = 8
CONV_SLAB = 512


def _proj_conv_kernel(h_ref, w_ref, cst_ref, wconv_ref, bconv_ref, o_ref, tail_ref, buf, *, bb, tt, tiles_per_seq):
    i = pl.program_id(1)
    tail = CONV_W - 1
    tn = w_ref.shape[1]

    @pl.when(i % tiles_per_seq == 0)
    def _():
        for s in range(bb):
            buf[s, 0:SUBLANES - tail, :] = jnp.zeros((SUBLANES - tail, tn), F32)
            buf[s, SUBLANES - tail:SUBLANES, :] = cst_ref[s]

    h = h_ref[...]
    for c in range(tn // CONV_SLAB):
        cs = slice(c * CONV_SLAB, (c + 1) * CONV_SLAB)
        acc = _dot(h, w_ref[:, cs])
        for s in range(bb):
            buf[s, SUBLANES:SUBLANES + tt, cs] = acc[s * tt:(s + 1) * tt]
            xb = buf[s, :, cs]
            xc = bconv_ref[:, cs] + wconv_ref[tail:tail + 1, cs] * xb[SUBLANES:]
            for jj in range(tail):
                shifted = pltpu.roll(xb, tail - jj, 0)[SUBLANES:]
                xc = xc + wconv_ref[jj:jj + 1, cs] * shifted
            o_ref[s * tt:(s + 1) * tt, cs] = _silu(xc).astype(o_ref.dtype)
            last = buf[s, tt:tt + SUBLANES, cs]
            tail_ref[s, :, cs] = last
            buf[s, 0:SUBLANES, cs] = last


def _proj_conv(h2d, w, conv_state, w_conv, b_conv, b, t, out_dtype):
    m, k = h2d.shape
    n = w.shape[1]
    bb, tt = _row_blocking(b, t)
    tm = bb * tt
    tiles_per_seq = t // tt
    tn = min(n, 2048)
    kern = functools.partial(_proj_conv_kernel, bb=bb, tt=tt, tiles_per_seq=tiles_per_seq)
    seq_block = lambda rows: pl.BlockSpec((bb, rows, tn), lambda j, i: (i // tiles_per_seq, 0, j))
    return pl.pallas_call(
        kern,
        grid=(n // tn, m // tm),
        in_specs=[
            pl.BlockSpec((tm, k), lambda j, i: (i, 0)),
            pl.BlockSpec((k, tn), lambda j, i: (0, j)),
            seq_block(CONV_W - 1),
            pl.BlockSpec((CONV_W, tn), lambda j, i: (0, j)),
            pl.BlockSpec((1, tn), lambda j, i: (0, j)),
        ],
        out_specs=[pl.BlockSpec((tm, tn), lambda j, i: (i, j)), seq_block(SUBLANES)],
        out_shape=[jax.ShapeDtypeStruct((m, n), out_dtype), jax.ShapeDtypeStruct((b, SUBLANES, n), F32)],
        scratch_shapes=[pltpu.VMEM((bb, SUBLANES + tt, tn), F32)],
        compiler_params=_cparams("parallel", "arbitrary"),
        name="proj_conv",
    )(h2d, w, conv_state, w_conv, b_conv)


def _store_by_head(ref, x):
    rows = x.shape[0]
    for g in range(N_KV_HEADS):
        ref[pl.ds(g, rows, stride=N_KV_HEADS), :] = x[:, g * HEAD_DIM:(g + 1) * HEAD_DIM]


def _load_head(ref, row0, rows, g):
    return ref[0, pl.ds(row0 * N_KV_HEADS + g, rows, stride=N_KV_HEADS), :]


def _qk_kernel(h_ref, w_ref, gsum_ref, gexp_ref, gain_ref, cos_ref, sin_ref, q_ref, k_ref):
    acc = _dot(h_ref[...], w_ref[...])
    ss = _exact_dot(acc * acc, gsum_ref[...], 2)
    rs = lax.rsqrt(ss * (1.0 / HEAD_DIM) + EPS)
    rs_full = _exact_dot(rs, gexp_ref[...], 2)
    xn = acc * rs_full * gain_ref[...]
    out = _rope(xn, cos_ref[...], sin_ref[...])
    nq = N_HEADS * HEAD_DIM
    q_ref[...] = (out[:, :nq] * (HEAD_DIM ** -0.5)).astype(q_ref.dtype)
    _store_by_head(k_ref, out[:, nq:])


def _vm_kernel(h_ref, w_ref, cos_ref, sin_ref, dtb_ref, v_ref, qi_ref, misc_ref):
    acc = _dot(h_ref[...], w_ref[...])
    cos, sin = cos_ref[...], sin_ref[...]
    _store_by_head(v_ref, acc[:, :KV_W])
    qi_ref[...] = _rope(acc[:, KV_W:KV_W + QI_W], cos, sin).astype(qi_ref.dtype)
    m = acc[:, KV_W + QI_W:]
    lane = lax.broadcasted_iota(I32, m.shape, 1)
    roped = _rope(m, cos, sin)
    dt = jax.nn.softplus(m + dtb_ref[...])
    misc_ref[...] = jnp.where(lane < MISC_WI, roped,
                              jnp.where(lane < MISC_DT, m * WI_SCALE,
                                        jnp.where(lane < MISC_DT + SSM_HEADS, dt, 0.0)))


def _table_spec(tab_rows, tm):
    nblk = tab_rows // tm
    return pl.BlockSpec((tm, LANES), lambda i: (i % nblk, 0))


def _proj_qk(h2d, w_qk, gsum, gexp, gain, cos_tab, sin_tab, tm):
    m, k = h2d.shape
    nq = N_HEADS * HEAD_DIM
    const = lambda i: (0, 0)
    return pl.pallas_call(
        _qk_kernel,
        grid=(m // tm,),
        in_specs=[
            pl.BlockSpec((tm, k), lambda i: (i, 0)),
            pl.BlockSpec((k, QK_W), const),
            pl.BlockSpec((2 * QK_W, LANES), const),
            pl.BlockSpec((2 * LANES, QK_W), const),
            pl.BlockSpec((1, QK_W), const),
            _table_spec(cos_tab.shape[0], tm),
            _table_spec(sin_tab.shape[0], tm),
        ],
        out_specs=[pl.BlockSpec((tm, nq), lambda i: (i, 0)), pl.BlockSpec((tm * N_KV_HEADS, HEAD_DIM), lambda i: (i, 0))],
        out_shape=[jax.ShapeDtypeStruct((m, nq), BF16), jax.ShapeDtypeStruct((m * N_KV_HEADS, HEAD_DIM), F32)],
        compiler_params=_cparams("parallel"),
        name="proj_qk",
    )(h2d, w_qk, gsum, gexp, gain, cos_tab, sin_tab)


def _proj_vm(h2d, w_vm, cos_tab, sin_tab, dtb, tm):
    m, k = h2d.shape
    wn = w_vm.shape[1]
    const = lambda i: (0, 0)
    return pl.pallas_call(
        _vm_kernel,
        grid=(m // tm,),
        in_specs=[
            pl.BlockSpec((tm, k), lambda i: (i, 0)),
            pl.BlockSpec((k, wn), const),
            _table_spec(cos_tab.shape[0], tm),
            _table_spec(sin_tab.shape[0], tm),
            pl.BlockSpec((1, LANES), const),
        ],
        out_specs=[pl.BlockSpec((tm * N_KV_HEADS, HEAD_DIM), lambda i: (i, 0)), pl.BlockSpec((tm, QI_W), lambda i: (i, 0)),
                   pl.BlockSpec((tm, LANES), lambda i: (i, 0))],
        out_shape=[jax.ShapeDtypeStruct((m * N_KV_HEADS, HEAD_DIM), F32), jax.ShapeDtypeStruct((m, QI_W), BF16),
                   jax.ShapeDtypeStruct((m, LANES), F32)],
        compiler_params=_cparams("parallel"),
        name="proj_vm",
    )(h2d, w_vm, cos_tab, sin_tab, dtb)


def _attn_kernel(*refs, tq, kb, topk, past, t_new, has_cache):
    n_in = 9 if has_cache else 6
    q_ref, qi_ref, wi_ref, k_ref, v_ref, kim_ref = refs[:6]
    o_ref = refs[n_in]
    kbf, vbf, kibf, sc_scr, keep_scr, qis, qs, mb_scr, acc_scr, s_scr = refs[n_in + 1:n_in + 11]
    if has_cache:
        ck_ref, cv_ref, cki_ref = refs[6:9]
        ckb, cvb = refs[n_in + 11:]
    j = pl.program_id(1)
    kbn = min(kb, t_new)
    n_cache_blocks = past // kb

    def ones_column(n):
        return jnp.where(lax.broadcasted_iota(I32, (n, HEAD_DIM), 1) == 0, 1.0, 0.0).astype(BF16)

    @pl.when(j == 0)
    def _():
        for g in range(N_KV_HEADS):
            kbf[g] = _load_head(k_ref, 0, t_new, g).astype(BF16)
            vbf[g, :, 0:HEAD_DIM] = _load_head(v_ref, 0, t_new, g).astype(BF16)
            vbf[g, :, HEAD_DIM:2 * HEAD_DIM] = ones_column(t_new)
        kibf[...] = kim_ref[0, :, 0:IDX_DIM].astype(BF16)
        if has_cache:
            def fill(i, c):
                row0 = pl.multiple_of(i * kb, kb)
                rs = pl.ds(row0, kb)
                for g in range(N_KV_HEADS):
                    ckb[g, rs, :] = _load_head(ck_ref, row0, kb, g).astype(BF16)
                    cvb[g, rs, 0:HEAD_DIM] = _load_head(cv_ref, row0, kb, g).astype(BF16)
                    cvb[g, rs, HEAD_DIM:2 * HEAD_DIM] = ones_column(kb)
                return c
            lax.fori_loop(0, n_cache_blocks, fill, 0)

    for h in range(IDX_HEADS):
        qis[h * tq:(h + 1) * tq, :] = qi_ref[0, :, h * IDX_DIM:(h + 1) * IDX_DIM]
    for h in range(N_HEADS):
        qs[h * tq:(h + 1) * tq, :] = q_ref[0, :, h * HEAD_DIM:(h + 1) * HEAD_DIM]
    wi_t = wi_ref[0, :, MISC_WI:MISC_WI + IDX_HEADS].T

    qpos = past + j * tq + lax.broadcasted_iota(I32, (1, tq), 1)
    limit = (qpos // CHUNK + 1) * CHUNK
    n_new_blocks = (j * tq + tq + kbn - 1) // kbn

    def over_cache(fn, init):
        if not has_cache:
            return init
        return lax.fori_loop(0, n_cache_blocks, lambda i, c: fn(pl.multiple_of(i * kb, kb), c), init)

    def over_new(fn, init):
        if t_new <= kb:
            return fn(0, init)
        return lax.fori_loop(0, n_new_blocks, lambda i, c: fn(pl.multiple_of(i * kbn, kbn), c), init)

    def lanes_at(off):
        return past + off if isinstance(off, int) else pl.multiple_of(past + off, LANES)

    def score_block(ki_blk, kpos0, width):
        lg = _dot_nt(ki_blk, qis[...])
        sc = jnp.zeros((width, tq), F32)
        for h in range(IDX_HEADS):
            sc = sc + wi_t[h:h + 1, :] * jnp.maximum(lg[:, h * tq:(h + 1) * tq], 0.0)
        kpos = kpos0 + lax.broadcasted_iota(I32, (width, 1), 0)
        sc_scr[pl.ds(kpos0, width), :] = jnp.where(kpos < limit, sc, -jnp.inf)

    def p1c(off, c):
        score_block(cki_ref[0, pl.ds(off, kb), :].astype(BF16), off, kb)
        return c

    def p1n(off, c):
        score_block(kibf[pl.ds(off, kbn), :], lanes_at(off), kbn)
        return c

    over_cache(p1c, 0)
    over_new(p1n, 0)

    def key_to_float(c):
        return pltpu.bitcast(jnp.where(c >= 0, c, c ^ 0x7FFFFFFF), F32)

    def count(cmp, cand):
        def cnt(row_off, width, acc):
            hit = jnp.where(cmp(sc_scr[pl.ds(row_off, width), :], cand), 1.0, 0.0)
            for r in range(width // COUNT_ROWS):
                acc = acc + hit[r * COUNT_ROWS:(r + 1) * COUNT_ROWS]
            return acc
        acc = jnp.zeros((COUNT_ROWS, tq), F32)
        acc = over_cache(lambda off, a: cnt(off, kb, a), acc)
        acc = over_new(lambda off, a: cnt(lanes_at(off), kbn, a), acc)
        return jnp.sum(acc, axis=0, keepdims=True)

    def bit_step(it, prefix):
        bit = jnp.left_shift(jnp.int32(1), 31 - it)
        cand = key_to_float((prefix | bit) ^ INT_MIN)
        return jnp.where(count(jnp.greater_equal, cand) >= float(topk), prefix | bit, prefix)

    prefix = lax.fori_loop(0, 32, bit_step, jnp.zeros((1, tq), I32))
    thr = key_to_float(prefix ^ INT_MIN)
    flt_max = float(jnp.finfo(F32).max)
    thr = jnp.where(thr >= -flt_max, thr, -flt_max)

    n_ge = count(jnp.greater_equal, thr)
    has_ties = jnp.max(n_ge) > float(topk)

    def keep_ranked(row_off, width, need, seen):
        sc = sc_scr[pl.ds(row_off, width), :]
        tie = sc == thr
        tie_f = jnp.where(tie, 1.0, 0.0)
        below = lax.broadcasted_iota(I32, (width, width), 1) <= lax.broadcasted_iota(I32, (width, width), 0)
        rank = seen + _dot(jnp.where(below, 1.0, 0.0).astype(BF16), tie_f.astype(BF16))
        keep = (sc > thr) | (tie & (rank <= need))
        keep_scr[pl.ds(row_off, width), :] = jnp.where(keep, 1.0, 0.0).astype(BF16)
        return seen + jnp.sum(tie_f, axis=0, keepdims=True)

    def keep_all_ties(row_off, width, c):
        sc = sc_scr[pl.ds(row_off, width), :]
        keep_scr[pl.ds(row_off, width), :] = jnp.where(sc >= thr, 1.0, 0.0).astype(BF16)
        return c

    def with_ties():
        need = float(topk) - count(jnp.greater, thr)
        seen = over_cache(lambda off, s: keep_ranked(off, kb, need, s), jnp.zeros((1, tq), F32))
        over_new(lambda off, s: keep_ranked(lanes_at(off), kbn, need, s), seen)

    def without_ties():
        over_cache(lambda off, c: keep_all_ties(off, kb, c), 0)
        over_new(lambda off, c: keep_all_ties(lanes_at(off), kbn, c), 0)

    lax.cond(has_ties, with_ties, without_ties)

    rows = Q_PER_KV * tq

    eye = jnp.where(lax.broadcasted_iota(I32, (tq, tq), 0) == lax.broadcasted_iota(I32, (tq, tq), 1), 1.0, 0.0).astype(BF16)

    def pass_a(k_of, lane_off, width):
        keep_q = _dot_nt(eye, keep_scr[pl.ds(lane_off, width), :])
        b = jnp.tile(jnp.where(keep_q > 0.5, 0.0, NEG_BIG), (Q_PER_KV, 1))
        for g in range(N_KV_HEADS):
            s = _dot_nt(qs[g * rows:(g + 1) * rows, :], k_of(g)) + b
            s_scr[g, :, pl.ds(lane_off, width)] = s
            m = mb_scr[g]
            if width % LANES:
                m = jnp.maximum(m, jnp.max(s, axis=1, keepdims=True))
            else:
                for c in range(width // LANES):
                    m = jnp.maximum(m, s[:, c * LANES:(c + 1) * LANES])
            mb_scr[g] = m

    def pass_b(v_of, lane_off, width):
        for g in range(N_KV_HEADS):
            m = mb_scr[g]
            m = m[:, :width] if width < LANES else jnp.tile(m, (1, width // LANES))
            p = jnp.exp(s_scr[g, :, pl.ds(lane_off, width)] - m)
            acc_scr[g] += _dot(p.astype(BF16), v_of(g))

    def cache_k(off):
        return lambda g: ckb[g, pl.ds(off, kb), :]

    def cache_v(off):
        return lambda g: cvb[g, pl.ds(off, kb), :]

    def new_k(off):
        return lambda g: kbf[g, pl.ds(off, kbn), :]

    def new_v(off):
        return lambda g: vbf[g, pl.ds(off, kbn), :]

    def run(fn, cache_args, new_args):
        def on_cache(off, c):
            fn(*[a(off) for a in cache_args], off, kb)
            return c

        def on_new(off, c):
            fn(*[a(off) for a in new_args], lanes_at(off), kbn)
            return c

        over_cache(on_cache, 0)
        over_new(on_new, 0)

    mb_scr[...] = jnp.full(mb_scr.shape, NEG_BIG, F32)
    run(pass_a, [cache_k], [new_k])
    for g in range(N_KV_HEADS):
        mb_scr[g] = jnp.broadcast_to(jnp.max(mb_scr[g], axis=1, keepdims=True), (rows, LANES))
    acc_scr[...] = jnp.zeros(acc_scr.shape, F32)
    run(pass_b, [cache_v], [new_v])

    for g in range(N_KV_HEADS):
        a = acc_scr[g]
        out = a[:, :HEAD_DIM] / a[:, HEAD_DIM:HEAD_DIM + 1]
        for r in range(Q_PER_KV):
            hh = g * Q_PER_KV + r
            o_ref[0, :, hh * HEAD_DIM:(hh + 1) * HEAD_DIM] = out[r * tq:(r + 1) * tq].astype(o_ref.dtype)


def _attention(q, qi, misc, k, v, cache=None):
    b, t, _ = q.shape
    has_cache = cache is not None
    past = cache[2].shape[1] if has_cache else 0
    n_keys = past + t
    topk = min(TOPK_MAX, n_keys // 4)
    tq = min(t, 128)
    kb = 256
    key_w = past + -(-t // LANES) * LANES
    rows = Q_PER_KV * tq
    qtile = lambda w: pl.BlockSpec((1, tq, w), lambda i, j: (i, j, 0))
    whole = lambda n, w: pl.BlockSpec((1, n, w), lambda i, j: (i, 0, 0))
    by_head = lambda n: whole(n * N_KV_HEADS, HEAD_DIM)
    in_specs = [qtile(N_HEADS * HEAD_DIM), qtile(QI_W), qtile(LANES), by_head(t), by_head(t), whole(t, LANES)]
    args = [q, qi, misc, k, v, misc]
    if has_cache:
        in_specs += [by_head(past), by_head(past), whole(past, IDX_DIM)]
        args += list(cache)
    kern = functools.partial(_attn_kernel, tq=tq, kb=kb, topk=topk, past=past, t_new=t, has_cache=has_cache)
    return pl.pallas_call(
        kern,
        grid=(b, t // tq),
        in_specs=in_specs,
        out_specs=qtile(N_HEADS * HEAD_DIM),
        out_shape=jax.ShapeDtypeStruct((b, t, N_HEADS * HEAD_DIM), BF16),
        scratch_shapes=[
            pltpu.VMEM((N_KV_HEADS, t, HEAD_DIM), BF16),
            pltpu.VMEM((N_KV_HEADS, t, 2 * HEAD_DIM), BF16),
            pltpu.VMEM((t, IDX_DIM), BF16),
            pltpu.VMEM((n_keys, tq), F32),
            pltpu.VMEM((n_keys, tq), BF16),
            pltpu.VMEM((IDX_HEADS * tq, IDX_DIM), BF16),
            pltpu.VMEM((N_HEADS * tq, HEAD_DIM), BF16),
            pltpu.VMEM((N_KV_HEADS, rows, LANES), F32),
            pltpu.VMEM((N_KV_HEADS, rows, 2 * HEAD_DIM), F32),
            pltpu.VMEM((N_KV_HEADS, rows, key_w), F32),
        ] + ([
            pltpu.VMEM((N_KV_HEADS, past, HEAD_DIM), BF16),
            pltpu.VMEM((N_KV_HEADS, past, 2 * HEAD_DIM), BF16),
        ] if has_cache else []),
        compiler_params=_cparams("parallel", "arbitrary"),
        name="attn_cache" if has_cache else "attn_prompt",
    )(*args)


def _ssd_kernel(xs_ref, bc_ref, misc_ref, zs_ref, h0_ref, alog_ref, dskip_ref, gnorm_ref, expand_ref, y_ref, hout_ref):
    c = pl.program_id(1)
    L = CHUNK

    @pl.when(c == 0)
    def _():
        hout_ref[...] = h0_ref[...]

    dt = misc_ref[0]
    a = -jnp.exp(alog_ref[...])
    ri = lax.broadcasted_iota(I32, (L, 3 * L), 0)
    ci = lax.broadcasted_iota(I32, (L, 3 * L), 1) % L
    tri3 = jnp.where(ri >= ci, 1.0, 0.0).astype(BF16)
    ones3 = jnp.ones((L, 3 * L), BF16)
    acs = _exact_dot_left(tri3, dt * a, 3)
    expand3 = expand_ref[...]
    col_acs = _exact_dot(acs, expand3, 3)
    col_dt = _exact_dot(dt, expand3[:2 * LANES], 2)
    row_i = lax.broadcasted_iota(I32, (L, D_INNER), 0)
    lane_j = lax.broadcasted_iota(I32, (L, D_INNER), 1) % SSM_HEAD_DIM
    eye = row_i == lane_j
    causal = row_i >= lane_j
    row_acs = _exact_dot_left(ones3, jnp.where(eye, col_acs, 0.0), 3)
    row_dt = _exact_dot_left(ones3[:, :2 * L], jnp.where(eye, col_dt, 0.0), 2)
    a_last = acs[L - 1:L, MISC_DT:MISC_DT + SSM_HEADS]

    blk_r = lax.broadcasted_iota(I32, (GROUP_W, GROUP_W), 0) // SSM_HEAD_DIM
    blk_c = lax.broadcasted_iota(I32, (GROUP_W, GROUP_W), 1) // SSM_HEAD_DIM
    same_head = blk_r == blk_c

    for g in range(SSM_GROUPS):
        gs = slice(g * GROUP_W, (g + 1) * GROUP_W)
        ns = slice(g * D_STATE, (g + 1) * D_STATE)
        ce, re = col_acs[:, gs], row_acs[:, gs]
        bg = bc_ref[0, :, ns]
        cg = bc_ref[0, :, SSM_GROUPS * D_STATE + g * D_STATE:SSM_GROUPS * D_STATE + (g + 1) * D_STATE]
        xg = xs_ref[0, :, gs]
        cb = _dot_nt(cg, jnp.tile(bg, (HEADS_PER_GROUP, 1)))
        mm = cb * jnp.exp(jnp.where(causal[:, gs], ce - re, -jnp.inf)) * row_dt[:, gs]
        xbd = jnp.where(same_head, jnp.tile(xg.astype(BF16), (HEADS_PER_GROUP, 1)), 0.0).astype(BF16)
        y_diag = _dot(mm.astype(BF16), xbd)
        hprev = hout_ref[0, g * HEADS_PER_GROUP:(g + 1) * HEADS_PER_GROUP].reshape(GROUP_W, D_STATE)
        y_off = jnp.exp(ce) * _dot_nt(cg, hprev.astype(BF16))
        w_state = jnp.exp(ce[L - 1:L, :] - ce) * col_dt[:, gs]
        st = _dot_tn((xg * w_state).astype(BF16), bg)
        for r in range(HEADS_PER_GROUP):
            hh = g * HEADS_PER_GROUP + r
            decay = jnp.exp(a_last[:, hh:hh + 1])
            rs = slice(r * SSM_HEAD_DIM, (r + 1) * SSM_HEAD_DIM)
            hout_ref[0, hh] = decay * hprev[rs] + st[rs]
        yt = (y_diag + y_off + dskip_ref[:, gs] * xg) * zs_ref[0, :, gs].astype(F32)
        ms = jnp.mean(yt * yt, axis=-1, keepdims=True)
        y_ref[0, :, gs] = (yt * lax.rsqrt(ms + EPS) * gnorm_ref[:, gs]).astype(y_ref.dtype)


def _ssd(xs, bc, misc, zs, h0, a_log, dskip_full, g_norm, expand):
    b, t, _ = xs.shape
    nc = t // CHUNK
    chunk = lambda w: pl.BlockSpec((1, CHUNK, w), lambda i, c: (i, c, 0))
    const2 = lambda r, w: pl.BlockSpec((r, w), lambda i, c: (0, 0))
    state = pl.BlockSpec((1, SSM_HEADS, SSM_HEAD_DIM, D_STATE), lambda i, c: (i, 0, 0, 0))
    return pl.pallas_call(
        _ssd_kernel,
        grid=(b, nc),
        in_specs=[
            chunk(D_INNER), chunk(2 * SSM_GROUPS * D_STATE), chunk(LANES), chunk(D_INNER), state,
            const2(1, LANES), const2(1, D_INNER), const2(1, D_INNER), const2(3 * LANES, D_INNER),
        ],
        out_specs=[chunk(D_INNER), state],
        out_shape=[jax.ShapeDtypeStruct((b, t, D_INNER), BF16),
                   jax.ShapeDtypeStruct((b, SSM_HEADS, SSM_HEAD_DIM, D_STATE), F32)],
        compiler_params=_cparams("parallel", "arbitrary"),
        name="ssd",
    )(xs, bc, misc, zs, h0, a_log, dskip_full, g_norm, expand)


def _merge_kernel(o_ref, y_ref, g_ref, x_ref, mod_ref, gain_ref, wa_ref, ws_ref, wo_ref, x1_ref, h2_ref):
    bb, tt, d = x_ref.shape
    rows = bb * tt
    o = o_ref[...].reshape(rows, -1)
    y = y_ref[...].reshape(rows, -1)
    gates = g_ref[...].reshape(rows, -1).astype(F32)
    mixed = gates[:, :d] * _dot(o, wa_ref[...]) + gates[:, d:] * _dot(y, ws_ref[...])
    out = _dot(mixed.astype(BF16), wo_ref[...]).reshape(bb, tt, d)
    mod = mod_ref[...]
    x1 = x_ref[...] + mod[:, 2:3, :] * out
    x1_ref[...] = x1
    h2_ref[...] = _modulated_norm(x1, mod, gain_ref[...], 3, 4).astype(h2_ref.dtype)


def _merge(o_attn, y, gates, x, mod3, mod_off, gain, wa, ws, wo):
    b, t, d = x.shape
    bb, tt = _row_blocking(b, t)
    assert mod_off % bb == 0, (mod_off, bb)
    off = mod_off // bb
    tile = lambda w: pl.BlockSpec((bb, tt, w), lambda i, j: (i, j, 0))
    const = lambda r, w: pl.BlockSpec((r, w), lambda i, j: (0, 0))
    return pl.pallas_call(
        _merge_kernel,
        grid=(b // bb, t // tt),
        in_specs=[
            tile(N_HEADS * HEAD_DIM), tile(D_INNER), tile(2 * D_MODEL), tile(d),
            pl.BlockSpec((bb, 6, d), lambda i, j: (i + off, 0, 0)),
            pl.BlockSpec((1, 1, d), lambda i, j: (0, 0, 0)),
            const(N_HEADS * HEAD_DIM, d), const(D_INNER, d), const(d, d),
        ],
        out_specs=[tile(d), tile(d)],
        out_shape=[jax.ShapeDtypeStruct((b, t, d), F32), jax.ShapeDtypeStruct((b, t, d), BF16)],
        compiler_params=_cparams("parallel", "parallel"),
        name="merge",
    )(o_attn, y, gates, x, mod3, gain.reshape(1, 1, d), wa, ws, wo)


def _ffn_kernel(h2_ref, x1_ref, mod_ref, wg_ref, wu_ref, wd_ref, o_ref, acc_ref):
    f = pl.program_id(2)
    bb, tt, d = x1_ref.shape
    h2 = h2_ref[...].reshape(bb * tt, d)
    act = _silu(_dot(h2, wg_ref[...])) * _dot(h2, wu_ref[...])
    part = _dot(act.astype(BF16), wd_ref[...])

    @pl.when(f == 0)
    def _():
        acc_ref[...] = part

    @pl.when(f > 0)
    def _():
        acc_ref[...] += part

    @pl.when(f == pl.num_programs(2) - 1)
    def _():
        o_ref[...] = x1_ref[...] + mod_ref[...][:, 5:6, :] * acc_ref[...].reshape(bb, tt, d)


def _ffn(h2, x1, mod3, mod_off, w_gu, w_down):
    b, t, d = x1.shape
    bb, tt = _row_blocking(b, t)
    assert mod_off % bb == 0, (mod_off, bb)
    off = mod_off // bb
    nf = 2
    tf = D_FF // nf
    tile = pl.BlockSpec((bb, tt, d), lambda i, j, f: (i, j, 0))
    return pl.pallas_call(
        _ffn_kernel,
        grid=(b // bb, t // tt, nf),
        in_specs=[
            tile, tile,
            pl.BlockSpec((bb, 6, d), lambda i, j, f: (i + off, 0, 0)),
            pl.BlockSpec((d, tf), lambda i, j, f: (0, f)),
            pl.BlockSpec((d, tf), lambda i, j, f: (0, nf + f)),
            pl.BlockSpec((tf, d), lambda i, j, f: (f, 0)),
        ],
        out_specs=tile,
        out_shape=jax.ShapeDtypeStruct((b, t, d), F32),
        scratch_shapes=[pltpu.VMEM((bb * tt, d), F32)],
        compiler_params=_cparams("parallel", "parallel", "arbitrary"),
        name="ffn",
    )(h2, x1, mod3, w_gu, w_gu, w_down)


def _rope_tables(t, past, tm):
    half = HEAD_DIM // 2
    inv = ROPE_THETA ** (-jnp.arange(half, dtype=F32) / half)
    ang = (past + jnp.arange(t)).astype(F32)[:, None] * inv[None, :]
    cos, sin = jnp.cos(ang), jnp.sin(ang)
    cos_t = jnp.concatenate([cos, cos, cos, cos], axis=1)
    sin_t = jnp.concatenate([-sin, sin, -sin, sin], axis=1)
    if t < tm:
        cos_t, sin_t = jnp.tile(cos_t, (tm // t, 1)), jnp.tile(sin_t, (tm // t, 1))
    return cos_t, sin_t


def _group_step(x, mod3, mod_off, cache, conv_state, ssm_state, p):
    b, t, d = x.shape
    m = b * t
    tm = min(512, m)
    past = cache[2].shape[1] if cache is not None else 0
    h = _hnorm(x, mod3, mod_off, p["g_norm_mix"]).reshape(m, d)
    cos_t, sin_t = _rope_tables(t, past, tm)
    q, k = _proj_qk(h, p["w_qk"], p["gsum"], p["gexp"], p["qk_gain"], cos_t, sin_t, tm)
    v, qi, misc = _proj_vm(h, p["w_vm"], cos_t, sin_t, p["dtb"], tm)
    zs = _proj_act(h, p["w_z"], "silu", BF16, tm, D_INNER)
    gates = _proj_act(h, p["w_g"], "sigmoid", BF16, tm, 2 * D_MODEL)
    half = CONV_CH // 2
    xs, tail_x = _proj_conv(h, p["w_xs"], conv_state[:, :, :half], p["w_conv"][:, :half],
                            p["b_conv"][:, :half], b, t, F32)
    bc, tail_bc = _proj_conv(h, p["w_bc"], conv_state[:, :, half:], p["w_conv"][:, half:],
                             p["b_conv"][:, half:], b, t, BF16)

    r3 = lambda a: a.reshape(b, t, a.shape[-1])
    by_head = lambda a: a.reshape(b, t * N_KV_HEADS, HEAD_DIM)
    o_attn = _attention(r3(q), r3(qi), r3(misc), by_head(k), by_head(v), cache)
    y, h_last = _ssd(r3(xs), r3(bc), r3(misc), r3(zs), ssm_state, p["a_log"], p["dskip_full"], p["g_ssm_norm"],
                     p["expand"])
    x1, h2 = _merge(o_attn, y, r3(gates), x, mod3, mod_off, p["g_norm_ffn"], p["w_ba"], p["w_bs"], p["w_out"])
    out = _ffn(h2, x1, mod3, mod_off, p["w_gu"], p["w_down"])
    conv_new = jnp.concatenate([tail_x, tail_bc], axis=2)[:, -(CONV_W - 1):]
    return (out, k.reshape(b, t, N_KV_HEADS, HEAD_DIM), v.reshape(b, t, N_KV_HEADS, HEAD_DIM),
            r3(misc)[:, :, :IDX_DIM], conv_new, h_last)


def _layer_params(l, w_in, g_q, g_k, g_norm_mix, g_norm_ffn, w_conv, b_conv, dt_bias, a_log, d_skip, g_ssm_norm,
                  w_branch_attn, w_branch_ssm, w_out, w_gate_up, w_down):
    sizes = (N_HEADS * HEAD_DIM, KV_W, KV_W, QI_W, IDX_DIM, IDX_HEADS, D_INNER, CONV_CH, SSM_HEADS, 2 * D_MODEL)
    offs = [0]
    for s in sizes:
        offs.append(offs[-1] + s)
    col = lambda i: w_in[l][:, offs[i]:offs[i + 1]]
    pad = LANES - (IDX_DIM + IDX_HEADS + SSM_HEADS)
    w_vm = jnp.concatenate([col(2), col(3), col(4), col(5), col(8), jnp.zeros((D_MODEL, pad), F32)], axis=1)
    head_of = jnp.arange(QK_W) // HEAD_DIM
    gsum = (head_of[:, None] == jnp.arange(LANES)[None, :]).astype(BF16)
    dtb = jnp.zeros((1, LANES), F32).at[0, MISC_DT:MISC_DT + SSM_HEADS].set(dt_bias[l])
    expand = ((jnp.arange(LANES) - MISC_DT)[:, None] == (jnp.arange(D_INNER) // SSM_HEAD_DIM)[None, :]).astype(BF16)
    a_log_lanes = jnp.zeros((1, LANES), F32).at[0, MISC_DT:MISC_DT + SSM_HEADS].set(a_log[l])
    return dict(
        w_qk=w_in[l][:, :QK_W].astype(BF16), w_vm=w_vm.astype(BF16), w_z=col(6).astype(BF16),
        w_xs=col(7)[:, :CONV_CH // 2].astype(BF16), w_bc=col(7)[:, CONV_CH // 2:].astype(BF16), w_g=col(9).astype(BF16),
        gsum=jnp.tile(gsum, (2, 1)), gexp=jnp.tile(gsum.T, (2, 1)),
        qk_gain=jnp.concatenate([jnp.tile(g_q[l], N_HEADS), jnp.tile(g_k[l], N_KV_HEADS)]).reshape(1, QK_W),
        dtb=dtb, expand=jnp.tile(expand, (3, 1)),
        g_norm_mix=g_norm_mix[l], g_norm_ffn=g_norm_ffn[l],
        w_conv=w_conv[l], b_conv=b_conv[l].reshape(1, CONV_CH), a_log=a_log_lanes,
        dskip_full=jnp.repeat(d_skip[l], SSM_HEAD_DIM).reshape(1, D_INNER), g_ssm_norm=g_ssm_norm[l].reshape(1, D_INNER),
        w_ba=w_branch_attn[l].astype(BF16), w_bs=w_branch_ssm[l].astype(BF16), w_out=w_out[l].astype(BF16),
        w_gu=w_gate_up[l].astype(BF16), w_down=w_down[l].astype(BF16),
    )


def kernel(x_prompt, x_sample, cache_k, cache_v, cache_ki, state_conv, state_ssm, c_prompt, c_sample, w_ada, b_ada, g_norm_mix, g_norm_ffn, w_in, g_q, g_k, w_conv, b_conv, dt_bias, a_log, d_skip, g_ssm_norm, w_branch_attn, w_branch_ssm, w_out, w_gate_up, w_down):
    depth = w_in.shape[0]
    bp, bs = x_prompt.shape[0], x_sample.shape[0]
    past = cache_k.shape[2]
    y_p, y_s = x_prompt, x_sample
    c_all = jnp.concatenate([c_prompt, c_sample], axis=0)
    new_p = [[] for _ in range(5)]
    new_s = [[] for _ in range(5)]
    for l in range(depth):
        p = _layer_params(l, w_in, g_q, g_k, g_norm_mix, g_norm_ffn, w_conv, b_conv, dt_bias, a_log, d_skip,
                          g_ssm_norm, w_branch_attn, w_branch_ssm, w_out, w_gate_up, w_down)
        mod3 = _ada_mod(c_all, w_ada[l], b_ada[l]).reshape(bp + bs, 6, D_MODEL)
        zero_conv = jnp.zeros((bp, CONV_W - 1, CONV_CH), F32)
        zero_ssm = jnp.zeros((bp, SSM_HEADS, SSM_HEAD_DIM, D_STATE), F32)
        y_p, *st_p = _group_step(y_p, mod3, 0, None, zero_conv, zero_ssm, p)
        by_head = (bs, past * N_KV_HEADS, HEAD_DIM)
        cache = (cache_k[l].reshape(by_head), cache_v[l].reshape(by_head), cache_ki[l])
        y_s, *st_s = _group_step(y_s, mod3, bp, cache, state_conv[l], state_ssm[l], p)
        for acc, a in zip(new_p, st_p):
            acc.append(a)
        for acc, a in zip(new_s, st_s):
            acc.append(a)
    return (y_p, y_s, *[jnp.stack(a) for a in new_p], *[jnp.stack(a) for a in new_s])
```

```python
import functools
import math

import jax
import jax.numpy as jnp
from jax import lax
from jax.experimental import pallas as pl
from jax.experimental.pallas import tpu as pltpu

F32, BF16, I32 = jnp.float32, jnp.bfloat16, jnp.int32

D_MODEL = 1024
CHUNK = 64
N_HEADS = 16
HEAD_DIM = 64
N_KV_HEADS = 4
Q_PER_KV = N_HEADS // N_KV_HEADS
IDX_HEADS = 8
IDX_DIM = 64
TOPK_MAX = 256
ROPE_THETA = 10000.0
D_INNER = 2 * D_MODEL
SSM_HEAD_DIM = 64
SSM_HEADS = D_INNER // SSM_HEAD_DIM
SSM_GROUPS = 8
HEADS_PER_GROUP = SSM_HEADS // SSM_GROUPS
GROUP_W = HEADS_PER_GROUP * SSM_HEAD_DIM
D_STATE = 128
CONV_W = 4
CONV_CH = D_INNER + 2 * SSM_GROUPS * D_STATE
D_FF = -(-8 * D_MODEL // (3 * 256)) * 256
EPS = 1e-6
QK_W = (N_HEADS + N_KV_HEADS) * HEAD_DIM
KV_W = N_KV_HEADS * HEAD_DIM
QI_W = IDX_HEADS * IDX_DIM
SSM_HEADS_N = SSM_HEADS

LANES = 128
MISC_WI = IDX_DIM
MISC_DT = IDX_DIM + IDX_HEADS
WI_SCALE = (IDX_HEADS ** -0.5) * (IDX_DIM ** -0.5)
INT_MIN = -(2 ** 31)
NEG_BIG = -1e30
COUNT_ROWS = 64
VMEM_LIMIT = 56 * 1024 * 1024


def _cparams(*sem):
    return pltpu.CompilerParams(dimension_semantics=sem, vmem_limit_bytes=VMEM_LIMIT)


def _silu(x):
    h = 0.5 * x
    return h + h * jnp.tanh(h)


def _split_bf16(x, n):
    pieces = []
    r = x
    for _ in range(n):
        p = r.astype(BF16)
        pieces.append(p)
        r = r - p.astype(F32)
    return pieces


def _dot(a, b):
    return jnp.dot(a, b, preferred_element_type=F32)


def _dot_nt(a, b):
    return lax.dot_general(a, b, (((1,), (1,)), ((), ())), preferred_element_type=F32)


def _dot_tn(a, b):
    return lax.dot_general(a, b, (((0,), (0,)), ((), ())), preferred_element_type=F32)


def _exact_dot(x, m_stacked, n):
    return _dot(jnp.concatenate(_split_bf16(x, n), axis=1), m_stacked)


def _exact_dot_left(m_tiled, x, n):
    return _dot(m_tiled, jnp.concatenate(_split_bf16(x, n), axis=0))


def _rotate_half(x):
    w = x.shape[-1]
    lane = lax.broadcasted_iota(I32, x.shape, x.ndim - 1)
    first = (lane % HEAD_DIM) < (HEAD_DIM // 2)
    return jnp.where(first, pltpu.roll(x, w - HEAD_DIM // 2, x.ndim - 1), pltpu.roll(x, HEAD_DIM // 2, x.ndim - 1))


def _rope(x, cos, sin):
    reps = x.shape[-1] // LANES
    if reps > 1:
        cos = jnp.tile(cos, (1, reps))
        sin = jnp.tile(sin, (1, reps))
    return x * cos + _rotate_half(x) * sin


def _mod_kernel(c_ref, w_ref, b_ref, o_ref):
    s = _silu(c_ref[...])
    o_ref[...] = _dot(s.astype(BF16), w_ref[...].astype(BF16)) + b_ref[...]


def _ada_mod(c_all, w_ada, b_ada):
    bt = c_all.shape[0]
    n = w_ada.shape[1]
    tn = D_MODEL
    return pl.pallas_call(
        _mod_kernel,
        grid=(n // tn,),
        in_specs=[
            pl.BlockSpec((bt, D_MODEL), lambda j: (0, 0)),
            pl.BlockSpec((D_MODEL, tn), lambda j: (0, j)),
            pl.BlockSpec((1, tn), lambda j: (0, j)),
        ],
        out_specs=pl.BlockSpec((bt, tn), lambda j: (0, j)),
        out_shape=jax.ShapeDtypeStruct((bt, n), F32),
        compiler_params=_cparams("parallel"),
        name="ada_mod",
    )(c_all, w_ada, b_ada.reshape(1, n))


def _modulated_norm(x, mod, gain, shift_idx, scale_idx):
    ms = jnp.mean(x * x, axis=-1, keepdims=True)
    xn = x * lax.rsqrt(ms + EPS)
    sh = mod[:, shift_idx:shift_idx + 1, :]
    sc = mod[:, scale_idx:scale_idx + 1, :]
    return xn * gain * (1.0 + sc) + sh


def _hnorm_kernel(x_ref, mod_ref, g_ref, o_ref):
    o_ref[...] = _modulated_norm(x_ref[...], mod_ref[...], g_ref[...], 0, 1).astype(o_ref.dtype)


def _row_blocking(b, t):
    tt = min(t, 512)
    bb = max(1, min(b, 512 // tt))
    assert t % tt == 0 and b % bb == 0 and tt % CHUNK == 0, (b, t)
    return bb, tt


def _hnorm(x, mod3, mod_off, gain):
    b, t, d = x.shape
    bb, tt = _row_blocking(b, t)
    assert mod_off % bb == 0, (mod_off, bb)
    off = mod_off // bb
    return pl.pallas_call(
        _hnorm_kernel,
        grid=(b // bb, t // tt),
        in_specs=[
            pl.BlockSpec((bb, tt, d), lambda i, j: (i, j, 0)),
            pl.BlockSpec((bb, 6, d), lambda i, j: (i + off, 0, 0)),
            pl.BlockSpec((1, 1, d), lambda i, j: (0, 0, 0)),
        ],
        out_specs=pl.BlockSpec((bb, tt, d), lambda i, j: (i, j, 0)),
        out_shape=jax.ShapeDtypeStruct((b, t, d), BF16),
        compiler_params=_cparams("parallel", "parallel"),
        name="hnorm",
    )(x, mod3, gain.reshape(1, 1, d))


def _proj_act_kernel(h_ref, w_ref, o_ref, *, act):
    acc = _dot(h_ref[...], w_ref[...])
    if act == "silu":
        acc = _silu(acc)
    elif act == "sigmoid":
        acc = jax.nn.sigmoid(acc)
    o_ref[...] = acc.astype(o_ref.dtype)


def _proj_act(h2d, w, act, out_dtype, tm, tn):
    m, k = h2d.shape
    n = w.shape[1]
    return pl.pallas_call(
        functools.partial(_proj_act_kernel, act=act),
        grid=(n // tn, m // tm),
        in_specs=[
            pl.BlockSpec((tm, k), lambda j, i: (i, 0)),
            pl.BlockSpec((k, tn), lambda j, i: (0, j)),
        ],
        out_specs=pl.BlockSpec((tm, tn), lambda j, i: (i, j)),
        out_shape=jax.ShapeDtypeStruct((m, n), out_dtype),
        compiler_params=_cparams("parallel", "parallel"),
        name="proj_" + act,
    )(h2d, w)


SUBLANES = 8
CONV_SLAB = 512


def _proj_conv_kernel(h_ref, w_ref, cst_ref, wconv_ref, bconv_ref, o_ref, tail_ref, buf, *, bb, tt, tiles_per_seq):
    i = pl.program_id(1)
    tail = CONV_W - 1
    tn = w_ref.shape[1]

    @pl.when(i % tiles_per_seq == 0)
    def _():
        for s in range(bb):
            buf[s, 0:SUBLANES - tail, :] = jnp.zeros((SUBLANES - tail, tn), F32)
            buf[s, SUBLANES - tail:SUBLANES, :] = cst_ref[s]

    h = h_ref[...]
    for c in range(tn // CONV_SLAB):
        cs = slice(c * CONV_SLAB, (c + 1) * CONV_SLAB)
        acc = _dot(h, w_ref[:, cs])
        for s in range(bb):
            buf[s, SUBLANES:SUBLANES + tt, cs] = acc[s * tt:(s + 1) * tt]
            xb = buf[s, :, cs]
            xc = bconv_ref[:, cs] + wconv_ref[tail:tail + 1, cs] * xb[SUBLANES:]
            for jj in range(tail):
                shifted = pltpu.roll(xb, tail - jj, 0)[SUBLANES:]
                xc = xc + wconv_ref[jj:jj + 1, cs] * shifted
            o_ref[s * tt:(s + 1) * tt, cs] = _silu(xc).astype(o_ref.dtype)
            last = buf[s, tt:tt + SUBLANES, cs]
            tail_ref[s, :, cs] = last
            buf[s, 0:SUBLANES, cs] = last


def _proj_conv(h2d, w, conv_state, w_conv, b_conv, b, t, out_dtype):
    m, k = h2d.shape
    n = w.shape[1]
    bb, tt = _row_blocking(b, t)
    tm = bb * tt
    tiles_per_seq = t // tt
    tn = min(n, 2048)
    kern = functools.partial(_proj_conv_kernel, bb=bb, tt=tt, tiles_per_seq=tiles_per_seq)
    seq_block = lambda rows: pl.BlockSpec((bb, rows, tn), lambda j, i: (i // tiles_per_seq, 0, j))
    return pl.pallas_call(
        kern,
        grid=(n // tn, m // tm),
        in_specs=[
            pl.BlockSpec((tm, k), lambda j, i: (i, 0)),
            pl.BlockSpec((k, tn), lambda j, i: (0, j)),
            seq_block(CONV_W - 1),
            pl.BlockSpec((CONV_W, tn), lambda j, i: (0, j)),
            pl.BlockSpec((1, tn), lambda j, i: (0, j)),
        ],
        out_specs=[pl.BlockSpec((tm, tn), lambda j, i: (i, j)), seq_block(SUBLANES)],
        out_shape=[jax.ShapeDtypeStruct((m, n), out_dtype), jax.ShapeDtypeStruct((b, SUBLANES, n), F32)],
        scratch_shapes=[pltpu.VMEM((bb, SUBLANES + tt, tn), F32)],
        compiler_params=_cparams("parallel", "arbitrary"),
        name="proj_conv",
    )(h2d, w, conv_state, w_conv, b_conv)


def _store_time_minor(ref, x):
    bb, tt = ref.shape[0], ref.shape[-1]
    for s in range(bb):
        ref[s] = x[s * tt:(s + 1) * tt, :].T.reshape(ref.shape[1:])


def _qk_kernel(h_ref, w_ref, gsum_ref, gexp_ref, gain_ref, cos_ref, sin_ref, q_ref, k_ref):
    acc = _dot(h_ref[...], w_ref[...])
    ss = _exact_dot(acc * acc, gsum_ref[...], 2)
    rs = lax.rsqrt(ss * (1.0 / HEAD_DIM) + EPS)
    rs_full = _exact_dot(rs, gexp_ref[...], 2)
    xn = acc * rs_full * gain_ref[...]
    out = _rope(xn, cos_ref[...], sin_ref[...])
    nq = N_HEADS * HEAD_DIM
    q_ref[...] = (out[:, :nq] * (HEAD_DIM ** -0.5)).astype(q_ref.dtype)
    _store_time_minor(k_ref, out[:, nq:])


def _vm_kernel(h_ref, w_ref, cos_ref, sin_ref, dtb_ref, v_ref, qi_ref, misc_ref, kit_ref):
    acc = _dot(h_ref[...], w_ref[...])
    cos, sin = cos_ref[...], sin_ref[...]
    _store_time_minor(v_ref, acc[:, :KV_W])
    qi_ref[...] = _rope(acc[:, KV_W:KV_W + QI_W], cos, sin).astype(qi_ref.dtype)
    m = acc[:, KV_W + QI_W:]
    lane = lax.broadcasted_iota(I32, m.shape, 1)
    roped = _rope(m, cos, sin)
    dt = jax.nn.softplus(m + dtb_ref[...])
    misc_ref[...] = jnp.where(lane < MISC_WI, roped,
                              jnp.where(lane < MISC_DT, m * WI_SCALE,
                                        jnp.where(lane < MISC_DT + SSM_HEADS, dt, 0.0)))
    _store_time_minor(kit_ref, roped[:, :IDX_DIM])


def _table_spec(tab_rows, tm):
    nblk = tab_rows // tm
    return pl.BlockSpec((tm, LANES), lambda i: (i % nblk, 0))


def _time_minor_spec(b, t, *mid):
    bb, tt = _row_blocking(b, t)
    per_seq = t // tt
    zeros = (0,) * len(mid)
    return pl.BlockSpec((bb, *mid, tt), lambda i: (i // per_seq, *zeros, i % per_seq))


def _proj_qk(h2d, w_qk, gsum, gexp, gain, cos_tab, sin_tab, b, t):
    m, k = h2d.shape
    tm = math.prod(_row_blocking(b, t))
    nq = N_HEADS * HEAD_DIM
    const = lambda i: (0, 0)
    return pl.pallas_call(
        _qk_kernel,
        grid=(m // tm,),
        in_specs=[
            pl.BlockSpec((tm, k), lambda i: (i, 0)),
            pl.BlockSpec((k, QK_W), const),
            pl.BlockSpec((2 * QK_W, LANES), const),
            pl.BlockSpec((2 * LANES, QK_W), const),
            pl.BlockSpec((1, QK_W), const),
            _table_spec(cos_tab.shape[0], tm),
            _table_spec(sin_tab.shape[0], tm),
        ],
        out_specs=[pl.BlockSpec((tm, nq), lambda i: (i, 0)), _time_minor_spec(b, t, N_KV_HEADS, HEAD_DIM)],
        out_shape=[jax.ShapeDtypeStruct((m, nq), BF16), jax.ShapeDtypeStruct((b, N_KV_HEADS, HEAD_DIM, t), F32)],
        compiler_params=_cparams("parallel"),
        name="proj_qk",
    )(h2d, w_qk, gsum, gexp, gain, cos_tab, sin_tab)


def _proj_vm(h2d, w_vm, cos_tab, sin_tab, dtb, b, t):
    m, k = h2d.shape
    tm = math.prod(_row_blocking(b, t))
    wn = w_vm.shape[1]
    const = lambda i: (0, 0)
    return pl.pallas_call(
        _vm_kernel,
        grid=(m // tm,),
        in_specs=[
            pl.BlockSpec((tm, k), lambda i: (i, 0)),
            pl.BlockSpec((k, wn), const),
            _table_spec(cos_tab.shape[0], tm),
            _table_spec(sin_tab.shape[0], tm),
            pl.BlockSpec((1, LANES), const),
        ],
        out_specs=[_time_minor_spec(b, t, N_KV_HEADS, HEAD_DIM), pl.BlockSpec((tm, QI_W), lambda i: (i, 0)),
                   pl.BlockSpec((tm, LANES), lambda i: (i, 0)), _time_minor_spec(b, t, IDX_DIM)],
        out_shape=[jax.ShapeDtypeStruct((b, N_KV_HEADS, HEAD_DIM, t), F32), jax.ShapeDtypeStruct((m, QI_W), BF16),
                   jax.ShapeDtypeStruct((m, LANES), F32), jax.ShapeDtypeStruct((b, IDX_DIM, t), F32)],
        compiler_params=_cparams("parallel"),
        name="proj_vm",
    )(h2d, w_vm, cos_tab, sin_tab, dtb)


def _attn_kernel(*refs, tq, kb, topk, past, t_new, has_cache):
    n_in = 9 if has_cache else 6
    q_ref, qi_ref, wi_ref, k_ref, v_ref, kit_ref = refs[:6]
    o_ref = refs[n_in]
    kbf, vbf, kibf, sc_scr, keep_scr, qis, qs, mb_scr, acc_scr, s_scr = refs[n_in + 1:n_in + 11]
    if has_cache:
        ck_ref, cv_ref, cki_ref = refs[6:9]
        ckb, cvb, ckib = refs[n_in + 11:]
    j = pl.program_id(1)
    kbn = min(kb, t_new)
    n_cache_blocks = past // kb

    def ones_row(n):
        return jnp.where(lax.broadcasted_iota(I32, (HEAD_DIM, n), 0) == 0, 1.0, 0.0).astype(BF16)

    def stage(k_src, v_src, ki_src, k_dst, v_dst, ki_dst, n):
        for g in range(N_KV_HEADS):
            k_dst[g] = k_src[0, g].astype(BF16)
            v_dst[g, 0:HEAD_DIM, :] = v_src[0, g].astype(BF16)
            v_dst[g, HEAD_DIM:2 * HEAD_DIM, :] = ones_row(n)
        ki_dst[...] = ki_src[0].T.astype(BF16)

    @pl.when(j == 0)
    def _():
        stage(k_ref, v_ref, kit_ref, kbf, vbf, kibf, t_new)
        if has_cache:
            stage(ck_ref, cv_ref, cki_ref, ckb, cvb, ckib, past)

    for h in range(IDX_HEADS):
        qis[h * tq:(h + 1) * tq, :] = qi_ref[0, :, h * IDX_DIM:(h + 1) * IDX_DIM]
    for h in range(N_HEADS):
        qs[h * tq:(h + 1) * tq, :] = q_ref[0, :, h * HEAD_DIM:(h + 1) * HEAD_DIM]
    wi_t = wi_ref[0, :, MISC_WI:MISC_WI + IDX_HEADS].T

    qpos = past + j * tq + lax.broadcasted_iota(I32, (1, tq), 1)
    limit = (qpos // CHUNK + 1) * CHUNK
    n_new_blocks = (j * tq + tq + kbn - 1) // kbn

    def over_cache(fn, init):
        if not has_cache:
            return init
        return lax.fori_loop(0, n_cache_blocks, lambda i, c: fn(pl.multiple_of(i * kb, kb), c), init)

    def over_new(fn, init):
        if t_new <= kb:
            return fn(0, init)
        return lax.fori_loop(0, n_new_blocks, lambda i, c: fn(pl.multiple_of(i * kbn, kbn), c), init)

    def lanes_at(off):
        return past + off if isinstance(off, int) else pl.multiple_of(past + off, LANES)

    def score_block(ki_blk, kpos0, width):
        lg = _dot_nt(ki_blk, qis[...])
        sc = jnp.zeros((width, tq), F32)
        for h in range(IDX_HEADS):
            sc = sc + wi_t[h:h + 1, :] * jnp.maximum(lg[:, h * tq:(h + 1) * tq], 0.0)
        kpos = kpos0 + lax.broadcasted_iota(I32, (width, 1), 0)
        sc_scr[pl.ds(kpos0, width), :] = jnp.where(kpos < limit, sc, -jnp.inf)

    def p1c(off, c):
        score_block(ckib[pl.ds(off, kb), :], off, kb)
        return c

    def p1n(off, c):
        score_block(kibf[pl.ds(off, kbn), :], lanes_at(off), kbn)
        return c

    over_cache(p1c, 0)
    over_new(p1n, 0)

    def key_to_float(c):
        return pltpu.bitcast(jnp.where(c >= 0, c, c ^ 0x7FFFFFFF), F32)

    def count(cmp, cand):
        def cnt(row_off, width, acc):
            hit = jnp.where(cmp(sc_scr[pl.ds(row_off, width), :], cand), 1.0, 0.0)
            for r in range(width // COUNT_ROWS):
                acc = acc + hit[r * COUNT_ROWS:(r + 1) * COUNT_ROWS]
            return acc
        acc = jnp.zeros((COUNT_ROWS, tq), F32)
        acc = over_cache(lambda off, a: cnt(off, kb, a), acc)
        acc = over_new(lambda off, a: cnt(lanes_at(off), kbn, a), acc)
        return jnp.sum(acc, axis=0, keepdims=True)

    def bit_step(it, prefix):
        bit = jnp.left_shift(jnp.int32(1), 31 - it)
        cand = key_to_float((prefix | bit) ^ INT_MIN)
        return jnp.where(count(jnp.greater_equal, cand) >= float(topk), prefix | bit, prefix)

    prefix = lax.fori_loop(0, 32, bit_step, jnp.zeros((1, tq), I32))
    thr = key_to_float(prefix ^ INT_MIN)
    flt_max = float(jnp.finfo(F32).max)
    thr = jnp.where(thr >= -flt_max, thr, -flt_max)

    n_ge = count(jnp.greater_equal, thr)
    has_ties = jnp.max(n_ge) > float(topk)

    def keep_ranked(row_off, width, need, seen):
        sc = sc_scr[pl.ds(row_off, width), :]
        tie = sc == thr
        tie_f = jnp.where(tie, 1.0, 0.0)
        below = lax.broadcasted_iota(I32, (width, width), 1) <= lax.broadcasted_iota(I32, (width, width), 0)
        rank = seen + _dot(jnp.where(below, 1.0, 0.0).astype(BF16), tie_f.astype(BF16))
        keep = (sc > thr) | (tie & (rank <= need))
        keep_scr[pl.ds(row_off, width), :] = jnp.where(keep, 1.0, 0.0).astype(BF16)
        return seen + jnp.sum(tie_f, axis=0, keepdims=True)

    def keep_all_ties(row_off, width, c):
        sc = sc_scr[pl.ds(row_off, width), :]
        keep_scr[pl.ds(row_off, width), :] = jnp.where(sc >= thr, 1.0, 0.0).astype(BF16)
        return c

    def with_ties():
        need = float(topk) - count(jnp.greater, thr)
        seen = over_cache(lambda off, s: keep_ranked(off, kb, need, s), jnp.zeros((1, tq), F32))
        over_new(lambda off, s: keep_ranked(lanes_at(off), kbn, need, s), seen)

    def without_ties():
        over_cache(lambda off, c: keep_all_ties(off, kb, c), 0)
        over_new(lambda off, c: keep_all_ties(lanes_at(off), kbn, c), 0)

    lax.cond(has_ties, with_ties, without_ties)

    rows = Q_PER_KV * tq

    eye = jnp.where(lax.broadcasted_iota(I32, (tq, tq), 0) == lax.broadcasted_iota(I32, (tq, tq), 1), 1.0, 0.0).astype(BF16)

    def pass_a(k_of, lane_off, width):
        keep_q = _dot_nt(eye, keep_scr[pl.ds(lane_off, width), :])
        b = jnp.tile(jnp.where(keep_q > 0.5, 0.0, NEG_BIG), (Q_PER_KV, 1))
        for g in range(N_KV_HEADS):
            s = _dot(qs[g * rows:(g + 1) * rows, :], k_of(g)) + b
            s_scr[g, :, pl.ds(lane_off, width)] = s
            m = mb_scr[g]
            if width % LANES:
                m = jnp.maximum(m, jnp.max(s, axis=1, keepdims=True))
            else:
                for c in range(width // LANES):
                    m = jnp.maximum(m, s[:, c * LANES:(c + 1) * LANES])
            mb_scr[g] = m

    def pass_b(v_of, lane_off, width):
        for g in range(N_KV_HEADS):
            m = mb_scr[g]
            m = m[:, :width] if width < LANES else jnp.tile(m, (1, width // LANES))
            p = jnp.exp(s_scr[g, :, pl.ds(lane_off, width)] - m)
            acc_scr[g] += _dot_nt(p.astype(BF16), v_of(g))

    def cache_k(off):
        return lambda g: ckb[g, :, pl.ds(off, kb)]

    def cache_v(off):
        return lambda g: cvb[g, :, pl.ds(off, kb)]

    def new_k(off):
        return lambda g: kbf[g, :, pl.ds(off, kbn)]

    def new_v(off):
        return lambda g: vbf[g, :, pl.ds(off, kbn)]

    def run(fn, cache_args, new_args):
        def on_cache(off, c):
            fn(*[a(off) for a in cache_args], off, kb)
            return c

        def on_new(off, c):
            fn(*[a(off) for a in new_args], lanes_at(off), kbn)
            return c

        over_cache(on_cache, 0)
        over_new(on_new, 0)

    mb_scr[...] = jnp.full(mb_scr.shape, NEG_BIG, F32)
    run(pass_a, [cache_k], [new_k])
    for g in range(N_KV_HEADS):
        mb_scr[g] = jnp.broadcast_to(jnp.max(mb_scr[g], axis=1, keepdims=True), (rows, LANES))
    acc_scr[...] = jnp.zeros(acc_scr.shape, F32)
    run(pass_b, [cache_v], [new_v])

    for g in range(N_KV_HEADS):
        a = acc_scr[g]
        out = a[:, :HEAD_DIM] / a[:, HEAD_DIM:HEAD_DIM + 1]
        for r in range(Q_PER_KV):
            hh = g * Q_PER_KV + r
            o_ref[0, :, hh * HEAD_DIM:(hh + 1) * HEAD_DIM] = out[r * tq:(r + 1) * tq].astype(o_ref.dtype)


def _attention(q, qi, misc, kt, vt, kit, cache=None):
    b, t, _ = q.shape
    has_cache = cache is not None
    past = cache[2].shape[2] if has_cache else 0
    n_keys = past + t
    topk = min(TOPK_MAX, n_keys // 4)
    tq = min(t, 128)
    kb = 256
    key_w = past + -(-t // LANES) * LANES
    rows = Q_PER_KV * tq
    qtile = lambda w: pl.BlockSpec((1, tq, w), lambda i, j: (i, j, 0))
    heads = lambda n: pl.BlockSpec((1, N_KV_HEADS, HEAD_DIM, n), lambda i, j: (i, 0, 0, 0))
    idx = lambda n: pl.BlockSpec((1, IDX_DIM, n), lambda i, j: (i, 0, 0))
    in_specs = [qtile(N_HEADS * HEAD_DIM), qtile(QI_W), qtile(LANES), heads(t), heads(t), idx(t)]
    args = [q, qi, misc, kt, vt, kit]
    if has_cache:
        in_specs += [heads(past), heads(past), idx(past)]
        args += list(cache)
    kern = functools.partial(_attn_kernel, tq=tq, kb=kb, topk=topk, past=past, t_new=t, has_cache=has_cache)
    return pl.pallas_call(
        kern,
        grid=(b, t // tq),
        in_specs=in_specs,
        out_specs=qtile(N_HEADS * HEAD_DIM),
        out_shape=jax.ShapeDtypeStruct((b, t, N_HEADS * HEAD_DIM), BF16),
        scratch_shapes=[
            pltpu.VMEM((N_KV_HEADS, HEAD_DIM, t), BF16),
            pltpu.VMEM((N_KV_HEADS, 2 * HEAD_DIM, t), BF16),
            pltpu.VMEM((t, IDX_DIM), BF16),
            pltpu.VMEM((n_keys, tq), F32),
            pltpu.VMEM((n_keys, tq), BF16),
            pltpu.VMEM((IDX_HEADS * tq, IDX_DIM), BF16),
            pltpu.VMEM((N_HEADS * tq, HEAD_DIM), BF16),
            pltpu.VMEM((N_KV_HEADS, rows, LANES), F32),
            pltpu.VMEM((N_KV_HEADS, rows, 2 * HEAD_DIM), F32),
            pltpu.VMEM((N_KV_HEADS, rows, key_w), F32),
        ] + ([
            pltpu.VMEM((N_KV_HEADS, HEAD_DIM, past), BF16),
            pltpu.VMEM((N_KV_HEADS, 2 * HEAD_DIM, past), BF16),
            pltpu.VMEM((past, IDX_DIM), BF16),
        ] if has_cache else []),
        compiler_params=_cparams("parallel", "arbitrary"),
        name="attn_cache" if has_cache else "attn_prompt",
    )(*args)


def _ssd_kernel(xs_ref, bc_ref, misc_ref, zs_ref, h0_ref, alog_ref, dskip_ref, gnorm_ref, expand_ref, y_ref, hout_ref):
    c = pl.program_id(1)
    L = CHUNK

    @pl.when(c == 0)
    def _():
        hout_ref[...] = h0_ref[...]

    dt = misc_ref[0]
    a = -jnp.exp(alog_ref[...])
    ri = lax.broadcasted_iota(I32, (L, 3 * L), 0)
    ci = lax.broadcasted_iota(I32, (L, 3 * L), 1) % L
    tri3 = jnp.where(ri >= ci, 1.0, 0.0).astype(BF16)
    ones3 = jnp.ones((L, 3 * L), BF16)
    acs = _exact_dot_left(tri3, dt * a, 3)
    expand3 = expand_ref[...]
    col_acs = _exact_dot(acs, expand3, 3)
    col_dt = _exact_dot(dt, expand3[:2 * LANES], 2)
    row_i = lax.broadcasted_iota(I32, (L, D_INNER), 0)
    lane_j = lax.broadcasted_iota(I32, (L, D_INNER), 1) % SSM_HEAD_DIM
    eye = row_i == lane_j
    causal = row_i >= lane_j
    row_acs = _exact_dot_left(ones3, jnp.where(eye, col_acs, 0.0), 3)
    row_dt = _exact_dot_left(ones3[:, :2 * L], jnp.where(eye, col_dt, 0.0), 2)
    a_last = acs[L - 1:L, MISC_DT:MISC_DT + SSM_HEADS]

    blk_r = lax.broadcasted_iota(I32, (GROUP_W, GROUP_W), 0) // SSM_HEAD_DIM
    blk_c = lax.broadcasted_iota(I32, (GROUP_W, GROUP_W), 1) // SSM_HEAD_DIM
    same_head = blk_r == blk_c

    for g in range(SSM_GROUPS):
        gs = slice(g * GROUP_W, (g + 1) * GROUP_W)
        ns = slice(g * D_STATE, (g + 1) * D_STATE)
        ce, re = col_acs[:, gs], row_acs[:, gs]
        bg = bc_ref[0, :, ns]
        cg = bc_ref[0, :, SSM_GROUPS * D_STATE + g * D_STATE:SSM_GROUPS * D_STATE + (g + 1) * D_STATE]
        xg = xs_ref[0, :, gs]
        cb = _dot_nt(cg, jnp.tile(bg, (HEADS_PER_GROUP, 1)))
        mm = cb * jnp.exp(jnp.where(causal[:, gs], ce - re, -jnp.inf)) * row_dt[:, gs]
        xbd = jnp.where(same_head, jnp.tile(xg.astype(BF16), (HEADS_PER_GROUP, 1)), 0.0).astype(BF16)
        y_diag = _dot(mm.astype(BF16), xbd)
        hprev = hout_ref[0, g * HEADS_PER_GROUP:(g + 1) * HEADS_PER_GROUP].reshape(GROUP_W, D_STATE)
        y_off = jnp.exp(ce) * _dot_nt(cg, hprev.astype(BF16))
        w_state = jnp.exp(ce[L - 1:L, :] - ce) * col_dt[:, gs]
        st = _dot_tn((xg * w_state).astype(BF16), bg)
        for r in range(HEADS_PER_GROUP):
            hh = g * HEADS_PER_GROUP + r
            decay = jnp.exp(a_last[:, hh:hh + 1])
            rs = slice(r * SSM_HEAD_DIM, (r + 1) * SSM_HEAD_DIM)
            hout_ref[0, hh] = decay * hprev[rs] + st[rs]
        yt = (y_diag + y_off + dskip_ref[:, gs] * xg) * zs_ref[0, :, gs].astype(F32)
        ms = jnp.mean(yt * yt, axis=-1, keepdims=True)
        y_ref[0, :, gs] = (yt * lax.rsqrt(ms + EPS) * gnorm_ref[:, gs]).astype(y_ref.dtype)


def _ssd(xs, bc, misc, zs, h0, a_log, dskip_full, g_norm, expand):
    b, t, _ = xs.shape
    nc = t // CHUNK
    chunk = lambda w: pl.BlockSpec((1, CHUNK, w), lambda i, c: (i, c, 0))
    const2 = lambda r, w: pl.BlockSpec((r, w), lambda i, c: (0, 0))
    state = pl.BlockSpec((1, SSM_HEADS, SSM_HEAD_DIM, D_STATE), lambda i, c: (i, 0, 0, 0))
    return pl.pallas_call(
        _ssd_kernel,
        grid=(b, nc),
        in_specs=[
            chunk(D_INNER), chunk(2 * SSM_GROUPS * D_STATE), chunk(LANES), chunk(D_INNER), state,
            const2(1, LANES), const2(1, D_INNER), const2(1, D_INNER), const2(3 * LANES, D_INNER),
        ],
        out_specs=[chunk(D_INNER), state],
        out_shape=[jax.ShapeDtypeStruct((b, t, D_INNER), BF16),
                   jax.ShapeDtypeStruct((b, SSM_HEADS, SSM_HEAD_DIM, D_STATE), F32)],
        compiler_params=_cparams("parallel", "arbitrary"),
        name="ssd",
    )(xs, bc, misc, zs, h0, a_log, dskip_full, g_norm, expand)


def _merge_kernel(o_ref, y_ref, g_ref, x_ref, mod_ref, gain_ref, wa_ref, ws_ref, wo_ref, x1_ref, h2_ref):
    bb, tt, d = x_ref.shape
    rows = bb * tt
    o = o_ref[...].reshape(rows, -1)
    y = y_ref[...].reshape(rows, -1)
    gates = g_ref[...].reshape(rows, -1).astype(F32)
    mixed = gates[:, :d] * _dot(o, wa_ref[...]) + gates[:, d:] * _dot(y, ws_ref[...])
    out = _dot(mixed.astype(BF16), wo_ref[...]).reshape(bb, tt, d)
    mod = mod_ref[...]
    x1 = x_ref[...] + mod[:, 2:3, :] * out
    x1_ref[...] = x1
    h2_ref[...] = _modulated_norm(x1, mod, gain_ref[...], 3, 4).astype(h2_ref.dtype)


def _merge(o_attn, y, gates, x, mod3, mod_off, gain, wa, ws, wo):
    b, t, d = x.shape
    bb, tt = _row_blocking(b, t)
    assert mod_off % bb == 0, (mod_off, bb)
    off = mod_off // bb
    tile = lambda w: pl.BlockSpec((bb, tt, w), lambda i, j: (i, j, 0))
    const = lambda r, w: pl.BlockSpec((r, w), lambda i, j: (0, 0))
    return pl.pallas_call(
        _merge_kernel,
        grid=(b // bb, t // tt),
        in_specs=[
            tile(N_HEADS * HEAD_DIM), tile(D_INNER), tile(2 * D_MODEL), tile(d),
            pl.BlockSpec((bb, 6, d), lambda i, j: (i + off, 0, 0)),
            pl.BlockSpec((1, 1, d), lambda i, j: (0, 0, 0)),
            const(N_HEADS * HEAD_DIM, d), const(D_INNER, d), const(d, d),
        ],
        out_specs=[tile(d), tile(d)],
        out_shape=[jax.ShapeDtypeStruct((b, t, d), F32), jax.ShapeDtypeStruct((b, t, d), BF16)],
        compiler_params=_cparams("parallel", "parallel"),
        name="merge",
    )(o_attn, y, gates, x, mod3, gain.reshape(1, 1, d), wa, ws, wo)


def _ffn_kernel(h2_ref, x1_ref, mod_ref, wg_ref, wu_ref, wd_ref, o_ref, acc_ref):
    f = pl.program_id(2)
    bb, tt, d = x1_ref.shape
    h2 = h2_ref[...].reshape(bb * tt, d)
    act = _silu(_dot(h2, wg_ref[...])) * _dot(h2, wu_ref[...])
    part = _dot(act.astype(BF16), wd_ref[...])

    @pl.when(f == 0)
    def _():
        acc_ref[...] = part

    @pl.when(f > 0)
    def _():
        acc_ref[...] += part

    @pl.when(f == pl.num_programs(2) - 1)
    def _():
        o_ref[...] = x1_ref[...] + mod_ref[...][:, 5:6, :] * acc_ref[...].reshape(bb, tt, d)


def _ffn(h2, x1, mod3, mod_off, w_gu, w_down):
    b, t, d = x1.shape
    bb, tt = _row_blocking(b, t)
    assert mod_off % bb == 0, (mod_off, bb)
    off = mod_off // bb
    nf = 2
    tf = D_FF // nf
    tile = pl.BlockSpec((bb, tt, d), lambda i, j, f: (i, j, 0))
    return pl.pallas_call(
        _ffn_kernel,
        grid=(b // bb, t // tt, nf),
        in_specs=[
            tile, tile,
            pl.BlockSpec((bb, 6, d), lambda i, j, f: (i + off, 0, 0)),
            pl.BlockSpec((d, tf), lambda i, j, f: (0, f)),
            pl.BlockSpec((d, tf), lambda i, j, f: (0, nf + f)),
            pl.BlockSpec((tf, d), lambda i, j, f: (f, 0)),
        ],
        out_specs=tile,
        out_shape=jax.ShapeDtypeStruct((b, t, d), F32),
        scratch_shapes=[pltpu.VMEM((bb * tt, d), F32)],
        compiler_params=_cparams("parallel", "parallel", "arbitrary"),
        name="ffn",
    )(h2, x1, mod3, w_gu, w_gu, w_down)


def _rope_tables(t, past, tm):
    half = HEAD_DIM // 2
    inv = ROPE_THETA ** (-jnp.arange(half, dtype=F32) / half)
    ang = (past + jnp.arange(t)).astype(F32)[:, None] * inv[None, :]
    cos, sin = jnp.cos(ang), jnp.sin(ang)
    cos_t = jnp.concatenate([cos, cos, cos, cos], axis=1)
    sin_t = jnp.concatenate([-sin, sin, -sin, sin], axis=1)
    if t < tm:
        cos_t, sin_t = jnp.tile(cos_t, (tm // t, 1)), jnp.tile(sin_t, (tm // t, 1))
    return cos_t, sin_t


def _group_step(x, mod3, mod_off, cache, conv_state, ssm_state, p):
    b, t, d = x.shape
    m = b * t
    tm = min(512, m)
    past = cache[2].shape[2] if cache is not None else 0
    h = _hnorm(x, mod3, mod_off, p["g_norm_mix"]).reshape(m, d)
    cos_t, sin_t = _rope_tables(t, past, tm)
    q, kt = _proj_qk(h, p["w_qk"], p["gsum"], p["gexp"], p["qk_gain"], cos_t, sin_t, b, t)
    vt, qi, misc, kit = _proj_vm(h, p["w_vm"], cos_t, sin_t, p["dtb"], b, t)
    zs = _proj_act(h, p["w_z"], "silu", BF16, tm, D_INNER)
    gates = _proj_act(h, p["w_g"], "sigmoid", BF16, tm, 2 * D_MODEL)
    half = CONV_CH // 2
    xs, tail_x = _proj_conv(h, p["w_xs"], conv_state[:, :, :half], p["w_conv"][:, :half],
                            p["b_conv"][:, :half], b, t, F32)
    bc, tail_bc = _proj_conv(h, p["w_bc"], conv_state[:, :, half:], p["w_conv"][:, half:],
                             p["b_conv"][:, half:], b, t, BF16)

    r3 = lambda a: a.reshape(b, t, a.shape[-1])
    o_attn = _attention(r3(q), r3(qi), r3(misc), kt, vt, kit, cache)
    y, h_last = _ssd(r3(xs), r3(bc), r3(misc), r3(zs), ssm_state, p["a_log"], p["dskip_full"], p["g_ssm_norm"],
                     p["expand"])
    x1, h2 = _merge(o_attn, y, r3(gates), x, mod3, mod_off, p["g_norm_ffn"], p["w_ba"], p["w_bs"], p["w_out"])
    out = _ffn(h2, x1, mod3, mod_off, p["w_gu"], p["w_down"])
    conv_new = jnp.concatenate([tail_x, tail_bc], axis=2)[:, -(CONV_W - 1):]
    token_major = lambda a: jnp.moveaxis(a, -1, 1)
    return out, token_major(kt), token_major(vt), token_major(kit), conv_new, h_last


def _layer_params(l, w_in, g_q, g_k, g_norm_mix, g_norm_ffn, w_conv, b_conv, dt_bias, a_log, d_skip, g_ssm_norm,
                  w_branch_attn, w_branch_ssm, w_out, w_gate_up, w_down):
    sizes = (N_HEADS * HEAD_DIM, KV_W, KV_W, QI_W, IDX_DIM, IDX_HEADS, D_INNER, CONV_CH, SSM_HEADS, 2 * D_MODEL)
    offs = [0]
    for s in sizes:
        offs.append(offs[-1] + s)
    col = lambda i: w_in[l][:, offs[i]:offs[i + 1]]
    pad = LANES - (IDX_DIM + IDX_HEADS + SSM_HEADS)
    w_vm = jnp.concatenate([col(2), col(3), col(4), col(5), col(8), jnp.zeros((D_MODEL, pad), F32)], axis=1)
    head_of = jnp.arange(QK_W) // HEAD_DIM
    gsum = (head_of[:, None] == jnp.arange(LANES)[None, :]).astype(BF16)
    dtb = jnp.zeros((1, LANES), F32).at[0, MISC_DT:MISC_DT + SSM_HEADS].set(dt_bias[l])
    expand = ((jnp.arange(LANES) - MISC_DT)[:, None] == (jnp.arange(D_INNER) // SSM_HEAD_DIM)[None, :]).astype(BF16)
    a_log_lanes = jnp.zeros((1, LANES), F32).at[0, MISC_DT:MISC_DT + SSM_HEADS].set(a_log[l])
    return dict(
        w_qk=w_in[l][:, :QK_W].astype(BF16), w_vm=w_vm.astype(BF16), w_z=col(6).astype(BF16),
        w_xs=col(7)[:, :CONV_CH // 2].astype(BF16), w_bc=col(7)[:, CONV_CH // 2:].astype(BF16), w_g=col(9).astype(BF16),
        gsum=jnp.tile(gsum, (2, 1)), gexp=jnp.tile(gsum.T, (2, 1)),
        qk_gain=jnp.concatenate([jnp.tile(g_q[l], N_HEADS), jnp.tile(g_k[l], N_KV_HEADS)]).reshape(1, QK_W),
        dtb=dtb, expand=jnp.tile(expand, (3, 1)),
        g_norm_mix=g_norm_mix[l], g_norm_ffn=g_norm_ffn[l],
        w_conv=w_conv[l], b_conv=b_conv[l].reshape(1, CONV_CH), a_log=a_log_lanes,
        dskip_full=jnp.repeat(d_skip[l], SSM_HEAD_DIM).reshape(1, D_INNER), g_ssm_norm=g_ssm_norm[l].reshape(1, D_INNER),
        w_ba=w_branch_attn[l].astype(BF16), w_bs=w_branch_ssm[l].astype(BF16), w_out=w_out[l].astype(BF16),
        w_gu=w_gate_up[l].astype(BF16), w_down=w_down[l].astype(BF16),
    )


def kernel(x_prompt, x_sample, cache_k, cache_v, cache_ki, state_conv, state_ssm, c_prompt, c_sample, w_ada, b_ada, g_norm_mix, g_norm_ffn, w_in, g_q, g_k, w_conv, b_conv, dt_bias, a_log, d_skip, g_ssm_norm, w_branch_attn, w_branch_ssm, w_out, w_gate_up, w_down):
    depth = w_in.shape[0]
    bp, bs = x_prompt.shape[0], x_sample.shape[0]
    past = cache_k.shape[2]
    y_p, y_s = x_prompt, x_sample
    c_all = jnp.concatenate([c_prompt, c_sample], axis=0)
    new_p = [[] for _ in range(5)]
    new_s = [[] for _ in range(5)]
    for l in range(depth):
        p = _layer_params(l, w_in, g_q, g_k, g_norm_mix, g_norm_ffn, w_conv, b_conv, dt_bias, a_log, d_skip,
                          g_ssm_norm, w_branch_attn, w_branch_ssm, w_out, w_gate_up, w_down)
        mod3 = _ada_mod(c_all, w_ada[l], b_ada[l]).reshape(bp + bs, 6, D_MODEL)
        zero_conv = jnp.zeros((bp, CONV_W - 1, CONV_CH), F32)
        zero_ssm = jnp.zeros((bp, SSM_HEADS, SSM_HEAD_DIM, D_STATE), F32)
        y_p, *st_p = _group_step(y_p, mod3, 0, None, zero_conv, zero_ssm, p)
        time_minor = lambda a: jnp.moveaxis(a, 1, -1)
        cache = (time_minor(cache_k[l]), time_minor(cache_v[l]), time_minor(cache_ki[l]))
        y_s, *st_s = _group_step(y_s, mod3, bp, cache, state_conv[l], state_ssm[l], p)
        for acc, a in zip(new_p, st_p):
            acc.append(a)
        for acc, a in zip(new_s, st_s):
            acc.append(a)
    return (y_p, y_s, *[jnp.stack(a) for a in new_p], *[jnp.stack(a) for a in new_s])
```

```python
import functools
import math

import jax
import jax.numpy as jnp
from jax import lax
from jax.experimental import pallas as pl
from jax.experimental.pallas import tpu as pltpu

F32, BF16, I32 = jnp.float32, jnp.bfloat16, jnp.int32

D_MODEL = 1024
CHUNK = 64
N_HEADS = 16
HEAD_DIM = 64
N_KV_HEADS = 4
Q_PER_KV = N_HEADS // N_KV_HEADS
IDX_HEADS = 8
IDX_DIM = 64
TOPK_MAX = 256
ROPE_THETA = 10000.0
D_INNER = 2 * D_MODEL
SSM_HEAD_DIM = 64
SSM_HEADS = D_INNER // SSM_HEAD_DIM
SSM_GROUPS = 8
HEADS_PER_GROUP = SSM_HEADS // SSM_GROUPS
GROUP_W = HEADS_PER_GROUP * SSM_HEAD_DIM
D_STATE = 128
CONV_W = 4
CONV_CH = D_INNER + 2 * SSM_GROUPS * D_STATE
D_FF = -(-8 * D_MODEL // (3 * 256)) * 256
EPS = 1e-6
QK_W = (N_HEADS + N_KV_HEADS) * HEAD_DIM
KV_W = N_KV_HEADS * HEAD_DIM
QI_W = IDX_HEADS * IDX_DIM
SSM_HEADS_N = SSM_HEADS

LANES = 128
MISC_WI = IDX_DIM
MISC_DT = IDX_DIM + IDX_HEADS
WI_SCALE = (IDX_HEADS ** -0.5) * (IDX_DIM ** -0.5)
INT_MIN = -(2 ** 31)
NEG_BIG = -1e30
COUNT_ROWS = 64
SSD_CHUNKS_PER_STEP = 2
ATTN_TQ = 256
ATTN_LOGITS_BYTES = 16 * 1024 * 1024
VMEM_LIMIT = 56 * 1024 * 1024


def _cparams(*sem):
    return pltpu.CompilerParams(dimension_semantics=sem, vmem_limit_bytes=VMEM_LIMIT)


def _silu(x):
    h = 0.5 * x
    return h + h * jnp.tanh(h)


def _split_bf16(x, n):
    pieces = []
    r = x
    for _ in range(n):
        p = r.astype(BF16)
        pieces.append(p)
        r = r - p.astype(F32)
    return pieces


def _dot(a, b):
    return jnp.dot(a, b, preferred_element_type=F32)


def _dot_nt(a, b):
    return lax.dot_general(a, b, (((1,), (1,)), ((), ())), preferred_element_type=F32)


def _dot_tn(a, b):
    return lax.dot_general(a, b, (((0,), (0,)), ((), ())), preferred_element_type=F32)


def _exact_dot(x, m_stacked, n):
    return _dot(jnp.concatenate(_split_bf16(x, n), axis=1), m_stacked)


def _exact_dot_left(m_tiled, x, n):
    return _dot(m_tiled, jnp.concatenate(_split_bf16(x, n), axis=0))


def _rotate_half(x):
    w = x.shape[-1]
    lane = lax.broadcasted_iota(I32, x.shape, x.ndim - 1)
    first = (lane % HEAD_DIM) < (HEAD_DIM // 2)
    return jnp.where(first, pltpu.roll(x, w - HEAD_DIM // 2, x.ndim - 1), pltpu.roll(x, HEAD_DIM // 2, x.ndim - 1))


def _rope(x, cos, sin):
    reps = x.shape[-1] // LANES
    if reps > 1:
        cos = jnp.tile(cos, (1, reps))
        sin = jnp.tile(sin, (1, reps))
    return x * cos + _rotate_half(x) * sin


def _mod_kernel(c_ref, w_ref, b_ref, o_ref):
    s = _silu(c_ref[...])
    o_ref[...] = _dot(s.astype(BF16), w_ref[...].astype(BF16)) + b_ref[...]


def _ada_mod(c_all, w_ada, b_ada):
    bt = c_all.shape[0]
    n = w_ada.shape[1]
    tn = D_MODEL
    return pl.pallas_call(
        _mod_kernel,
        grid=(n // tn,),
        in_specs=[
            pl.BlockSpec((bt, D_MODEL), lambda j: (0, 0)),
            pl.BlockSpec((D_MODEL, tn), lambda j: (0, j)),
            pl.BlockSpec((1, tn), lambda j: (0, j)),
        ],
        out_specs=pl.BlockSpec((bt, tn), lambda j: (0, j)),
        out_shape=jax.ShapeDtypeStruct((bt, n), F32),
        compiler_params=_cparams("parallel"),
        name="ada_mod",
    )(c_all, w_ada, b_ada.reshape(1, n))


def _modulated_norm(x, mod, gain, shift_idx, scale_idx):
    ms = jnp.mean(x * x, axis=-1, keepdims=True)
    xn = x * lax.rsqrt(ms + EPS)
    sh = mod[:, shift_idx:shift_idx + 1, :]
    sc = mod[:, scale_idx:scale_idx + 1, :]
    return xn * gain * (1.0 + sc) + sh


def _hnorm_kernel(x_ref, mod_ref, g_ref, o_ref):
    o_ref[...] = _modulated_norm(x_ref[...], mod_ref[...], g_ref[...], 0, 1).astype(o_ref.dtype)


def _row_blocking(b, t):
    tt = min(t, 512)
    bb = max(1, min(b, 512 // tt))
    assert t % tt == 0 and b % bb == 0 and tt % CHUNK == 0, (b, t)
    return bb, tt


def _hnorm(x, mod3, mod_off, gain):
    b, t, d = x.shape
    bb, tt = _row_blocking(b, t)
    assert mod_off % bb == 0, (mod_off, bb)
    off = mod_off // bb
    return pl.pallas_call(
        _hnorm_kernel,
        grid=(b // bb, t // tt),
        in_specs=[
            pl.BlockSpec((bb, tt, d), lambda i, j: (i, j, 0)),
            pl.BlockSpec((bb, 6, d), lambda i, j: (i + off, 0, 0)),
            pl.BlockSpec((1, 1, d), lambda i, j: (0, 0, 0)),
        ],
        out_specs=pl.BlockSpec((bb, tt, d), lambda i, j: (i, j, 0)),
        out_shape=jax.ShapeDtypeStruct((b, t, d), BF16),
        compiler_params=_cparams("parallel", "parallel"),
        name="hnorm",
    )(x, mod3, gain.reshape(1, 1, d))


def _proj_act_kernel(h_ref, w_ref, o_ref, *, act):
    acc = _dot(h_ref[...], w_ref[...])
    if act == "silu":
        acc = _silu(acc)
    elif act == "sigmoid":
        acc = jax.nn.sigmoid(acc)
    o_ref[...] = acc.astype(o_ref.dtype)


def _proj_act(h2d, w, act, out_dtype, tm, tn):
    m, k = h2d.shape
    n = w.shape[1]
    return pl.pallas_call(
        functools.partial(_proj_act_kernel, act=act),
        grid=(n // tn, m // tm),
        in_specs=[
            pl.BlockSpec((tm, k), lambda j, i: (i, 0)),
            pl.BlockSpec((k, tn), lambda j, i: (0, j)),
        ],
        out_specs=pl.BlockSpec((tm, tn), lambda j, i: (i, j)),
        out_shape=jax.ShapeDtypeStruct((m, n), out_dtype),
        compiler_params=_cparams("parallel", "parallel"),
        name="proj_" + act,
    )(h2d, w)


SUBLANES = 8
CONV_SLAB = 256


def _proj_conv_kernel(h_ref, w_ref, cst_ref, wconv_ref, bconv_ref, o_ref, tail_ref, buf, *, bb, tt, tiles_per_seq):
    i = pl.program_id(1)
    tail = CONV_W - 1
    tn = w_ref.shape[1]

    @pl.when(i % tiles_per_seq == 0)
    def _():
        for s in range(bb):
            buf[s, 0:SUBLANES - tail, :] = jnp.zeros((SUBLANES - tail, tn), F32)
            buf[s, SUBLANES - tail:SUBLANES, :] = cst_ref[s]

    h = h_ref[...]
    n_slabs = tn // CONV_SLAB
    slab = lambda c: slice(c * CONV_SLAB, (c + 1) * CONV_SLAB)
    acc_next = _dot(h, w_ref[:, slab(0)])
    for c in range(n_slabs):
        cs = slab(c)
        acc = acc_next
        if c + 1 < n_slabs:
            acc_next = _dot(h, w_ref[:, slab(c + 1)])
        for s in range(bb):
            buf[s, SUBLANES:SUBLANES + tt, cs] = acc[s * tt:(s + 1) * tt]
            xb = buf[s, :, cs]
            xc = bconv_ref[:, cs] + wconv_ref[tail:tail + 1, cs] * xb[SUBLANES:]
            for jj in range(tail):
                shifted = pltpu.roll(xb, tail - jj, 0)[SUBLANES:]
                xc = xc + wconv_ref[jj:jj + 1, cs] * shifted
            o_ref[s * tt:(s + 1) * tt, cs] = _silu(xc).astype(o_ref.dtype)
            last = buf[s, tt:tt + SUBLANES, cs]
            tail_ref[s, :, cs] = last
            buf[s, 0:SUBLANES, cs] = last


def _proj_conv(h2d, w, conv_state, w_conv, b_conv, b, t, out_dtype):
    m, k = h2d.shape
    n = w.shape[1]
    bb, tt = _row_blocking(b, t)
    tm = bb * tt
    tiles_per_seq = t // tt
    tn = min(n, 2048)
    kern = functools.partial(_proj_conv_kernel, bb=bb, tt=tt, tiles_per_seq=tiles_per_seq)
    seq_block = lambda rows: pl.BlockSpec((bb, rows, tn), lambda j, i: (i // tiles_per_seq, 0, j))
    return pl.pallas_call(
        kern,
        grid=(n // tn, m // tm),
        in_specs=[
            pl.BlockSpec((tm, k), lambda j, i: (i, 0)),
            pl.BlockSpec((k, tn), lambda j, i: (0, j)),
            seq_block(CONV_W - 1),
            pl.BlockSpec((CONV_W, tn), lambda j, i: (0, j)),
            pl.BlockSpec((1, tn), lambda j, i: (0, j)),
        ],
        out_specs=[pl.BlockSpec((tm, tn), lambda j, i: (i, j)), seq_block(SUBLANES)],
        out_shape=[jax.ShapeDtypeStruct((m, n), out_dtype), jax.ShapeDtypeStruct((b, SUBLANES, n), F32)],
        scratch_shapes=[pltpu.VMEM((bb, SUBLANES + tt, tn), F32)],
        compiler_params=_cparams("parallel", "arbitrary"),
        name="proj_conv",
    )(h2d, w, conv_state, w_conv, b_conv)


def _store_time_minor(ref, x):
    bb, tt = ref.shape[0], ref.shape[-1]
    for s in range(bb):
        ref[s] = x[s * tt:(s + 1) * tt, :].T.reshape(ref.shape[1:])


def _qk_kernel(h_ref, w_ref, gsum_ref, gexp_ref, gain_ref, cos_ref, sin_ref, q_ref, k_ref):
    acc = _dot(h_ref[...], w_ref[...])
    ss = _exact_dot(acc * acc, gsum_ref[...], 2)
    rs = lax.rsqrt(ss * (1.0 / HEAD_DIM) + EPS)
    rs_full = _exact_dot(rs, gexp_ref[...], 2)
    xn = acc * rs_full * gain_ref[...]
    out = _rope(xn, cos_ref[...], sin_ref[...])
    nq = N_HEADS * HEAD_DIM
    q_ref[...] = (out[:, :nq] * (HEAD_DIM ** -0.5)).astype(q_ref.dtype)
    _store_time_minor(k_ref, out[:, nq:])


def _vm_kernel(h_ref, w_ref, cos_ref, sin_ref, dtb_ref, v_ref, qi_ref, misc_ref, kit_ref):
    acc = _dot(h_ref[...], w_ref[...])
    cos, sin = cos_ref[...], sin_ref[...]
    _store_time_minor(v_ref, acc[:, :KV_W])
    qi_ref[...] = _rope(acc[:, KV_W:KV_W + QI_W], cos, sin).astype(qi_ref.dtype)
    m = acc[:, KV_W + QI_W:]
    lane = lax.broadcasted_iota(I32, m.shape, 1)
    roped = _rope(m, cos, sin)
    dt = jax.nn.softplus(m + dtb_ref[...])
    misc_ref[...] = jnp.where(lane < MISC_WI, roped,
                              jnp.where(lane < MISC_DT, m * WI_SCALE,
                                        jnp.where(lane < MISC_DT + SSM_HEADS, dt, 0.0)))
    _store_time_minor(kit_ref, roped[:, :IDX_DIM])


def _table_spec(tab_rows, tm):
    nblk = tab_rows // tm
    return pl.BlockSpec((tm, LANES), lambda i: (i % nblk, 0))


def _time_minor_spec(b, t, *mid):
    bb, tt = _row_blocking(b, t)
    per_seq = t // tt
    zeros = (0,) * len(mid)
    return pl.BlockSpec((bb, *mid, tt), lambda i: (i // per_seq, *zeros, i % per_seq))


def _proj_qk(h2d, w_qk, gsum, gexp, gain, cos_tab, sin_tab, b, t):
    m, k = h2d.shape
    tm = math.prod(_row_blocking(b, t))
    nq = N_HEADS * HEAD_DIM
    const = lambda i: (0, 0)
    return pl.pallas_call(
        _qk_kernel,
        grid=(m // tm,),
        in_specs=[
            pl.BlockSpec((tm, k), lambda i: (i, 0)),
            pl.BlockSpec((k, QK_W), const),
            pl.BlockSpec((2 * QK_W, LANES), const),
            pl.BlockSpec((2 * LANES, QK_W), const),
            pl.BlockSpec((1, QK_W), const),
            _table_spec(cos_tab.shape[0], tm),
            _table_spec(sin_tab.shape[0], tm),
        ],
        out_specs=[pl.BlockSpec((tm, nq), lambda i: (i, 0)), _time_minor_spec(b, t, N_KV_HEADS, HEAD_DIM)],
        out_shape=[jax.ShapeDtypeStruct((m, nq), BF16), jax.ShapeDtypeStruct((b, N_KV_HEADS, HEAD_DIM, t), F32)],
        compiler_params=_cparams("parallel"),
        name="proj_qk",
    )(h2d, w_qk, gsum, gexp, gain, cos_tab, sin_tab)


def _proj_vm(h2d, w_vm, cos_tab, sin_tab, dtb, b, t):
    m, k = h2d.shape
    tm = math.prod(_row_blocking(b, t))
    wn = w_vm.shape[1]
    const = lambda i: (0, 0)
    return pl.pallas_call(
        _vm_kernel,
        grid=(m // tm,),
        in_specs=[
            pl.BlockSpec((tm, k), lambda i: (i, 0)),
            pl.BlockSpec((k, wn), const),
            _table_spec(cos_tab.shape[0], tm),
            _table_spec(sin_tab.shape[0], tm),
            pl.BlockSpec((1, LANES), const),
        ],
        out_specs=[_time_minor_spec(b, t, N_KV_HEADS, HEAD_DIM), pl.BlockSpec((tm, QI_W), lambda i: (i, 0)),
                   pl.BlockSpec((tm, LANES), lambda i: (i, 0)), _time_minor_spec(b, t, IDX_DIM)],
        out_shape=[jax.ShapeDtypeStruct((b, N_KV_HEADS, HEAD_DIM, t), F32), jax.ShapeDtypeStruct((m, QI_W), BF16),
                   jax.ShapeDtypeStruct((m, LANES), F32), jax.ShapeDtypeStruct((b, IDX_DIM, t), F32)],
        compiler_params=_cparams("parallel"),
        name="proj_vm",
    )(h2d, w_vm, cos_tab, sin_tab, dtb)


def _attn_kernel(*refs, tq, kb, gp, topk, past, t_new, has_cache):
    n_in = 9 if has_cache else 6
    q_ref, qi_ref, wi_ref, k_ref, v_ref, kit_ref = refs[:6]
    o_ref = refs[n_in]
    kbf, vbf, kibf, sc_scr, keep_scr, qis, qs, mb_scr, acc_scr, s_scr = refs[n_in + 1:n_in + 11]
    if has_cache:
        ck_ref, cv_ref, cki_ref = refs[6:9]
        ckb, cvb, ckib = refs[n_in + 11:]
    j = pl.program_id(1)
    kbn = min(kb, t_new)
    n_cache_blocks = past // kb

    def ones_row(n):
        return jnp.where(lax.broadcasted_iota(I32, (HEAD_DIM, n), 0) == 0, 1.0, 0.0).astype(BF16)

    def stage(k_src, v_src, ki_src, k_dst, v_dst, ki_dst, n):
        for g in range(N_KV_HEADS):
            k_dst[g] = k_src[0, g].astype(BF16)
            v_dst[g, 0:HEAD_DIM, :] = v_src[0, g].astype(BF16)
            v_dst[g, HEAD_DIM:2 * HEAD_DIM, :] = ones_row(n)
        ki_dst[...] = ki_src[0].T.astype(BF16)

    @pl.when(j == 0)
    def _():
        stage(k_ref, v_ref, kit_ref, kbf, vbf, kibf, t_new)
        if has_cache:
            stage(ck_ref, cv_ref, cki_ref, ckb, cvb, ckib, past)

    for h in range(IDX_HEADS):
        qis[h * tq:(h + 1) * tq, :] = qi_ref[0, :, h * IDX_DIM:(h + 1) * IDX_DIM]
    for h in range(N_HEADS):
        qs[h * tq:(h + 1) * tq, :] = q_ref[0, :, h * HEAD_DIM:(h + 1) * HEAD_DIM]
    wi_t = wi_ref[0, :, MISC_WI:MISC_WI + IDX_HEADS].T

    qpos = past + j * tq + lax.broadcasted_iota(I32, (1, tq), 1)
    limit = (qpos // CHUNK + 1) * CHUNK
    n_new_blocks = (j * tq + tq + kbn - 1) // kbn

    def over_cache(fn, init):
        if not has_cache:
            return init
        return lax.fori_loop(0, n_cache_blocks, lambda i, c: fn(pl.multiple_of(i * kb, kb), c), init)

    def over_new(fn, init):
        if t_new <= kb:
            return fn(0, init)
        return lax.fori_loop(0, n_new_blocks, lambda i, c: fn(pl.multiple_of(i * kbn, kbn), c), init)

    def lanes_at(off):
        return past + off if isinstance(off, int) else pl.multiple_of(past + off, LANES)

    def score_block(ki_blk, kpos0, width):
        lg = _dot_nt(ki_blk, qis[...])
        sc = jnp.zeros((width, tq), F32)
        for h in range(IDX_HEADS):
            sc = sc + wi_t[h:h + 1, :] * jnp.maximum(lg[:, h * tq:(h + 1) * tq], 0.0)
        kpos = kpos0 + lax.broadcasted_iota(I32, (width, 1), 0)
        sc_scr[pl.ds(kpos0, width), :] = jnp.where(kpos < limit, sc, -jnp.inf)

    def p1c(off, c):
        score_block(ckib[pl.ds(off, kb), :], off, kb)
        return c

    def p1n(off, c):
        score_block(kibf[pl.ds(off, kbn), :], lanes_at(off), kbn)
        return c

    over_cache(p1c, 0)
    over_new(p1n, 0)

    def key_to_float(c):
        return pltpu.bitcast(jnp.where(c >= 0, c, c ^ 0x7FFFFFFF), F32)

    def count(cmp, cand):
        def cnt(row_off, width, acc):
            hit = jnp.where(cmp(sc_scr[pl.ds(row_off, width), :], cand), 1.0, 0.0)
            for r in range(width // COUNT_ROWS):
                acc = acc + hit[r * COUNT_ROWS:(r + 1) * COUNT_ROWS]
            return acc
        acc = jnp.zeros((COUNT_ROWS, tq), F32)
        acc = over_cache(lambda off, a: cnt(off, kb, a), acc)
        acc = over_new(lambda off, a: cnt(lanes_at(off), kbn, a), acc)
        return jnp.sum(acc, axis=0, keepdims=True)

    def bit_step(it, prefix):
        bit = jnp.left_shift(jnp.int32(1), 31 - it)
        cand = key_to_float((prefix | bit) ^ INT_MIN)
        return jnp.where(count(jnp.greater_equal, cand) >= float(topk), prefix | bit, prefix)

    prefix = lax.fori_loop(0, 32, bit_step, jnp.zeros((1, tq), I32))
    thr = key_to_float(prefix ^ INT_MIN)
    flt_max = float(jnp.finfo(F32).max)
    thr = jnp.where(thr >= -flt_max, thr, -flt_max)

    n_ge = count(jnp.greater_equal, thr)
    has_ties = jnp.max(n_ge) > float(topk)

    def keep_ranked(row_off, width, need, seen):
        sc = sc_scr[pl.ds(row_off, width), :]
        tie = sc == thr
        tie_f = jnp.where(tie, 1.0, 0.0)
        below = lax.broadcasted_iota(I32, (width, width), 1) <= lax.broadcasted_iota(I32, (width, width), 0)
        rank = seen + _dot(jnp.where(below, 1.0, 0.0).astype(BF16), tie_f.astype(BF16))
        keep = (sc > thr) | (tie & (rank <= need))
        keep_scr[pl.ds(row_off, width), :] = jnp.where(keep, 1.0, 0.0).astype(BF16)
        return seen + jnp.sum(tie_f, axis=0, keepdims=True)

    def keep_all_ties(row_off, width, c):
        sc = sc_scr[pl.ds(row_off, width), :]
        keep_scr[pl.ds(row_off, width), :] = jnp.where(sc >= thr, 1.0, 0.0).astype(BF16)
        return c

    def with_ties():
        need = float(topk) - count(jnp.greater, thr)
        seen = over_cache(lambda off, s: keep_ranked(off, kb, need, s), jnp.zeros((1, tq), F32))
        over_new(lambda off, s: keep_ranked(lanes_at(off), kbn, need, s), seen)

    def without_ties():
        over_cache(lambda off, c: keep_all_ties(off, kb, c), 0)
        over_new(lambda off, c: keep_all_ties(lanes_at(off), kbn, c), 0)

    lax.cond(has_ties, with_ties, without_ties)

    rows = Q_PER_KV * tq

    eye = jnp.where(lax.broadcasted_iota(I32, (tq, tq), 0) == lax.broadcasted_iota(I32, (tq, tq), 1), 1.0, 0.0).astype(BF16)

    def pass_a(g0, k_of, lane_off, width):
        keep_q = _dot_nt(eye, keep_scr[pl.ds(lane_off, width), :])
        b = jnp.tile(jnp.where(keep_q > 0.5, 0.0, NEG_BIG), (Q_PER_KV, 1))
        for gi in range(gp):
            g = g0 + gi
            s = _dot(qs[g * rows:(g + 1) * rows, :], k_of(g)) + b
            s_scr[gi, :, pl.ds(lane_off, width)] = s
            m = mb_scr[gi]
            if width % LANES:
                m = jnp.maximum(m, jnp.max(s, axis=1, keepdims=True))
            else:
                for c in range(width // LANES):
                    m = jnp.maximum(m, s[:, c * LANES:(c + 1) * LANES])
            mb_scr[gi] = m

    def pass_b(g0, v_of, lane_off, width):
        for gi in range(gp):
            m = mb_scr[gi]
            m = m[:, :width] if width < LANES else jnp.tile(m, (1, width // LANES))
            p = jnp.exp(s_scr[gi, :, pl.ds(lane_off, width)] - m)
            acc_scr[gi] += _dot_nt(p.astype(BF16), v_of(g0 + gi))

    def cache_k(off):
        return lambda g: ckb[g, :, pl.ds(off, kb)]

    def cache_v(off):
        return lambda g: cvb[g, :, pl.ds(off, kb)]

    def new_k(off):
        return lambda g: kbf[g, :, pl.ds(off, kbn)]

    def new_v(off):
        return lambda g: vbf[g, :, pl.ds(off, kbn)]

    def run(fn, cache_args, new_args):
        def on_cache(off, c):
            fn(*[a(off) for a in cache_args], off, kb)
            return c

        def on_new(off, c):
            fn(*[a(off) for a in new_args], lanes_at(off), kbn)
            return c

        over_cache(on_cache, 0)
        over_new(on_new, 0)

    for g0 in range(0, N_KV_HEADS, gp):
        mb_scr[...] = jnp.full(mb_scr.shape, NEG_BIG, F32)
        run(functools.partial(pass_a, g0), [cache_k], [new_k])
        for gi in range(gp):
            mb_scr[gi] = jnp.broadcast_to(jnp.max(mb_scr[gi], axis=1, keepdims=True), (rows, LANES))
        acc_scr[...] = jnp.zeros(acc_scr.shape, F32)
        run(functools.partial(pass_b, g0), [cache_v], [new_v])

        for gi in range(gp):
            a = acc_scr[gi]
            out = a[:, :HEAD_DIM] / a[:, HEAD_DIM:HEAD_DIM + 1]
            for r in range(Q_PER_KV):
                hh = (g0 + gi) * Q_PER_KV + r
                o_ref[0, :, hh * HEAD_DIM:(hh + 1) * HEAD_DIM] = out[r * tq:(r + 1) * tq].astype(o_ref.dtype)


def _attention(q, qi, misc, kt, vt, kit, cache=None):
    b, t, _ = q.shape
    has_cache = cache is not None
    past = cache[2].shape[2] if has_cache else 0
    n_keys = past + t
    topk = min(TOPK_MAX, n_keys // 4)
    tq = min(t, ATTN_TQ)
    kb = 256
    rows = Q_PER_KV * tq
    key_w = past + -(-t // LANES) * LANES
    gp = N_KV_HEADS
    while gp > 1 and gp * rows * key_w * 4 > ATTN_LOGITS_BYTES:
        gp //= 2
    qtile = lambda w: pl.BlockSpec((1, tq, w), lambda i, j: (i, j, 0))
    heads = lambda n: pl.BlockSpec((1, N_KV_HEADS, HEAD_DIM, n), lambda i, j: (i, 0, 0, 0))
    idx = lambda n: pl.BlockSpec((1, IDX_DIM, n), lambda i, j: (i, 0, 0))
    in_specs = [qtile(N_HEADS * HEAD_DIM), qtile(QI_W), qtile(LANES), heads(t), heads(t), idx(t)]
    args = [q, qi, misc, kt, vt, kit]
    if has_cache:
        in_specs += [heads(past), heads(past), idx(past)]
        args += list(cache)
    kern = functools.partial(_attn_kernel, tq=tq, kb=kb, gp=gp, topk=topk, past=past, t_new=t, has_cache=has_cache)
    return pl.pallas_call(
        kern,
        grid=(b, t // tq),
        in_specs=in_specs,
        out_specs=qtile(N_HEADS * HEAD_DIM),
        out_shape=jax.ShapeDtypeStruct((b, t, N_HEADS * HEAD_DIM), BF16),
        scratch_shapes=[
            pltpu.VMEM((N_KV_HEADS, HEAD_DIM, t), BF16),
            pltpu.VMEM((N_KV_HEADS, 2 * HEAD_DIM, t), BF16),
            pltpu.VMEM((t, IDX_DIM), BF16),
            pltpu.VMEM((n_keys, tq), F32),
            pltpu.VMEM((n_keys, tq), BF16),
            pltpu.VMEM((IDX_HEADS * tq, IDX_DIM), BF16),
            pltpu.VMEM((N_HEADS * tq, HEAD_DIM), BF16),
            pltpu.VMEM((gp, rows, LANES), F32),
            pltpu.VMEM((gp, rows, 2 * HEAD_DIM), F32),
            pltpu.VMEM((gp, rows, key_w), F32),
        ] + ([
            pltpu.VMEM((N_KV_HEADS, HEAD_DIM, past), BF16),
            pltpu.VMEM((N_KV_HEADS, 2 * HEAD_DIM, past), BF16),
            pltpu.VMEM((past, IDX_DIM), BF16),
        ] if has_cache else []),
        compiler_params=_cparams("parallel", "arbitrary"),
        name="attn_cache" if has_cache else "attn_prompt",
    )(*args)


def _ssd_kernel(xs_ref, bc_ref, misc_ref, zs_ref, h0_ref, alog_ref, dskip_ref, gnorm_ref, expand_ref, y_ref, hout_ref):
    c = pl.program_id(1)
    L = CHUNK

    @pl.when(c == 0)
    def _():
        hout_ref[...] = h0_ref[...]

    a = -jnp.exp(alog_ref[...])
    ri = lax.broadcasted_iota(I32, (L, 3 * L), 0)
    ci = lax.broadcasted_iota(I32, (L, 3 * L), 1) % L
    tri3 = jnp.where(ri >= ci, 1.0, 0.0).astype(BF16)
    ones3 = jnp.ones((L, 3 * L), BF16)
    expand3 = expand_ref[...]
    row_i = lax.broadcasted_iota(I32, (L, D_INNER), 0)
    lane_j = lax.broadcasted_iota(I32, (L, D_INNER), 1) % SSM_HEAD_DIM
    eye = row_i == lane_j
    causal = row_i >= lane_j
    blk_r = lax.broadcasted_iota(I32, (GROUP_W, GROUP_W), 0) // SSM_HEAD_DIM
    blk_c = lax.broadcasted_iota(I32, (GROUP_W, GROUP_W), 1) // SSM_HEAD_DIM
    same_head = blk_r == blk_c

    for sub in range(xs_ref.shape[1] // L):
        _ssd_chunk(slice(sub * L, (sub + 1) * L), xs_ref, bc_ref, misc_ref, zs_ref, dskip_ref, gnorm_ref, y_ref, hout_ref,
                   a, tri3, ones3, expand3, eye, causal, same_head)


def _ssd_chunk(rows, xs_ref, bc_ref, misc_ref, zs_ref, dskip_ref, gnorm_ref, y_ref, hout_ref,
               a, tri3, ones3, expand3, eye, causal, same_head):
    L = CHUNK
    dt = misc_ref[0, rows, :]
    acs = _exact_dot_left(tri3, dt * a, 3)
    col_acs = _exact_dot(acs, expand3, 3)
    col_dt = _exact_dot(dt, expand3[:2 * LANES], 2)
    row_acs = _exact_dot_left(ones3, jnp.where(eye, col_acs, 0.0), 3)
    row_dt = _exact_dot_left(ones3[:, :2 * L], jnp.where(eye, col_dt, 0.0), 2)
    a_last = acs[L - 1:L, MISC_DT:MISC_DT + SSM_HEADS]

    groups = range(SSM_GROUPS)
    gsl = [slice(g * GROUP_W, (g + 1) * GROUP_W) for g in groups]
    bgs = [bc_ref[0, rows, g * D_STATE:(g + 1) * D_STATE] for g in groups]
    cgs = [bc_ref[0, rows, (SSM_GROUPS + g) * D_STATE:(SSM_GROUPS + g + 1) * D_STATE] for g in groups]
    xgs = [xs_ref[0, rows, gs] for gs in gsl]
    hprevs = [hout_ref[0, g * HEADS_PER_GROUP:(g + 1) * HEADS_PER_GROUP].reshape(GROUP_W, D_STATE) for g in groups]
    cbs = [_dot_nt(cgs[g], jnp.tile(bgs[g], (HEADS_PER_GROUP, 1))) for g in groups]
    offs = [_dot_nt(cgs[g], hprevs[g].astype(BF16)) for g in groups]
    sts = []
    for g in groups:
        ce = col_acs[:, gsl[g]]
        w_state = jnp.exp(ce[L - 1:L, :] - ce) * col_dt[:, gsl[g]]
        sts.append(_dot_tn((xgs[g] * w_state).astype(BF16), bgs[g]))
    y_diags = []
    for g in groups:
        gs = gsl[g]
        mm = cbs[g] * jnp.exp(jnp.where(causal[:, gs], col_acs[:, gs] - row_acs[:, gs], -jnp.inf)) * row_dt[:, gs]
        xbd = jnp.where(same_head, jnp.tile(xgs[g].astype(BF16), (HEADS_PER_GROUP, 1)), 0.0).astype(BF16)
        y_diags.append(_dot(mm.astype(BF16), xbd))
    for g in groups:
        for r in range(HEADS_PER_GROUP):
            hh = g * HEADS_PER_GROUP + r
            decay = jnp.exp(a_last[:, hh:hh + 1])
            rs = slice(r * SSM_HEAD_DIM, (r + 1) * SSM_HEAD_DIM)
            hout_ref[0, hh] = decay * hprevs[g][rs] + sts[g][rs]
    for g in groups:
        gs = gsl[g]
        y_off = jnp.exp(col_acs[:, gs]) * offs[g]
        yt = (y_diags[g] + y_off + dskip_ref[:, gs] * xgs[g]) * zs_ref[0, rows, gs].astype(F32)
        ms = jnp.mean(yt * yt, axis=-1, keepdims=True)
        y_ref[0, rows, gs] = (yt * lax.rsqrt(ms + EPS) * gnorm_ref[:, gs]).astype(y_ref.dtype)


def _ssd(xs, bc, misc, zs, h0, a_log, dskip_full, g_norm, expand):
    b, t, _ = xs.shape
    rows = min(t, SSD_CHUNKS_PER_STEP * CHUNK)
    nc = t // rows
    chunk = lambda w: pl.BlockSpec((1, rows, w), lambda i, c: (i, c, 0))
    const2 = lambda r, w: pl.BlockSpec((r, w), lambda i, c: (0, 0))
    state = pl.BlockSpec((1, SSM_HEADS, SSM_HEAD_DIM, D_STATE), lambda i, c: (i, 0, 0, 0))
    return pl.pallas_call(
        _ssd_kernel,
        grid=(b, nc),
        in_specs=[
            chunk(D_INNER), chunk(2 * SSM_GROUPS * D_STATE), chunk(LANES), chunk(D_INNER), state,
            const2(1, LANES), const2(1, D_INNER), const2(1, D_INNER), const2(3 * LANES, D_INNER),
        ],
        out_specs=[chunk(D_INNER), state],
        out_shape=[jax.ShapeDtypeStruct((b, t, D_INNER), BF16),
                   jax.ShapeDtypeStruct((b, SSM_HEADS, SSM_HEAD_DIM, D_STATE), F32)],
        compiler_params=_cparams("parallel", "arbitrary"),
        name="ssd",
    )(xs, bc, misc, zs, h0, a_log, dskip_full, g_norm, expand)


def _merge_kernel(o_ref, y_ref, g_ref, x_ref, mod_ref, gain_ref, wa_ref, ws_ref, wo_ref, x1_ref, h2_ref):
    bb, tt, d = x_ref.shape
    rows = bb * tt
    o = o_ref[...].reshape(rows, -1)
    y = y_ref[...].reshape(rows, -1)
    gates = g_ref[...].reshape(rows, -1).astype(F32)
    mixed = gates[:, :d] * _dot(o, wa_ref[...]) + gates[:, d:] * _dot(y, ws_ref[...])
    out = _dot(mixed.astype(BF16), wo_ref[...]).reshape(bb, tt, d)
    mod = mod_ref[...]
    x1 = x_ref[...] + mod[:, 2:3, :] * out
    x1_ref[...] = x1
    h2_ref[...] = _modulated_norm(x1, mod, gain_ref[...], 3, 4).astype(h2_ref.dtype)


def _merge(o_attn, y, gates, x, mod3, mod_off, gain, wa, ws, wo):
    b, t, d = x.shape
    bb, tt = _row_blocking(b, t)
    assert mod_off % bb == 0, (mod_off, bb)
    off = mod_off // bb
    tile = lambda w: pl.BlockSpec((bb, tt, w), lambda i, j: (i, j, 0))
    const = lambda r, w: pl.BlockSpec((r, w), lambda i, j: (0, 0))
    return pl.pallas_call(
        _merge_kernel,
        grid=(b // bb, t // tt),
        in_specs=[
            tile(N_HEADS * HEAD_DIM), tile(D_INNER), tile(2 * D_MODEL), tile(d),
            pl.BlockSpec((bb, 6, d), lambda i, j: (i + off, 0, 0)),
            pl.BlockSpec((1, 1, d), lambda i, j: (0, 0, 0)),
            const(N_HEADS * HEAD_DIM, d), const(D_INNER, d), const(d, d),
        ],
        out_specs=[tile(d), tile(d)],
        out_shape=[jax.ShapeDtypeStruct((b, t, d), F32), jax.ShapeDtypeStruct((b, t, d), BF16)],
        compiler_params=_cparams("parallel", "parallel"),
        name="merge",
    )(o_attn, y, gates, x, mod3, gain.reshape(1, 1, d), wa, ws, wo)


def _ffn_kernel(h2_ref, x1_ref, mod_ref, wg_ref, wu_ref, wd_ref, o_ref, acc_ref):
    f = pl.program_id(2)
    bb, tt, d = x1_ref.shape
    h2 = h2_ref[...].reshape(bb * tt, d)
    act = _silu(_dot(h2, wg_ref[...])) * _dot(h2, wu_ref[...])
    part = _dot(act.astype(BF16), wd_ref[...])

    @pl.when(f == 0)
    def _():
        acc_ref[...] = part

    @pl.when(f > 0)
    def _():
        acc_ref[...] += part

    @pl.when(f == pl.num_programs(2) - 1)
    def _():
        o_ref[...] = x1_ref[...] + mod_ref[...][:, 5:6, :] * acc_ref[...].reshape(bb, tt, d)


def _ffn(h2, x1, mod3, mod_off, w_gu, w_down):
    b, t, d = x1.shape
    bb, tt = _row_blocking(b, t)
    assert mod_off % bb == 0, (mod_off, bb)
    off = mod_off // bb
    nf = 2
    tf = D_FF // nf
    tile = pl.BlockSpec((bb, tt, d), lambda i, j, f: (i, j, 0))
    return pl.pallas_call(
        _ffn_kernel,
        grid=(b // bb, t // tt, nf),
        in_specs=[
            tile, tile,
            pl.BlockSpec((bb, 6, d), lambda i, j, f: (i + off, 0, 0)),
            pl.BlockSpec((d, tf), lambda i, j, f: (0, f)),
            pl.BlockSpec((d, tf), lambda i, j, f: (0, nf + f)),
            pl.BlockSpec((tf, d), lambda i, j, f: (f, 0)),
        ],
        out_specs=tile,
        out_shape=jax.ShapeDtypeStruct((b, t, d), F32),
        scratch_shapes=[pltpu.VMEM((bb * tt, d), F32)],
        compiler_params=_cparams("parallel", "parallel", "arbitrary"),
        name="ffn",
    )(h2, x1, mod3, w_gu, w_gu, w_down)


def _rope_tables(t, past, tm):
    half = HEAD_DIM // 2
    inv = ROPE_THETA ** (-jnp.arange(half, dtype=F32) / half)
    ang = (past + jnp.arange(t)).astype(F32)[:, None] * inv[None, :]
    cos, sin = jnp.cos(ang), jnp.sin(ang)
    cos_t = jnp.concatenate([cos, cos, cos, cos], axis=1)
    sin_t = jnp.concatenate([-sin, sin, -sin, sin], axis=1)
    if t < tm:
        cos_t, sin_t = jnp.tile(cos_t, (tm // t, 1)), jnp.tile(sin_t, (tm // t, 1))
    return cos_t, sin_t


def _group_step(x, mod3, mod_off, cache, conv_state, ssm_state, p):
    b, t, d = x.shape
    m = b * t
    tm = min(512, m)
    past = cache[2].shape[2] if cache is not None else 0
    h = _hnorm(x, mod3, mod_off, p["g_norm_mix"]).reshape(m, d)
    cos_t, sin_t = _rope_tables(t, past, tm)
    q, kt = _proj_qk(h, p["w_qk"], p["gsum"], p["gexp"], p["qk_gain"], cos_t, sin_t, b, t)
    vt, qi, misc, kit = _proj_vm(h, p["w_vm"], cos_t, sin_t, p["dtb"], b, t)
    zs = _proj_act(h, p["w_z"], "silu", BF16, tm, D_INNER)
    gates = _proj_act(h, p["w_g"], "sigmoid", BF16, tm, 2 * D_MODEL)
    half = CONV_CH // 2
    xs, tail_x = _proj_conv(h, p["w_xs"], conv_state[:, :, :half], p["w_conv"][:, :half],
                            p["b_conv"][:, :half], b, t, F32)
    bc, tail_bc = _proj_conv(h, p["w_bc"], conv_state[:, :, half:], p["w_conv"][:, half:],
                             p["b_conv"][:, half:], b, t, BF16)

    r3 = lambda a: a.reshape(b, t, a.shape[-1])
    o_attn = _attention(r3(q), r3(qi), r3(misc), kt, vt, kit, cache)
    y, h_last = _ssd(r3(xs), r3(bc), r3(misc), r3(zs), ssm_state, p["a_log"], p["dskip_full"], p["g_ssm_norm"],
                     p["expand"])
    x1, h2 = _merge(o_attn, y, r3(gates), x, mod3, mod_off, p["g_norm_ffn"], p["w_ba"], p["w_bs"], p["w_out"])
    out = _ffn(h2, x1, mod3, mod_off, p["w_gu"], p["w_down"])
    conv_new = jnp.concatenate([tail_x, tail_bc], axis=2)[:, -(CONV_W - 1):]
    token_major = lambda a: jnp.moveaxis(a, -1, 1)
    return out, token_major(kt), token_major(vt), token_major(kit), conv_new, h_last


def _layer_params(l, w_in, g_q, g_k, g_norm_mix, g_norm_ffn, w_conv, b_conv, dt_bias, a_log, d_skip, g_ssm_norm,
                  w_branch_attn, w_branch_ssm, w_out, w_gate_up, w_down):
    sizes = (N_HEADS * HEAD_DIM, KV_W, KV_W, QI_W, IDX_DIM, IDX_HEADS, D_INNER, CONV_CH, SSM_HEADS, 2 * D_MODEL)
    offs = [0]
    for s in sizes:
        offs.append(offs[-1] + s)
    col = lambda i: w_in[l][:, offs[i]:offs[i + 1]]
    pad = LANES - (IDX_DIM + IDX_HEADS + SSM_HEADS)
    w_vm = jnp.concatenate([col(2), col(3), col(4), col(5), col(8), jnp.zeros((D_MODEL, pad), F32)], axis=1)
    head_of = jnp.arange(QK_W) // HEAD_DIM
    gsum = (head_of[:, None] == jnp.arange(LANES)[None, :]).astype(BF16)
    dtb = jnp.zeros((1, LANES), F32).at[0, MISC_DT:MISC_DT + SSM_HEADS].set(dt_bias[l])
    expand = ((jnp.arange(LANES) - MISC_DT)[:, None] == (jnp.arange(D_INNER) // SSM_HEAD_DIM)[None, :]).astype(BF16)
    a_log_lanes = jnp.zeros((1, LANES), F32).at[0, MISC_DT:MISC_DT + SSM_HEADS].set(a_log[l])
    return dict(
        w_qk=w_in[l][:, :QK_W].astype(BF16), w_vm=w_vm.astype(BF16), w_z=col(6).astype(BF16),
        w_xs=col(7)[:, :CONV_CH // 2].astype(BF16), w_bc=col(7)[:, CONV_CH // 2:].astype(BF16), w_g=col(9).astype(BF16),
        gsum=jnp.tile(gsum, (2, 1)), gexp=jnp.tile(gsum.T, (2, 1)),
        qk_gain=jnp.concatenate([jnp.tile(g_q[l], N_HEADS), jnp.tile(g_k[l], N_KV_HEADS)]).reshape(1, QK_W),
        dtb=dtb, expand=jnp.tile(expand, (3, 1)),
        g_norm_mix=g_norm_mix[l], g_norm_ffn=g_norm_ffn[l],
        w_conv=w_conv[l], b_conv=b_conv[l].reshape(1, CONV_CH), a_log=a_log_lanes,
        dskip_full=jnp.repeat(d_skip[l], SSM_HEAD_DIM).reshape(1, D_INNER), g_ssm_norm=g_ssm_norm[l].reshape(1, D_INNER),
        w_ba=w_branch_attn[l].astype(BF16), w_bs=w_branch_ssm[l].astype(BF16), w_out=w_out[l].astype(BF16),
        w_gu=w_gate_up[l].astype(BF16), w_down=w_down[l].astype(BF16),
    )


def kernel(x_prompt, x_sample, cache_k, cache_v, cache_ki, state_conv, state_ssm, c_prompt, c_sample, w_ada, b_ada, g_norm_mix, g_norm_ffn, w_in, g_q, g_k, w_conv, b_conv, dt_bias, a_log, d_skip, g_ssm_norm, w_branch_attn, w_branch_ssm, w_out, w_gate_up, w_down):
    depth = w_in.shape[0]
    bp, bs = x_prompt.shape[0], x_sample.shape[0]
    past = cache_k.shape[2]
    y_p, y_s = x_prompt, x_sample
    c_all = jnp.concatenate([c_prompt, c_sample], axis=0)
    new_p = [[] for _ in range(5)]
    new_s = [[] for _ in range(5)]
    for l in range(depth):
        p = _layer_params(l, w_in, g_q, g_k, g_norm_mix, g_norm_ffn, w_conv, b_conv, dt_bias, a_log, d_skip,
                          g_ssm_norm, w_branch_attn, w_branch_ssm, w_out, w_gate_up, w_down)
        mod3 = _ada_mod(c_all, w_ada[l], b_ada[l]).reshape(bp + bs, 6, D_MODEL)
        zero_conv = jnp.zeros((bp, CONV_W - 1, CONV_CH), F32)
        zero_ssm = jnp.zeros((bp, SSM_HEADS, SSM_HEAD_DIM, D_STATE), F32)
        y_p, *st_p = _group_step(y_p, mod3, 0, None, zero_conv, zero_ssm, p)
        time_minor = lambda a: jnp.moveaxis(a, 1, -1)
        cache = (time_minor(cache_k[l]), time_minor(cache_v[l]), time_minor(cache_ki[l]))
        y_s, *st_s = _group_step(y_s, mod3, bp, cache, state_conv[l], state_ssm[l], p)
        for acc, a in zip(new_p, st_p):
            acc.append(a)
        for acc, a in zip(new_s, st_s):
            acc.append(a)
    return (y_p, y_s, *[jnp.stack(a) for a in new_p], *[jnp.stack(a) for a in new_s])
```

```python
import functools
import math

import jax
import jax.numpy as jnp
from jax import lax
from jax.experimental import pallas as pl
from jax.experimental.pallas import tpu as pltpu

F32, BF16, I32 = jnp.float32, jnp.bfloat16, jnp.int32

D_MODEL = 1024
CHUNK = 64
N_HEADS = 16
HEAD_DIM = 64
N_KV_HEADS = 4
Q_PER_KV = N_HEADS // N_KV_HEADS
IDX_HEADS = 8
IDX_DIM = 64
TOPK_MAX = 256
ROPE_THETA = 10000.0
D_INNER = 2 * D_MODEL
SSM_HEAD_DIM = 64
SSM_HEADS = D_INNER // SSM_HEAD_DIM
SSM_GROUPS = 8
HEADS_PER_GROUP = SSM_HEADS // SSM_GROUPS
GROUP_W = HEADS_PER_GROUP * SSM_HEAD_DIM
D_STATE = 128
CONV_W = 4
CONV_CH = D_INNER + 2 * SSM_GROUPS * D_STATE
D_FF = -(-8 * D_MODEL // (3 * 256)) * 256
EPS = 1e-6
QK_W = (N_HEADS + N_KV_HEADS) * HEAD_DIM
KV_W = N_KV_HEADS * HEAD_DIM
QI_W = IDX_HEADS * IDX_DIM
SSM_HEADS_N = SSM_HEADS

LANES = 128
MISC_WI = IDX_DIM
MISC_DT = IDX_DIM + IDX_HEADS
WI_SCALE = (IDX_HEADS ** -0.5) * (IDX_DIM ** -0.5)
INT_MIN = -(2 ** 31)
NEG_BIG = -1e30
COUNT_ROWS = 64
SSD_CHUNKS_PER_STEP = 2
ATTN_TQ = 256
ATTN_LOGITS_BYTES = 16 * 1024 * 1024
VMEM_LIMIT = 56 * 1024 * 1024


def _cparams(*sem):
    return pltpu.CompilerParams(dimension_semantics=sem, vmem_limit_bytes=VMEM_LIMIT)


def _silu(x):
    h = 0.5 * x
    return h + h * jnp.tanh(h)


def _split_bf16(x, n):
    pieces = []
    r = x
    for _ in range(n):
        p = r.astype(BF16)
        pieces.append(p)
        r = r - p.astype(F32)
    return pieces


def _dot(a, b):
    return jnp.dot(a, b, preferred_element_type=F32)


def _dot_nt(a, b):
    return lax.dot_general(a, b, (((1,), (1,)), ((), ())), preferred_element_type=F32)


def _dot_tn(a, b):
    return lax.dot_general(a, b, (((0,), (0,)), ((), ())), preferred_element_type=F32)


def _exact_dot(x, m_stacked, n):
    return _dot(jnp.concatenate(_split_bf16(x, n), axis=1), m_stacked)


def _exact_dot_left(m_tiled, x, n):
    return _dot(m_tiled, jnp.concatenate(_split_bf16(x, n), axis=0))


def _rotate_half(x):
    w = x.shape[-1]
    lane = lax.broadcasted_iota(I32, x.shape, x.ndim - 1)
    first = (lane % HEAD_DIM) < (HEAD_DIM // 2)
    return jnp.where(first, pltpu.roll(x, w - HEAD_DIM // 2, x.ndim - 1), pltpu.roll(x, HEAD_DIM // 2, x.ndim - 1))


def _rope(x, cos, sin):
    reps = x.shape[-1] // LANES
    if reps > 1:
        cos = jnp.tile(cos, (1, reps))
        sin = jnp.tile(sin, (1, reps))
    return x * cos + _rotate_half(x) * sin


def _mod_kernel(c_ref, w_ref, b_ref, o_ref):
    s = _silu(c_ref[...])
    o_ref[...] = _dot(s.astype(BF16), w_ref[...].astype(BF16)) + b_ref[...]


def _ada_mod(c_all, w_ada, b_ada):
    bt = c_all.shape[0]
    n = w_ada.shape[1]
    tn = D_MODEL
    return pl.pallas_call(
        _mod_kernel,
        grid=(n // tn,),
        in_specs=[
            pl.BlockSpec((bt, D_MODEL), lambda j: (0, 0)),
            pl.BlockSpec((D_MODEL, tn), lambda j: (0, j)),
            pl.BlockSpec((1, tn), lambda j: (0, j)),
        ],
        out_specs=pl.BlockSpec((bt, tn), lambda j: (0, j)),
        out_shape=jax.ShapeDtypeStruct((bt, n), F32),
        compiler_params=_cparams("parallel"),
        name="ada_mod",
    )(c_all, w_ada, b_ada.reshape(1, n))


def _modulated_norm(x, mod, gain, shift_idx, scale_idx):
    ms = jnp.mean(x * x, axis=-1, keepdims=True)
    xn = x * lax.rsqrt(ms + EPS)
    sh = mod[:, shift_idx:shift_idx + 1, :]
    sc = mod[:, scale_idx:scale_idx + 1, :]
    return xn * gain * (1.0 + sc) + sh


def _hnorm_kernel(x_ref, mod_ref, g_ref, o_ref):
    o_ref[...] = _modulated_norm(x_ref[...], mod_ref[...], g_ref[...], 0, 1).astype(o_ref.dtype)


def _row_blocking(b, t):
    tt = min(t, 512)
    bb = max(1, min(b, 512 // tt))
    assert t % tt == 0 and b % bb == 0 and tt % CHUNK == 0, (b, t)
    return bb, tt


def _hnorm(x, mod3, mod_off, gain):
    b, t, d = x.shape
    bb, tt = _row_blocking(b, t)
    assert mod_off % bb == 0, (mod_off, bb)
    off = mod_off // bb
    return pl.pallas_call(
        _hnorm_kernel,
        grid=(b // bb, t // tt),
        in_specs=[
            pl.BlockSpec((bb, tt, d), lambda i, j: (i, j, 0)),
            pl.BlockSpec((bb, 6, d), lambda i, j: (i + off, 0, 0)),
            pl.BlockSpec((1, 1, d), lambda i, j: (0, 0, 0)),
        ],
        out_specs=pl.BlockSpec((bb, tt, d), lambda i, j: (i, j, 0)),
        out_shape=jax.ShapeDtypeStruct((b, t, d), BF16),
        compiler_params=_cparams("parallel", "parallel"),
        name="hnorm",
    )(x, mod3, gain.reshape(1, 1, d))


def _proj_act_kernel(h_ref, w_ref, o_ref, *, act):
    acc = _dot(h_ref[...], w_ref[...])
    if act == "silu":
        acc = _silu(acc)
    elif act == "sigmoid":
        acc = jax.nn.sigmoid(acc)
    o_ref[...] = acc.astype(o_ref.dtype)


def _proj_act(h2d, w, act, out_dtype, tm, tn):
    m, k = h2d.shape
    n = w.shape[1]
    return pl.pallas_call(
        functools.partial(_proj_act_kernel, act=act),
        grid=(n // tn, m // tm),
        in_specs=[
            pl.BlockSpec((tm, k), lambda j, i: (i, 0)),
            pl.BlockSpec((k, tn), lambda j, i: (0, j)),
        ],
        out_specs=pl.BlockSpec((tm, tn), lambda j, i: (i, j)),
        out_shape=jax.ShapeDtypeStruct((m, n), out_dtype),
        compiler_params=_cparams("parallel", "parallel"),
        name="proj_" + act,
    )(h2d, w)


SUBLANES = 8
CONV_SLAB = 256


def _proj_conv_kernel(h_ref, w_ref, cst_ref, wconv_ref, bconv_ref, o_ref, tail_ref, buf, *, bb, tt, tiles_per_seq):
    i = pl.program_id(1)
    tail = CONV_W - 1
    tn = w_ref.shape[1]

    @pl.when(i % tiles_per_seq == 0)
    def _():
        for s in range(bb):
            buf[s, 0:SUBLANES - tail, :] = jnp.zeros((SUBLANES - tail, tn), F32)
            buf[s, SUBLANES - tail:SUBLANES, :] = cst_ref[s]

    h = h_ref[...]
    n_slabs = tn // CONV_SLAB
    slab = lambda c: slice(c * CONV_SLAB, (c + 1) * CONV_SLAB)
    acc_next = _dot(h, w_ref[:, slab(0)])
    for c in range(n_slabs):
        cs = slab(c)
        acc = acc_next
        if c + 1 < n_slabs:
            acc_next = _dot(h, w_ref[:, slab(c + 1)])
        for s in range(bb):
            buf[s, SUBLANES:SUBLANES + tt, cs] = acc[s * tt:(s + 1) * tt]
            xb = buf[s, :, cs]
            xc = bconv_ref[:, cs] + wconv_ref[tail:tail + 1, cs] * xb[SUBLANES:]
            for jj in range(tail):
                shifted = pltpu.roll(xb, tail - jj, 0)[SUBLANES:]
                xc = xc + wconv_ref[jj:jj + 1, cs] * shifted
            o_ref[s * tt:(s + 1) * tt, cs] = _silu(xc).astype(o_ref.dtype)
            last = buf[s, tt:tt + SUBLANES, cs]
            tail_ref[s, :, cs] = last
            buf[s, 0:SUBLANES, cs] = last


def _proj_conv(h2d, w, conv_state, w_conv, b_conv, b, t, out_dtype):
    m, k = h2d.shape
    n = w.shape[1]
    bb, tt = _row_blocking(b, t)
    tm = bb * tt
    tiles_per_seq = t // tt
    tn = min(n, 2048)
    kern = functools.partial(_proj_conv_kernel, bb=bb, tt=tt, tiles_per_seq=tiles_per_seq)
    seq_block = lambda rows: pl.BlockSpec((bb, rows, tn), lambda j, i: (i // tiles_per_seq, 0, j))
    return pl.pallas_call(
        kern,
        grid=(n // tn, m // tm),
        in_specs=[
            pl.BlockSpec((tm, k), lambda j, i: (i, 0)),
            pl.BlockSpec((k, tn), lambda j, i: (0, j)),
            seq_block(CONV_W - 1),
            pl.BlockSpec((CONV_W, tn), lambda j, i: (0, j)),
            pl.BlockSpec((1, tn), lambda j, i: (0, j)),
        ],
        out_specs=[pl.BlockSpec((tm, tn), lambda j, i: (i, j)), seq_block(SUBLANES)],
        out_shape=[jax.ShapeDtypeStruct((m, n), out_dtype), jax.ShapeDtypeStruct((b, SUBLANES, n), F32)],
        scratch_shapes=[pltpu.VMEM((bb, SUBLANES + tt, tn), F32)],
        compiler_params=_cparams("parallel", "arbitrary"),
        name="proj_conv",
    )(h2d, w, conv_state, w_conv, b_conv)


def _store_time_minor(ref, x):
    bb, tt = ref.shape[0], ref.shape[-1]
    for s in range(bb):
        ref[s] = x[s * tt:(s + 1) * tt, :].T.reshape(ref.shape[1:])


def _qk_kernel(h_ref, w_ref, gsum_ref, gexp_ref, gain_ref, cos_ref, sin_ref, q_ref, k_ref):
    acc = _dot(h_ref[...], w_ref[...])
    ss = _exact_dot(acc * acc, gsum_ref[...], 2)
    rs = lax.rsqrt(ss * (1.0 / HEAD_DIM) + EPS)
    rs_full = _exact_dot(rs, gexp_ref[...], 2)
    xn = acc * rs_full * gain_ref[...]
    out = _rope(xn, cos_ref[...], sin_ref[...])
    nq = N_HEADS * HEAD_DIM
    q_ref[...] = (out[:, :nq] * (HEAD_DIM ** -0.5)).astype(q_ref.dtype)
    _store_time_minor(k_ref, out[:, nq:])


def _vm_kernel(h_ref, w_ref, cos_ref, sin_ref, dtb_ref, v_ref, qi_ref, misc_ref, kit_ref):
    acc = _dot(h_ref[...], w_ref[...])
    cos, sin = cos_ref[...], sin_ref[...]
    _store_time_minor(v_ref, acc[:, :KV_W])
    qi_ref[...] = _rope(acc[:, KV_W:KV_W + QI_W], cos, sin).astype(qi_ref.dtype)
    m = acc[:, KV_W + QI_W:]
    lane = lax.broadcasted_iota(I32, m.shape, 1)
    roped = _rope(m, cos, sin)
    dt = jax.nn.softplus(m + dtb_ref[...])
    misc_ref[...] = jnp.where(lane < MISC_WI, roped,
                              jnp.where(lane < MISC_DT, m * WI_SCALE,
                                        jnp.where(lane < MISC_DT + SSM_HEADS, dt, 0.0)))
    _store_time_minor(kit_ref, roped[:, :IDX_DIM])


def _table_spec(tab_rows, tm):
    nblk = tab_rows // tm
    return pl.BlockSpec((tm, LANES), lambda i: (i % nblk, 0))


def _time_minor_spec(b, t, *mid):
    bb, tt = _row_blocking(b, t)
    per_seq = t // tt
    zeros = (0,) * len(mid)
    return pl.BlockSpec((bb, *mid, tt), lambda i: (i // per_seq, *zeros, i % per_seq))


def _proj_qk(h2d, w_qk, gsum, gexp, gain, cos_tab, sin_tab, b, t):
    m, k = h2d.shape
    tm = math.prod(_row_blocking(b, t))
    nq = N_HEADS * HEAD_DIM
    const = lambda i: (0, 0)
    return pl.pallas_call(
        _qk_kernel,
        grid=(m // tm,),
        in_specs=[
            pl.BlockSpec((tm, k), lambda i: (i, 0)),
            pl.BlockSpec((k, QK_W), const),
            pl.BlockSpec((2 * QK_W, LANES), const),
            pl.BlockSpec((2 * LANES, QK_W), const),
            pl.BlockSpec((1, QK_W), const),
            _table_spec(cos_tab.shape[0], tm),
            _table_spec(sin_tab.shape[0], tm),
        ],
        out_specs=[pl.BlockSpec((tm, nq), lambda i: (i, 0)), _time_minor_spec(b, t, N_KV_HEADS, HEAD_DIM)],
        out_shape=[jax.ShapeDtypeStruct((m, nq), BF16), jax.ShapeDtypeStruct((b, N_KV_HEADS, HEAD_DIM, t), F32)],
        compiler_params=_cparams("parallel"),
        name="proj_qk",
    )(h2d, w_qk, gsum, gexp, gain, cos_tab, sin_tab)


def _proj_vm(h2d, w_vm, cos_tab, sin_tab, dtb, b, t):
    m, k = h2d.shape
    tm = math.prod(_row_blocking(b, t))
    wn = w_vm.shape[1]
    const = lambda i: (0, 0)
    return pl.pallas_call(
        _vm_kernel,
        grid=(m // tm,),
        in_specs=[
            pl.BlockSpec((tm, k), lambda i: (i, 0)),
            pl.BlockSpec((k, wn), const),
            _table_spec(cos_tab.shape[0], tm),
            _table_spec(sin_tab.shape[0], tm),
            pl.BlockSpec((1, LANES), const),
        ],
        out_specs=[_time_minor_spec(b, t, N_KV_HEADS, HEAD_DIM), pl.BlockSpec((tm, QI_W), lambda i: (i, 0)),
                   pl.BlockSpec((tm, LANES), lambda i: (i, 0)), _time_minor_spec(b, t, IDX_DIM)],
        out_shape=[jax.ShapeDtypeStruct((b, N_KV_HEADS, HEAD_DIM, t), F32), jax.ShapeDtypeStruct((m, QI_W), BF16),
                   jax.ShapeDtypeStruct((m, LANES), F32), jax.ShapeDtypeStruct((b, IDX_DIM, t), F32)],
        compiler_params=_cparams("parallel"),
        name="proj_vm",
    )(h2d, w_vm, cos_tab, sin_tab, dtb)


def _attn_kernel(*refs, tq, kb, gp, topk, past, t_new, has_cache):
    n_in = 9 if has_cache else 6
    q_ref, qi_ref, wi_ref, k_ref, v_ref, kit_ref = refs[:6]
    o_ref = refs[n_in]
    kbf, vbf, kibf, sc_scr, keep_scr, qis, qs, mb_scr, acc_scr, s_scr = refs[n_in + 1:n_in + 11]
    if has_cache:
        ck_ref, cv_ref, cki_ref = refs[6:9]
        ckb, cvb, ckib = refs[n_in + 11:]
    j = pl.program_id(1)
    kbn = min(kb, t_new)
    n_cache_blocks = past // kb

    def ones_row(n):
        return jnp.where(lax.broadcasted_iota(I32, (HEAD_DIM, n), 0) == 0, 1.0, 0.0).astype(BF16)

    def stage(k_src, v_src, ki_src, k_dst, v_dst, ki_dst, n):
        for g in range(N_KV_HEADS):
            k_dst[g] = k_src[0, g].astype(BF16)
            v_dst[g, 0:HEAD_DIM, :] = v_src[0, g].astype(BF16)
            v_dst[g, HEAD_DIM:2 * HEAD_DIM, :] = ones_row(n)
        ki_dst[...] = ki_src[0].T.astype(BF16)

    @pl.when(j == 0)
    def _():
        stage(k_ref, v_ref, kit_ref, kbf, vbf, kibf, t_new)
        if has_cache:
            stage(ck_ref, cv_ref, cki_ref, ckb, cvb, ckib, past)

    for h in range(IDX_HEADS):
        qis[h * tq:(h + 1) * tq, :] = qi_ref[0, :, h * IDX_DIM:(h + 1) * IDX_DIM]
    for h in range(N_HEADS):
        qs[h * tq:(h + 1) * tq, :] = q_ref[0, :, h * HEAD_DIM:(h + 1) * HEAD_DIM]
    wi_t = wi_ref[0, :, MISC_WI:MISC_WI + IDX_HEADS].T

    qpos = past + j * tq + lax.broadcasted_iota(I32, (1, tq), 1)
    limit = (qpos // CHUNK + 1) * CHUNK
    n_new_blocks = (j * tq + tq + kbn - 1) // kbn

    def by_pairs(fn, n_blocks, width, init):
        def pair(i, c):
            off = pl.multiple_of(i * (2 * width), 2 * width)
            return fn(pl.multiple_of(off + width, width), fn(off, c))
        c = lax.fori_loop(0, n_blocks // 2, pair, init)
        if isinstance(n_blocks, int):
            return fn((n_blocks - 1) * width, c) if n_blocks % 2 else c
        last = pl.multiple_of((n_blocks - 1) * width, width)
        return lax.cond(n_blocks % 2 == 1, lambda c: fn(last, c), lambda c: c, c)

    def over_cache(fn, init):
        if not has_cache:
            return init
        return by_pairs(fn, n_cache_blocks, kb, init)

    def over_new(fn, init):
        if t_new <= kb:
            return fn(0, init)
        return by_pairs(fn, n_new_blocks, kbn, init)

    def lanes_at(off):
        return past + off if isinstance(off, int) else pl.multiple_of(past + off, LANES)

    def score_block(ki_blk, kpos0, width):
        lg = _dot_nt(ki_blk, qis[...])
        sc = jnp.zeros((width, tq), F32)
        for h in range(IDX_HEADS):
            sc = sc + wi_t[h:h + 1, :] * jnp.maximum(lg[:, h * tq:(h + 1) * tq], 0.0)
        kpos = kpos0 + lax.broadcasted_iota(I32, (width, 1), 0)
        sc_scr[pl.ds(kpos0, width), :] = jnp.where(kpos < limit, sc, -jnp.inf)

    def p1c(off, c):
        score_block(ckib[pl.ds(off, kb), :], off, kb)
        return c

    def p1n(off, c):
        score_block(kibf[pl.ds(off, kbn), :], lanes_at(off), kbn)
        return c

    over_cache(p1c, 0)
    over_new(p1n, 0)

    def key_to_float(c):
        return pltpu.bitcast(jnp.where(c >= 0, c, c ^ 0x7FFFFFFF), F32)

    def count(cmp, cand):
        def cnt(row_off, width, acc):
            hit = jnp.where(cmp(sc_scr[pl.ds(row_off, width), :], cand), 1.0, 0.0)
            for r in range(width // COUNT_ROWS):
                acc = acc + hit[r * COUNT_ROWS:(r + 1) * COUNT_ROWS]
            return acc
        acc = jnp.zeros((COUNT_ROWS, tq), F32)
        acc = over_cache(lambda off, a: cnt(off, kb, a), acc)
        acc = over_new(lambda off, a: cnt(lanes_at(off), kbn, a), acc)
        return jnp.sum(acc, axis=0, keepdims=True)

    def bit_step(it, prefix):
        bit = jnp.left_shift(jnp.int32(1), 31 - it)
        cand = key_to_float((prefix | bit) ^ INT_MIN)
        return jnp.where(count(jnp.greater_equal, cand) >= float(topk), prefix | bit, prefix)

    prefix = lax.fori_loop(0, 32, bit_step, jnp.zeros((1, tq), I32))
    thr = key_to_float(prefix ^ INT_MIN)
    flt_max = float(jnp.finfo(F32).max)
    thr = jnp.where(thr >= -flt_max, thr, -flt_max)

    n_ge = count(jnp.greater_equal, thr)
    has_ties = jnp.max(n_ge) > float(topk)

    def keep_ranked(row_off, width, need, seen):
        sc = sc_scr[pl.ds(row_off, width), :]
        tie = sc == thr
        tie_f = jnp.where(tie, 1.0, 0.0)
        below = lax.broadcasted_iota(I32, (width, width), 1) <= lax.broadcasted_iota(I32, (width, width), 0)
        rank = seen + _dot(jnp.where(below, 1.0, 0.0).astype(BF16), tie_f.astype(BF16))
        keep = (sc > thr) | (tie & (rank <= need))
        keep_scr[pl.ds(row_off, width), :] = jnp.where(keep, 1.0, 0.0).astype(BF16)
        return seen + jnp.sum(tie_f, axis=0, keepdims=True)

    def keep_all_ties(row_off, width, c):
        sc = sc_scr[pl.ds(row_off, width), :]
        keep_scr[pl.ds(row_off, width), :] = jnp.where(sc >= thr, 1.0, 0.0).astype(BF16)
        return c

    def with_ties():
        need = float(topk) - count(jnp.greater, thr)
        seen = over_cache(lambda off, s: keep_ranked(off, kb, need, s), jnp.zeros((1, tq), F32))
        over_new(lambda off, s: keep_ranked(lanes_at(off), kbn, need, s), seen)

    def without_ties():
        over_cache(lambda off, c: keep_all_ties(off, kb, c), 0)
        over_new(lambda off, c: keep_all_ties(lanes_at(off), kbn, c), 0)

    lax.cond(has_ties, with_ties, without_ties)

    rows = Q_PER_KV * tq

    eye = jnp.where(lax.broadcasted_iota(I32, (tq, tq), 0) == lax.broadcasted_iota(I32, (tq, tq), 1), 1.0, 0.0).astype(BF16)

    def pass_a(g0, k_of, lane_off, width):
        keep_q = _dot_nt(eye, keep_scr[pl.ds(lane_off, width), :])
        b = jnp.tile(jnp.where(keep_q > 0.5, 0.0, NEG_BIG), (Q_PER_KV, 1))
        for gi in range(gp):
            g = g0 + gi
            s = _dot(qs[g * rows:(g + 1) * rows, :], k_of(g)) + b
            s_scr[gi, :, pl.ds(lane_off, width)] = s
            m = mb_scr[gi]
            if width % LANES:
                m = jnp.maximum(m, jnp.max(s, axis=1, keepdims=True))
            else:
                for c in range(width // LANES):
                    m = jnp.maximum(m, s[:, c * LANES:(c + 1) * LANES])
            mb_scr[gi] = m

    def pass_b(g0, v_of, lane_off, width):
        for gi in range(gp):
            m = mb_scr[gi]
            m = m[:, :width] if width < LANES else jnp.tile(m, (1, width // LANES))
            p = jnp.exp(s_scr[gi, :, pl.ds(lane_off, width)] - m)
            acc_scr[gi] += _dot_nt(p.astype(BF16), v_of(g0 + gi))

    def cache_k(off):
        return lambda g: ckb[g, :, pl.ds(off, kb)]

    def cache_v(off):
        return lambda g: cvb[g, :, pl.ds(off, kb)]

    def new_k(off):
        return lambda g: kbf[g, :, pl.ds(off, kbn)]

    def new_v(off):
        return lambda g: vbf[g, :, pl.ds(off, kbn)]

    def run(fn, cache_args, new_args):
        def on_cache(off, c):
            fn(*[a(off) for a in cache_args], off, kb)
            return c

        def on_new(off, c):
            fn(*[a(off) for a in new_args], lanes_at(off), kbn)
            return c

        over_cache(on_cache, 0)
        over_new(on_new, 0)

    for g0 in range(0, N_KV_HEADS, gp):
        mb_scr[...] = jnp.full(mb_scr.shape, NEG_BIG, F32)
        run(functools.partial(pass_a, g0), [cache_k], [new_k])
        for gi in range(gp):
            mb_scr[gi] = jnp.broadcast_to(jnp.max(mb_scr[gi], axis=1, keepdims=True), (rows, LANES))
        acc_scr[...] = jnp.zeros(acc_scr.shape, F32)
        run(functools.partial(pass_b, g0), [cache_v], [new_v])

        for gi in range(gp):
            a = acc_scr[gi]
            out = a[:, :HEAD_DIM] / a[:, HEAD_DIM:HEAD_DIM + 1]
            for r in range(Q_PER_KV):
                hh = (g0 + gi) * Q_PER_KV + r
                o_ref[0, :, hh * HEAD_DIM:(hh + 1) * HEAD_DIM] = out[r * tq:(r + 1) * tq].astype(o_ref.dtype)


def _attention(q, qi, misc, kt, vt, kit, cache=None):
    b, t, _ = q.shape
    has_cache = cache is not None
    past = cache[2].shape[2] if has_cache else 0
    n_keys = past + t
    topk = min(TOPK_MAX, n_keys // 4)
    tq = min(t, ATTN_TQ)
    kb = 256
    rows = Q_PER_KV * tq
    key_w = past + -(-t // LANES) * LANES
    gp = N_KV_HEADS
    while gp > 1 and gp * rows * key_w * 4 > ATTN_LOGITS_BYTES:
        gp //= 2
    qtile = lambda w: pl.BlockSpec((1, tq, w), lambda i, j: (i, j, 0))
    heads = lambda n: pl.BlockSpec((1, N_KV_HEADS, HEAD_DIM, n), lambda i, j: (i, 0, 0, 0))
    idx = lambda n: pl.BlockSpec((1, IDX_DIM, n), lambda i, j: (i, 0, 0))
    in_specs = [qtile(N_HEADS * HEAD_DIM), qtile(QI_W), qtile(LANES), heads(t), heads(t), idx(t)]
    args = [q, qi, misc, kt, vt, kit]
    if has_cache:
        in_specs += [heads(past), heads(past), idx(past)]
        args += list(cache)
    kern = functools.partial(_attn_kernel, tq=tq, kb=kb, gp=gp, topk=topk, past=past, t_new=t, has_cache=has_cache)
    return pl.pallas_call(
        kern,
        grid=(b, t // tq),
        in_specs=in_specs,
        out_specs=qtile(N_HEADS * HEAD_DIM),
        out_shape=jax.ShapeDtypeStruct((b, t, N_HEADS * HEAD_DIM), BF16),
        scratch_shapes=[
            pltpu.VMEM((N_KV_HEADS, HEAD_DIM, t), BF16),
            pltpu.VMEM((N_KV_HEADS, 2 * HEAD_DIM, t), BF16),
            pltpu.VMEM((t, IDX_DIM), BF16),
            pltpu.VMEM((n_keys, tq), F32),
            pltpu.VMEM((n_keys, tq), BF16),
            pltpu.VMEM((IDX_HEADS * tq, IDX_DIM), BF16),
            pltpu.VMEM((N_HEADS * tq, HEAD_DIM), BF16),
            pltpu.VMEM((gp, rows, LANES), F32),
            pltpu.VMEM((gp, rows, 2 * HEAD_DIM), F32),
            pltpu.VMEM((gp, rows, key_w), F32),
        ] + ([
            pltpu.VMEM((N_KV_HEADS, HEAD_DIM, past), BF16),
            pltpu.VMEM((N_KV_HEADS, 2 * HEAD_DIM, past), BF16),
            pltpu.VMEM((past, IDX_DIM), BF16),
        ] if has_cache else []),
        compiler_params=_cparams("parallel", "arbitrary"),
        name="attn_cache" if has_cache else "attn_prompt",
    )(*args)


def _ssd_kernel(xs_ref, bc_ref, misc_ref, zs_ref, h0_ref, alog_ref, dskip_ref, gnorm_ref, expand_ref, y_ref, hout_ref):
    c = pl.program_id(1)
    L = CHUNK

    @pl.when(c == 0)
    def _():
        hout_ref[...] = h0_ref[...]

    a = -jnp.exp(alog_ref[...])
    ri = lax.broadcasted_iota(I32, (L, 3 * L), 0)
    ci = lax.broadcasted_iota(I32, (L, 3 * L), 1) % L
    tri3 = jnp.where(ri >= ci, 1.0, 0.0).astype(BF16)
    ones3 = jnp.ones((L, 3 * L), BF16)
    expand3 = expand_ref[...]
    row_i = lax.broadcasted_iota(I32, (L, D_INNER), 0)
    lane_j = lax.broadcasted_iota(I32, (L, D_INNER), 1) % SSM_HEAD_DIM
    eye = row_i == lane_j
    causal = row_i >= lane_j
    blk_r = lax.broadcasted_iota(I32, (GROUP_W, GROUP_W), 0) // SSM_HEAD_DIM
    blk_c = lax.broadcasted_iota(I32, (GROUP_W, GROUP_W), 1) // SSM_HEAD_DIM
    same_head = blk_r == blk_c

    for sub in range(xs_ref.shape[1] // L):
        _ssd_chunk(slice(sub * L, (sub + 1) * L), xs_ref, bc_ref, misc_ref, zs_ref, dskip_ref, gnorm_ref, y_ref, hout_ref,
                   a, tri3, ones3, expand3, eye, causal, same_head)


def _ssd_chunk(rows, xs_ref, bc_ref, misc_ref, zs_ref, dskip_ref, gnorm_ref, y_ref, hout_ref,
               a, tri3, ones3, expand3, eye, causal, same_head):
    L = CHUNK
    dt = misc_ref[0, rows, :]
    acs = _exact_dot_left(tri3, dt * a, 3)
    col_acs = _exact_dot(acs, expand3, 3)
    col_dt = _exact_dot(dt, expand3[:2 * LANES], 2)
    row_acs = _exact_dot_left(ones3, jnp.where(eye, col_acs, 0.0), 3)
    row_dt = _exact_dot_left(ones3[:, :2 * L], jnp.where(eye, col_dt, 0.0), 2)
    a_last = acs[L - 1:L, MISC_DT:MISC_DT + SSM_HEADS]

    groups = range(SSM_GROUPS)
    gsl = [slice(g * GROUP_W, (g + 1) * GROUP_W) for g in groups]
    bgs = [bc_ref[0, rows, g * D_STATE:(g + 1) * D_STATE] for g in groups]
    cgs = [bc_ref[0, rows, (SSM_GROUPS + g) * D_STATE:(SSM_GROUPS + g + 1) * D_STATE] for g in groups]
    xgs = [xs_ref[0, rows, gs] for gs in gsl]
    hprevs = [hout_ref[0, g * HEADS_PER_GROUP:(g + 1) * HEADS_PER_GROUP].reshape(GROUP_W, D_STATE) for g in groups]
    cbs = [_dot_nt(cgs[g], jnp.tile(bgs[g], (HEADS_PER_GROUP, 1))) for g in groups]
    offs = [_dot_nt(cgs[g], hprevs[g].astype(BF16)) for g in groups]
    sts = []
    for g in groups:
        ce = col_acs[:, gsl[g]]
        w_state = jnp.exp(ce[L - 1:L, :] - ce) * col_dt[:, gsl[g]]
        sts.append(_dot_tn((xgs[g] * w_state).astype(BF16), bgs[g]))
    y_diags = []
    for g in groups:
        gs = gsl[g]
        mm = cbs[g] * jnp.exp(jnp.where(causal[:, gs], col_acs[:, gs] - row_acs[:, gs], -jnp.inf)) * row_dt[:, gs]
        xbd = jnp.where(same_head, jnp.tile(xgs[g].astype(BF16), (HEADS_PER_GROUP, 1)), 0.0).astype(BF16)
        y_diags.append(_dot(mm.astype(BF16), xbd))
    for g in groups:
        for r in range(HEADS_PER_GROUP):
            hh = g * HEADS_PER_GROUP + r
            decay = jnp.exp(a_last[:, hh:hh + 1])
            rs = slice(r * SSM_HEAD_DIM, (r + 1) * SSM_HEAD_DIM)
            hout_ref[0, hh] = decay * hprevs[g][rs] + sts[g][rs]
    for g in groups:
        gs = gsl[g]
        y_off = jnp.exp(col_acs[:, gs]) * offs[g]
        yt = (y_diags[g] + y_off + dskip_ref[:, gs] * xgs[g]) * zs_ref[0, rows, gs].astype(F32)
        ms = jnp.mean(yt * yt, axis=-1, keepdims=True)
        y_ref[0, rows, gs] = (yt * lax.rsqrt(ms + EPS) * gnorm_ref[:, gs]).astype(y_ref.dtype)


def _ssd(xs, bc, misc, zs, h0, a_log, dskip_full, g_norm, expand):
    b, t, _ = xs.shape
    rows = min(t, SSD_CHUNKS_PER_STEP * CHUNK)
    nc = t // rows
    chunk = lambda w: pl.BlockSpec((1, rows, w), lambda i, c: (i, c, 0))
    const2 = lambda r, w: pl.BlockSpec((r, w), lambda i, c: (0, 0))
    state = pl.BlockSpec((1, SSM_HEADS, SSM_HEAD_DIM, D_STATE), lambda i, c: (i, 0, 0, 0))
    return pl.pallas_call(
        _ssd_kernel,
        grid=(b, nc),
        in_specs=[
            chunk(D_INNER), chunk(2 * SSM_GROUPS * D_STATE), chunk(LANES), chunk(D_INNER), state,
            const2(1, LANES), const2(1, D_INNER), const2(1, D_INNER), const2(3 * LANES, D_INNER),
        ],
        out_specs=[chunk(D_INNER), state],
        out_shape=[jax.ShapeDtypeStruct((b, t, D_INNER), BF16),
                   jax.ShapeDtypeStruct((b, SSM_HEADS, SSM_HEAD_DIM, D_STATE), F32)],
        compiler_params=_cparams("parallel", "arbitrary"),
        name="ssd",
    )(xs, bc, misc, zs, h0, a_log, dskip_full, g_norm, expand)


def _merge_kernel(o_ref, y_ref, g_ref, x_ref, mod_ref, gain_ref, wa_ref, ws_ref, wo_ref, x1_ref, h2_ref):
    bb, tt, d = x_ref.shape
    rows = bb * tt
    o = o_ref[...].reshape(rows, -1)
    y = y_ref[...].reshape(rows, -1)
    gates = g_ref[...].reshape(rows, -1).astype(F32)
    mixed = gates[:, :d] * _dot(o, wa_ref[...]) + gates[:, d:] * _dot(y, ws_ref[...])
    out = _dot(mixed.astype(BF16), wo_ref[...]).reshape(bb, tt, d)
    mod = mod_ref[...]
    x1 = x_ref[...] + mod[:, 2:3, :] * out
    x1_ref[...] = x1
    h2_ref[...] = _modulated_norm(x1, mod, gain_ref[...], 3, 4).astype(h2_ref.dtype)


def _merge(o_attn, y, gates, x, mod3, mod_off, gain, wa, ws, wo):
    b, t, d = x.shape
    bb, tt = _row_blocking(b, t)
    assert mod_off % bb == 0, (mod_off, bb)
    off = mod_off // bb
    tile = lambda w: pl.BlockSpec((bb, tt, w), lambda i, j: (i, j, 0))
    const = lambda r, w: pl.BlockSpec((r, w), lambda i, j: (0, 0))
    return pl.pallas_call(
        _merge_kernel,
        grid=(b // bb, t // tt),
        in_specs=[
            tile(N_HEADS * HEAD_DIM), tile(D_INNER), tile(2 * D_MODEL), tile(d),
            pl.BlockSpec((bb, 6, d), lambda i, j: (i + off, 0, 0)),
            pl.BlockSpec((1, 1, d), lambda i, j: (0, 0, 0)),
            const(N_HEADS * HEAD_DIM, d), const(D_INNER, d), const(d, d),
        ],
        out_specs=[tile(d), tile(d)],
        out_shape=[jax.ShapeDtypeStruct((b, t, d), F32), jax.ShapeDtypeStruct((b, t, d), BF16)],
        compiler_params=_cparams("parallel", "parallel"),
        name="merge",
    )(o_attn, y, gates, x, mod3, gain.reshape(1, 1, d), wa, ws, wo)


def _ffn_kernel(h2_ref, x1_ref, mod_ref, wg_ref, wu_ref, wd_ref, o_ref, acc_ref):
    f = pl.program_id(2)
    bb, tt, d = x1_ref.shape
    h2 = h2_ref[...].reshape(bb * tt, d)
    act = _silu(_dot(h2, wg_ref[...])) * _dot(h2, wu_ref[...])
    part = _dot(act.astype(BF16), wd_ref[...])

    @pl.when(f == 0)
    def _():
        acc_ref[...] = part

    @pl.when(f > 0)
    def _():
        acc_ref[...] += part

    @pl.when(f == pl.num_programs(2) - 1)
    def _():
        o_ref[...] = x1_ref[...] + mod_ref[...][:, 5:6, :] * acc_ref[...].reshape(bb, tt, d)


def _ffn(h2, x1, mod3, mod_off, w_gu, w_down):
    b, t, d = x1.shape
    bb, tt = _row_blocking(b, t)
    assert mod_off % bb == 0, (mod_off, bb)
    off = mod_off // bb
    nf = 2
    tf = D_FF // nf
    tile = pl.BlockSpec((bb, tt, d), lambda i, j, f: (i, j, 0))
    return pl.pallas_call(
        _ffn_kernel,
        grid=(b // bb, t // tt, nf),
        in_specs=[
            tile, tile,
            pl.BlockSpec((bb, 6, d), lambda i, j, f: (i + off, 0, 0)),
            pl.BlockSpec((d, tf), lambda i, j, f: (0, f)),
            pl.BlockSpec((d, tf), lambda i, j, f: (0, nf + f)),
            pl.BlockSpec((tf, d), lambda i, j, f: (f, 0)),
        ],
        out_specs=tile,
        out_shape=jax.ShapeDtypeStruct((b, t, d), F32),
        scratch_shapes=[pltpu.VMEM((bb * tt, d), F32)],
        compiler_params=_cparams("parallel", "parallel", "arbitrary"),
        name="ffn",
    )(h2, x1, mod3, w_gu, w_gu, w_down)


def _rope_tables(t, past, tm):
    half = HEAD_DIM // 2
    inv = ROPE_THETA ** (-jnp.arange(half, dtype=F32) / half)
    ang = (past + jnp.arange(t)).astype(F32)[:, None] * inv[None, :]
    cos, sin = jnp.cos(ang), jnp.sin(ang)
    cos_t = jnp.concatenate([cos, cos, cos, cos], axis=1)
    sin_t = jnp.concatenate([-sin, sin, -sin, sin], axis=1)
    if t < tm:
        cos_t, sin_t = jnp.tile(cos_t, (tm // t, 1)), jnp.tile(sin_t, (tm // t, 1))
    return cos_t, sin_t


def _group_step(x, mod3, mod_off, cache, conv_state, ssm_state, p):
    b, t, d = x.shape
    m = b * t
    tm = min(512, m)
    past = cache[2].shape[2] if cache is not None else 0
    h = _hnorm(x, mod3, mod_off, p["g_norm_mix"]).reshape(m, d)
    cos_t, sin_t = _rope_tables(t, past, tm)
    q, kt = _proj_qk(h, p["w_qk"], p["gsum"], p["gexp"], p["qk_gain"], cos_t, sin_t, b, t)
    vt, qi, misc, kit = _proj_vm(h, p["w_vm"], cos_t, sin_t, p["dtb"], b, t)
    zs = _proj_act(h, p["w_z"], "silu", BF16, tm, D_INNER)
    gates = _proj_act(h, p["w_g"], "sigmoid", BF16, tm, 2 * D_MODEL)
    half = CONV_CH // 2
    xs, tail_x = _proj_conv(h, p["w_xs"], conv_state[:, :, :half], p["w_conv"][:, :half],
                            p["b_conv"][:, :half], b, t, F32)
    bc, tail_bc = _proj_conv(h, p["w_bc"], conv_state[:, :, half:], p["w_conv"][:, half:],
                             p["b_conv"][:, half:], b, t, BF16)

    r3 = lambda a: a.reshape(b, t, a.shape[-1])
    o_attn = _attention(r3(q), r3(qi), r3(misc), kt, vt, kit, cache)
    y, h_last = _ssd(r3(xs), r3(bc), r3(misc), r3(zs), ssm_state, p["a_log"], p["dskip_full"], p["g_ssm_norm"],
                     p["expand"])
    x1, h2 = _merge(o_attn, y, r3(gates), x, mod3, mod_off, p["g_norm_ffn"], p["w_ba"], p["w_bs"], p["w_out"])
    out = _ffn(h2, x1, mod3, mod_off, p["w_gu"], p["w_down"])
    conv_new = jnp.concatenate([tail_x, tail_bc], axis=2)[:, -(CONV_W - 1):]
    token_major = lambda a: jnp.moveaxis(a, -1, 1)
    return out, token_major(kt), token_major(vt), token_major(kit), conv_new, h_last


def _layer_params(l, w_in, g_q, g_k, g_norm_mix, g_norm_ffn, w_conv, b_conv, dt_bias, a_log, d_skip, g_ssm_norm,
                  w_branch_attn, w_branch_ssm, w_out, w_gate_up, w_down):
    sizes = (N_HEADS * HEAD_DIM, KV_W, KV_W, QI_W, IDX_DIM, IDX_HEADS, D_INNER, CONV_CH, SSM_HEADS, 2 * D_MODEL)
    offs = [0]
    for s in sizes:
        offs.append(offs[-1] + s)
    col = lambda i: w_in[l][:, offs[i]:offs[i + 1]]
    pad = LANES - (IDX_DIM + IDX_HEADS + SSM_HEADS)
    w_vm = jnp.concatenate([col(2), col(3), col(4), col(5), col(8), jnp.zeros((D_MODEL, pad), F32)], axis=1)
    head_of = jnp.arange(QK_W) // HEAD_DIM
    gsum = (head_of[:, None] == jnp.arange(LANES)[None, :]).astype(BF16)
    dtb = jnp.zeros((1, LANES), F32).at[0, MISC_DT:MISC_DT + SSM_HEADS].set(dt_bias[l])
    expand = ((jnp.arange(LANES) - MISC_DT)[:, None] == (jnp.arange(D_INNER) // SSM_HEAD_DIM)[None, :]).astype(BF16)
    a_log_lanes = jnp.zeros((1, LANES), F32).at[0, MISC_DT:MISC_DT + SSM_HEADS].set(a_log[l])
    return dict(
        w_qk=w_in[l][:, :QK_W].astype(BF16), w_vm=w_vm.astype(BF16), w_z=col(6).astype(BF16),
        w_xs=col(7)[:, :CONV_CH // 2].astype(BF16), w_bc=col(7)[:, CONV_CH // 2:].astype(BF16), w_g=col(9).astype(BF16),
        gsum=jnp.tile(gsum, (2, 1)), gexp=jnp.tile(gsum.T, (2, 1)),
        qk_gain=jnp.concatenate([jnp.tile(g_q[l], N_HEADS), jnp.tile(g_k[l], N_KV_HEADS)]).reshape(1, QK_W),
        dtb=dtb, expand=jnp.tile(expand, (3, 1)),
        g_norm_mix=g_norm_mix[l], g_norm_ffn=g_norm_ffn[l],
        w_conv=w_conv[l], b_conv=b_conv[l].reshape(1, CONV_CH), a_log=a_log_lanes,
        dskip_full=jnp.repeat(d_skip[l], SSM_HEAD_DIM).reshape(1, D_INNER), g_ssm_norm=g_ssm_norm[l].reshape(1, D_INNER),
        w_ba=w_branch_attn[l].astype(BF16), w_bs=w_branch_ssm[l].astype(BF16), w_out=w_out[l].astype(BF16),
        w_gu=w_gate_up[l].astype(BF16), w_down=w_down[l].astype(BF16),
    )


def kernel(x_prompt, x_sample, cache_k, cache_v, cache_ki, state_conv, state_ssm, c_prompt, c_sample, w_ada, b_ada, g_norm_mix, g_norm_ffn, w_in, g_q, g_k, w_conv, b_conv, dt_bias, a_log, d_skip, g_ssm_norm, w_branch_attn, w_branch_ssm, w_out, w_gate_up, w_down):
    depth = w_in.shape[0]
    bp, bs = x_prompt.shape[0], x_sample.shape[0]
    past = cache_k.shape[2]
    y_p, y_s = x_prompt, x_sample
    c_all = jnp.concatenate([c_prompt, c_sample], axis=0)
    new_p = [[] for _ in range(5)]
    new_s = [[] for _ in range(5)]
    for l in range(depth):
        p = _layer_params(l, w_in, g_q, g_k, g_norm_mix, g_norm_ffn, w_conv, b_conv, dt_bias, a_log, d_skip,
                          g_ssm_norm, w_branch_attn, w_branch_ssm, w_out, w_gate_up, w_down)
        mod3 = _ada_mod(c_all, w_ada[l], b_ada[l]).reshape(bp + bs, 6, D_MODEL)
        zero_conv = jnp.zeros((bp, CONV_W - 1, CONV_CH), F32)
        zero_ssm = jnp.zeros((bp, SSM_HEADS, SSM_HEAD_DIM, D_STATE), F32)
        y_p, *st_p = _group_step(y_p, mod3, 0, None, zero_conv, zero_ssm, p)
        time_minor = lambda a: jnp.moveaxis(a, 1, -1)
        cache = (time_minor(cache_k[l]), time_minor(cache_v[l]), time_minor(cache_ki[l]))
        y_s, *st_s = _group_step(y_s, mod3, bp, cache, state_conv[l], state_ssm[l], p)
        for acc, a in zip(new_p, st_p):
            acc.append(a)
        for acc, a in zip(new_s, st_s):
            acc.append(a)
    return (y_p, y_s, *[jnp.stack(a) for a in new_p], *[jnp.stack(a) for a in new_s])
```

```python
import functools
import math

import jax
import jax.numpy as jnp
from jax import lax
from jax.experimental import pallas as pl
from jax.experimental.pallas import tpu as pltpu

F32, BF16, I32 = jnp.float32, jnp.bfloat16, jnp.int32

D_MODEL = 1024
CHUNK = 64
N_HEADS = 16
HEAD_DIM = 64
N_KV_HEADS = 4
Q_PER_KV = N_HEADS // N_KV_HEADS
IDX_HEADS = 8
IDX_DIM = 64
TOPK_MAX = 256
ROPE_THETA = 10000.0
D_INNER = 2 * D_MODEL
SSM_HEAD_DIM = 64
SSM_HEADS = D_INNER // SSM_HEAD_DIM
SSM_GROUPS = 8
HEADS_PER_GROUP = SSM_HEADS // SSM_GROUPS
GROUP_W = HEADS_PER_GROUP * SSM_HEAD_DIM
D_STATE = 128
CONV_W = 4
CONV_CH = D_INNER + 2 * SSM_GROUPS * D_STATE
D_FF = -(-8 * D_MODEL // (3 * 256)) * 256
EPS = 1e-6
QK_W = (N_HEADS + N_KV_HEADS) * HEAD_DIM
KV_W = N_KV_HEADS * HEAD_DIM
QI_W = IDX_HEADS * IDX_DIM
SSM_HEADS_N = SSM_HEADS

LANES = 128
MISC_WI = IDX_DIM
MISC_DT = IDX_DIM + IDX_HEADS
WI_SCALE = (IDX_HEADS ** -0.5) * (IDX_DIM ** -0.5)
INT_MIN = -(2 ** 31)
NEG_BIG = -1e30
COUNT_ROWS = 64
SSD_CHUNKS_PER_STEP = 4
ATTN_TQ = 256
ATTN_LOGITS_BYTES = 16 * 1024 * 1024
VMEM_LIMIT = 56 * 1024 * 1024


def _cparams(*sem):
    return pltpu.CompilerParams(dimension_semantics=sem, vmem_limit_bytes=VMEM_LIMIT)


def _silu(x):
    h = 0.5 * x
    return h + h * jnp.tanh(h)


def _split_bf16(x, n):
    pieces = []
    r = x
    for _ in range(n):
        p = r.astype(BF16)
        pieces.append(p)
        r = r - p.astype(F32)
    return pieces


def _dot(a, b):
    return jnp.dot(a, b, preferred_element_type=F32)


def _dot_nt(a, b):
    return lax.dot_general(a, b, (((1,), (1,)), ((), ())), preferred_element_type=F32)


def _dot_tn(a, b):
    return lax.dot_general(a, b, (((0,), (0,)), ((), ())), preferred_element_type=F32)


def _exact_dot(x, m_stacked, n):
    return _dot(jnp.concatenate(_split_bf16(x, n), axis=1), m_stacked)


def _exact_dot_left(m_tiled, x, n):
    return _dot(m_tiled, jnp.concatenate(_split_bf16(x, n), axis=0))


def _rotate_half(x):
    w = x.shape[-1]
    lane = lax.broadcasted_iota(I32, x.shape, x.ndim - 1)
    first = (lane % HEAD_DIM) < (HEAD_DIM // 2)
    return jnp.where(first, pltpu.roll(x, w - HEAD_DIM // 2, x.ndim - 1), pltpu.roll(x, HEAD_DIM // 2, x.ndim - 1))


def _rope(x, cos, sin):
    reps = x.shape[-1] // LANES
    if reps > 1:
        cos = jnp.tile(cos, (1, reps))
        sin = jnp.tile(sin, (1, reps))
    return x * cos + _rotate_half(x) * sin


def _mod_kernel(c_ref, w_ref, b_ref, o_ref):
    s = _silu(c_ref[...])
    o_ref[...] = _dot(s.astype(BF16), w_ref[...].astype(BF16)) + b_ref[...]


def _ada_mod(c_all, w_ada, b_ada):
    bt = c_all.shape[0]
    n = w_ada.shape[1]
    tn = D_MODEL
    return pl.pallas_call(
        _mod_kernel,
        grid=(n // tn,),
        in_specs=[
            pl.BlockSpec((bt, D_MODEL), lambda j: (0, 0)),
            pl.BlockSpec((D_MODEL, tn), lambda j: (0, j)),
            pl.BlockSpec((1, tn), lambda j: (0, j)),
        ],
        out_specs=pl.BlockSpec((bt, tn), lambda j: (0, j)),
        out_shape=jax.ShapeDtypeStruct((bt, n), F32),
        compiler_params=_cparams("parallel"),
        name="ada_mod",
    )(c_all, w_ada, b_ada.reshape(1, n))


def _modulated_norm(x, mod, gain, shift_idx, scale_idx):
    ms = jnp.mean(x * x, axis=-1, keepdims=True)
    xn = x * lax.rsqrt(ms + EPS)
    sh = mod[:, shift_idx:shift_idx + 1, :]
    sc = mod[:, scale_idx:scale_idx + 1, :]
    return xn * gain * (1.0 + sc) + sh


def _hnorm_kernel(x_ref, mod_ref, g_ref, o_ref):
    o_ref[...] = _modulated_norm(x_ref[...], mod_ref[...], g_ref[...], 0, 1).astype(o_ref.dtype)


def _row_blocking(b, t):
    tt = min(t, 512)
    bb = max(1, min(b, 512 // tt))
    assert t % tt == 0 and b % bb == 0 and tt % CHUNK == 0, (b, t)
    return bb, tt


def _hnorm(x, mod3, mod_off, gain):
    b, t, d = x.shape
    bb, tt = _row_blocking(b, t)
    assert mod_off % bb == 0, (mod_off, bb)
    off = mod_off // bb
    return pl.pallas_call(
        _hnorm_kernel,
        grid=(b // bb, t // tt),
        in_specs=[
            pl.BlockSpec((bb, tt, d), lambda i, j: (i, j, 0)),
            pl.BlockSpec((bb, 6, d), lambda i, j: (i + off, 0, 0)),
            pl.BlockSpec((1, 1, d), lambda i, j: (0, 0, 0)),
        ],
        out_specs=pl.BlockSpec((bb, tt, d), lambda i, j: (i, j, 0)),
        out_shape=jax.ShapeDtypeStruct((b, t, d), BF16),
        compiler_params=_cparams("parallel", "parallel"),
        name="hnorm",
    )(x, mod3, gain.reshape(1, 1, d))


def _proj_act_kernel(h_ref, w_ref, o_ref, *, act):
    acc = _dot(h_ref[...], w_ref[...])
    if act == "silu":
        acc = _silu(acc)
    elif act == "sigmoid":
        acc = jax.nn.sigmoid(acc)
    o_ref[...] = acc.astype(o_ref.dtype)


def _proj_act(h2d, w, act, out_dtype, tm, tn):
    m, k = h2d.shape
    n = w.shape[1]
    return pl.pallas_call(
        functools.partial(_proj_act_kernel, act=act),
        grid=(n // tn, m // tm),
        in_specs=[
            pl.BlockSpec((tm, k), lambda j, i: (i, 0)),
            pl.BlockSpec((k, tn), lambda j, i: (0, j)),
        ],
        out_specs=pl.BlockSpec((tm, tn), lambda j, i: (i, j)),
        out_shape=jax.ShapeDtypeStruct((m, n), out_dtype),
        compiler_params=_cparams("parallel", "parallel"),
        name="proj_" + act,
    )(h2d, w)


SUBLANES = 8
CONV_SLAB = 256


def _proj_conv_kernel(h_ref, w_ref, cst_ref, wconv_ref, bconv_ref, o_ref, tail_ref, buf, *, bb, tt, tiles_per_seq):
    i = pl.program_id(1)
    tail = CONV_W - 1
    tn = w_ref.shape[1]

    @pl.when(i % tiles_per_seq == 0)
    def _():
        for s in range(bb):
            buf[s, 0:SUBLANES - tail, :] = jnp.zeros((SUBLANES - tail, tn), F32)
            buf[s, SUBLANES - tail:SUBLANES, :] = cst_ref[s]

    h = h_ref[...]
    n_slabs = tn // CONV_SLAB
    slab = lambda c: slice(c * CONV_SLAB, (c + 1) * CONV_SLAB)
    acc_next = _dot(h, w_ref[:, slab(0)])
    for c in range(n_slabs):
        cs = slab(c)
        acc = acc_next
        if c + 1 < n_slabs:
            acc_next = _dot(h, w_ref[:, slab(c + 1)])
        for s in range(bb):
            buf[s, SUBLANES:SUBLANES + tt, cs] = acc[s * tt:(s + 1) * tt]
            xb = buf[s, :, cs]
            xc = bconv_ref[:, cs] + wconv_ref[tail:tail + 1, cs] * xb[SUBLANES:]
            for jj in range(tail):
                shifted = pltpu.roll(xb, tail - jj, 0)[SUBLANES:]
                xc = xc + wconv_ref[jj:jj + 1, cs] * shifted
            o_ref[s * tt:(s + 1) * tt, cs] = _silu(xc).astype(o_ref.dtype)
            last = buf[s, tt:tt + SUBLANES, cs]
            tail_ref[s, :, cs] = last
            buf[s, 0:SUBLANES, cs] = last


def _proj_conv(h2d, w, conv_state, w_conv, b_conv, b, t, out_dtype):
    m, k = h2d.shape
    n = w.shape[1]
    bb, tt = _row_blocking(b, t)
    tm = bb * tt
    tiles_per_seq = t // tt
    tn = min(n, 2048)
    kern = functools.partial(_proj_conv_kernel, bb=bb, tt=tt, tiles_per_seq=tiles_per_seq)
    seq_block = lambda rows: pl.BlockSpec((bb, rows, tn), lambda j, i: (i // tiles_per_seq, 0, j))
    return pl.pallas_call(
        kern,
        grid=(n // tn, m // tm),
        in_specs=[
            pl.BlockSpec((tm, k), lambda j, i: (i, 0)),
            pl.BlockSpec((k, tn), lambda j, i: (0, j)),
            seq_block(CONV_W - 1),
            pl.BlockSpec((CONV_W, tn), lambda j, i: (0, j)),
            pl.BlockSpec((1, tn), lambda j, i: (0, j)),
        ],
        out_specs=[pl.BlockSpec((tm, tn), lambda j, i: (i, j)), seq_block(SUBLANES)],
        out_shape=[jax.ShapeDtypeStruct((m, n), out_dtype), jax.ShapeDtypeStruct((b, SUBLANES, n), F32)],
        scratch_shapes=[pltpu.VMEM((bb, SUBLANES + tt, tn), F32)],
        compiler_params=_cparams("parallel", "arbitrary"),
        name="proj_conv",
    )(h2d, w, conv_state, w_conv, b_conv)


def _store_time_minor(ref, x):
    bb, tt = ref.shape[0], ref.shape[-1]
    for s in range(bb):
        ref[s] = x[s * tt:(s + 1) * tt, :].T.reshape(ref.shape[1:])


def _qk_kernel(h_ref, w_ref, gsum_ref, gexp_ref, gain_ref, cos_ref, sin_ref, q_ref, k_ref):
    acc = _dot(h_ref[...], w_ref[...])
    ss = _exact_dot(acc * acc, gsum_ref[...], 2)
    rs = lax.rsqrt(ss * (1.0 / HEAD_DIM) + EPS)
    rs_full = _exact_dot(rs, gexp_ref[...], 2)
    xn = acc * rs_full * gain_ref[...]
    out = _rope(xn, cos_ref[...], sin_ref[...])
    nq = N_HEADS * HEAD_DIM
    q_ref[...] = (out[:, :nq] * (HEAD_DIM ** -0.5)).astype(q_ref.dtype)
    _store_time_minor(k_ref, out[:, nq:])


def _vm_kernel(h_ref, w_ref, cos_ref, sin_ref, dtb_ref, v_ref, qi_ref, misc_ref, kit_ref):
    acc = _dot(h_ref[...], w_ref[...])
    cos, sin = cos_ref[...], sin_ref[...]
    _store_time_minor(v_ref, acc[:, :KV_W])
    qi_ref[...] = _rope(acc[:, KV_W:KV_W + QI_W], cos, sin).astype(qi_ref.dtype)
    m = acc[:, KV_W + QI_W:]
    lane = lax.broadcasted_iota(I32, m.shape, 1)
    roped = _rope(m, cos, sin)
    dt = jax.nn.softplus(m + dtb_ref[...])
    misc_ref[...] = jnp.where(lane < MISC_WI, roped,
                              jnp.where(lane < MISC_DT, m * WI_SCALE,
                                        jnp.where(lane < MISC_DT + SSM_HEADS, dt, 0.0)))
    _store_time_minor(kit_ref, roped[:, :IDX_DIM])


def _table_spec(tab_rows, tm):
    nblk = tab_rows // tm
    return pl.BlockSpec((tm, LANES), lambda i: (i % nblk, 0))


def _time_minor_spec(b, t, *mid):
    bb, tt = _row_blocking(b, t)
    per_seq = t // tt
    zeros = (0,) * len(mid)
    return pl.BlockSpec((bb, *mid, tt), lambda i: (i // per_seq, *zeros, i % per_seq))


def _proj_qk(h2d, w_qk, gsum, gexp, gain, cos_tab, sin_tab, b, t):
    m, k = h2d.shape
    tm = math.prod(_row_blocking(b, t))
    nq = N_HEADS * HEAD_DIM
    const = lambda i: (0, 0)
    return pl.pallas_call(
        _qk_kernel,
        grid=(m // tm,),
        in_specs=[
            pl.BlockSpec((tm, k), lambda i: (i, 0)),
            pl.BlockSpec((k, QK_W), const),
            pl.BlockSpec((2 * QK_W, LANES), const),
            pl.BlockSpec((2 * LANES, QK_W), const),
            pl.BlockSpec((1, QK_W), const),
            _table_spec(cos_tab.shape[0], tm),
            _table_spec(sin_tab.shape[0], tm),
        ],
        out_specs=[pl.BlockSpec((tm, nq), lambda i: (i, 0)), _time_minor_spec(b, t, N_KV_HEADS, HEAD_DIM)],
        out_shape=[jax.ShapeDtypeStruct((m, nq), BF16), jax.ShapeDtypeStruct((b, N_KV_HEADS, HEAD_DIM, t), F32)],
        compiler_params=_cparams("parallel"),
        name="proj_qk",
    )(h2d, w_qk, gsum, gexp, gain, cos_tab, sin_tab)


def _proj_vm(h2d, w_vm, cos_tab, sin_tab, dtb, b, t):
    m, k = h2d.shape
    tm = math.prod(_row_blocking(b, t))
    wn = w_vm.shape[1]
    const = lambda i: (0, 0)
    return pl.pallas_call(
        _vm_kernel,
        grid=(m // tm,),
        in_specs=[
            pl.BlockSpec((tm, k), lambda i: (i, 0)),
            pl.BlockSpec((k, wn), const),
            _table_spec(cos_tab.shape[0], tm),
            _table_spec(sin_tab.shape[0], tm),
            pl.BlockSpec((1, LANES), const),
        ],
        out_specs=[_time_minor_spec(b, t, N_KV_HEADS, HEAD_DIM), pl.BlockSpec((tm, QI_W), lambda i: (i, 0)),
                   pl.BlockSpec((tm, LANES), lambda i: (i, 0)), _time_minor_spec(b, t, IDX_DIM)],
        out_shape=[jax.ShapeDtypeStruct((b, N_KV_HEADS, HEAD_DIM, t), F32), jax.ShapeDtypeStruct((m, QI_W), BF16),
                   jax.ShapeDtypeStruct((m, LANES), F32), jax.ShapeDtypeStruct((b, IDX_DIM, t), F32)],
        compiler_params=_cparams("parallel"),
        name="proj_vm",
    )(h2d, w_vm, cos_tab, sin_tab, dtb)


def _attn_kernel(*refs, tq, kb, gp, topk, past, t_new, has_cache):
    n_in = 9 if has_cache else 6
    q_ref, qi_ref, wi_ref, k_ref, v_ref, kit_ref = refs[:6]
    o_ref = refs[n_in]
    kbf, vbf, kibf, sc_scr, keep_scr, qis, qs, mb_scr, acc_scr, s_scr = refs[n_in + 1:n_in + 11]
    if has_cache:
        ck_ref, cv_ref, cki_ref = refs[6:9]
        ckb, cvb, ckib = refs[n_in + 11:]
    j = pl.program_id(1)
    kbn = min(kb, t_new)
    n_cache_blocks = past // kb

    def ones_row(n):
        return jnp.where(lax.broadcasted_iota(I32, (HEAD_DIM, n), 0) == 0, 1.0, 0.0).astype(BF16)

    def stage(k_src, v_src, ki_src, k_dst, v_dst, ki_dst, n):
        for g in range(N_KV_HEADS):
            k_dst[g] = k_src[0, g].astype(BF16)
            v_dst[g, 0:HEAD_DIM, :] = v_src[0, g].astype(BF16)
            v_dst[g, HEAD_DIM:2 * HEAD_DIM, :] = ones_row(n)
        ki_dst[...] = ki_src[0].T.astype(BF16)

    @pl.when(j == 0)
    def _():
        stage(k_ref, v_ref, kit_ref, kbf, vbf, kibf, t_new)
        if has_cache:
            stage(ck_ref, cv_ref, cki_ref, ckb, cvb, ckib, past)

    for h in range(IDX_HEADS):
        qis[h * tq:(h + 1) * tq, :] = qi_ref[0, :, h * IDX_DIM:(h + 1) * IDX_DIM]
    for h in range(N_HEADS):
        qs[h * tq:(h + 1) * tq, :] = q_ref[0, :, h * HEAD_DIM:(h + 1) * HEAD_DIM]
    wi_t = wi_ref[0, :, MISC_WI:MISC_WI + IDX_HEADS].T

    qpos = past + j * tq + lax.broadcasted_iota(I32, (1, tq), 1)
    limit = (qpos // CHUNK + 1) * CHUNK
    n_new_blocks = (j * tq + tq + kbn - 1) // kbn

    def by_pairs(fn, n_blocks, width, init):
        def pair(i, c):
            off = pl.multiple_of(i * (2 * width), 2 * width)
            return fn(pl.multiple_of(off + width, width), fn(off, c))
        c = lax.fori_loop(0, n_blocks // 2, pair, init)
        if isinstance(n_blocks, int):
            return fn((n_blocks - 1) * width, c) if n_blocks % 2 else c
        last = pl.multiple_of((n_blocks - 1) * width, width)
        return lax.cond(n_blocks % 2 == 1, lambda c: fn(last, c), lambda c: c, c)

    def over_cache(fn, init):
        if not has_cache:
            return init
        return by_pairs(fn, n_cache_blocks, kb, init)

    def over_new(fn, init):
        if t_new <= kb:
            return fn(0, init)
        return by_pairs(fn, n_new_blocks, kbn, init)

    def lanes_at(off):
        return past + off if isinstance(off, int) else pl.multiple_of(past + off, LANES)

    def score_block(ki_blk, kpos0, width):
        lg = _dot_nt(ki_blk, qis[...])
        sc = jnp.zeros((width, tq), F32)
        for h in range(IDX_HEADS):
            sc = sc + wi_t[h:h + 1, :] * jnp.maximum(lg[:, h * tq:(h + 1) * tq], 0.0)
        kpos = kpos0 + lax.broadcasted_iota(I32, (width, 1), 0)
        sc_scr[pl.ds(kpos0, width), :] = jnp.where(kpos < limit, sc, -jnp.inf)

    def p1c(off, c):
        score_block(ckib[pl.ds(off, kb), :], off, kb)
        return c

    def p1n(off, c):
        score_block(kibf[pl.ds(off, kbn), :], lanes_at(off), kbn)
        return c

    over_cache(p1c, 0)
    over_new(p1n, 0)

    def key_to_float(c):
        return pltpu.bitcast(jnp.where(c >= 0, c, c ^ 0x7FFFFFFF), F32)

    def count(cmp, cand):
        def cnt(row_off, width, acc):
            hit = jnp.where(cmp(sc_scr[pl.ds(row_off, width), :], cand), 1.0, 0.0)
            for r in range(width // COUNT_ROWS):
                acc = acc + hit[r * COUNT_ROWS:(r + 1) * COUNT_ROWS]
            return acc
        acc = jnp.zeros((COUNT_ROWS, tq), F32)
        acc = over_cache(lambda off, a: cnt(off, kb, a), acc)
        acc = over_new(lambda off, a: cnt(lanes_at(off), kbn, a), acc)
        return jnp.sum(acc, axis=0, keepdims=True)

    def bit_step(it, prefix):
        bit = jnp.left_shift(jnp.int32(1), 31 - it)
        cand = key_to_float((prefix | bit) ^ INT_MIN)
        return jnp.where(count(jnp.greater_equal, cand) >= float(topk), prefix | bit, prefix)

    prefix = lax.fori_loop(0, 32, bit_step, jnp.zeros((1, tq), I32))
    thr = key_to_float(prefix ^ INT_MIN)
    flt_max = float(jnp.finfo(F32).max)
    thr = jnp.where(thr >= -flt_max, thr, -flt_max)

    n_ge = count(jnp.greater_equal, thr)
    has_ties = jnp.max(n_ge) > float(topk)

    def keep_ranked(row_off, width, need, seen):
        sc = sc_scr[pl.ds(row_off, width), :]
        tie = sc == thr
        tie_f = jnp.where(tie, 1.0, 0.0)
        below = lax.broadcasted_iota(I32, (width, width), 1) <= lax.broadcasted_iota(I32, (width, width), 0)
        rank = seen + _dot(jnp.where(below, 1.0, 0.0).astype(BF16), tie_f.astype(BF16))
        keep = (sc > thr) | (tie & (rank <= need))
        keep_scr[pl.ds(row_off, width), :] = jnp.where(keep, 1.0, 0.0).astype(BF16)
        return seen + jnp.sum(tie_f, axis=0, keepdims=True)

    def keep_all_ties(row_off, width, c):
        sc = sc_scr[pl.ds(row_off, width), :]
        keep_scr[pl.ds(row_off, width), :] = jnp.where(sc >= thr, 1.0, 0.0).astype(BF16)
        return c

    def with_ties():
        need = float(topk) - count(jnp.greater, thr)
        seen = over_cache(lambda off, s: keep_ranked(off, kb, need, s), jnp.zeros((1, tq), F32))
        over_new(lambda off, s: keep_ranked(lanes_at(off), kbn, need, s), seen)

    def without_ties():
        over_cache(lambda off, c: keep_all_ties(off, kb, c), 0)
        over_new(lambda off, c: keep_all_ties(lanes_at(off), kbn, c), 0)

    lax.cond(has_ties, with_ties, without_ties)

    rows = Q_PER_KV * tq

    eye = jnp.where(lax.broadcasted_iota(I32, (tq, tq), 0) == lax.broadcasted_iota(I32, (tq, tq), 1), 1.0, 0.0).astype(BF16)

    def pass_a(g0, k_of, lane_off, width):
        keep_q = _dot_nt(eye, keep_scr[pl.ds(lane_off, width), :])
        b = jnp.tile(jnp.where(keep_q > 0.5, 0.0, NEG_BIG), (Q_PER_KV, 1))
        for gi in range(gp):
            g = g0 + gi
            s = _dot(qs[g * rows:(g + 1) * rows, :], k_of(g)) + b
            s_scr[gi, :, pl.ds(lane_off, width)] = s
            m = mb_scr[gi]
            if width % LANES:
                m = jnp.maximum(m, jnp.max(s, axis=1, keepdims=True))
            else:
                for c in range(width // LANES):
                    m = jnp.maximum(m, s[:, c * LANES:(c + 1) * LANES])
            mb_scr[gi] = m

    def pass_b(g0, v_of, lane_off, width):
        for gi in range(gp):
            m = mb_scr[gi]
            m = m[:, :width] if width < LANES else jnp.tile(m, (1, width // LANES))
            p = jnp.exp(s_scr[gi, :, pl.ds(lane_off, width)] - m)
            acc_scr[gi] += _dot_nt(p.astype(BF16), v_of(g0 + gi))

    def cache_k(off):
        return lambda g: ckb[g, :, pl.ds(off, kb)]

    def cache_v(off):
        return lambda g: cvb[g, :, pl.ds(off, kb)]

    def new_k(off):
        return lambda g: kbf[g, :, pl.ds(off, kbn)]

    def new_v(off):
        return lambda g: vbf[g, :, pl.ds(off, kbn)]

    def run(fn, cache_args, new_args):
        def on_cache(off, c):
            fn(*[a(off) for a in cache_args], off, kb)
            return c

        def on_new(off, c):
            fn(*[a(off) for a in new_args], lanes_at(off), kbn)
            return c

        over_cache(on_cache, 0)
        over_new(on_new, 0)

    for g0 in range(0, N_KV_HEADS, gp):
        mb_scr[...] = jnp.full(mb_scr.shape, NEG_BIG, F32)
        run(functools.partial(pass_a, g0), [cache_k], [new_k])
        for gi in range(gp):
            mb_scr[gi] = jnp.broadcast_to(jnp.max(mb_scr[gi], axis=1, keepdims=True), (rows, LANES))
        acc_scr[...] = jnp.zeros(acc_scr.shape, F32)
        run(functools.partial(pass_b, g0), [cache_v], [new_v])

        for gi in range(gp):
            a = acc_scr[gi]
            out = a[:, :HEAD_DIM] / a[:, HEAD_DIM:HEAD_DIM + 1]
            for r in range(Q_PER_KV):
                hh = (g0 + gi) * Q_PER_KV + r
                o_ref[0, :, hh * HEAD_DIM:(hh + 1) * HEAD_DIM] = out[r * tq:(r + 1) * tq].astype(o_ref.dtype)


def _attention(q, qi, misc, kt, vt, kit, cache=None):
    b, t, _ = q.shape
    has_cache = cache is not None
    past = cache[2].shape[2] if has_cache else 0
    n_keys = past + t
    topk = min(TOPK_MAX, n_keys // 4)
    tq = min(t, ATTN_TQ)
    kb = 256
    rows = Q_PER_KV * tq
    key_w = past + -(-t // LANES) * LANES
    gp = N_KV_HEADS
    while gp > 1 and gp * rows * key_w * 4 > ATTN_LOGITS_BYTES:
        gp //= 2
    qtile = lambda w: pl.BlockSpec((1, tq, w), lambda i, j: (i, j, 0))
    heads = lambda n: pl.BlockSpec((1, N_KV_HEADS, HEAD_DIM, n), lambda i, j: (i, 0, 0, 0))
    idx = lambda n: pl.BlockSpec((1, IDX_DIM, n), lambda i, j: (i, 0, 0))
    in_specs = [qtile(N_HEADS * HEAD_DIM), qtile(QI_W), qtile(LANES), heads(t), heads(t), idx(t)]
    args = [q, qi, misc, kt, vt, kit]
    if has_cache:
        in_specs += [heads(past), heads(past), idx(past)]
        args += list(cache)
    kern = functools.partial(_attn_kernel, tq=tq, kb=kb, gp=gp, topk=topk, past=past, t_new=t, has_cache=has_cache)
    return pl.pallas_call(
        kern,
        grid=(b, t // tq),
        in_specs=in_specs,
        out_specs=qtile(N_HEADS * HEAD_DIM),
        out_shape=jax.ShapeDtypeStruct((b, t, N_HEADS * HEAD_DIM), BF16),
        scratch_shapes=[
            pltpu.VMEM((N_KV_HEADS, HEAD_DIM, t), BF16),
            pltpu.VMEM((N_KV_HEADS, 2 * HEAD_DIM, t), BF16),
            pltpu.VMEM((t, IDX_DIM), BF16),
            pltpu.VMEM((n_keys, tq), F32),
            pltpu.VMEM((n_keys, tq), BF16),
            pltpu.VMEM((IDX_HEADS * tq, IDX_DIM), BF16),
            pltpu.VMEM((N_HEADS * tq, HEAD_DIM), BF16),
            pltpu.VMEM((gp, rows, LANES), F32),
            pltpu.VMEM((gp, rows, 2 * HEAD_DIM), F32),
            pltpu.VMEM((gp, rows, key_w), F32),
        ] + ([
            pltpu.VMEM((N_KV_HEADS, HEAD_DIM, past), BF16),
            pltpu.VMEM((N_KV_HEADS, 2 * HEAD_DIM, past), BF16),
            pltpu.VMEM((past, IDX_DIM), BF16),
        ] if has_cache else []),
        compiler_params=_cparams("parallel", "arbitrary"),
        name="attn_cache" if has_cache else "attn_prompt",
    )(*args)


def _ssd_kernel(xs_ref, bc_ref, misc_ref, zs_ref, h0_ref, alog_ref, dskip_ref, gnorm_ref, expand_ref, y_ref, hout_ref):
    c = pl.program_id(1)
    L = CHUNK

    @pl.when(c == 0)
    def _():
        hout_ref[...] = h0_ref[...]

    a = -jnp.exp(alog_ref[...])
    ri = lax.broadcasted_iota(I32, (L, 3 * L), 0)
    ci = lax.broadcasted_iota(I32, (L, 3 * L), 1) % L
    tri3 = jnp.where(ri >= ci, 1.0, 0.0).astype(BF16)
    expand3 = expand_ref[...]
    row_i = lax.broadcasted_iota(I32, (L, D_INNER), 0)
    lane_j = lax.broadcasted_iota(I32, (L, D_INNER), 1) % SSM_HEAD_DIM
    causal = row_i >= lane_j
    blk_r = lax.broadcasted_iota(I32, (GROUP_W, GROUP_W), 0) // SSM_HEAD_DIM
    blk_c = lax.broadcasted_iota(I32, (GROUP_W, GROUP_W), 1) // SSM_HEAD_DIM
    same_head = blk_r == blk_c

    for sub in range(xs_ref.shape[1] // L):
        _ssd_chunk(slice(sub * L, (sub + 1) * L), xs_ref, bc_ref, misc_ref, zs_ref, dskip_ref, gnorm_ref, y_ref, hout_ref,
                   a, tri3, expand3, causal, same_head)


def _heads_over_time(x):
    L = x.shape[0]
    xt = x.T
    xtt = jnp.concatenate([xt, xt], axis=1)
    first = lax.broadcasted_iota(I32, (L, 2 * L), 1) < L
    cols = []
    for h in range(0, SSM_HEADS, 2):
        r = MISC_DT + h
        cols.append(jnp.where(first, jnp.broadcast_to(xtt[r:r + 1, :], (L, 2 * L)),
                              jnp.broadcast_to(xtt[r + 1:r + 2, :], (L, 2 * L))))
    return jnp.concatenate(cols, axis=1)


def _ssd_chunk(rows, xs_ref, bc_ref, misc_ref, zs_ref, dskip_ref, gnorm_ref, y_ref, hout_ref,
               a, tri3, expand3, causal, same_head):
    L = CHUNK
    dt = misc_ref[0, rows, :]
    acs = _exact_dot_left(tri3, dt * a, 3)
    col_acs = _exact_dot(acs, expand3, 3)
    col_dt = _exact_dot(dt, expand3[:2 * LANES], 2)
    row_acs = _heads_over_time(acs)
    row_dt = _heads_over_time(dt)
    a_last = acs[L - 1:L, MISC_DT:MISC_DT + SSM_HEADS]

    groups = range(SSM_GROUPS)
    gsl = [slice(g * GROUP_W, (g + 1) * GROUP_W) for g in groups]
    bgs = [bc_ref[0, rows, g * D_STATE:(g + 1) * D_STATE] for g in groups]
    cgs = [bc_ref[0, rows, (SSM_GROUPS + g) * D_STATE:(SSM_GROUPS + g + 1) * D_STATE] for g in groups]
    xgs = [xs_ref[0, rows, gs] for gs in gsl]
    hprevs = [hout_ref[0, g * HEADS_PER_GROUP:(g + 1) * HEADS_PER_GROUP].reshape(GROUP_W, D_STATE) for g in groups]
    cbs = [_dot_nt(cgs[g], jnp.tile(bgs[g], (HEADS_PER_GROUP, 1))) for g in groups]
    offs = [_dot_nt(cgs[g], hprevs[g].astype(BF16)) for g in groups]
    sts = []
    for g in groups:
        ce = col_acs[:, gsl[g]]
        w_state = jnp.exp(ce[L - 1:L, :] - ce) * col_dt[:, gsl[g]]
        sts.append(_dot_tn((xgs[g] * w_state).astype(BF16), bgs[g]))
    y_diags = []
    for g in groups:
        gs = gsl[g]
        mm = cbs[g] * jnp.exp(jnp.where(causal[:, gs], col_acs[:, gs] - row_acs[:, gs], -jnp.inf)) * row_dt[:, gs]
        xbd = jnp.where(same_head, jnp.tile(xgs[g].astype(BF16), (HEADS_PER_GROUP, 1)), 0.0).astype(BF16)
        y_diags.append(_dot(mm.astype(BF16), xbd))
    for g in groups:
        for r in range(HEADS_PER_GROUP):
            hh = g * HEADS_PER_GROUP + r
            decay = jnp.exp(a_last[:, hh:hh + 1])
            rs = slice(r * SSM_HEAD_DIM, (r + 1) * SSM_HEAD_DIM)
            hout_ref[0, hh] = decay * hprevs[g][rs] + sts[g][rs]
    for g in groups:
        gs = gsl[g]
        y_off = jnp.exp(col_acs[:, gs]) * offs[g]
        yt = (y_diags[g] + y_off + dskip_ref[:, gs] * xgs[g]) * zs_ref[0, rows, gs].astype(F32)
        ms = jnp.mean(yt * yt, axis=-1, keepdims=True)
        y_ref[0, rows, gs] = (yt * lax.rsqrt(ms + EPS) * gnorm_ref[:, gs]).astype(y_ref.dtype)


def _ssd(xs, bc, misc, zs, h0, a_log, dskip_full, g_norm, expand):
    b, t, _ = xs.shape
    rows = min(t, SSD_CHUNKS_PER_STEP * CHUNK)
    nc = t // rows
    chunk = lambda w: pl.BlockSpec((1, rows, w), lambda i, c: (i, c, 0))
    const2 = lambda r, w: pl.BlockSpec((r, w), lambda i, c: (0, 0))
    state = pl.BlockSpec((1, SSM_HEADS, SSM_HEAD_DIM, D_STATE), lambda i, c: (i, 0, 0, 0))
    return pl.pallas_call(
        _ssd_kernel,
        grid=(b, nc),
        in_specs=[
            chunk(D_INNER), chunk(2 * SSM_GROUPS * D_STATE), chunk(LANES), chunk(D_INNER), state,
            const2(1, LANES), const2(1, D_INNER), const2(1, D_INNER), const2(3 * LANES, D_INNER),
        ],
        out_specs=[chunk(D_INNER), state],
        out_shape=[jax.ShapeDtypeStruct((b, t, D_INNER), BF16),
                   jax.ShapeDtypeStruct((b, SSM_HEADS, SSM_HEAD_DIM, D_STATE), F32)],
        compiler_params=_cparams("parallel", "arbitrary"),
        name="ssd",
    )(xs, bc, misc, zs, h0, a_log, dskip_full, g_norm, expand)


def _merge_kernel(o_ref, y_ref, g_ref, x_ref, mod_ref, gain_ref, wa_ref, ws_ref, wo_ref, x1_ref, h2_ref):
    bb, tt, d = x_ref.shape
    rows = bb * tt
    o = o_ref[...].reshape(rows, -1)
    y = y_ref[...].reshape(rows, -1)
    gates = g_ref[...].reshape(rows, -1).astype(F32)
    mixed = gates[:, :d] * _dot(o, wa_ref[...]) + gates[:, d:] * _dot(y, ws_ref[...])
    out = _dot(mixed.astype(BF16), wo_ref[...]).reshape(bb, tt, d)
    mod = mod_ref[...]
    x1 = x_ref[...] + mod[:, 2:3, :] * out
    x1_ref[...] = x1
    h2_ref[...] = _modulated_norm(x1, mod, gain_ref[...], 3, 4).astype(h2_ref.dtype)


def _merge(o_attn, y, gates, x, mod3, mod_off, gain, wa, ws, wo):
    b, t, d = x.shape
    bb, tt = _row_blocking(b, t)
    assert mod_off % bb == 0, (mod_off, bb)
    off = mod_off // bb
    tile = lambda w: pl.BlockSpec((bb, tt, w), lambda i, j: (i, j, 0))
    const = lambda r, w: pl.BlockSpec((r, w), lambda i, j: (0, 0))
    return pl.pallas_call(
        _merge_kernel,
        grid=(b // bb, t // tt),
        in_specs=[
            tile(N_HEADS * HEAD_DIM), tile(D_INNER), tile(2 * D_MODEL), tile(d),
            pl.BlockSpec((bb, 6, d), lambda i, j: (i + off, 0, 0)),
            pl.BlockSpec((1, 1, d), lambda i, j: (0, 0, 0)),
            const(N_HEADS * HEAD_DIM, d), const(D_INNER, d), const(d, d),
        ],
        out_specs=[tile(d), tile(d)],
        out_shape=[jax.ShapeDtypeStruct((b, t, d), F32), jax.ShapeDtypeStruct((b, t, d), BF16)],
        compiler_params=_cparams("parallel", "parallel"),
        name="merge",
    )(o_attn, y, gates, x, mod3, gain.reshape(1, 1, d), wa, ws, wo)


def _ffn_kernel(h2_ref, x1_ref, mod_ref, wg_ref, wu_ref, wd_ref, o_ref, acc_ref):
    f = pl.program_id(2)
    bb, tt, d = x1_ref.shape
    h2 = h2_ref[...].reshape(bb * tt, d)
    act = _silu(_dot(h2, wg_ref[...])) * _dot(h2, wu_ref[...])
    part = _dot(act.astype(BF16), wd_ref[...])

    @pl.when(f == 0)
    def _():
        acc_ref[...] = part

    @pl.when(f > 0)
    def _():
        acc_ref[...] += part

    @pl.when(f == pl.num_programs(2) - 1)
    def _():
        o_ref[...] = x1_ref[...] + mod_ref[...][:, 5:6, :] * acc_ref[...].reshape(bb, tt, d)


def _ffn(h2, x1, mod3, mod_off, w_gu, w_down):
    b, t, d = x1.shape
    bb, tt = _row_blocking(b, t)
    assert mod_off % bb == 0, (mod_off, bb)
    off = mod_off // bb
    nf = 2
    tf = D_FF // nf
    tile = pl.BlockSpec((bb, tt, d), lambda i, j, f: (i, j, 0))
    return pl.pallas_call(
        _ffn_kernel,
        grid=(b // bb, t // tt, nf),
        in_specs=[
            tile, tile,
            pl.BlockSpec((bb, 6, d), lambda i, j, f: (i + off, 0, 0)),
            pl.BlockSpec((d, tf), lambda i, j, f: (0, f)),
            pl.BlockSpec((d, tf), lambda i, j, f: (0, nf + f)),
            pl.BlockSpec((tf, d), lambda i, j, f: (f, 0)),
        ],
        out_specs=tile,
        out_shape=jax.ShapeDtypeStruct((b, t, d), F32),
        scratch_shapes=[pltpu.VMEM((bb * tt, d), F32)],
        compiler_params=_cparams("parallel", "parallel", "arbitrary"),
        name="ffn",
    )(h2, x1, mod3, w_gu, w_gu, w_down)


def _rope_tables(t, past, tm):
    half = HEAD_DIM // 2
    inv = ROPE_THETA ** (-jnp.arange(half, dtype=F32) / half)
    ang = (past + jnp.arange(t)).astype(F32)[:, None] * inv[None, :]
    cos, sin = jnp.cos(ang), jnp.sin(ang)
    cos_t = jnp.concatenate([cos, cos, cos, cos], axis=1)
    sin_t = jnp.concatenate([-sin, sin, -sin, sin], axis=1)
    if t < tm:
        cos_t, sin_t = jnp.tile(cos_t, (tm // t, 1)), jnp.tile(sin_t, (tm // t, 1))
    return cos_t, sin_t


def _group_step(x, mod3, mod_off, cache, conv_state, ssm_state, p):
    b, t, d = x.shape
    m = b * t
    tm = min(512, m)
    past = cache[2].shape[2] if cache is not None else 0
    h = _hnorm(x, mod3, mod_off, p["g_norm_mix"]).reshape(m, d)
    cos_t, sin_t = _rope_tables(t, past, tm)
    q, kt = _proj_qk(h, p["w_qk"], p["gsum"], p["gexp"], p["qk_gain"], cos_t, sin_t, b, t)
    vt, qi, misc, kit = _proj_vm(h, p["w_vm"], cos_t, sin_t, p["dtb"], b, t)
    zs = _proj_act(h, p["w_z"], "silu", BF16, tm, D_INNER)
    gates = _proj_act(h, p["w_g"], "sigmoid", BF16, tm, 2 * D_MODEL)
    half = CONV_CH // 2
    xs, tail_x = _proj_conv(h, p["w_xs"], conv_state[:, :, :half], p["w_conv"][:, :half],
                            p["b_conv"][:, :half], b, t, F32)
    bc, tail_bc = _proj_conv(h, p["w_bc"], conv_state[:, :, half:], p["w_conv"][:, half:],
                             p["b_conv"][:, half:], b, t, BF16)

    r3 = lambda a: a.reshape(b, t, a.shape[-1])
    o_attn = _attention(r3(q), r3(qi), r3(misc), kt, vt, kit, cache)
    y, h_last = _ssd(r3(xs), r3(bc), r3(misc), r3(zs), ssm_state, p["a_log"], p["dskip_full"], p["g_ssm_norm"],
                     p["expand"])
    x1, h2 = _merge(o_attn, y, r3(gates), x, mod3, mod_off, p["g_norm_ffn"], p["w_ba"], p["w_bs"], p["w_out"])
    out = _ffn(h2, x1, mod3, mod_off, p["w_gu"], p["w_down"])
    conv_new = jnp.concatenate([tail_x, tail_bc], axis=2)[:, -(CONV_W - 1):]
    token_major = lambda a: jnp.moveaxis(a, -1, 1)
    return out, token_major(kt), token_major(vt), token_major(kit), conv_new, h_last


def _layer_params(l, w_in, g_q, g_k, g_norm_mix, g_norm_ffn, w_conv, b_conv, dt_bias, a_log, d_skip, g_ssm_norm,
                  w_branch_attn, w_branch_ssm, w_out, w_gate_up, w_down):
    sizes = (N_HEADS * HEAD_DIM, KV_W, KV_W, QI_W, IDX_DIM, IDX_HEADS, D_INNER, CONV_CH, SSM_HEADS, 2 * D_MODEL)
    offs = [0]
    for s in sizes:
        offs.append(offs[-1] + s)
    col = lambda i: w_in[l][:, offs[i]:offs[i + 1]]
    pad = LANES - (IDX_DIM + IDX_HEADS + SSM_HEADS)
    w_vm = jnp.concatenate([col(2), col(3), col(4), col(5), col(8), jnp.zeros((D_MODEL, pad), F32)], axis=1)
    head_of = jnp.arange(QK_W) // HEAD_DIM
    gsum = (head_of[:, None] == jnp.arange(LANES)[None, :]).astype(BF16)
    dtb = jnp.zeros((1, LANES), F32).at[0, MISC_DT:MISC_DT + SSM_HEADS].set(dt_bias[l])
    expand = ((jnp.arange(LANES) - MISC_DT)[:, None] == (jnp.arange(D_INNER) // SSM_HEAD_DIM)[None, :]).astype(BF16)
    a_log_lanes = jnp.zeros((1, LANES), F32).at[0, MISC_DT:MISC_DT + SSM_HEADS].set(a_log[l])
    return dict(
        w_qk=w_in[l][:, :QK_W].astype(BF16), w_vm=w_vm.astype(BF16), w_z=col(6).astype(BF16),
        w_xs=col(7)[:, :CONV_CH // 2].astype(BF16), w_bc=col(7)[:, CONV_CH // 2:].astype(BF16), w_g=col(9).astype(BF16),
        gsum=jnp.tile(gsum, (2, 1)), gexp=jnp.tile(gsum.T, (2, 1)),
        qk_gain=jnp.concatenate([jnp.tile(g_q[l], N_HEADS), jnp.tile(g_k[l], N_KV_HEADS)]).reshape(1, QK_W),
        dtb=dtb, expand=jnp.tile(expand, (3, 1)),
        g_norm_mix=g_norm_mix[l], g_norm_ffn=g_norm_ffn[l],
        w_conv=w_conv[l], b_conv=b_conv[l].reshape(1, CONV_CH), a_log=a_log_lanes,
        dskip_full=jnp.repeat(d_skip[l], SSM_HEAD_DIM).reshape(1, D_INNER), g_ssm_norm=g_ssm_norm[l].reshape(1, D_INNER),
        w_ba=w_branch_attn[l].astype(BF16), w_bs=w_branch_ssm[l].astype(BF16), w_out=w_out[l].astype(BF16),
        w_gu=w_gate_up[l].astype(BF16), w_down=w_down[l].astype(BF16),
    )


def kernel(x_prompt, x_sample, cache_k, cache_v, cache_ki, state_conv, state_ssm, c_prompt, c_sample, w_ada, b_ada, g_norm_mix, g_norm_ffn, w_in, g_q, g_k, w_conv, b_conv, dt_bias, a_log, d_skip, g_ssm_norm, w_branch_attn, w_branch_ssm, w_out, w_gate_up, w_down):
    depth = w_in.shape[0]
    bp, bs = x_prompt.shape[0], x_sample.shape[0]
    past = cache_k.shape[2]
    y_p, y_s = x_prompt, x_sample
    c_all = jnp.concatenate([c_prompt, c_sample], axis=0)
    new_p = [[] for _ in range(5)]
    new_s = [[] for _ in range(5)]
    for l in range(depth):
        p = _layer_params(l, w_in, g_q, g_k, g_norm_mix, g_norm_ffn, w_conv, b_conv, dt_bias, a_log, d_skip,
                          g_ssm_norm, w_branch_attn, w_branch_ssm, w_out, w_gate_up, w_down)
        mod3 = _ada_mod(c_all, w_ada[l], b_ada[l]).reshape(bp + bs, 6, D_MODEL)
        zero_conv = jnp.zeros((bp, CONV_W - 1, CONV_CH), F32)
        zero_ssm = jnp.zeros((bp, SSM_HEADS, SSM_HEAD_DIM, D_STATE), F32)
        y_p, *st_p = _group_step(y_p, mod3, 0, None, zero_conv, zero_ssm, p)
        time_minor = lambda a: jnp.moveaxis(a, 1, -1)
        cache = (time_minor(cache_k[l]), time_minor(cache_v[l]), time_minor(cache_ki[l]))
        y_s, *st_s = _group_step(y_s, mod3, bp, cache, state_conv[l], state_ssm[l], p)
        for acc, a in zip(new_p, st_p):
            acc.append(a)
        for acc, a in zip(new_s, st_s):
            acc.append(a)
    return (y_p, y_s, *[jnp.stack(a) for a in new_p], *[jnp.stack(a) for a in new_s])
```

```python
import functools
import math

import jax
import jax.numpy as jnp
from jax import lax
from jax.experimental import pallas as pl
from jax.experimental.pallas import tpu as pltpu

F32, BF16, I32 = jnp.float32, jnp.bfloat16, jnp.int32

D_MODEL = 1024
CHUNK = 64
N_HEADS = 16
HEAD_DIM = 64
N_KV_HEADS = 4
Q_PER_KV = N_HEADS // N_KV_HEADS
IDX_HEADS = 8
IDX_DIM = 64
TOPK_MAX = 256
ROPE_THETA = 10000.0
D_INNER = 2 * D_MODEL
SSM_HEAD_DIM = 64
SSM_HEADS = D_INNER // SSM_HEAD_DIM
SSM_GROUPS = 8
HEADS_PER_GROUP = SSM_HEADS // SSM_GROUPS
GROUP_W = HEADS_PER_GROUP * SSM_HEAD_DIM
D_STATE = 128
CONV_W = 4
CONV_CH = D_INNER + 2 * SSM_GROUPS * D_STATE
D_FF = -(-8 * D_MODEL // (3 * 256)) * 256
EPS = 1e-6
QK_W = (N_HEADS + N_KV_HEADS) * HEAD_DIM
KV_W = N_KV_HEADS * HEAD_DIM
QI_W = IDX_HEADS * IDX_DIM

LANES = 128
MISC_WI = IDX_DIM
MISC_DT = IDX_DIM + IDX_HEADS
WI_SCALE = (IDX_HEADS ** -0.5) * (IDX_DIM ** -0.5)
INT_MIN = -(2 ** 31)
NEG_BIG = -1e30
COUNT_ROWS = 64
SSD_CHUNKS_PER_STEP = 4
ATTN_TQ = 256
ATTN_LOGITS_BYTES = 16 * 1024 * 1024
VMEM_LIMIT = 56 * 1024 * 1024


def _cparams(*sem):
    return pltpu.CompilerParams(dimension_semantics=sem, vmem_limit_bytes=VMEM_LIMIT)


def _silu(x):
    h = 0.5 * x
    return h + h * jnp.tanh(h)


def _split_bf16(x, n):
    pieces = []
    r = x
    for _ in range(n):
        p = r.astype(BF16)
        pieces.append(p)
        r = r - p.astype(F32)
    return pieces


def _dot(a, b):
    return jnp.dot(a, b, preferred_element_type=F32)


def _dot_nt(a, b):
    return lax.dot_general(a, b, (((1,), (1,)), ((), ())), preferred_element_type=F32)


def _dot_tn(a, b):
    return lax.dot_general(a, b, (((0,), (0,)), ((), ())), preferred_element_type=F32)


def _exact_dot(x, m_stacked, n):
    return _dot(jnp.concatenate(_split_bf16(x, n), axis=1), m_stacked)


def _exact_dot_left(m_tiled, x, n):
    return _dot(m_tiled, jnp.concatenate(_split_bf16(x, n), axis=0))


def _rotate_half(x):
    w = x.shape[-1]
    lane = lax.broadcasted_iota(I32, x.shape, x.ndim - 1)
    first = (lane % HEAD_DIM) < (HEAD_DIM // 2)
    return jnp.where(first, pltpu.roll(x, w - HEAD_DIM // 2, x.ndim - 1), pltpu.roll(x, HEAD_DIM // 2, x.ndim - 1))


def _rope(x, cos, sin):
    reps = x.shape[-1] // LANES
    if reps > 1:
        cos = jnp.tile(cos, (1, reps))
        sin = jnp.tile(sin, (1, reps))
    return x * cos + _rotate_half(x) * sin


def _mod_kernel(c_ref, w_ref, b_ref, o_ref):
    s = _silu(c_ref[...])
    o_ref[...] = _dot(s.astype(BF16), w_ref[...].astype(BF16)) + b_ref[...]


def _ada_mod(c_all, w_ada, b_ada):
    bt = c_all.shape[0]
    n = w_ada.shape[1]
    tn = D_MODEL
    return pl.pallas_call(
        _mod_kernel,
        grid=(n // tn,),
        in_specs=[
            pl.BlockSpec((bt, D_MODEL), lambda j: (0, 0)),
            pl.BlockSpec((D_MODEL, tn), lambda j: (0, j)),
            pl.BlockSpec((1, tn), lambda j: (0, j)),
        ],
        out_specs=pl.BlockSpec((bt, tn), lambda j: (0, j)),
        out_shape=jax.ShapeDtypeStruct((bt, n), F32),
        compiler_params=_cparams("parallel"),
        name="ada_mod",
    )(c_all, w_ada, b_ada.reshape(1, n))


def _modulated_norm(x, mod, gain, shift_idx, scale_idx):
    ms = jnp.mean(x * x, axis=-1, keepdims=True)
    xn = x * lax.rsqrt(ms + EPS)
    sh = mod[:, shift_idx:shift_idx + 1, :]
    sc = mod[:, scale_idx:scale_idx + 1, :]
    return xn * gain * (1.0 + sc) + sh


def _hnorm_kernel(x_ref, mod_ref, g_ref, o_ref):
    o_ref[...] = _modulated_norm(x_ref[...], mod_ref[...], g_ref[...], 0, 1).astype(o_ref.dtype)


def _row_blocking(b, t):
    tt = min(t, 512)
    bb = max(1, min(b, 512 // tt))
    assert t % tt == 0 and b % bb == 0 and tt % CHUNK == 0, (b, t)
    return bb, tt


def _hnorm(x, mod3, mod_off, gain):
    b, t, d = x.shape
    bb, tt = _row_blocking(b, t)
    assert mod_off % bb == 0, (mod_off, bb)
    off = mod_off // bb
    return pl.pallas_call(
        _hnorm_kernel,
        grid=(b // bb, t // tt),
        in_specs=[
            pl.BlockSpec((bb, tt, d), lambda i, j: (i, j, 0)),
            pl.BlockSpec((bb, 6, d), lambda i, j: (i + off, 0, 0)),
            pl.BlockSpec((1, 1, d), lambda i, j: (0, 0, 0)),
        ],
        out_specs=pl.BlockSpec((bb, tt, d), lambda i, j: (i, j, 0)),
        out_shape=jax.ShapeDtypeStruct((b, t, d), BF16),
        compiler_params=_cparams("parallel", "parallel"),
        name="hnorm",
    )(x, mod3, gain.reshape(1, 1, d))


def _proj_act_kernel(h_ref, w_ref, o_ref, *, act):
    acc = _dot(h_ref[...], w_ref[...])
    if act == "silu":
        acc = _silu(acc)
    elif act == "sigmoid":
        acc = jax.nn.sigmoid(acc)
    o_ref[...] = acc.astype(o_ref.dtype)


def _proj_act(h2d, w, act, out_dtype, tm, tn):
    m, k = h2d.shape
    n = w.shape[1]
    return pl.pallas_call(
        functools.partial(_proj_act_kernel, act=act),
        grid=(n // tn, m // tm),
        in_specs=[
            pl.BlockSpec((tm, k), lambda j, i: (i, 0)),
            pl.BlockSpec((k, tn), lambda j, i: (0, j)),
        ],
        out_specs=pl.BlockSpec((tm, tn), lambda j, i: (i, j)),
        out_shape=jax.ShapeDtypeStruct((m, n), out_dtype),
        compiler_params=_cparams("parallel", "parallel"),
        name="proj_" + act,
    )(h2d, w)


SUBLANES = 8
CONV_SLAB = 256


def _proj_conv_kernel(h_ref, w_ref, cst_ref, wconv_ref, bconv_ref, o_ref, tail_ref, buf, *, bb, tt, tiles_per_seq):
    i = pl.program_id(1)
    tail = CONV_W - 1
    tn = w_ref.shape[1]

    @pl.when(i % tiles_per_seq == 0)
    def _():
        for s in range(bb):
            buf[s, 0:SUBLANES - tail, :] = jnp.zeros((SUBLANES - tail, tn), F32)
            buf[s, SUBLANES - tail:SUBLANES, :] = cst_ref[s]

    h = h_ref[...]
    for c in range(tn // CONV_SLAB):
        cs = slice(c * CONV_SLAB, (c + 1) * CONV_SLAB)
        acc = _dot(h, w_ref[:, cs])
        for s in range(bb):
            buf[s, SUBLANES:SUBLANES + tt, cs] = acc[s * tt:(s + 1) * tt]
            xb = buf[s, :, cs]
            xc = bconv_ref[:, cs] + wconv_ref[tail:tail + 1, cs] * xb[SUBLANES:]
            for jj in range(tail):
                shifted = pltpu.roll(xb, tail - jj, 0)[SUBLANES:]
                xc = xc + wconv_ref[jj:jj + 1, cs] * shifted
            o_ref[s * tt:(s + 1) * tt, cs] = _silu(xc).astype(o_ref.dtype)
            last = buf[s, tt:tt + SUBLANES, cs]
            tail_ref[s, :, cs] = last
            buf[s, 0:SUBLANES, cs] = last


def _proj_conv(h2d, w, conv_state, w_conv, b_conv, b, t, out_dtype):
    m, k = h2d.shape
    n = w.shape[1]
    bb, tt = _row_blocking(b, t)
    tm = bb * tt
    tiles_per_seq = t // tt
    tn = min(n, 2048)
    kern = functools.partial(_proj_conv_kernel, bb=bb, tt=tt, tiles_per_seq=tiles_per_seq)
    seq_block = lambda rows: pl.BlockSpec((bb, rows, tn), lambda j, i: (i // tiles_per_seq, 0, j))
    return pl.pallas_call(
        kern,
        grid=(n // tn, m // tm),
        in_specs=[
            pl.BlockSpec((tm, k), lambda j, i: (i, 0)),
            pl.BlockSpec((k, tn), lambda j, i: (0, j)),
            seq_block(CONV_W - 1),
            pl.BlockSpec((CONV_W, tn), lambda j, i: (0, j)),
            pl.BlockSpec((1, tn), lambda j, i: (0, j)),
        ],
        out_specs=[pl.BlockSpec((tm, tn), lambda j, i: (i, j)), seq_block(SUBLANES)],
        out_shape=[jax.ShapeDtypeStruct((m, n), out_dtype), jax.ShapeDtypeStruct((b, SUBLANES, n), F32)],
        scratch_shapes=[pltpu.VMEM((bb, SUBLANES + tt, tn), F32)],
        compiler_params=_cparams("parallel", "arbitrary"),
        name="proj_conv",
    )(h2d, w, conv_state, w_conv, b_conv)


def _store_time_minor(ref, x):
    bb, tt = ref.shape[0], ref.shape[-1]
    for s in range(bb):
        ref[s] = x[s * tt:(s + 1) * tt, :].T.reshape(ref.shape[1:])


def _qk_kernel(h_ref, w_ref, gsum_ref, gexp_ref, gain_ref, cos_ref, sin_ref, q_ref, k_ref):
    acc = _dot(h_ref[...], w_ref[...])
    ss = _exact_dot(acc * acc, gsum_ref[...], 2)
    rs = lax.rsqrt(ss * (1.0 / HEAD_DIM) + EPS)
    rs_full = _exact_dot(rs, gexp_ref[...], 2)
    xn = acc * rs_full * gain_ref[...]
    out = _rope(xn, cos_ref[...], sin_ref[...])
    nq = N_HEADS * HEAD_DIM
    q_ref[...] = (out[:, :nq] * (HEAD_DIM ** -0.5)).astype(q_ref.dtype)
    _store_time_minor(k_ref, out[:, nq:])


def _vm_kernel(h_ref, w_ref, cos_ref, sin_ref, dtb_ref, v_ref, qi_ref, misc_ref, kit_ref):
    acc = _dot(h_ref[...], w_ref[...])
    cos, sin = cos_ref[...], sin_ref[...]
    _store_time_minor(v_ref, acc[:, :KV_W])
    qi_ref[...] = _rope(acc[:, KV_W:KV_W + QI_W], cos, sin).astype(qi_ref.dtype)
    m = acc[:, KV_W + QI_W:]
    lane = lax.broadcasted_iota(I32, m.shape, 1)
    roped = _rope(m, cos, sin)
    dt = jax.nn.softplus(m + dtb_ref[...])
    misc_ref[...] = jnp.where(lane < MISC_WI, roped,
                              jnp.where(lane < MISC_DT, m * WI_SCALE,
                                        jnp.where(lane < MISC_DT + SSM_HEADS, dt, 0.0)))
    _store_time_minor(kit_ref, roped[:, :IDX_DIM])


def _table_spec(tab_rows, tm):
    nblk = tab_rows // tm
    return pl.BlockSpec((tm, LANES), lambda i: (i % nblk, 0))


def _time_minor_spec(b, t, *mid):
    bb, tt = _row_blocking(b, t)
    per_seq = t // tt
    zeros = (0,) * len(mid)
    return pl.BlockSpec((bb, *mid, tt), lambda i: (i // per_seq, *zeros, i % per_seq))


def _proj_qk(h2d, w_qk, gsum, gexp, gain, cos_tab, sin_tab, b, t):
    m, k = h2d.shape
    tm = math.prod(_row_blocking(b, t))
    nq = N_HEADS * HEAD_DIM
    const = lambda i: (0, 0)
    return pl.pallas_call(
        _qk_kernel,
        grid=(m // tm,),
        in_specs=[
            pl.BlockSpec((tm, k), lambda i: (i, 0)),
            pl.BlockSpec((k, QK_W), const),
            pl.BlockSpec((2 * QK_W, LANES), const),
            pl.BlockSpec((2 * LANES, QK_W), const),
            pl.BlockSpec((1, QK_W), const),
            _table_spec(cos_tab.shape[0], tm),
            _table_spec(sin_tab.shape[0], tm),
        ],
        out_specs=[pl.BlockSpec((tm, nq), lambda i: (i, 0)), _time_minor_spec(b, t, N_KV_HEADS, HEAD_DIM)],
        out_shape=[jax.ShapeDtypeStruct((m, nq), BF16), jax.ShapeDtypeStruct((b, N_KV_HEADS, HEAD_DIM, t), F32)],
        compiler_params=_cparams("parallel"),
        name="proj_qk",
    )(h2d, w_qk, gsum, gexp, gain, cos_tab, sin_tab)


def _proj_vm(h2d, w_vm, cos_tab, sin_tab, dtb, b, t):
    m, k = h2d.shape
    tm = math.prod(_row_blocking(b, t))
    wn = w_vm.shape[1]
    const = lambda i: (0, 0)
    return pl.pallas_call(
        _vm_kernel,
        grid=(m // tm,),
        in_specs=[
            pl.BlockSpec((tm, k), lambda i: (i, 0)),
            pl.BlockSpec((k, wn), const),
            _table_spec(cos_tab.shape[0], tm),
            _table_spec(sin_tab.shape[0], tm),
            pl.BlockSpec((1, LANES), const),
        ],
        out_specs=[_time_minor_spec(b, t, N_KV_HEADS, HEAD_DIM), pl.BlockSpec((tm, QI_W), lambda i: (i, 0)),
                   pl.BlockSpec((tm, LANES), lambda i: (i, 0)), _time_minor_spec(b, t, IDX_DIM)],
        out_shape=[jax.ShapeDtypeStruct((b, N_KV_HEADS, HEAD_DIM, t), F32), jax.ShapeDtypeStruct((m, QI_W), BF16),
                   jax.ShapeDtypeStruct((m, LANES), F32), jax.ShapeDtypeStruct((b, IDX_DIM, t), F32)],
        compiler_params=_cparams("parallel"),
        name="proj_vm",
    )(h2d, w_vm, cos_tab, sin_tab, dtb)


def _attn_kernel(*refs, tq, kb, gp, topk, past, t_new, has_cache):
    n_in = 9 if has_cache else 6
    q_ref, qi_ref, wi_ref, k_ref, v_ref, kit_ref = refs[:6]
    o_ref = refs[n_in]
    kbf, vbf, kibf, sc_scr, keep_scr, qis, qs, mb_scr, acc_scr, s_scr = refs[n_in + 1:n_in + 11]
    if has_cache:
        ck_ref, cv_ref, cki_ref = refs[6:9]
        ckb, cvb, ckib = refs[n_in + 11:]
    j = pl.program_id(1)
    kbn = min(kb, t_new)
    n_cache_blocks = past // kb

    def ones_row(n):
        return jnp.where(lax.broadcasted_iota(I32, (HEAD_DIM, n), 0) == 0, 1.0, 0.0).astype(BF16)

    def stage(k_src, v_src, ki_src, k_dst, v_dst, ki_dst, n):
        for g in range(N_KV_HEADS):
            k_dst[g] = k_src[0, g].astype(BF16)
            v_dst[g, 0:HEAD_DIM, :] = v_src[0, g].astype(BF16)
            v_dst[g, HEAD_DIM:2 * HEAD_DIM, :] = ones_row(n)
        ki_dst[...] = ki_src[0].T.astype(BF16)

    @pl.when(j == 0)
    def _():
        stage(k_ref, v_ref, kit_ref, kbf, vbf, kibf, t_new)
        if has_cache:
            stage(ck_ref, cv_ref, cki_ref, ckb, cvb, ckib, past)

    for h in range(IDX_HEADS):
        qis[h * tq:(h + 1) * tq, :] = qi_ref[0, :, h * IDX_DIM:(h + 1) * IDX_DIM]
    for h in range(N_HEADS):
        qs[h * tq:(h + 1) * tq, :] = q_ref[0, :, h * HEAD_DIM:(h + 1) * HEAD_DIM]
    wi_t = wi_ref[0, :, MISC_WI:MISC_WI + IDX_HEADS].T
    hpv = max(1, LANES // tq)
    wi_wide = [jnp.concatenate([wi_t[h + p:h + p + 1, :] for p in range(hpv)], axis=1) if hpv > 1 else wi_t[h:h + 1, :]
               for h in range(0, IDX_HEADS, hpv)]

    qpos = past + j * tq + lax.broadcasted_iota(I32, (1, tq), 1)
    limit = (qpos // CHUNK + 1) * CHUNK
    n_new_blocks = (j * tq + tq + kbn - 1) // kbn

    def by_pairs(fn, n_blocks, width, init):
        def pair(i, c):
            off = pl.multiple_of(i * (2 * width), 2 * width)
            return fn(pl.multiple_of(off + width, width), fn(off, c))
        c = lax.fori_loop(0, n_blocks // 2, pair, init)
        if isinstance(n_blocks, int):
            return fn((n_blocks - 1) * width, c) if n_blocks % 2 else c
        last = pl.multiple_of((n_blocks - 1) * width, width)
        return lax.cond(n_blocks % 2 == 1, lambda c: fn(last, c), lambda c: c, c)

    def over_cache(fn, init):
        if not has_cache:
            return init
        return by_pairs(fn, n_cache_blocks, kb, init)

    def over_new(fn, init):
        if t_new <= kb:
            return fn(0, init)
        return by_pairs(fn, n_new_blocks, kbn, init)

    def lanes_at(off):
        return past + off if isinstance(off, int) else pl.multiple_of(past + off, LANES)

    def score_block(ki_blk, kpos0, width):
        lg = _dot_nt(ki_blk, qis[...])
        wide = jnp.zeros((width, hpv * tq), F32)
        for i, w_row in enumerate(wi_wide):
            wide = wide + w_row * jnp.maximum(lg[:, i * hpv * tq:(i + 1) * hpv * tq], 0.0)
        sc = wide[:, :tq]
        for p in range(1, hpv):
            sc = sc + wide[:, p * tq:(p + 1) * tq]
        kpos = kpos0 + lax.broadcasted_iota(I32, (width, 1), 0)
        sc_scr[pl.ds(kpos0, width), :] = jnp.where(kpos < limit, sc, -jnp.inf)

    def p1c(off, c):
        score_block(ckib[pl.ds(off, kb), :], off, kb)
        return c

    def p1n(off, c):
        score_block(kibf[pl.ds(off, kbn), :], lanes_at(off), kbn)
        return c

    over_cache(p1c, 0)
    over_new(p1n, 0)

    def key_to_float(c):
        return pltpu.bitcast(jnp.where(c >= 0, c, c ^ 0x7FFFFFFF), F32)

    def count(cmp, cand):
        def cnt(row_off, width, acc):
            hit = jnp.where(cmp(sc_scr[pl.ds(row_off, width), :], cand), 1.0, 0.0)
            for r in range(width // COUNT_ROWS):
                acc = acc + hit[r * COUNT_ROWS:(r + 1) * COUNT_ROWS]
            return acc
        acc = jnp.zeros((COUNT_ROWS, tq), F32)
        acc = over_cache(lambda off, a: cnt(off, kb, a), acc)
        acc = over_new(lambda off, a: cnt(lanes_at(off), kbn, a), acc)
        return jnp.sum(acc, axis=0, keepdims=True)

    def bit_step(it, prefix):
        bit = jnp.left_shift(jnp.int32(1), 31 - it)
        cand = key_to_float((prefix | bit) ^ INT_MIN)
        return jnp.where(count(jnp.greater_equal, cand) >= float(topk), prefix | bit, prefix)

    prefix = lax.fori_loop(0, 32, bit_step, jnp.zeros((1, tq), I32))
    thr = key_to_float(prefix ^ INT_MIN)
    flt_max = float(jnp.finfo(F32).max)
    thr = jnp.where(thr >= -flt_max, thr, -flt_max)

    n_ge = count(jnp.greater_equal, thr)
    has_ties = jnp.max(n_ge) > float(topk)

    def keep_ranked(row_off, width, need, seen):
        sc = sc_scr[pl.ds(row_off, width), :]
        tie = sc == thr
        tie_f = jnp.where(tie, 1.0, 0.0)
        below = lax.broadcasted_iota(I32, (width, width), 1) <= lax.broadcasted_iota(I32, (width, width), 0)
        rank = seen + _dot(jnp.where(below, 1.0, 0.0).astype(BF16), tie_f.astype(BF16))
        keep = (sc > thr) | (tie & (rank <= need))
        keep_scr[pl.ds(row_off, width), :] = jnp.where(keep, 1.0, 0.0).astype(BF16)
        return seen + jnp.sum(tie_f, axis=0, keepdims=True)

    def keep_all_ties(row_off, width, c):
        sc = sc_scr[pl.ds(row_off, width), :]
        keep_scr[pl.ds(row_off, width), :] = jnp.where(sc >= thr, 1.0, 0.0).astype(BF16)
        return c

    def with_ties():
        need = float(topk) - count(jnp.greater, thr)
        seen = over_cache(lambda off, s: keep_ranked(off, kb, need, s), jnp.zeros((1, tq), F32))
        over_new(lambda off, s: keep_ranked(lanes_at(off), kbn, need, s), seen)

    def without_ties():
        over_cache(lambda off, c: keep_all_ties(off, kb, c), 0)
        over_new(lambda off, c: keep_all_ties(lanes_at(off), kbn, c), 0)

    lax.cond(has_ties, with_ties, without_ties)

    rows = Q_PER_KV * tq

    eye = jnp.where(lax.broadcasted_iota(I32, (tq, tq), 0) == lax.broadcasted_iota(I32, (tq, tq), 1), 1.0, 0.0).astype(BF16)

    def pass_a(g0, k_of, lane_off, width):
        keep_q = _dot_nt(eye, keep_scr[pl.ds(lane_off, width), :])
        b = jnp.tile(jnp.where(keep_q > 0.5, 0.0, NEG_BIG), (Q_PER_KV, 1))
        for gi in range(gp):
            g = g0 + gi
            s = _dot(qs[g * rows:(g + 1) * rows, :], k_of(g)) + b
            s_scr[gi, :, pl.ds(lane_off, width)] = s
            m = mb_scr[gi]
            if width % LANES:
                m = jnp.maximum(m, jnp.max(s, axis=1, keepdims=True))
            else:
                for c in range(width // LANES):
                    m = jnp.maximum(m, s[:, c * LANES:(c + 1) * LANES])
            mb_scr[gi] = m

    def pass_b(g0, v_of, lane_off, width):
        for gi in range(gp):
            m = mb_scr[gi]
            m = m[:, :width] if width < LANES else jnp.tile(m, (1, width // LANES))
            p = jnp.exp(s_scr[gi, :, pl.ds(lane_off, width)] - m)
            acc_scr[gi] += _dot_nt(p.astype(BF16), v_of(g0 + gi))

    def cache_k(off):
        return lambda g: ckb[g, :, pl.ds(off, kb)]

    def cache_v(off):
        return lambda g: cvb[g, :, pl.ds(off, kb)]

    def new_k(off):
        return lambda g: kbf[g, :, pl.ds(off, kbn)]

    def new_v(off):
        return lambda g: vbf[g, :, pl.ds(off, kbn)]

    def run(fn, cache_args, new_args):
        def on_cache(off, c):
            fn(*[a(off) for a in cache_args], off, kb)
            return c

        def on_new(off, c):
            fn(*[a(off) for a in new_args], lanes_at(off), kbn)
            return c

        over_cache(on_cache, 0)
        over_new(on_new, 0)

    for g0 in range(0, N_KV_HEADS, gp):
        mb_scr[...] = jnp.full(mb_scr.shape, NEG_BIG, F32)
        run(functools.partial(pass_a, g0), [cache_k], [new_k])
        for gi in range(gp):
            mb_scr[gi] = jnp.broadcast_to(jnp.max(mb_scr[gi], axis=1, keepdims=True), (rows, LANES))
        acc_scr[...] = jnp.zeros(acc_scr.shape, F32)
        run(functools.partial(pass_b, g0), [cache_v], [new_v])

        for gi in range(gp):
            a = acc_scr[gi]
            out = a[:, :HEAD_DIM] / a[:, HEAD_DIM:HEAD_DIM + 1]
            for r in range(Q_PER_KV):
                hh = (g0 + gi) * Q_PER_KV + r
                o_ref[0, :, hh * HEAD_DIM:(hh + 1) * HEAD_DIM] = out[r * tq:(r + 1) * tq].astype(o_ref.dtype)


def _attention(q, qi, misc, kt, vt, kit, cache=None):
    b, t, _ = q.shape
    has_cache = cache is not None
    past = cache[2].shape[2] if has_cache else 0
    n_keys = past + t
    topk = min(TOPK_MAX, n_keys // 4)
    tq = min(t, ATTN_TQ)
    kb = 256
    rows = Q_PER_KV * tq
    key_w = past + -(-t // LANES) * LANES
    gp = N_KV_HEADS
    while gp > 1 and gp * rows * key_w * 4 > ATTN_LOGITS_BYTES:
        gp //= 2
    qtile = lambda w: pl.BlockSpec((1, tq, w), lambda i, j: (i, j, 0))
    heads = lambda n: pl.BlockSpec((1, N_KV_HEADS, HEAD_DIM, n), lambda i, j: (i, 0, 0, 0))
    idx = lambda n: pl.BlockSpec((1, IDX_DIM, n), lambda i, j: (i, 0, 0))
    in_specs = [qtile(N_HEADS * HEAD_DIM), qtile(QI_W), qtile(LANES), heads(t), heads(t), idx(t)]
    args = [q, qi, misc, kt, vt, kit]
    if has_cache:
        in_specs += [heads(past), heads(past), idx(past)]
        args += list(cache)
    kern = functools.partial(_attn_kernel, tq=tq, kb=kb, gp=gp, topk=topk, past=past, t_new=t, has_cache=has_cache)
    return pl.pallas_call(
        kern,
        grid=(b, t // tq),
        in_specs=in_specs,
        out_specs=qtile(N_HEADS * HEAD_DIM),
        out_shape=jax.ShapeDtypeStruct((b, t, N_HEADS * HEAD_DIM), BF16),
        scratch_shapes=[
            pltpu.VMEM((N_KV_HEADS, HEAD_DIM, t), BF16),
            pltpu.VMEM((N_KV_HEADS, 2 * HEAD_DIM, t), BF16),
            pltpu.VMEM((t, IDX_DIM), BF16),
            pltpu.VMEM((n_keys, tq), F32),
            pltpu.VMEM((n_keys, tq), BF16),
            pltpu.VMEM((IDX_HEADS * tq, IDX_DIM), BF16),
            pltpu.VMEM((N_HEADS * tq, HEAD_DIM), BF16),
            pltpu.VMEM((gp, rows, LANES), F32),
            pltpu.VMEM((gp, rows, 2 * HEAD_DIM), F32),
            pltpu.VMEM((gp, rows, key_w), F32),
        ] + ([
            pltpu.VMEM((N_KV_HEADS, HEAD_DIM, past), BF16),
            pltpu.VMEM((N_KV_HEADS, 2 * HEAD_DIM, past), BF16),
            pltpu.VMEM((past, IDX_DIM), BF16),
        ] if has_cache else []),
        compiler_params=_cparams("parallel", "arbitrary"),
        name="attn_cache" if has_cache else "attn_prompt",
    )(*args)


def _ssd_kernel(xs_ref, bc_ref, misc_ref, zs_ref, h0_ref, alog_ref, dskip_ref, gnorm_ref, expand_ref, y_ref, hout_ref):
    c = pl.program_id(1)
    L = CHUNK

    @pl.when(c == 0)
    def _():
        hout_ref[...] = h0_ref[...]

    a = -jnp.exp(alog_ref[...])
    ri = lax.broadcasted_iota(I32, (L, 3 * L), 0)
    ci = lax.broadcasted_iota(I32, (L, 3 * L), 1) % L
    tri3 = jnp.where(ri >= ci, 1.0, 0.0).astype(BF16)
    expand3 = expand_ref[...]
    row_i = lax.broadcasted_iota(I32, (L, D_INNER), 0)
    lane_j = lax.broadcasted_iota(I32, (L, D_INNER), 1) % SSM_HEAD_DIM
    causal = row_i >= lane_j
    blk_r = lax.broadcasted_iota(I32, (GROUP_W, GROUP_W), 0) // SSM_HEAD_DIM
    blk_c = lax.broadcasted_iota(I32, (GROUP_W, GROUP_W), 1) // SSM_HEAD_DIM
    same_head = blk_r == blk_c

    for sub in range(xs_ref.shape[1] // L):
        _ssd_chunk(slice(sub * L, (sub + 1) * L), xs_ref, bc_ref, misc_ref, zs_ref, dskip_ref, gnorm_ref, y_ref, hout_ref,
                   a, tri3, expand3, causal, same_head)


def _heads_over_time(x):
    L = x.shape[0]
    xt = x.T
    xtt = jnp.concatenate([xt, xt], axis=1)
    first = lax.broadcasted_iota(I32, (L, 2 * L), 1) < L
    cols = []
    for h in range(0, SSM_HEADS, 2):
        r = MISC_DT + h
        cols.append(jnp.where(first, jnp.broadcast_to(xtt[r:r + 1, :], (L, 2 * L)),
                              jnp.broadcast_to(xtt[r + 1:r + 2, :], (L, 2 * L))))
    return jnp.concatenate(cols, axis=1)


def _ssd_chunk(rows, xs_ref, bc_ref, misc_ref, zs_ref, dskip_ref, gnorm_ref, y_ref, hout_ref,
               a, tri3, expand3, causal, same_head):
    L = CHUNK
    dt = misc_ref[0, rows, :]
    acs = _exact_dot_left(tri3, dt * a, 3)
    col_acs = _exact_dot(acs, expand3, 3)
    col_dt = _exact_dot(dt, expand3[:2 * LANES], 2)
    row_acs = _heads_over_time(acs)
    row_dt = _heads_over_time(dt)
    a_last = acs[L - 1:L, MISC_DT:MISC_DT + SSM_HEADS]

    groups = range(SSM_GROUPS)
    gsl = [slice(g * GROUP_W, (g + 1) * GROUP_W) for g in groups]
    bgs = [bc_ref[0, rows, g * D_STATE:(g + 1) * D_STATE] for g in groups]
    cgs = [bc_ref[0, rows, (SSM_GROUPS + g) * D_STATE:(SSM_GROUPS + g + 1) * D_STATE] for g in groups]
    xgs = [xs_ref[0, rows, gs] for gs in gsl]
    hprevs = [hout_ref[0, g * HEADS_PER_GROUP:(g + 1) * HEADS_PER_GROUP].reshape(GROUP_W, D_STATE) for g in groups]
    cbs = [_dot_nt(cgs[g], jnp.tile(bgs[g], (HEADS_PER_GROUP, 1))) for g in groups]
    offs = [_dot_nt(cgs[g], hprevs[g].astype(BF16)) for g in groups]
    sts = []
    for g in groups:
        ce = col_acs[:, gsl[g]]
        w_state = jnp.exp(ce[L - 1:L, :] - ce) * col_dt[:, gsl[g]]
        sts.append(_dot_tn((xgs[g] * w_state).astype(BF16), bgs[g]))
    y_diags = []
    for g in groups:
        gs = gsl[g]
        mm = cbs[g] * jnp.exp(jnp.where(causal[:, gs], col_acs[:, gs] - row_acs[:, gs], -jnp.inf)) * row_dt[:, gs]
        xbd = jnp.where(same_head, jnp.tile(xgs[g].astype(BF16), (HEADS_PER_GROUP, 1)), 0.0).astype(BF16)
        y_diags.append(_dot(mm.astype(BF16), xbd))
    for g in groups:
        for r in range(HEADS_PER_GROUP):
            hh = g * HEADS_PER_GROUP + r
            decay = jnp.exp(a_last[:, hh:hh + 1])
            rs = slice(r * SSM_HEAD_DIM, (r + 1) * SSM_HEAD_DIM)
            hout_ref[0, hh] = decay * hprevs[g][rs] + sts[g][rs]
    for g in groups:
        gs = gsl[g]
        y_off = jnp.exp(col_acs[:, gs]) * offs[g]
        yt = (y_diags[g] + y_off + dskip_ref[:, gs] * xgs[g]) * zs_ref[0, rows, gs].astype(F32)
        ms = jnp.mean(yt * yt, axis=-1, keepdims=True)
        y_ref[0, rows, gs] = (yt * lax.rsqrt(ms + EPS) * gnorm_ref[:, gs]).astype(y_ref.dtype)


def _ssd(xs, bc, misc, zs, h0, a_log, dskip_full, g_norm, expand):
    b, t, _ = xs.shape
    rows = min(t, SSD_CHUNKS_PER_STEP * CHUNK)
    nc = t // rows
    chunk = lambda w: pl.BlockSpec((1, rows, w), lambda i, c: (i, c, 0))
    const2 = lambda r, w: pl.BlockSpec((r, w), lambda i, c: (0, 0))
    state = pl.BlockSpec((1, SSM_HEADS, SSM_HEAD_DIM, D_STATE), lambda i, c: (i, 0, 0, 0))
    return pl.pallas_call(
        _ssd_kernel,
        grid=(b, nc),
        in_specs=[
            chunk(D_INNER), chunk(2 * SSM_GROUPS * D_STATE), chunk(LANES), chunk(D_INNER), state,
            const2(1, LANES), const2(1, D_INNER), const2(1, D_INNER), const2(3 * LANES, D_INNER),
        ],
        out_specs=[chunk(D_INNER), state],
        out_shape=[jax.ShapeDtypeStruct((b, t, D_INNER), BF16),
                   jax.ShapeDtypeStruct((b, SSM_HEADS, SSM_HEAD_DIM, D_STATE), F32)],
        compiler_params=_cparams("parallel", "arbitrary"),
        name="ssd",
    )(xs, bc, misc, zs, h0, a_log, dskip_full, g_norm, expand)


def _merge_kernel(o_ref, y_ref, g_ref, x_ref, mod_ref, gain_ref, wa_ref, ws_ref, wo_ref, x1_ref, h2_ref):
    bb, tt, d = x_ref.shape
    rows = bb * tt
    o = o_ref[...].reshape(rows, -1)
    y = y_ref[...].reshape(rows, -1)
    gates = g_ref[...].reshape(rows, -1).astype(F32)
    mixed = gates[:, :d] * _dot(o, wa_ref[...]) + gates[:, d:] * _dot(y, ws_ref[...])
    out = _dot(mixed.astype(BF16), wo_ref[...]).reshape(bb, tt, d)
    mod = mod_ref[...]
    x1 = x_ref[...] + mod[:, 2:3, :] * out
    x1_ref[...] = x1
    h2_ref[...] = _modulated_norm(x1, mod, gain_ref[...], 3, 4).astype(h2_ref.dtype)


def _merge(o_attn, y, gates, x, mod3, mod_off, gain, wa, ws, wo):
    b, t, d = x.shape
    bb, tt = _row_blocking(b, t)
    assert mod_off % bb == 0, (mod_off, bb)
    off = mod_off // bb
    tile = lambda w: pl.BlockSpec((bb, tt, w), lambda i, j: (i, j, 0))
    const = lambda r, w: pl.BlockSpec((r, w), lambda i, j: (0, 0))
    return pl.pallas_call(
        _merge_kernel,
        grid=(b // bb, t // tt),
        in_specs=[
            tile(N_HEADS * HEAD_DIM), tile(D_INNER), tile(2 * D_MODEL), tile(d),
            pl.BlockSpec((bb, 6, d), lambda i, j: (i + off, 0, 0)),
            pl.BlockSpec((1, 1, d), lambda i, j: (0, 0, 0)),
            const(N_HEADS * HEAD_DIM, d), const(D_INNER, d), const(d, d),
        ],
        out_specs=[tile(d), tile(d)],
        out_shape=[jax.ShapeDtypeStruct((b, t, d), F32), jax.ShapeDtypeStruct((b, t, d), BF16)],
        compiler_params=_cparams("parallel", "parallel"),
        name="merge",
    )(o_attn, y, gates, x, mod3, gain.reshape(1, 1, d), wa, ws, wo)


def _ffn_kernel(h2_ref, x1_ref, mod_ref, wg_ref, wu_ref, wd_ref, o_ref, acc_ref):
    f = pl.program_id(2)
    bb, tt, d = x1_ref.shape
    h2 = h2_ref[...].reshape(bb * tt, d)
    act = _silu(_dot(h2, wg_ref[...])) * _dot(h2, wu_ref[...])
    part = _dot(act.astype(BF16), wd_ref[...])

    @pl.when(f == 0)
    def _():
        acc_ref[...] = part

    @pl.when(f > 0)
    def _():
        acc_ref[...] += part

    @pl.when(f == pl.num_programs(2) - 1)
    def _():
        o_ref[...] = x1_ref[...] + mod_ref[...][:, 5:6, :] * acc_ref[...].reshape(bb, tt, d)


def _ffn(h2, x1, mod3, mod_off, w_gu, w_down):
    b, t, d = x1.shape
    bb, tt = _row_blocking(b, t)
    assert mod_off % bb == 0, (mod_off, bb)
    off = mod_off // bb
    nf = 2
    tf = D_FF // nf
    tile = pl.BlockSpec((bb, tt, d), lambda i, j, f: (i, j, 0))
    return pl.pallas_call(
        _ffn_kernel,
        grid=(b // bb, t // tt, nf),
        in_specs=[
            tile, tile,
            pl.BlockSpec((bb, 6, d), lambda i, j, f: (i + off, 0, 0)),
            pl.BlockSpec((d, tf), lambda i, j, f: (0, f)),
            pl.BlockSpec((d, tf), lambda i, j, f: (0, nf + f)),
            pl.BlockSpec((tf, d), lambda i, j, f: (f, 0)),
        ],
        out_specs=tile,
        out_shape=jax.ShapeDtypeStruct((b, t, d), F32),
        scratch_shapes=[pltpu.VMEM((bb * tt, d), F32)],
        compiler_params=_cparams("parallel", "parallel", "arbitrary"),
        name="ffn",
    )(h2, x1, mod3, w_gu, w_gu, w_down)


def _rope_tables(t, past, tm):
    half = HEAD_DIM // 2
    inv = ROPE_THETA ** (-jnp.arange(half, dtype=F32) / half)
    ang = (past + jnp.arange(t)).astype(F32)[:, None] * inv[None, :]
    cos, sin = jnp.cos(ang), jnp.sin(ang)
    cos_t = jnp.concatenate([cos, cos, cos, cos], axis=1)
    sin_t = jnp.concatenate([-sin, sin, -sin, sin], axis=1)
    if t < tm:
        cos_t, sin_t = jnp.tile(cos_t, (tm // t, 1)), jnp.tile(sin_t, (tm // t, 1))
    return cos_t, sin_t


def _group_step(x, mod3, mod_off, cache, conv_state, ssm_state, p):
    b, t, d = x.shape
    m = b * t
    tm = min(512, m)
    past = cache[2].shape[2] if cache is not None else 0
    h = _hnorm(x, mod3, mod_off, p["g_norm_mix"]).reshape(m, d)
    cos_t, sin_t = _rope_tables(t, past, tm)
    q, kt = _proj_qk(h, p["w_qk"], p["gsum"], p["gexp"], p["qk_gain"], cos_t, sin_t, b, t)
    vt, qi, misc, kit = _proj_vm(h, p["w_vm"], cos_t, sin_t, p["dtb"], b, t)
    zs = _proj_act(h, p["w_z"], "silu", BF16, tm, D_INNER)
    gates = _proj_act(h, p["w_g"], "sigmoid", BF16, tm, 2 * D_MODEL)
    half = CONV_CH // 2
    xs, tail_x = _proj_conv(h, p["w_xs"], conv_state[:, :, :half], p["w_conv"][:, :half],
                            p["b_conv"][:, :half], b, t, F32)
    bc, tail_bc = _proj_conv(h, p["w_bc"], conv_state[:, :, half:], p["w_conv"][:, half:],
                             p["b_conv"][:, half:], b, t, BF16)

    r3 = lambda a: a.reshape(b, t, a.shape[-1])
    o_attn = _attention(r3(q), r3(qi), r3(misc), kt, vt, kit, cache)
    y, h_last = _ssd(r3(xs), r3(bc), r3(misc), r3(zs), ssm_state, p["a_log"], p["dskip_full"], p["g_ssm_norm"],
                     p["expand"])
    x1, h2 = _merge(o_attn, y, r3(gates), x, mod3, mod_off, p["g_norm_ffn"], p["w_ba"], p["w_bs"], p["w_out"])
    out = _ffn(h2, x1, mod3, mod_off, p["w_gu"], p["w_down"])
    conv_new = jnp.concatenate([tail_x, tail_bc], axis=2)[:, -(CONV_W - 1):]
    token_major = lambda a: jnp.moveaxis(a, -1, 1)
    return out, token_major(kt), token_major(vt), token_major(kit), conv_new, h_last


def _layer_params(l, w_in, g_q, g_k, g_norm_mix, g_norm_ffn, w_conv, b_conv, dt_bias, a_log, d_skip, g_ssm_norm,
                  w_branch_attn, w_branch_ssm, w_out, w_gate_up, w_down):
    sizes = (N_HEADS * HEAD_DIM, KV_W, KV_W, QI_W, IDX_DIM, IDX_HEADS, D_INNER, CONV_CH, SSM_HEADS, 2 * D_MODEL)
    offs = [0]
    for s in sizes:
        offs.append(offs[-1] + s)
    col = lambda i: w_in[l][:, offs[i]:offs[i + 1]]
    pad = LANES - (IDX_DIM + IDX_HEADS + SSM_HEADS)
    w_vm = jnp.concatenate([col(2), col(3), col(4), col(5), col(8), jnp.zeros((D_MODEL, pad), F32)], axis=1)
    head_of = jnp.arange(QK_W) // HEAD_DIM
    gsum = (head_of[:, None] == jnp.arange(LANES)[None, :]).astype(BF16)
    dtb = jnp.zeros((1, LANES), F32).at[0, MISC_DT:MISC_DT + SSM_HEADS].set(dt_bias[l])
    expand = ((jnp.arange(LANES) - MISC_DT)[:, None] == (jnp.arange(D_INNER) // SSM_HEAD_DIM)[None, :]).astype(BF16)
    a_log_lanes = jnp.zeros((1, LANES), F32).at[0, MISC_DT:MISC_DT + SSM_HEADS].set(a_log[l])
    return dict(
        w_qk=w_in[l][:, :QK_W].astype(BF16), w_vm=w_vm.astype(BF16), w_z=col(6).astype(BF16),
        w_xs=col(7)[:, :CONV_CH // 2].astype(BF16), w_bc=col(7)[:, CONV_CH // 2:].astype(BF16), w_g=col(9).astype(BF16),
        gsum=jnp.tile(gsum, (2, 1)), gexp=jnp.tile(gsum.T, (2, 1)),
        qk_gain=jnp.concatenate([jnp.tile(g_q[l], N_HEADS), jnp.tile(g_k[l], N_KV_HEADS)]).reshape(1, QK_W),
        dtb=dtb, expand=jnp.tile(expand, (3, 1)),
        g_norm_mix=g_norm_mix[l], g_norm_ffn=g_norm_ffn[l],
        w_conv=w_conv[l], b_conv=b_conv[l].reshape(1, CONV_CH), a_log=a_log_lanes,
        dskip_full=jnp.repeat(d_skip[l], SSM_HEAD_DIM).reshape(1, D_INNER), g_ssm_norm=g_ssm_norm[l].reshape(1, D_INNER),
        w_ba=w_branch_attn[l].astype(BF16), w_bs=w_branch_ssm[l].astype(BF16), w_out=w_out[l].astype(BF16),
        w_gu=w_gate_up[l].astype(BF16), w_down=w_down[l].astype(BF16),
    )


def kernel(x_prompt, x_sample, cache_k, cache_v, cache_ki, state_conv, state_ssm, c_prompt, c_sample, w_ada, b_ada, g_norm_mix, g_norm_ffn, w_in, g_q, g_k, w_conv, b_conv, dt_bias, a_log, d_skip, g_ssm_norm, w_branch_attn, w_branch_ssm, w_out, w_gate_up, w_down):
    depth = w_in.shape[0]
    bp, bs = x_prompt.shape[0], x_sample.shape[0]
    past = cache_k.shape[2]
    y_p, y_s = x_prompt, x_sample
    c_all = jnp.concatenate([c_prompt, c_sample], axis=0)
    new_p = [[] for _ in range(5)]
    new_s = [[] for _ in range(5)]
    for l in range(depth):
        p = _layer_params(l, w_in, g_q, g_k, g_norm_mix, g_norm_ffn, w_conv, b_conv, dt_bias, a_log, d_skip,
                          g_ssm_norm, w_branch_attn, w_branch_ssm, w_out, w_gate_up, w_down)
        mod3 = _ada_mod(c_all, w_ada[l], b_ada[l]).reshape(bp + bs, 6, D_MODEL)
        zero_conv = jnp.zeros((bp, CONV_W - 1, CONV_CH), F32)
        zero_ssm = jnp.zeros((bp, SSM_HEADS, SSM_HEAD_DIM, D_STATE), F32)
        y_p, *st_p = _group_step(y_p, mod3, 0, None, zero_conv, zero_ssm, p)
        time_minor = lambda a: jnp.moveaxis(a, 1, -1)
        cache = (time_minor(cache_k[l]), time_minor(cache_v[l]), time_minor(cache_ki[l]))
        y_s, *st_s = _group_step(y_s, mod3, bp, cache, state_conv[l], state_ssm[l], p)
        for acc, a in zip(new_p, st_p):
            acc.append(a)
        for acc, a in zip(new_s, st_s):
            acc.append(a)
    return (y_p, y_s, *[jnp.stack(a) for a in new_p], *[jnp.stack(a) for a in new_s])
```

```python
import functools
import math

import jax
import jax.numpy as jnp
from jax import lax
from jax.experimental import pallas as pl
from jax.experimental.pallas import tpu as pltpu

F32, BF16, I32 = jnp.float32, jnp.bfloat16, jnp.int32

D_MODEL = 1024
CHUNK = 64
N_HEADS = 16
HEAD_DIM = 64
N_KV_HEADS = 4
Q_PER_KV = N_HEADS // N_KV_HEADS
IDX_HEADS = 8
IDX_DIM = 64
TOPK_MAX = 256
ROPE_THETA = 10000.0
D_INNER = 2 * D_MODEL
SSM_HEAD_DIM = 64
SSM_HEADS = D_INNER // SSM_HEAD_DIM
SSM_GROUPS = 8
HEADS_PER_GROUP = SSM_HEADS // SSM_GROUPS
GROUP_W = HEADS_PER_GROUP * SSM_HEAD_DIM
D_STATE = 128
CONV_W = 4
CONV_CH = D_INNER + 2 * SSM_GROUPS * D_STATE
D_FF = -(-8 * D_MODEL // (3 * 256)) * 256
EPS = 1e-6
QK_W = (N_HEADS + N_KV_HEADS) * HEAD_DIM
KV_W = N_KV_HEADS * HEAD_DIM
QI_W = IDX_HEADS * IDX_DIM

LANES = 128
MISC_WI = IDX_DIM
MISC_DT = IDX_DIM + IDX_HEADS
WI_SCALE = (IDX_HEADS ** -0.5) * (IDX_DIM ** -0.5)
INT_MIN = -(2 ** 31)
NEG_BIG = -1e30
COUNT_ROWS = 64
SSD_CHUNKS_PER_STEP = 4
ATTN_TQ = 256
ATTN_LOGITS_BYTES = 16 * 1024 * 1024
VMEM_LIMIT = 56 * 1024 * 1024


def _cparams(*sem):
    return pltpu.CompilerParams(dimension_semantics=sem, vmem_limit_bytes=VMEM_LIMIT)


def _silu(x):
    h = 0.5 * x
    return h + h * jnp.tanh(h)


def _split_bf16(x, n):
    pieces = []
    r = x
    for _ in range(n):
        p = r.astype(BF16)
        pieces.append(p)
        r = r - p.astype(F32)
    return pieces


def _dot(a, b):
    return jnp.dot(a, b, preferred_element_type=F32)


def _dot_nt(a, b):
    return lax.dot_general(a, b, (((1,), (1,)), ((), ())), preferred_element_type=F32)


def _dot_tn(a, b):
    return lax.dot_general(a, b, (((0,), (0,)), ((), ())), preferred_element_type=F32)


def _exact_dot(x, m_stacked, n):
    return _dot(jnp.concatenate(_split_bf16(x, n), axis=1), m_stacked)


def _exact_dot_left(m_tiled, x, n):
    return _dot(m_tiled, jnp.concatenate(_split_bf16(x, n), axis=0))


def _rotate_half(x):
    w = x.shape[-1]
    lane = lax.broadcasted_iota(I32, x.shape, x.ndim - 1)
    first = (lane % HEAD_DIM) < (HEAD_DIM // 2)
    return jnp.where(first, pltpu.roll(x, w - HEAD_DIM // 2, x.ndim - 1), pltpu.roll(x, HEAD_DIM // 2, x.ndim - 1))


def _rope(x, cos, sin):
    reps = x.shape[-1] // LANES
    if reps > 1:
        cos = jnp.tile(cos, (1, reps))
        sin = jnp.tile(sin, (1, reps))
    return x * cos + _rotate_half(x) * sin


def _mod_kernel(c_ref, w_ref, b_ref, o_ref):
    s = _silu(c_ref[...])
    o_ref[...] = _dot(s.astype(BF16), w_ref[...].astype(BF16)) + b_ref[...]


def _ada_mod(c_all, w_ada, b_ada):
    bt = c_all.shape[0]
    n = w_ada.shape[1]
    tn = D_MODEL
    return pl.pallas_call(
        _mod_kernel,
        grid=(n // tn,),
        in_specs=[
            pl.BlockSpec((bt, D_MODEL), lambda j: (0, 0)),
            pl.BlockSpec((D_MODEL, tn), lambda j: (0, j)),
            pl.BlockSpec((1, tn), lambda j: (0, j)),
        ],
        out_specs=pl.BlockSpec((bt, tn), lambda j: (0, j)),
        out_shape=jax.ShapeDtypeStruct((bt, n), F32),
        compiler_params=_cparams("parallel"),
        name="ada_mod",
    )(c_all, w_ada, b_ada.reshape(1, n))


def _modulated_norm(x, mod, gain, shift_idx, scale_idx):
    ms = jnp.mean(x * x, axis=-1, keepdims=True)
    xn = x * lax.rsqrt(ms + EPS)
    sh = mod[:, shift_idx:shift_idx + 1, :]
    sc = mod[:, scale_idx:scale_idx + 1, :]
    return xn * gain * (1.0 + sc) + sh


def _row_blocking(b, t):
    tt = min(t, 512)
    bb = max(1, min(b, 512 // tt))
    assert t % tt == 0 and b % bb == 0 and tt % CHUNK == 0, (b, t)
    return bb, tt


def _proj_act_kernel(h_ref, w_ref, o_ref, *, act):
    acc = _dot(h_ref[...], w_ref[...])
    if act == "silu":
        acc = _silu(acc)
    elif act == "sigmoid":
        acc = jax.nn.sigmoid(acc)
    o_ref[...] = acc.astype(o_ref.dtype)


def _proj_act(h2d, w, act, out_dtype, tm, tn):
    m, k = h2d.shape
    n = w.shape[1]
    return pl.pallas_call(
        functools.partial(_proj_act_kernel, act=act),
        grid=(n // tn, m // tm),
        in_specs=[
            pl.BlockSpec((tm, k), lambda j, i: (i, 0)),
            pl.BlockSpec((k, tn), lambda j, i: (0, j)),
        ],
        out_specs=pl.BlockSpec((tm, tn), lambda j, i: (i, j)),
        out_shape=jax.ShapeDtypeStruct((m, n), out_dtype),
        compiler_params=_cparams("parallel", "parallel"),
        name="proj_" + act,
    )(h2d, w)


SUBLANES = 8
CONV_SLAB = 256


def _proj_conv_kernel(h_ref, w_ref, cst_ref, wconv_ref, bconv_ref, o_ref, tail_ref, buf, *, bb, tt, tiles_per_seq):
    i = pl.program_id(1)
    tail = CONV_W - 1
    tn = w_ref.shape[1]

    @pl.when(i % tiles_per_seq == 0)
    def _():
        for s in range(bb):
            buf[s, 0:SUBLANES - tail, :] = jnp.zeros((SUBLANES - tail, tn), F32)
            buf[s, SUBLANES - tail:SUBLANES, :] = cst_ref[s]

    h = h_ref[...]
    for c in range(tn // CONV_SLAB):
        cs = slice(c * CONV_SLAB, (c + 1) * CONV_SLAB)
        acc = _dot(h, w_ref[:, cs])
        for s in range(bb):
            buf[s, SUBLANES:SUBLANES + tt, cs] = acc[s * tt:(s + 1) * tt]
            xb = buf[s, :, cs]
            xc = bconv_ref[:, cs] + wconv_ref[tail:tail + 1, cs] * xb[SUBLANES:]
            for jj in range(tail):
                shifted = pltpu.roll(xb, tail - jj, 0)[SUBLANES:]
                xc = xc + wconv_ref[jj:jj + 1, cs] * shifted
            o_ref[s * tt:(s + 1) * tt, cs] = _silu(xc).astype(o_ref.dtype)
            last = buf[s, tt:tt + SUBLANES, cs]
            tail_ref[s, :, cs] = last
            buf[s, 0:SUBLANES, cs] = last


def _proj_conv(h2d, w, conv_state, w_conv, b_conv, b, t, out_dtype):
    m, k = h2d.shape
    n = w.shape[1]
    bb, tt = _row_blocking(b, t)
    tm = bb * tt
    tiles_per_seq = t // tt
    tn = min(n, 2048)
    kern = functools.partial(_proj_conv_kernel, bb=bb, tt=tt, tiles_per_seq=tiles_per_seq)
    seq_block = lambda rows: pl.BlockSpec((bb, rows, tn), lambda j, i: (i // tiles_per_seq, 0, j))
    return pl.pallas_call(
        kern,
        grid=(n // tn, m // tm),
        in_specs=[
            pl.BlockSpec((tm, k), lambda j, i: (i, 0)),
            pl.BlockSpec((k, tn), lambda j, i: (0, j)),
            seq_block(CONV_W - 1),
            pl.BlockSpec((CONV_W, tn), lambda j, i: (0, j)),
            pl.BlockSpec((1, tn), lambda j, i: (0, j)),
        ],
        out_specs=[pl.BlockSpec((tm, tn), lambda j, i: (i, j)), seq_block(SUBLANES)],
        out_shape=[jax.ShapeDtypeStruct((m, n), out_dtype), jax.ShapeDtypeStruct((b, SUBLANES, n), F32)],
        scratch_shapes=[pltpu.VMEM((bb, SUBLANES + tt, tn), F32)],
        compiler_params=_cparams("parallel", "arbitrary"),
        name="proj_conv",
    )(h2d, w, conv_state, w_conv, b_conv)


def _store_time_minor(ref, x):
    bb, tt = ref.shape[0], ref.shape[-1]
    for s in range(bb):
        ref[s] = x[s * tt:(s + 1) * tt, :].T.reshape(ref.shape[1:])


def _qk_kernel(x_ref, mod_ref, gmix_ref, w_ref, gsum_ref, gexp_ref, gain_ref, cos_ref, sin_ref, q_ref, k_ref, h_ref):
    bb, tt, d = x_ref.shape
    h = _modulated_norm(x_ref[...], mod_ref[...], gmix_ref[...], 0, 1).astype(h_ref.dtype).reshape(bb * tt, d)
    h_ref[...] = h
    acc = _dot(h, w_ref[...])
    ss = _exact_dot(acc * acc, gsum_ref[...], 2)
    rs = lax.rsqrt(ss * (1.0 / HEAD_DIM) + EPS)
    rs_full = _exact_dot(rs, gexp_ref[...], 2)
    xn = acc * rs_full * gain_ref[...]
    out = _rope(xn, cos_ref[...], sin_ref[...])
    nq = N_HEADS * HEAD_DIM
    q_ref[...] = (out[:, :nq] * (HEAD_DIM ** -0.5)).astype(q_ref.dtype)
    _store_time_minor(k_ref, out[:, nq:])


def _vm_kernel(h_ref, w_ref, cos_ref, sin_ref, dtb_ref, v_ref, qi_ref, misc_ref, kit_ref):
    acc = _dot(h_ref[...], w_ref[...])
    cos, sin = cos_ref[...], sin_ref[...]
    _store_time_minor(v_ref, acc[:, :KV_W])
    qi_ref[...] = _rope(acc[:, KV_W:KV_W + QI_W], cos, sin).astype(qi_ref.dtype)
    m = acc[:, KV_W + QI_W:]
    lane = lax.broadcasted_iota(I32, m.shape, 1)
    roped = _rope(m, cos, sin)
    dt = jax.nn.softplus(m + dtb_ref[...])
    misc_ref[...] = jnp.where(lane < MISC_WI, roped,
                              jnp.where(lane < MISC_DT, m * WI_SCALE,
                                        jnp.where(lane < MISC_DT + SSM_HEADS, dt, 0.0)))
    _store_time_minor(kit_ref, roped[:, :IDX_DIM])


def _table_spec(tab_rows, tm):
    nblk = tab_rows // tm
    return pl.BlockSpec((tm, LANES), lambda i: (i % nblk, 0))


def _time_minor_spec(b, t, *mid):
    bb, tt = _row_blocking(b, t)
    per_seq = t // tt
    zeros = (0,) * len(mid)
    return pl.BlockSpec((bb, *mid, tt), lambda i: (i // per_seq, *zeros, i % per_seq))


def _proj_qk(x, mod3, mod_off, g_mix, w_qk, gsum, gexp, gain, cos_tab, sin_tab):
    b, t, k = x.shape
    m = b * t
    bb, tt = _row_blocking(b, t)
    tm = bb * tt
    per_seq = t // tt
    assert mod_off % bb == 0, (mod_off, bb)
    off = mod_off // bb
    nq = N_HEADS * HEAD_DIM
    const = lambda i: (0, 0)
    return pl.pallas_call(
        _qk_kernel,
        grid=(m // tm,),
        in_specs=[
            pl.BlockSpec((bb, tt, k), lambda i: (i // per_seq, i % per_seq, 0)),
            pl.BlockSpec((bb, 6, k), lambda i: (i // per_seq + off, 0, 0)),
            pl.BlockSpec((1, 1, k), lambda i: (0, 0, 0)),
            pl.BlockSpec((k, QK_W), const),
            pl.BlockSpec((2 * QK_W, LANES), const),
            pl.BlockSpec((2 * LANES, QK_W), const),
            pl.BlockSpec((1, QK_W), const),
            _table_spec(cos_tab.shape[0], tm),
            _table_spec(sin_tab.shape[0], tm),
        ],
        out_specs=[pl.BlockSpec((tm, nq), lambda i: (i, 0)), _time_minor_spec(b, t, N_KV_HEADS, HEAD_DIM),
                   pl.BlockSpec((tm, k), lambda i: (i, 0))],
        out_shape=[jax.ShapeDtypeStruct((m, nq), BF16), jax.ShapeDtypeStruct((b, N_KV_HEADS, HEAD_DIM, t), F32),
                   jax.ShapeDtypeStruct((m, k), BF16)],
        compiler_params=_cparams("parallel"),
        name="proj_qk",
    )(x, mod3, g_mix.reshape(1, 1, k), w_qk, gsum, gexp, gain, cos_tab, sin_tab)


def _proj_vm(h2d, w_vm, cos_tab, sin_tab, dtb, b, t):
    m, k = h2d.shape
    tm = math.prod(_row_blocking(b, t))
    wn = w_vm.shape[1]
    const = lambda i: (0, 0)
    return pl.pallas_call(
        _vm_kernel,
        grid=(m // tm,),
        in_specs=[
            pl.BlockSpec((tm, k), lambda i: (i, 0)),
            pl.BlockSpec((k, wn), const),
            _table_spec(cos_tab.shape[0], tm),
            _table_spec(sin_tab.shape[0], tm),
            pl.BlockSpec((1, LANES), const),
        ],
        out_specs=[_time_minor_spec(b, t, N_KV_HEADS, HEAD_DIM), pl.BlockSpec((tm, QI_W), lambda i: (i, 0)),
                   pl.BlockSpec((tm, LANES), lambda i: (i, 0)), _time_minor_spec(b, t, IDX_DIM)],
        out_shape=[jax.ShapeDtypeStruct((b, N_KV_HEADS, HEAD_DIM, t), F32), jax.ShapeDtypeStruct((m, QI_W), BF16),
                   jax.ShapeDtypeStruct((m, LANES), F32), jax.ShapeDtypeStruct((b, IDX_DIM, t), F32)],
        compiler_params=_cparams("parallel"),
        name="proj_vm",
    )(h2d, w_vm, cos_tab, sin_tab, dtb)


def _attn_kernel(*refs, tq, kb, gp, topk, past, t_new, has_cache):
    n_in = 9 if has_cache else 6
    q_ref, qi_ref, wi_ref, k_ref, v_ref, kit_ref = refs[:6]
    o_ref = refs[n_in]
    kbf, vbf, kibf, sc_scr, keep_scr, qis, qs, mb_scr, acc_scr, s_scr = refs[n_in + 1:n_in + 11]
    if has_cache:
        ck_ref, cv_ref, cki_ref = refs[6:9]
        ckb, cvb, ckib = refs[n_in + 11:]
    j = pl.program_id(1)
    kbn = min(kb, t_new)
    n_cache_blocks = past // kb

    def ones_row(n):
        return jnp.where(lax.broadcasted_iota(I32, (HEAD_DIM, n), 0) == 0, 1.0, 0.0).astype(BF16)

    def stage(k_src, v_src, ki_src, k_dst, v_dst, ki_dst, n):
        for g in range(N_KV_HEADS):
            k_dst[g] = k_src[0, g].astype(BF16)
            v_dst[g, 0:HEAD_DIM, :] = v_src[0, g].astype(BF16)
            v_dst[g, HEAD_DIM:2 * HEAD_DIM, :] = ones_row(n)
        ki_dst[...] = ki_src[0].T.astype(BF16)

    @pl.when(j == 0)
    def _():
        stage(k_ref, v_ref, kit_ref, kbf, vbf, kibf, t_new)
        if has_cache:
            stage(ck_ref, cv_ref, cki_ref, ckb, cvb, ckib, past)

    for h in range(IDX_HEADS):
        qis[h * tq:(h + 1) * tq, :] = qi_ref[0, :, h * IDX_DIM:(h + 1) * IDX_DIM]
    for h in range(N_HEADS):
        qs[h * tq:(h + 1) * tq, :] = q_ref[0, :, h * HEAD_DIM:(h + 1) * HEAD_DIM]
    wi_t = wi_ref[0, :, MISC_WI:MISC_WI + IDX_HEADS].T
    hpv = max(1, LANES // tq)
    wi_wide = [jnp.concatenate([wi_t[h + p:h + p + 1, :] for p in range(hpv)], axis=1) if hpv > 1 else wi_t[h:h + 1, :]
               for h in range(0, IDX_HEADS, hpv)]

    qpos = past + j * tq + lax.broadcasted_iota(I32, (1, tq), 1)
    limit = (qpos // CHUNK + 1) * CHUNK
    n_new_blocks = (j * tq + tq + kbn - 1) // kbn

    def by_pairs(fn, n_blocks, width, init):
        def pair(i, c):
            off = pl.multiple_of(i * (2 * width), 2 * width)
            return fn(pl.multiple_of(off + width, width), fn(off, c))
        c = lax.fori_loop(0, n_blocks // 2, pair, init)
        if isinstance(n_blocks, int):
            return fn((n_blocks - 1) * width, c) if n_blocks % 2 else c
        last = pl.multiple_of((n_blocks - 1) * width, width)
        return lax.cond(n_blocks % 2 == 1, lambda c: fn(last, c), lambda c: c, c)

    def over_cache(fn, init):
        if not has_cache:
            return init
        return by_pairs(fn, n_cache_blocks, kb, init)

    def over_new(fn, init):
        if t_new <= kb:
            return fn(0, init)
        return by_pairs(fn, n_new_blocks, kbn, init)

    def lanes_at(off):
        return past + off if isinstance(off, int) else pl.multiple_of(past + off, LANES)

    def score_block(ki_blk, kpos0, width):
        lg = _dot_nt(ki_blk, qis[...])
        wide = jnp.zeros((width, hpv * tq), F32)
        for i, w_row in enumerate(wi_wide):
            wide = wide + w_row * jnp.maximum(lg[:, i * hpv * tq:(i + 1) * hpv * tq], 0.0)
        sc = wide[:, :tq]
        for p in range(1, hpv):
            sc = sc + wide[:, p * tq:(p + 1) * tq]
        kpos = kpos0 + lax.broadcasted_iota(I32, (width, 1), 0)
        sc_scr[pl.ds(kpos0, width), :] = jnp.where(kpos < limit, sc, -jnp.inf)

    def p1c(off, c):
        score_block(ckib[pl.ds(off, kb), :], off, kb)
        return c

    def p1n(off, c):
        score_block(kibf[pl.ds(off, kbn), :], lanes_at(off), kbn)
        return c

    over_cache(p1c, 0)
    over_new(p1n, 0)

    def key_to_float(c):
        return pltpu.bitcast(jnp.where(c >= 0, c, c ^ 0x7FFFFFFF), F32)

    def count(cmp, cand):
        def cnt(row_off, width, acc):
            hit = jnp.where(cmp(sc_scr[pl.ds(row_off, width), :], cand), 1.0, 0.0)
            for r in range(width // COUNT_ROWS):
                acc = acc + hit[r * COUNT_ROWS:(r + 1) * COUNT_ROWS]
            return acc
        acc = jnp.zeros((COUNT_ROWS, tq), F32)
        acc = over_cache(lambda off, a: cnt(off, kb, a), acc)
        acc = over_new(lambda off, a: cnt(lanes_at(off), kbn, a), acc)
        return jnp.sum(acc, axis=0, keepdims=True)

    def bit_step(it, prefix):
        bit = jnp.left_shift(jnp.int32(1), 31 - it)
        cand = key_to_float((prefix | bit) ^ INT_MIN)
        return jnp.where(count(jnp.greater_equal, cand) >= float(topk), prefix | bit, prefix)

    prefix = lax.fori_loop(0, 32, bit_step, jnp.zeros((1, tq), I32))
    thr = key_to_float(prefix ^ INT_MIN)
    flt_max = float(jnp.finfo(F32).max)
    thr = jnp.where(thr >= -flt_max, thr, -flt_max)

    n_ge = count(jnp.greater_equal, thr)
    has_ties = jnp.max(n_ge) > float(topk)

    def keep_ranked(row_off, width, need, seen):
        sc = sc_scr[pl.ds(row_off, width), :]
        tie = sc == thr
        tie_f = jnp.where(tie, 1.0, 0.0)
        below = lax.broadcasted_iota(I32, (width, width), 1) <= lax.broadcasted_iota(I32, (width, width), 0)
        rank = seen + _dot(jnp.where(below, 1.0, 0.0).astype(BF16), tie_f.astype(BF16))
        keep = (sc > thr) | (tie & (rank <= need))
        keep_scr[pl.ds(row_off, width), :] = jnp.where(keep, 1.0, 0.0).astype(BF16)
        return seen + jnp.sum(tie_f, axis=0, keepdims=True)

    def keep_all_ties(row_off, width, c):
        sc = sc_scr[pl.ds(row_off, width), :]
        keep_scr[pl.ds(row_off, width), :] = jnp.where(sc >= thr, 1.0, 0.0).astype(BF16)
        return c

    def with_ties():
        need = float(topk) - count(jnp.greater, thr)
        seen = over_cache(lambda off, s: keep_ranked(off, kb, need, s), jnp.zeros((1, tq), F32))
        over_new(lambda off, s: keep_ranked(lanes_at(off), kbn, need, s), seen)

    def without_ties():
        over_cache(lambda off, c: keep_all_ties(off, kb, c), 0)
        over_new(lambda off, c: keep_all_ties(lanes_at(off), kbn, c), 0)

    lax.cond(has_ties, with_ties, without_ties)

    rows = Q_PER_KV * tq

    eye = jnp.where(lax.broadcasted_iota(I32, (tq, tq), 0) == lax.broadcasted_iota(I32, (tq, tq), 1), 1.0, 0.0).astype(BF16)

    def pass_a(g0, k_of, lane_off, width):
        keep_q = _dot_nt(eye, keep_scr[pl.ds(lane_off, width), :])
        b = jnp.tile(jnp.where(keep_q > 0.5, 0.0, NEG_BIG), (Q_PER_KV, 1))
        for gi in range(gp):
            g = g0 + gi
            s = _dot(qs[g * rows:(g + 1) * rows, :], k_of(g)) + b
            s_scr[gi, :, pl.ds(lane_off, width)] = s
            m = mb_scr[gi]
            if width % LANES:
                m = jnp.maximum(m, jnp.max(s, axis=1, keepdims=True))
            else:
                for c in range(width // LANES):
                    m = jnp.maximum(m, s[:, c * LANES:(c + 1) * LANES])
            mb_scr[gi] = m

    def pass_b(g0, v_of, lane_off, width):
        for gi in range(gp):
            m = mb_scr[gi]
            m = m[:, :width] if width < LANES else jnp.tile(m, (1, width // LANES))
            p = jnp.exp(s_scr[gi, :, pl.ds(lane_off, width)] - m)
            acc_scr[gi] += _dot_nt(p.astype(BF16), v_of(g0 + gi))

    def cache_k(off):
        return lambda g: ckb[g, :, pl.ds(off, kb)]

    def cache_v(off):
        return lambda g: cvb[g, :, pl.ds(off, kb)]

    def new_k(off):
        return lambda g: kbf[g, :, pl.ds(off, kbn)]

    def new_v(off):
        return lambda g: vbf[g, :, pl.ds(off, kbn)]

    def run(fn, cache_args, new_args):
        def on_cache(off, c):
            fn(*[a(off) for a in cache_args], off, kb)
            return c

        def on_new(off, c):
            fn(*[a(off) for a in new_args], lanes_at(off), kbn)
            return c

        over_cache(on_cache, 0)
        over_new(on_new, 0)

    for g0 in range(0, N_KV_HEADS, gp):
        mb_scr[...] = jnp.full(mb_scr.shape, NEG_BIG, F32)
        run(functools.partial(pass_a, g0), [cache_k], [new_k])
        for gi in range(gp):
            mb_scr[gi] = jnp.broadcast_to(jnp.max(mb_scr[gi], axis=1, keepdims=True), (rows, LANES))
        acc_scr[...] = jnp.zeros(acc_scr.shape, F32)
        run(functools.partial(pass_b, g0), [cache_v], [new_v])

        for gi in range(gp):
            a = acc_scr[gi]
            out = a[:, :HEAD_DIM] / a[:, HEAD_DIM:HEAD_DIM + 1]
            for r in range(Q_PER_KV):
                hh = (g0 + gi) * Q_PER_KV + r
                o_ref[0, :, hh * HEAD_DIM:(hh + 1) * HEAD_DIM] = out[r * tq:(r + 1) * tq].astype(o_ref.dtype)


def _attention(q, qi, misc, kt, vt, kit, cache=None):
    b, t, _ = q.shape
    has_cache = cache is not None
    past = cache[2].shape[2] if has_cache else 0
    n_keys = past + t
    topk = min(TOPK_MAX, n_keys // 4)
    tq = min(t, ATTN_TQ)
    kb = 256
    rows = Q_PER_KV * tq
    key_w = past + -(-t // LANES) * LANES
    gp = N_KV_HEADS
    while gp > 1 and gp * rows * key_w * 4 > ATTN_LOGITS_BYTES:
        gp //= 2
    qtile = lambda w: pl.BlockSpec((1, tq, w), lambda i, j: (i, j, 0))
    heads = lambda n: pl.BlockSpec((1, N_KV_HEADS, HEAD_DIM, n), lambda i, j: (i, 0, 0, 0))
    idx = lambda n: pl.BlockSpec((1, IDX_DIM, n), lambda i, j: (i, 0, 0))
    in_specs = [qtile(N_HEADS * HEAD_DIM), qtile(QI_W), qtile(LANES), heads(t), heads(t), idx(t)]
    args = [q, qi, misc, kt, vt, kit]
    if has_cache:
        in_specs += [heads(past), heads(past), idx(past)]
        args += list(cache)
    kern = functools.partial(_attn_kernel, tq=tq, kb=kb, gp=gp, topk=topk, past=past, t_new=t, has_cache=has_cache)
    return pl.pallas_call(
        kern,
        grid=(b, t // tq),
        in_specs=in_specs,
        out_specs=qtile(N_HEADS * HEAD_DIM),
        out_shape=jax.ShapeDtypeStruct((b, t, N_HEADS * HEAD_DIM), BF16),
        scratch_shapes=[
            pltpu.VMEM((N_KV_HEADS, HEAD_DIM, t), BF16),
            pltpu.VMEM((N_KV_HEADS, 2 * HEAD_DIM, t), BF16),
            pltpu.VMEM((t, IDX_DIM), BF16),
            pltpu.VMEM((n_keys, tq), F32),
            pltpu.VMEM((n_keys, tq), BF16),
            pltpu.VMEM((IDX_HEADS * tq, IDX_DIM), BF16),
            pltpu.VMEM((N_HEADS * tq, HEAD_DIM), BF16),
            pltpu.VMEM((gp, rows, LANES), F32),
            pltpu.VMEM((gp, rows, 2 * HEAD_DIM), F32),
            pltpu.VMEM((gp, rows, key_w), F32),
        ] + ([
            pltpu.VMEM((N_KV_HEADS, HEAD_DIM, past), BF16),
            pltpu.VMEM((N_KV_HEADS, 2 * HEAD_DIM, past), BF16),
            pltpu.VMEM((past, IDX_DIM), BF16),
        ] if has_cache else []),
        compiler_params=_cparams("parallel", "arbitrary"),
        name="attn_cache" if has_cache else "attn_prompt",
    )(*args)


def _ssd_kernel(xs_ref, bc_ref, misc_ref, zs_ref, h0_ref, alog_ref, dskip_ref, gnorm_ref, expand_ref, y_ref, hout_ref):
    c = pl.program_id(1)
    L = CHUNK

    @pl.when(c == 0)
    def _():
        hout_ref[...] = h0_ref[...]

    a = -jnp.exp(alog_ref[...])
    ri = lax.broadcasted_iota(I32, (L, 3 * L), 0)
    ci = lax.broadcasted_iota(I32, (L, 3 * L), 1) % L
    tri3 = jnp.where(ri >= ci, 1.0, 0.0).astype(BF16)
    expand3 = expand_ref[...]
    row_i = lax.broadcasted_iota(I32, (L, D_INNER), 0)
    lane_j = lax.broadcasted_iota(I32, (L, D_INNER), 1) % SSM_HEAD_DIM
    causal = row_i >= lane_j
    blk_r = lax.broadcasted_iota(I32, (GROUP_W, GROUP_W), 0) // SSM_HEAD_DIM
    blk_c = lax.broadcasted_iota(I32, (GROUP_W, GROUP_W), 1) // SSM_HEAD_DIM
    same_head = blk_r == blk_c

    for sub in range(xs_ref.shape[1] // L):
        _ssd_chunk(slice(sub * L, (sub + 1) * L), xs_ref, bc_ref, misc_ref, zs_ref, dskip_ref, gnorm_ref, y_ref, hout_ref,
                   a, tri3, expand3, causal, same_head)


def _heads_over_time(x):
    L = x.shape[0]
    xt = x.T
    xtt = jnp.concatenate([xt, xt], axis=1)
    first = lax.broadcasted_iota(I32, (L, 2 * L), 1) < L
    cols = []
    for h in range(0, SSM_HEADS, 2):
        r = MISC_DT + h
        cols.append(jnp.where(first, jnp.broadcast_to(xtt[r:r + 1, :], (L, 2 * L)),
                              jnp.broadcast_to(xtt[r + 1:r + 2, :], (L, 2 * L))))
    return jnp.concatenate(cols, axis=1)


def _ssd_chunk(rows, xs_ref, bc_ref, misc_ref, zs_ref, dskip_ref, gnorm_ref, y_ref, hout_ref,
               a, tri3, expand3, causal, same_head):
    L = CHUNK
    dt = misc_ref[0, rows, :]
    acs = _exact_dot_left(tri3, dt * a, 3)
    col_acs = _exact_dot(acs, expand3, 3)
    col_dt = _exact_dot(dt, expand3[:2 * LANES], 2)
    row_acs = _heads_over_time(acs)
    row_dt = _heads_over_time(dt)
    a_last = acs[L - 1:L, MISC_DT:MISC_DT + SSM_HEADS]

    groups = range(SSM_GROUPS)
    gsl = [slice(g * GROUP_W, (g + 1) * GROUP_W) for g in groups]
    bgs = [bc_ref[0, rows, g * D_STATE:(g + 1) * D_STATE] for g in groups]
    cgs = [bc_ref[0, rows, (SSM_GROUPS + g) * D_STATE:(SSM_GROUPS + g + 1) * D_STATE] for g in groups]
    xgs = [xs_ref[0, rows, gs] for gs in gsl]
    hprevs = [hout_ref[0, g * HEADS_PER_GROUP:(g + 1) * HEADS_PER_GROUP].reshape(GROUP_W, D_STATE) for g in groups]
    cbs = [_dot_nt(cgs[g], jnp.tile(bgs[g], (HEADS_PER_GROUP, 1))) for g in groups]
    offs = [_dot_nt(cgs[g], hprevs[g].astype(BF16)) for g in groups]
    sts = []
    for g in groups:
        ce = col_acs[:, gsl[g]]
        w_state = jnp.exp(ce[L - 1:L, :] - ce) * col_dt[:, gsl[g]]
        sts.append(_dot_tn((xgs[g] * w_state).astype(BF16), bgs[g]))
    y_diags = []
    for g in groups:
        gs = gsl[g]
        mm = cbs[g] * jnp.exp(jnp.where(causal[:, gs], col_acs[:, gs] - row_acs[:, gs], -jnp.inf)) * row_dt[:, gs]
        xbd = jnp.where(same_head, jnp.tile(xgs[g].astype(BF16), (HEADS_PER_GROUP, 1)), 0.0).astype(BF16)
        y_diags.append(_dot(mm.astype(BF16), xbd))
    for g in groups:
        for r in range(HEADS_PER_GROUP):
            hh = g * HEADS_PER_GROUP + r
            decay = jnp.exp(a_last[:, hh:hh + 1])
            rs = slice(r * SSM_HEAD_DIM, (r + 1) * SSM_HEAD_DIM)
            hout_ref[0, hh] = decay * hprevs[g][rs] + sts[g][rs]
    for g in groups:
        gs = gsl[g]
        y_off = jnp.exp(col_acs[:, gs]) * offs[g]
        yt = (y_diags[g] + y_off + dskip_ref[:, gs] * xgs[g]) * zs_ref[0, rows, gs].astype(F32)
        ms = jnp.mean(yt * yt, axis=-1, keepdims=True)
        y_ref[0, rows, gs] = (yt * lax.rsqrt(ms + EPS) * gnorm_ref[:, gs]).astype(y_ref.dtype)


def _ssd(xs, bc, misc, zs, h0, a_log, dskip_full, g_norm, expand):
    b, t, _ = xs.shape
    rows = min(t, SSD_CHUNKS_PER_STEP * CHUNK)
    nc = t // rows
    chunk = lambda w: pl.BlockSpec((1, rows, w), lambda i, c: (i, c, 0))
    const2 = lambda r, w: pl.BlockSpec((r, w), lambda i, c: (0, 0))
    state = pl.BlockSpec((1, SSM_HEADS, SSM_HEAD_DIM, D_STATE), lambda i, c: (i, 0, 0, 0))
    return pl.pallas_call(
        _ssd_kernel,
        grid=(b, nc),
        in_specs=[
            chunk(D_INNER), chunk(2 * SSM_GROUPS * D_STATE), chunk(LANES), chunk(D_INNER), state,
            const2(1, LANES), const2(1, D_INNER), const2(1, D_INNER), const2(3 * LANES, D_INNER),
        ],
        out_specs=[chunk(D_INNER), state],
        out_shape=[jax.ShapeDtypeStruct((b, t, D_INNER), BF16),
                   jax.ShapeDtypeStruct((b, SSM_HEADS, SSM_HEAD_DIM, D_STATE), F32)],
        compiler_params=_cparams("parallel", "arbitrary"),
        name="ssd",
    )(xs, bc, misc, zs, h0, a_log, dskip_full, g_norm, expand)


def _merge_kernel(o_ref, y_ref, g_ref, x_ref, mod_ref, gain_ref, wa_ref, ws_ref, wo_ref, x1_ref, h2_ref):
    bb, tt, d = x_ref.shape
    rows = bb * tt
    o = o_ref[...].reshape(rows, -1)
    y = y_ref[...].reshape(rows, -1)
    gates = g_ref[...].reshape(rows, -1).astype(F32)
    mixed = gates[:, :d] * _dot(o, wa_ref[...]) + gates[:, d:] * _dot(y, ws_ref[...])
    out = _dot(mixed.astype(BF16), wo_ref[...]).reshape(bb, tt, d)
    mod = mod_ref[...]
    x1 = x_ref[...] + mod[:, 2:3, :] * out
    x1_ref[...] = x1
    h2_ref[...] = _modulated_norm(x1, mod, gain_ref[...], 3, 4).astype(h2_ref.dtype)


def _merge(o_attn, y, gates, x, mod3, mod_off, gain, wa, ws, wo):
    b, t, d = x.shape
    bb, tt = _row_blocking(b, t)
    assert mod_off % bb == 0, (mod_off, bb)
    off = mod_off // bb
    tile = lambda w: pl.BlockSpec((bb, tt, w), lambda i, j: (i, j, 0))
    const = lambda r, w: pl.BlockSpec((r, w), lambda i, j: (0, 0))
    return pl.pallas_call(
        _merge_kernel,
        grid=(b // bb, t // tt),
        in_specs=[
            tile(N_HEADS * HEAD_DIM), tile(D_INNER), tile(2 * D_MODEL), tile(d),
            pl.BlockSpec((bb, 6, d), lambda i, j: (i + off, 0, 0)),
            pl.BlockSpec((1, 1, d), lambda i, j: (0, 0, 0)),
            const(N_HEADS * HEAD_DIM, d), const(D_INNER, d), const(d, d),
        ],
        out_specs=[tile(d), tile(d)],
        out_shape=[jax.ShapeDtypeStruct((b, t, d), F32), jax.ShapeDtypeStruct((b, t, d), BF16)],
        compiler_params=_cparams("parallel", "parallel"),
        name="merge",
    )(o_attn, y, gates, x, mod3, gain.reshape(1, 1, d), wa, ws, wo)


def _ffn_kernel(h2_ref, x1_ref, mod_ref, wg_ref, wu_ref, wd_ref, o_ref, acc_ref):
    f = pl.program_id(2)
    bb, tt, d = x1_ref.shape
    h2 = h2_ref[...].reshape(bb * tt, d)
    act = _silu(_dot(h2, wg_ref[...])) * _dot(h2, wu_ref[...])
    part = _dot(act.astype(BF16), wd_ref[...])

    @pl.when(f == 0)
    def _():
        acc_ref[...] = part

    @pl.when(f > 0)
    def _():
        acc_ref[...] += part

    @pl.when(f == pl.num_programs(2) - 1)
    def _():
        o_ref[...] = x1_ref[...] + mod_ref[...][:, 5:6, :] * acc_ref[...].reshape(bb, tt, d)


def _ffn(h2, x1, mod3, mod_off, w_gu, w_down):
    b, t, d = x1.shape
    bb, tt = _row_blocking(b, t)
    assert mod_off % bb == 0, (mod_off, bb)
    off = mod_off // bb
    nf = 2
    tf = D_FF // nf
    tile = pl.BlockSpec((bb, tt, d), lambda i, j, f: (i, j, 0))
    return pl.pallas_call(
        _ffn_kernel,
        grid=(b // bb, t // tt, nf),
        in_specs=[
            tile, tile,
            pl.BlockSpec((bb, 6, d), lambda i, j, f: (i + off, 0, 0)),
            pl.BlockSpec((d, tf), lambda i, j, f: (0, f)),
            pl.BlockSpec((d, tf), lambda i, j, f: (0, nf + f)),
            pl.BlockSpec((tf, d), lambda i, j, f: (f, 0)),
        ],
        out_specs=tile,
        out_shape=jax.ShapeDtypeStruct((b, t, d), F32),
        scratch_shapes=[pltpu.VMEM((bb * tt, d), F32)],
        compiler_params=_cparams("parallel", "parallel", "arbitrary"),
        name="ffn",
    )(h2, x1, mod3, w_gu, w_gu, w_down)


def _rope_tables(t, past, tm):
    half = HEAD_DIM // 2
    inv = ROPE_THETA ** (-jnp.arange(half, dtype=F32) / half)
    ang = (past + jnp.arange(t)).astype(F32)[:, None] * inv[None, :]
    cos, sin = jnp.cos(ang), jnp.sin(ang)
    cos_t = jnp.concatenate([cos, cos, cos, cos], axis=1)
    sin_t = jnp.concatenate([-sin, sin, -sin, sin], axis=1)
    if t < tm:
        cos_t, sin_t = jnp.tile(cos_t, (tm // t, 1)), jnp.tile(sin_t, (tm // t, 1))
    return cos_t, sin_t


def _group_step(x, mod3, mod_off, cache, conv_state, ssm_state, p):
    b, t, d = x.shape
    m = b * t
    tm = min(512, m)
    past = cache[2].shape[2] if cache is not None else 0
    cos_t, sin_t = _rope_tables(t, past, tm)
    q, kt, h = _proj_qk(x, mod3, mod_off, p["g_norm_mix"], p["w_qk"], p["gsum"], p["gexp"], p["qk_gain"], cos_t, sin_t)
    vt, qi, misc, kit = _proj_vm(h, p["w_vm"], cos_t, sin_t, p["dtb"], b, t)
    zs = _proj_act(h, p["w_z"], "silu", BF16, tm, D_INNER)
    gates = _proj_act(h, p["w_g"], "sigmoid", BF16, tm, 2 * D_MODEL)
    half = CONV_CH // 2
    xs, tail_x = _proj_conv(h, p["w_xs"], conv_state[:, :, :half], p["w_conv"][:, :half],
                            p["b_conv"][:, :half], b, t, F32)
    bc, tail_bc = _proj_conv(h, p["w_bc"], conv_state[:, :, half:], p["w_conv"][:, half:],
                             p["b_conv"][:, half:], b, t, BF16)

    r3 = lambda a: a.reshape(b, t, a.shape[-1])
    o_attn = _attention(r3(q), r3(qi), r3(misc), kt, vt, kit, cache)
    y, h_last = _ssd(r3(xs), r3(bc), r3(misc), r3(zs), ssm_state, p["a_log"], p["dskip_full"], p["g_ssm_norm"],
                     p["expand"])
    x1, h2 = _merge(o_attn, y, r3(gates), x, mod3, mod_off, p["g_norm_ffn"], p["w_ba"], p["w_bs"], p["w_out"])
    out = _ffn(h2, x1, mod3, mod_off, p["w_gu"], p["w_down"])
    conv_new = jnp.concatenate([tail_x, tail_bc], axis=2)[:, -(CONV_W - 1):]
    token_major = lambda a: jnp.moveaxis(a, -1, 1)
    return out, token_major(kt), token_major(vt), token_major(kit), conv_new, h_last


IN_SIZES = (N_HEADS * HEAD_DIM, KV_W, KV_W, QI_W, IDX_DIM, IDX_HEADS, D_INNER, CONV_CH, SSM_HEADS, 2 * D_MODEL)
IN_OFFS = tuple(sum(IN_SIZES[:i]) for i in range(len(IN_SIZES) + 1))
VM_W = KV_W + QI_W + LANES


def _repack_kernel(w_ref, qk_ref, vm_ref, z_ref, xs_ref, bc_ref, g_ref):
    o = IN_OFFS
    col = lambda a, b: w_ref[:, a:b].astype(BF16)
    qk_ref[...] = col(o[0], o[2])
    n_vik = o[6] - o[2]
    vm_ref[:, 0:n_vik] = col(o[2], o[6])
    vm_ref[:, n_vik:n_vik + SSM_HEADS] = col(o[8], o[9])
    vm_ref[:, n_vik + SSM_HEADS:] = jnp.zeros((w_ref.shape[0], VM_W - n_vik - SSM_HEADS), BF16)
    z_ref[...] = col(o[6], o[7])
    xs_ref[...] = col(o[7], o[7] + CONV_CH // 2)
    bc_ref[...] = col(o[7] + CONV_CH // 2, o[8])
    g_ref[...] = col(o[9], o[10])


def _repack_w_in(w):
    d, n = w.shape
    tr = 128
    widths = (QK_W, VM_W, D_INNER, CONV_CH // 2, CONV_CH // 2, 2 * D_MODEL)
    return pl.pallas_call(
        _repack_kernel,
        grid=(d // tr,),
        in_specs=[pl.BlockSpec((tr, n), lambda i: (i, 0))],
        out_specs=[pl.BlockSpec((tr, wd), lambda i: (i, 0)) for wd in widths],
        out_shape=[jax.ShapeDtypeStruct((d, wd), BF16) for wd in widths],
        compiler_params=_cparams("parallel"),
        name="repack_w_in",
    )(w)


def _layer_params(l, w_in, g_q, g_k, g_norm_mix, g_norm_ffn, w_conv, b_conv, dt_bias, a_log, d_skip, g_ssm_norm,
                  w_branch_attn, w_branch_ssm, w_out, w_gate_up, w_down):
    w_qk, w_vm, w_z, w_xs, w_bc, w_g = _repack_w_in(w_in[l])
    head_of = jnp.arange(QK_W) // HEAD_DIM
    gsum = (head_of[:, None] == jnp.arange(LANES)[None, :]).astype(BF16)
    dtb = jnp.zeros((1, LANES), F32).at[0, MISC_DT:MISC_DT + SSM_HEADS].set(dt_bias[l])
    expand = ((jnp.arange(LANES) - MISC_DT)[:, None] == (jnp.arange(D_INNER) // SSM_HEAD_DIM)[None, :]).astype(BF16)
    a_log_lanes = jnp.zeros((1, LANES), F32).at[0, MISC_DT:MISC_DT + SSM_HEADS].set(a_log[l])
    return dict(
        w_qk=w_qk, w_vm=w_vm, w_z=w_z, w_xs=w_xs, w_bc=w_bc, w_g=w_g,
        gsum=jnp.tile(gsum, (2, 1)), gexp=jnp.tile(gsum.T, (2, 1)),
        qk_gain=jnp.concatenate([jnp.tile(g_q[l], N_HEADS), jnp.tile(g_k[l], N_KV_HEADS)]).reshape(1, QK_W),
        dtb=dtb, expand=jnp.tile(expand, (3, 1)),
        g_norm_mix=g_norm_mix[l], g_norm_ffn=g_norm_ffn[l],
        w_conv=w_conv[l], b_conv=b_conv[l].reshape(1, CONV_CH), a_log=a_log_lanes,
        dskip_full=jnp.repeat(d_skip[l], SSM_HEAD_DIM).reshape(1, D_INNER), g_ssm_norm=g_ssm_norm[l].reshape(1, D_INNER),
        w_ba=w_branch_attn[l].astype(BF16), w_bs=w_branch_ssm[l].astype(BF16), w_out=w_out[l].astype(BF16),
        w_gu=w_gate_up[l].astype(BF16), w_down=w_down[l].astype(BF16),
    )


def kernel(x_prompt, x_sample, cache_k, cache_v, cache_ki, state_conv, state_ssm, c_prompt, c_sample, w_ada, b_ada, g_norm_mix, g_norm_ffn, w_in, g_q, g_k, w_conv, b_conv, dt_bias, a_log, d_skip, g_ssm_norm, w_branch_attn, w_branch_ssm, w_out, w_gate_up, w_down):
    depth = w_in.shape[0]
    bp, bs = x_prompt.shape[0], x_sample.shape[0]
    past = cache_k.shape[2]
    y_p, y_s = x_prompt, x_sample
    c_all = jnp.concatenate([c_prompt, c_sample], axis=0)
    new_p = [[] for _ in range(5)]
    new_s = [[] for _ in range(5)]
    for l in range(depth):
        p = _layer_params(l, w_in, g_q, g_k, g_norm_mix, g_norm_ffn, w_conv, b_conv, dt_bias, a_log, d_skip,
                          g_ssm_norm, w_branch_attn, w_branch_ssm, w_out, w_gate_up, w_down)
        mod3 = _ada_mod(c_all, w_ada[l], b_ada[l]).reshape(bp + bs, 6, D_MODEL)
        zero_conv = jnp.zeros((bp, CONV_W - 1, CONV_CH), F32)
        zero_ssm = jnp.zeros((bp, SSM_HEADS, SSM_HEAD_DIM, D_STATE), F32)
        y_p, *st_p = _group_step(y_p, mod3, 0, None, zero_conv, zero_ssm, p)
        time_minor = lambda a: jnp.moveaxis(a, 1, -1)
        cache = (time_minor(cache_k[l]), time_minor(cache_v[l]), time_minor(cache_ki[l]))
        y_s, *st_s = _group_step(y_s, mod3, bp, cache, state_conv[l], state_ssm[l], p)
        for acc, a in zip(new_p, st_p):
            acc.append(a)
        for acc, a in zip(new_s, st_s):
            acc.append(a)
    return (y_p, y_s, *[jnp.stack(a) for a in new_p], *[jnp.stack(a) for a in new_s])
```

```python
import functools
import math

import jax
import jax.numpy as jnp
from jax import lax
from jax.experimental import pallas as pl
from jax.experimental.pallas import tpu as pltpu

F32, BF16, I32 = jnp.float32, jnp.bfloat16, jnp.int32

D_MODEL = 1024
CHUNK = 64
N_HEADS = 16
HEAD_DIM = 64
N_KV_HEADS = 4
Q_PER_KV = N_HEADS // N_KV_HEADS
IDX_HEADS = 8
IDX_DIM = 64
TOPK_MAX = 256
ROPE_THETA = 10000.0
D_INNER = 2 * D_MODEL
SSM_HEAD_DIM = 64
SSM_HEADS = D_INNER // SSM_HEAD_DIM
SSM_GROUPS = 8
HEADS_PER_GROUP = SSM_HEADS // SSM_GROUPS
GROUP_W = HEADS_PER_GROUP * SSM_HEAD_DIM
D_STATE = 128
CONV_W = 4
CONV_CH = D_INNER + 2 * SSM_GROUPS * D_STATE
D_FF = -(-8 * D_MODEL // (3 * 256)) * 256
EPS = 1e-6
QK_W = (N_HEADS + N_KV_HEADS) * HEAD_DIM
KV_W = N_KV_HEADS * HEAD_DIM
QI_W = IDX_HEADS * IDX_DIM

LANES = 128
MISC_WI = IDX_DIM
MISC_DT = IDX_DIM + IDX_HEADS
WI_SCALE = (IDX_HEADS ** -0.5) * (IDX_DIM ** -0.5)
INT_MIN = -(2 ** 31)
NEG_BIG = -1e30
COUNT_ROWS = 64
SSD_CHUNKS_PER_STEP = 4
ATTN_TQ = 256
ATTN_LOGITS_BYTES = 16 * 1024 * 1024
VMEM_LIMIT = 56 * 1024 * 1024


def _cparams(*sem):
    return pltpu.CompilerParams(dimension_semantics=sem, vmem_limit_bytes=VMEM_LIMIT)


def _silu(x):
    h = 0.5 * x
    return h + h * jnp.tanh(h)


def _split_bf16(x, n):
    pieces = []
    r = x
    for _ in range(n):
        p = r.astype(BF16)
        pieces.append(p)
        r = r - p.astype(F32)
    return pieces


def _dot(a, b):
    return jnp.dot(a, b, preferred_element_type=F32)


def _dot_nt(a, b):
    return lax.dot_general(a, b, (((1,), (1,)), ((), ())), preferred_element_type=F32)


def _dot_tn(a, b):
    return lax.dot_general(a, b, (((0,), (0,)), ((), ())), preferred_element_type=F32)


def _exact_dot(x, m_stacked, n):
    return _dot(jnp.concatenate(_split_bf16(x, n), axis=1), m_stacked)


def _exact_dot_left(m_tiled, x, n):
    return _dot(m_tiled, jnp.concatenate(_split_bf16(x, n), axis=0))


def _rotate_half(x):
    w = x.shape[-1]
    lane = lax.broadcasted_iota(I32, x.shape, x.ndim - 1)
    first = (lane % HEAD_DIM) < (HEAD_DIM // 2)
    return jnp.where(first, pltpu.roll(x, w - HEAD_DIM // 2, x.ndim - 1), pltpu.roll(x, HEAD_DIM // 2, x.ndim - 1))


def _rope(x, cos, sin):
    reps = x.shape[-1] // LANES
    if reps > 1:
        cos = jnp.tile(cos, (1, reps))
        sin = jnp.tile(sin, (1, reps))
    return x * cos + _rotate_half(x) * sin


def _mod_kernel(c_ref, w_ref, b_ref, o_ref):
    s = _silu(c_ref[...])
    o_ref[...] = _dot(s.astype(BF16), w_ref[...].astype(BF16)) + b_ref[...]


def _ada_mod(c_all, w_ada, b_ada):
    bt = c_all.shape[0]
    n = w_ada.shape[1]
    tn = D_MODEL
    return pl.pallas_call(
        _mod_kernel,
        grid=(n // tn,),
        in_specs=[
            pl.BlockSpec((bt, D_MODEL), lambda j: (0, 0)),
            pl.BlockSpec((D_MODEL, tn), lambda j: (0, j)),
            pl.BlockSpec((1, tn), lambda j: (0, j)),
        ],
        out_specs=pl.BlockSpec((bt, tn), lambda j: (0, j)),
        out_shape=jax.ShapeDtypeStruct((bt, n), F32),
        compiler_params=_cparams("parallel"),
        name="ada_mod",
    )(c_all, w_ada, b_ada.reshape(1, n))


def _modulated_norm(x, mod, gain, shift_idx, scale_idx):
    ms = jnp.mean(x * x, axis=-1, keepdims=True)
    xn = x * lax.rsqrt(ms + EPS)
    sh = mod[:, shift_idx:shift_idx + 1, :]
    sc = mod[:, scale_idx:scale_idx + 1, :]
    return xn * gain * (1.0 + sc) + sh


def _row_blocking(b, t):
    tt = min(t, 512)
    bb = max(1, min(b, 512 // tt))
    assert t % tt == 0 and b % bb == 0 and tt % CHUNK == 0, (b, t)
    return bb, tt


def _proj_act_kernel(h_ref, w_ref, o_ref, *, act):
    acc = _dot(h_ref[...], w_ref[...])
    if act == "silu":
        acc = _silu(acc)
    elif act == "sigmoid":
        acc = jax.nn.sigmoid(acc)
    o_ref[...] = acc.astype(o_ref.dtype)


def _proj_act(h2d, w, act, out_dtype, tm, tn):
    m, k = h2d.shape
    n = w.shape[1]
    return pl.pallas_call(
        functools.partial(_proj_act_kernel, act=act),
        grid=(n // tn, m // tm),
        in_specs=[
            pl.BlockSpec((tm, k), lambda j, i: (i, 0)),
            pl.BlockSpec((k, tn), lambda j, i: (0, j)),
        ],
        out_specs=pl.BlockSpec((tm, tn), lambda j, i: (i, j)),
        out_shape=jax.ShapeDtypeStruct((m, n), out_dtype),
        compiler_params=_cparams("parallel", "parallel"),
        name="proj_" + act,
    )(h2d, w)


SUBLANES = 8
CONV_SLAB = 256


def _proj_conv_kernel(h_ref, w_ref, cst_ref, wconv_ref, bconv_ref, o_ref, tail_ref, buf, *, bb, tt, tiles_per_seq):
    i = pl.program_id(1)
    tail = CONV_W - 1
    tn = w_ref.shape[1]

    @pl.when(i % tiles_per_seq == 0)
    def _():
        for s in range(bb):
            buf[s, 0:SUBLANES - tail, :] = jnp.zeros((SUBLANES - tail, tn), F32)
            buf[s, SUBLANES - tail:SUBLANES, :] = cst_ref[s]

    h = h_ref[...]
    for c in range(tn // CONV_SLAB):
        cs = slice(c * CONV_SLAB, (c + 1) * CONV_SLAB)
        acc = _dot(h, w_ref[:, cs])
        for s in range(bb):
            buf[s, SUBLANES:SUBLANES + tt, cs] = acc[s * tt:(s + 1) * tt]
            xb = buf[s, :, cs]
            xc = bconv_ref[:, cs] + wconv_ref[tail:tail + 1, cs] * xb[SUBLANES:]
            for jj in range(tail):
                shifted = pltpu.roll(xb, tail - jj, 0)[SUBLANES:]
                xc = xc + wconv_ref[jj:jj + 1, cs] * shifted
            o_ref[s * tt:(s + 1) * tt, cs] = _silu(xc).astype(o_ref.dtype)
            last = buf[s, tt:tt + SUBLANES, cs]
            tail_ref[s, :, cs] = last
            buf[s, 0:SUBLANES, cs] = last


def _proj_conv(h2d, w, conv_state, w_conv, b_conv, b, t, out_dtype):
    m, k = h2d.shape
    n = w.shape[1]
    bb, tt = _row_blocking(b, t)
    tm = bb * tt
    tiles_per_seq = t // tt
    tn = min(n, 2048)
    kern = functools.partial(_proj_conv_kernel, bb=bb, tt=tt, tiles_per_seq=tiles_per_seq)
    seq_block = lambda rows: pl.BlockSpec((bb, rows, tn), lambda j, i: (i // tiles_per_seq, 0, j))
    return pl.pallas_call(
        kern,
        grid=(n // tn, m // tm),
        in_specs=[
            pl.BlockSpec((tm, k), lambda j, i: (i, 0)),
            pl.BlockSpec((k, tn), lambda j, i: (0, j)),
            seq_block(CONV_W - 1),
            pl.BlockSpec((CONV_W, tn), lambda j, i: (0, j)),
            pl.BlockSpec((1, tn), lambda j, i: (0, j)),
        ],
        out_specs=[pl.BlockSpec((tm, tn), lambda j, i: (i, j)), seq_block(SUBLANES)],
        out_shape=[jax.ShapeDtypeStruct((m, n), out_dtype), jax.ShapeDtypeStruct((b, SUBLANES, n), F32)],
        scratch_shapes=[pltpu.VMEM((bb, SUBLANES + tt, tn), F32)],
        compiler_params=_cparams("parallel", "arbitrary"),
        name="proj_conv",
    )(h2d, w, conv_state, w_conv, b_conv)


def _store_time_minor(ref, x):
    bb, tt = ref.shape[0], ref.shape[-1]
    for s in range(bb):
        ref[s] = x[s * tt:(s + 1) * tt, :].T.reshape(ref.shape[1:])


def _qk_kernel(x_ref, mod_ref, gmix_ref, w_ref, gsum_ref, gexp_ref, gain_ref, cos_ref, sin_ref, q_ref, k_ref, h_ref):
    bb, tt, d = x_ref.shape
    h = _modulated_norm(x_ref[...], mod_ref[...], gmix_ref[...], 0, 1).astype(h_ref.dtype).reshape(bb * tt, d)
    h_ref[...] = h
    acc = _dot(h, w_ref[...])
    ss = _exact_dot(acc * acc, gsum_ref[...], 2)
    rs = lax.rsqrt(ss * (1.0 / HEAD_DIM) + EPS)
    rs_full = _exact_dot(rs, gexp_ref[...], 2)
    xn = acc * rs_full * gain_ref[...]
    out = _rope(xn, cos_ref[...], sin_ref[...])
    nq = N_HEADS * HEAD_DIM
    q_ref[...] = (out[:, :nq] * (HEAD_DIM ** -0.5)).astype(q_ref.dtype)
    _store_time_minor(k_ref, out[:, nq:])


def _vm_kernel(h_ref, w_ref, cos_ref, sin_ref, dtb_ref, v_ref, qi_ref, misc_ref, kit_ref):
    acc = _dot(h_ref[...], w_ref[...])
    cos, sin = cos_ref[...], sin_ref[...]
    _store_time_minor(v_ref, acc[:, :KV_W])
    qi_ref[...] = _rope(acc[:, KV_W:KV_W + QI_W], cos, sin).astype(qi_ref.dtype)
    m = acc[:, KV_W + QI_W:]
    lane = lax.broadcasted_iota(I32, m.shape, 1)
    roped = _rope(m, cos, sin)
    dt = jax.nn.softplus(m + dtb_ref[...])
    misc_ref[...] = jnp.where(lane < MISC_WI, roped,
                              jnp.where(lane < MISC_DT, m * WI_SCALE,
                                        jnp.where(lane < MISC_DT + SSM_HEADS, dt, 0.0)))
    _store_time_minor(kit_ref, roped[:, :IDX_DIM])


def _table_spec(tab_rows, tm):
    nblk = tab_rows // tm
    return pl.BlockSpec((tm, LANES), lambda i: (i % nblk, 0))


def _time_minor_spec(b, t, *mid):
    bb, tt = _row_blocking(b, t)
    per_seq = t // tt
    zeros = (0,) * len(mid)
    return pl.BlockSpec((bb, *mid, tt), lambda i: (i // per_seq, *zeros, i % per_seq))


def _proj_qk(x, mod3, mod_off, g_mix, w_qk, gsum, gexp, gain, cos_tab, sin_tab):
    b, t, k = x.shape
    m = b * t
    bb, tt = _row_blocking(b, t)
    tm = bb * tt
    per_seq = t // tt
    assert mod_off % bb == 0, (mod_off, bb)
    off = mod_off // bb
    nq = N_HEADS * HEAD_DIM
    const = lambda i: (0, 0)
    return pl.pallas_call(
        _qk_kernel,
        grid=(m // tm,),
        in_specs=[
            pl.BlockSpec((bb, tt, k), lambda i: (i // per_seq, i % per_seq, 0)),
            pl.BlockSpec((bb, 6, k), lambda i: (i // per_seq + off, 0, 0)),
            pl.BlockSpec((1, 1, k), lambda i: (0, 0, 0)),
            pl.BlockSpec((k, QK_W), const),
            pl.BlockSpec((2 * QK_W, LANES), const),
            pl.BlockSpec((2 * LANES, QK_W), const),
            pl.BlockSpec((1, QK_W), const),
            _table_spec(cos_tab.shape[0], tm),
            _table_spec(sin_tab.shape[0], tm),
        ],
        out_specs=[pl.BlockSpec((tm, nq), lambda i: (i, 0)), _time_minor_spec(b, t, N_KV_HEADS, HEAD_DIM),
                   pl.BlockSpec((tm, k), lambda i: (i, 0))],
        out_shape=[jax.ShapeDtypeStruct((m, nq), BF16), jax.ShapeDtypeStruct((b, N_KV_HEADS, HEAD_DIM, t), F32),
                   jax.ShapeDtypeStruct((m, k), BF16)],
        compiler_params=_cparams("parallel"),
        name="proj_qk",
    )(x, mod3, g_mix.reshape(1, 1, k), w_qk, gsum, gexp, gain, cos_tab, sin_tab)


def _proj_vm(h2d, w_vm, cos_tab, sin_tab, dtb, b, t):
    m, k = h2d.shape
    tm = math.prod(_row_blocking(b, t))
    wn = w_vm.shape[1]
    const = lambda i: (0, 0)
    return pl.pallas_call(
        _vm_kernel,
        grid=(m // tm,),
        in_specs=[
            pl.BlockSpec((tm, k), lambda i: (i, 0)),
            pl.BlockSpec((k, wn), const),
            _table_spec(cos_tab.shape[0], tm),
            _table_spec(sin_tab.shape[0], tm),
            pl.BlockSpec((1, LANES), const),
        ],
        out_specs=[_time_minor_spec(b, t, N_KV_HEADS, HEAD_DIM), pl.BlockSpec((tm, QI_W), lambda i: (i, 0)),
                   pl.BlockSpec((tm, LANES), lambda i: (i, 0)), _time_minor_spec(b, t, IDX_DIM)],
        out_shape=[jax.ShapeDtypeStruct((b, N_KV_HEADS, HEAD_DIM, t), F32), jax.ShapeDtypeStruct((m, QI_W), BF16),
                   jax.ShapeDtypeStruct((m, LANES), F32), jax.ShapeDtypeStruct((b, IDX_DIM, t), F32)],
        compiler_params=_cparams("parallel"),
        name="proj_vm",
    )(h2d, w_vm, cos_tab, sin_tab, dtb)


def _attn_kernel(*refs, tq, kb, gp, topk, past, t_new, has_cache):
    n_in = 9 if has_cache else 6
    q_ref, qi_ref, wi_ref, k_ref, v_ref, kit_ref = refs[:6]
    o_ref = refs[n_in]
    kbf, vbf, kibf, sc_scr, keep_scr, qis, qs, mb_scr, acc_scr, s_scr = refs[n_in + 1:n_in + 11]
    if has_cache:
        ck_ref, cv_ref, cki_ref = refs[6:9]
        ckb, cvb, ckib = refs[n_in + 11:]
    j = pl.program_id(1)
    kbn = min(kb, t_new)
    n_cache_blocks = past // kb

    def ones_row(n):
        return jnp.where(lax.broadcasted_iota(I32, (HEAD_DIM, n), 0) == 0, 1.0, 0.0).astype(BF16)

    def stage(k_src, v_src, ki_src, k_dst, v_dst, ki_dst, n):
        for g in range(N_KV_HEADS):
            k_dst[g] = k_src[0, g].astype(BF16)
            v_dst[g, 0:HEAD_DIM, :] = v_src[0, g].astype(BF16)
            v_dst[g, HEAD_DIM:2 * HEAD_DIM, :] = ones_row(n)
        ki_dst[...] = ki_src[0].T.astype(BF16)

    @pl.when(j == 0)
    def _():
        stage(k_ref, v_ref, kit_ref, kbf, vbf, kibf, t_new)
        if has_cache:
            stage(ck_ref, cv_ref, cki_ref, ckb, cvb, ckib, past)

    for h in range(IDX_HEADS):
        qis[h * tq:(h + 1) * tq, :] = qi_ref[0, :, h * IDX_DIM:(h + 1) * IDX_DIM]
    for h in range(N_HEADS):
        qs[h * tq:(h + 1) * tq, :] = q_ref[0, :, h * HEAD_DIM:(h + 1) * HEAD_DIM]
    wi_t = wi_ref[0, :, MISC_WI:MISC_WI + IDX_HEADS].T
    hpv = max(1, LANES // tq)
    wi_wide = [jnp.concatenate([wi_t[h + p:h + p + 1, :] for p in range(hpv)], axis=1) if hpv > 1 else wi_t[h:h + 1, :]
               for h in range(0, IDX_HEADS, hpv)]

    qpos = past + j * tq + lax.broadcasted_iota(I32, (1, tq), 1)
    limit = (qpos // CHUNK + 1) * CHUNK
    n_new_blocks = (j * tq + tq + kbn - 1) // kbn

    def by_pairs(fn, n_blocks, width, init):
        def pair(i, c):
            off = pl.multiple_of(i * (2 * width), 2 * width)
            return fn(pl.multiple_of(off + width, width), fn(off, c))
        c = lax.fori_loop(0, n_blocks // 2, pair, init)
        if isinstance(n_blocks, int):
            return fn((n_blocks - 1) * width, c) if n_blocks % 2 else c
        last = pl.multiple_of((n_blocks - 1) * width, width)
        return lax.cond(n_blocks % 2 == 1, lambda c: fn(last, c), lambda c: c, c)

    def over_cache(fn, init):
        if not has_cache:
            return init
        return by_pairs(fn, n_cache_blocks, kb, init)

    def over_new(fn, init):
        if t_new <= kb:
            return fn(0, init)
        return by_pairs(fn, n_new_blocks, kbn, init)

    def lanes_at(off):
        return past + off if isinstance(off, int) else pl.multiple_of(past + off, LANES)

    def score_block(ki_blk, kpos0, width):
        lg = _dot_nt(ki_blk, qis[...])
        wide = jnp.zeros((width, hpv * tq), F32)
        for i, w_row in enumerate(wi_wide):
            wide = wide + w_row * jnp.maximum(lg[:, i * hpv * tq:(i + 1) * hpv * tq], 0.0)
        sc = wide[:, :tq]
        for p in range(1, hpv):
            sc = sc + wide[:, p * tq:(p + 1) * tq]
        kpos = kpos0 + lax.broadcasted_iota(I32, (width, 1), 0)
        sc_scr[pl.ds(kpos0, width), :] = jnp.where(kpos < limit, sc, -jnp.inf)

    def p1c(off, c):
        score_block(ckib[pl.ds(off, kb), :], off, kb)
        return c

    def p1n(off, c):
        score_block(kibf[pl.ds(off, kbn), :], lanes_at(off), kbn)
        return c

    over_cache(p1c, 0)
    over_new(p1n, 0)

    def key_to_float(c):
        return pltpu.bitcast(jnp.where(c >= 0, c, c ^ 0x7FFFFFFF), F32)

    def count(cmp, cand):
        def cnt(row_off, width, acc):
            hit = jnp.where(cmp(sc_scr[pl.ds(row_off, width), :], cand), 1.0, 0.0)
            for r in range(width // COUNT_ROWS):
                acc = acc + hit[r * COUNT_ROWS:(r + 1) * COUNT_ROWS]
            return acc
        acc = jnp.zeros((COUNT_ROWS, tq), F32)
        acc = over_cache(lambda off, a: cnt(off, kb, a), acc)
        acc = over_new(lambda off, a: cnt(lanes_at(off), kbn, a), acc)
        return jnp.sum(acc, axis=0, keepdims=True)

    def bit_step(it, prefix):
        bit = jnp.left_shift(jnp.int32(1), 31 - it)
        cand = key_to_float((prefix | bit) ^ INT_MIN)
        return jnp.where(count(jnp.greater_equal, cand) >= float(topk), prefix | bit, prefix)

    prefix = lax.fori_loop(0, 32, bit_step, jnp.zeros((1, tq), I32))
    thr = key_to_float(prefix ^ INT_MIN)
    flt_max = float(jnp.finfo(F32).max)
    thr = jnp.where(thr >= -flt_max, thr, -flt_max)

    n_ge = count(jnp.greater_equal, thr)
    has_ties = jnp.max(n_ge) > float(topk)

    def keep_ranked(row_off, width, need, seen):
        sc = sc_scr[pl.ds(row_off, width), :]
        tie = sc == thr
        tie_f = jnp.where(tie, 1.0, 0.0)
        below = lax.broadcasted_iota(I32, (width, width), 1) <= lax.broadcasted_iota(I32, (width, width), 0)
        rank = seen + _dot(jnp.where(below, 1.0, 0.0).astype(BF16), tie_f.astype(BF16))
        keep = (sc > thr) | (tie & (rank <= need))
        keep_scr[pl.ds(row_off, width), :] = jnp.where(keep, 1.0, 0.0).astype(BF16)
        return seen + jnp.sum(tie_f, axis=0, keepdims=True)

    def keep_all_ties(row_off, width, c):
        sc = sc_scr[pl.ds(row_off, width), :]
        keep_scr[pl.ds(row_off, width), :] = jnp.where(sc >= thr, 1.0, 0.0).astype(BF16)
        return c

    def with_ties():
        need = float(topk) - count(jnp.greater, thr)
        seen = over_cache(lambda off, s: keep_ranked(off, kb, need, s), jnp.zeros((1, tq), F32))
        over_new(lambda off, s: keep_ranked(lanes_at(off), kbn, need, s), seen)

    def without_ties():
        over_cache(lambda off, c: keep_all_ties(off, kb, c), 0)
        over_new(lambda off, c: keep_all_ties(lanes_at(off), kbn, c), 0)

    lax.cond(has_ties, with_ties, without_ties)

    rows = Q_PER_KV * tq

    eye = jnp.where(lax.broadcasted_iota(I32, (tq, tq), 0) == lax.broadcasted_iota(I32, (tq, tq), 1), 1.0, 0.0).astype(BF16)

    def pass_a(g0, k_of, lane_off, width):
        keep_q = _dot_nt(eye, keep_scr[pl.ds(lane_off, width), :])
        b = jnp.tile(jnp.where(keep_q > 0.5, 0.0, NEG_BIG), (Q_PER_KV, 1))
        for gi in range(gp):
            g = g0 + gi
            s = _dot(qs[g * rows:(g + 1) * rows, :], k_of(g)) + b
            s_scr[gi, :, pl.ds(lane_off, width)] = s
            m = mb_scr[gi]
            if width % LANES:
                m = jnp.maximum(m, jnp.max(s, axis=1, keepdims=True))
            else:
                for c in range(width // LANES):
                    m = jnp.maximum(m, s[:, c * LANES:(c + 1) * LANES])
            mb_scr[gi] = m

    def pass_b(g0, v_of, lane_off, width):
        for gi in range(gp):
            m = mb_scr[gi]
            m = m[:, :width] if width < LANES else jnp.tile(m, (1, width // LANES))
            p = jnp.exp(s_scr[gi, :, pl.ds(lane_off, width)] - m)
            acc_scr[gi] += _dot_nt(p.astype(BF16), v_of(g0 + gi))

    def cache_k(off):
        return lambda g: ckb[g, :, pl.ds(off, kb)]

    def cache_v(off):
        return lambda g: cvb[g, :, pl.ds(off, kb)]

    def new_k(off):
        return lambda g: kbf[g, :, pl.ds(off, kbn)]

    def new_v(off):
        return lambda g: vbf[g, :, pl.ds(off, kbn)]

    def run(fn, cache_args, new_args):
        def on_cache(off, c):
            fn(*[a(off) for a in cache_args], off, kb)
            return c

        def on_new(off, c):
            fn(*[a(off) for a in new_args], lanes_at(off), kbn)
            return c

        over_cache(on_cache, 0)
        over_new(on_new, 0)

    for g0 in range(0, N_KV_HEADS, gp):
        mb_scr[...] = jnp.full(mb_scr.shape, NEG_BIG, F32)
        run(functools.partial(pass_a, g0), [cache_k], [new_k])
        for gi in range(gp):
            mb_scr[gi] = jnp.broadcast_to(jnp.max(mb_scr[gi], axis=1, keepdims=True), (rows, LANES))
        acc_scr[...] = jnp.zeros(acc_scr.shape, F32)
        run(functools.partial(pass_b, g0), [cache_v], [new_v])

        for gi in range(gp):
            a = acc_scr[gi]
            out = a[:, :HEAD_DIM] / a[:, HEAD_DIM:HEAD_DIM + 1]
            for r in range(Q_PER_KV):
                hh = (g0 + gi) * Q_PER_KV + r
                o_ref[0, :, hh * HEAD_DIM:(hh + 1) * HEAD_DIM] = out[r * tq:(r + 1) * tq].astype(o_ref.dtype)


def _attention(q, qi, misc, kt, vt, kit, cache=None):
    b, t, _ = q.shape
    has_cache = cache is not None
    past = cache[2].shape[2] if has_cache else 0
    n_keys = past + t
    topk = min(TOPK_MAX, n_keys // 4)
    tq = min(t, ATTN_TQ)
    kb = 256
    rows = Q_PER_KV * tq
    key_w = past + -(-t // LANES) * LANES
    gp = N_KV_HEADS
    while gp > 1 and gp * rows * key_w * 4 > ATTN_LOGITS_BYTES:
        gp //= 2
    qtile = lambda w: pl.BlockSpec((1, tq, w), lambda i, j: (i, j, 0))
    heads = lambda n: pl.BlockSpec((1, N_KV_HEADS, HEAD_DIM, n), lambda i, j: (i, 0, 0, 0))
    idx = lambda n: pl.BlockSpec((1, IDX_DIM, n), lambda i, j: (i, 0, 0))
    in_specs = [qtile(N_HEADS * HEAD_DIM), qtile(QI_W), qtile(LANES), heads(t), heads(t), idx(t)]
    args = [q, qi, misc, kt, vt, kit]
    if has_cache:
        in_specs += [heads(past), heads(past), idx(past)]
        args += list(cache)
    kern = functools.partial(_attn_kernel, tq=tq, kb=kb, gp=gp, topk=topk, past=past, t_new=t, has_cache=has_cache)
    return pl.pallas_call(
        kern,
        grid=(b, t // tq),
        in_specs=in_specs,
        out_specs=qtile(N_HEADS * HEAD_DIM),
        out_shape=jax.ShapeDtypeStruct((b, t, N_HEADS * HEAD_DIM), BF16),
        scratch_shapes=[
            pltpu.VMEM((N_KV_HEADS, HEAD_DIM, t), BF16),
            pltpu.VMEM((N_KV_HEADS, 2 * HEAD_DIM, t), BF16),
            pltpu.VMEM((t, IDX_DIM), BF16),
            pltpu.VMEM((n_keys, tq), F32),
            pltpu.VMEM((n_keys, tq), BF16),
            pltpu.VMEM((IDX_HEADS * tq, IDX_DIM), BF16),
            pltpu.VMEM((N_HEADS * tq, HEAD_DIM), BF16),
            pltpu.VMEM((gp, rows, LANES), F32),
            pltpu.VMEM((gp, rows, 2 * HEAD_DIM), F32),
            pltpu.VMEM((gp, rows, key_w), F32),
        ] + ([
            pltpu.VMEM((N_KV_HEADS, HEAD_DIM, past), BF16),
            pltpu.VMEM((N_KV_HEADS, 2 * HEAD_DIM, past), BF16),
            pltpu.VMEM((past, IDX_DIM), BF16),
        ] if has_cache else []),
        compiler_params=_cparams("parallel", "arbitrary"),
        name="attn_cache" if has_cache else "attn_prompt",
    )(*args)


def _ssd_kernel(xs_ref, bc_ref, misc_ref, zs_ref, h0_ref, alog_ref, dskip_ref, gnorm_ref, expand_ref, y_ref, hout_ref):
    c = pl.program_id(1)
    L = CHUNK

    @pl.when(c == 0)
    def _():
        hout_ref[...] = h0_ref[...]

    a = -jnp.exp(alog_ref[...])
    ri = lax.broadcasted_iota(I32, (L, 3 * L), 0)
    ci = lax.broadcasted_iota(I32, (L, 3 * L), 1) % L
    tri3 = jnp.where(ri >= ci, 1.0, 0.0).astype(BF16)
    expand3 = expand_ref[...]
    row_i = lax.broadcasted_iota(I32, (L, D_INNER), 0)
    lane_j = lax.broadcasted_iota(I32, (L, D_INNER), 1) % SSM_HEAD_DIM
    causal = row_i >= lane_j
    blk_r = lax.broadcasted_iota(I32, (GROUP_W, GROUP_W), 0) // SSM_HEAD_DIM
    blk_c = lax.broadcasted_iota(I32, (GROUP_W, GROUP_W), 1) // SSM_HEAD_DIM
    same_head = blk_r == blk_c

    for sub in range(xs_ref.shape[1] // L):
        _ssd_chunk(slice(sub * L, (sub + 1) * L), xs_ref, bc_ref, misc_ref, zs_ref, dskip_ref, gnorm_ref, y_ref, hout_ref,
                   a, tri3, expand3, causal, same_head)


def _heads_over_time(x):
    L = x.shape[0]
    xt = x.T
    xtt = jnp.concatenate([xt, xt], axis=1)
    first = lax.broadcasted_iota(I32, (L, 2 * L), 1) < L
    cols = []
    for h in range(0, SSM_HEADS, 2):
        r = MISC_DT + h
        cols.append(jnp.where(first, jnp.broadcast_to(xtt[r:r + 1, :], (L, 2 * L)),
                              jnp.broadcast_to(xtt[r + 1:r + 2, :], (L, 2 * L))))
    return jnp.concatenate(cols, axis=1)


def _ssd_chunk(rows, xs_ref, bc_ref, misc_ref, zs_ref, dskip_ref, gnorm_ref, y_ref, hout_ref,
               a, tri3, expand3, causal, same_head):
    L = CHUNK
    dt = misc_ref[0, rows, :]
    acs = _exact_dot_left(tri3, dt * a, 3)
    col_acs = _exact_dot(acs, expand3, 3)
    col_dt = _exact_dot(dt, expand3[:2 * LANES], 2)
    row_acs = _heads_over_time(acs)
    row_dt = _heads_over_time(dt)
    a_last = acs[L - 1:L, MISC_DT:MISC_DT + SSM_HEADS]

    groups = range(SSM_GROUPS)
    gsl = [slice(g * GROUP_W, (g + 1) * GROUP_W) for g in groups]
    bgs = [bc_ref[0, rows, g * D_STATE:(g + 1) * D_STATE] for g in groups]
    cgs = [bc_ref[0, rows, (SSM_GROUPS + g) * D_STATE:(SSM_GROUPS + g + 1) * D_STATE] for g in groups]
    xgs = [xs_ref[0, rows, gs] for gs in gsl]
    hprevs = [hout_ref[0, g * HEADS_PER_GROUP:(g + 1) * HEADS_PER_GROUP].reshape(GROUP_W, D_STATE) for g in groups]
    cbs = [_dot_nt(cgs[g], jnp.tile(bgs[g], (HEADS_PER_GROUP, 1))) for g in groups]
    offs = [_dot_nt(cgs[g], hprevs[g].astype(BF16)) for g in groups]
    sts = []
    for g in groups:
        ce = col_acs[:, gsl[g]]
        w_state = jnp.exp(ce[L - 1:L, :] - ce) * col_dt[:, gsl[g]]
        sts.append(_dot_tn((xgs[g] * w_state).astype(BF16), bgs[g]))
    y_diags = []
    for g in groups:
        gs = gsl[g]
        mm = cbs[g] * jnp.exp(jnp.where(causal[:, gs], col_acs[:, gs] - row_acs[:, gs], -jnp.inf)) * row_dt[:, gs]
        xbd = jnp.where(same_head, jnp.tile(xgs[g].astype(BF16), (HEADS_PER_GROUP, 1)), 0.0).astype(BF16)
        y_diags.append(_dot(mm.astype(BF16), xbd))
    for g in groups:
        for r in range(HEADS_PER_GROUP):
            hh = g * HEADS_PER_GROUP + r
            decay = jnp.exp(a_last[:, hh:hh + 1])
            rs = slice(r * SSM_HEAD_DIM, (r + 1) * SSM_HEAD_DIM)
            hout_ref[0, hh] = decay * hprevs[g][rs] + sts[g][rs]
    for g in groups:
        gs = gsl[g]
        y_off = jnp.exp(col_acs[:, gs]) * offs[g]
        yt = (y_diags[g] + y_off + dskip_ref[:, gs] * xgs[g]) * zs_ref[0, rows, gs].astype(F32)
        ms = jnp.mean(yt * yt, axis=-1, keepdims=True)
        y_ref[0, rows, gs] = (yt * lax.rsqrt(ms + EPS) * gnorm_ref[:, gs]).astype(y_ref.dtype)


def _ssd(xs, bc, misc, zs, h0, a_log, dskip_full, g_norm, expand):
    b, t, _ = xs.shape
    rows = min(t, SSD_CHUNKS_PER_STEP * CHUNK)
    nc = t // rows
    chunk = lambda w: pl.BlockSpec((1, rows, w), lambda i, c: (i, c, 0))
    const2 = lambda r, w: pl.BlockSpec((r, w), lambda i, c: (0, 0))
    state = pl.BlockSpec((1, SSM_HEADS, SSM_HEAD_DIM, D_STATE), lambda i, c: (i, 0, 0, 0))
    return pl.pallas_call(
        _ssd_kernel,
        grid=(b, nc),
        in_specs=[
            chunk(D_INNER), chunk(2 * SSM_GROUPS * D_STATE), chunk(LANES), chunk(D_INNER), state,
            const2(1, LANES), const2(1, D_INNER), const2(1, D_INNER), const2(3 * LANES, D_INNER),
        ],
        out_specs=[chunk(D_INNER), state],
        out_shape=[jax.ShapeDtypeStruct((b, t, D_INNER), BF16),
                   jax.ShapeDtypeStruct((b, SSM_HEADS, SSM_HEAD_DIM, D_STATE), F32)],
        compiler_params=_cparams("parallel", "arbitrary"),
        name="ssd",
    )(xs, bc, misc, zs, h0, a_log, dskip_full, g_norm, expand)


def _merge_kernel(o_ref, y_ref, g_ref, x_ref, mod_ref, gain_ref, wa_ref, ws_ref, wo_ref, x1_ref, h2_ref):
    bb, tt, d = x_ref.shape
    rows = bb * tt
    o = o_ref[...].reshape(rows, -1)
    y = y_ref[...].reshape(rows, -1)
    gates = g_ref[...].reshape(rows, -1).astype(F32)
    mixed = gates[:, :d] * _dot(o, wa_ref[...]) + gates[:, d:] * _dot(y, ws_ref[...])
    out = _dot(mixed.astype(BF16), wo_ref[...]).reshape(bb, tt, d)
    mod = mod_ref[...]
    x1 = x_ref[...] + mod[:, 2:3, :] * out
    x1_ref[...] = x1
    h2_ref[...] = _modulated_norm(x1, mod, gain_ref[...], 3, 4).astype(h2_ref.dtype)


def _merge(o_attn, y, gates, x, mod3, mod_off, gain, wa, ws, wo):
    b, t, d = x.shape
    bb, tt = _row_blocking(b, t)
    assert mod_off % bb == 0, (mod_off, bb)
    off = mod_off // bb
    tile = lambda w: pl.BlockSpec((bb, tt, w), lambda i, j: (i, j, 0))
    const = lambda r, w: pl.BlockSpec((r, w), lambda i, j: (0, 0))
    return pl.pallas_call(
        _merge_kernel,
        grid=(b // bb, t // tt),
        in_specs=[
            tile(N_HEADS * HEAD_DIM), tile(D_INNER), tile(2 * D_MODEL), tile(d),
            pl.BlockSpec((bb, 6, d), lambda i, j: (i + off, 0, 0)),
            pl.BlockSpec((1, 1, d), lambda i, j: (0, 0, 0)),
            const(N_HEADS * HEAD_DIM, d), const(D_INNER, d), const(d, d),
        ],
        out_specs=[tile(d), tile(d)],
        out_shape=[jax.ShapeDtypeStruct((b, t, d), F32), jax.ShapeDtypeStruct((b, t, d), BF16)],
        compiler_params=_cparams("parallel", "parallel"),
        name="merge",
    )(o_attn, y, gates, x, mod3, gain.reshape(1, 1, d), wa, ws, wo)


def _ffn_kernel(h2_ref, x1_ref, mod_ref, wg_ref, wu_ref, wd_ref, o_ref, acc_ref):
    f = pl.program_id(2)
    bb, tt, d = x1_ref.shape
    h2 = h2_ref[...].reshape(bb * tt, d)
    act = _silu(_dot(h2, wg_ref[...])) * _dot(h2, wu_ref[...])
    part = _dot(act.astype(BF16), wd_ref[...])

    @pl.when(f == 0)
    def _():
        acc_ref[...] = part

    @pl.when(f > 0)
    def _():
        acc_ref[...] += part

    @pl.when(f == pl.num_programs(2) - 1)
    def _():
        o_ref[...] = x1_ref[...] + mod_ref[...][:, 5:6, :] * acc_ref[...].reshape(bb, tt, d)


def _ffn(h2, x1, mod3, mod_off, w_gu, w_down):
    b, t, d = x1.shape
    bb, tt = _row_blocking(b, t)
    assert mod_off % bb == 0, (mod_off, bb)
    off = mod_off // bb
    nf = 2
    tf = D_FF // nf
    tile = pl.BlockSpec((bb, tt, d), lambda i, j, f: (i, j, 0))
    return pl.pallas_call(
        _ffn_kernel,
        grid=(b // bb, t // tt, nf),
        in_specs=[
            tile, tile,
            pl.BlockSpec((bb, 6, d), lambda i, j, f: (i + off, 0, 0)),
            pl.BlockSpec((d, tf), lambda i, j, f: (0, f)),
            pl.BlockSpec((d, tf), lambda i, j, f: (0, nf + f)),
            pl.BlockSpec((tf, d), lambda i, j, f: (f, 0)),
        ],
        out_specs=tile,
        out_shape=jax.ShapeDtypeStruct((b, t, d), F32),
        scratch_shapes=[pltpu.VMEM((bb * tt, d), F32)],
        compiler_params=_cparams("parallel", "parallel", "arbitrary"),
        name="ffn",
    )(h2, x1, mod3, w_gu, w_gu, w_down)


def _rope_tables(t, past, tm):
    half = HEAD_DIM // 2
    inv = ROPE_THETA ** (-jnp.arange(half, dtype=F32) / half)
    ang = (past + jnp.arange(t)).astype(F32)[:, None] * inv[None, :]
    cos, sin = jnp.cos(ang), jnp.sin(ang)
    cos_t = jnp.concatenate([cos, cos, cos, cos], axis=1)
    sin_t = jnp.concatenate([-sin, sin, -sin, sin], axis=1)
    if t < tm:
        cos_t, sin_t = jnp.tile(cos_t, (tm // t, 1)), jnp.tile(sin_t, (tm // t, 1))
    return cos_t, sin_t


def _group_step(x, mod3, mod_off, cache, conv_state, ssm_state, p):
    b, t, d = x.shape
    m = b * t
    tm = min(512, m)
    past = cache[2].shape[2] if cache is not None else 0
    cos_t, sin_t = _rope_tables(t, past, tm)
    q, kt, h = _proj_qk(x, mod3, mod_off, p["g_norm_mix"], p["w_qk"], p["gsum"], p["gexp"], p["qk_gain"], cos_t, sin_t)
    vt, qi, misc, kit = _proj_vm(h, p["w_vm"], cos_t, sin_t, p["dtb"], b, t)
    zs = _proj_act(h, p["w_z"], "silu", BF16, tm, D_INNER)
    gates = _proj_act(h, p["w_g"], "sigmoid", BF16, tm, 2 * D_MODEL)
    half = CONV_CH // 2
    xs, tail_x = _proj_conv(h, p["w_xs"], conv_state[:, :, :half], p["w_conv"][:, :half],
                            p["b_conv"][:, :half], b, t, F32)
    bc, tail_bc = _proj_conv(h, p["w_bc"], conv_state[:, :, half:], p["w_conv"][:, half:],
                             p["b_conv"][:, half:], b, t, BF16)

    r3 = lambda a: a.reshape(b, t, a.shape[-1])
    o_attn = _attention(r3(q), r3(qi), r3(misc), kt, vt, kit, cache)
    y, h_last = _ssd(r3(xs), r3(bc), r3(misc), r3(zs), ssm_state, p["a_log"], p["dskip_full"], p["g_ssm_norm"],
                     p["expand"])
    x1, h2 = _merge(o_attn, y, r3(gates), x, mod3, mod_off, p["g_norm_ffn"], p["w_ba"], p["w_bs"], p["w_out"])
    out = _ffn(h2, x1, mod3, mod_off, p["w_gu"], p["w_down"])
    conv_new = jnp.concatenate([tail_x, tail_bc], axis=2)[:, -(CONV_W - 1):]
    token_major = lambda a: jnp.moveaxis(a, -1, 1)
    return out, token_major(kt), token_major(vt), token_major(kit), conv_new, h_last


IN_SIZES = (N_HEADS * HEAD_DIM, KV_W, KV_W, QI_W, IDX_DIM, IDX_HEADS, D_INNER, CONV_CH, SSM_HEADS, 2 * D_MODEL)
IN_OFFS = tuple(sum(IN_SIZES[:i]) for i in range(len(IN_SIZES) + 1))
VM_W = KV_W + QI_W + LANES


def _repack_kernel(wt_ref, qk_ref, vm_ref, z_ref, xs_ref, bc_ref, g_ref):
    o = IN_OFFS
    col = lambda a, b: wt_ref[a:b, :].T.astype(BF16)
    qk_ref[...] = col(o[0], o[2])
    whole = (o[4] - o[2]) // LANES * LANES
    vm_ref[:, 0:whole] = col(o[2], o[2] + whole)
    a_ki, a_dt = o[2] + whole, (o[8] - MISC_DT) // LANES * LANES
    assert a_ki == o[4] and o[8] - a_dt == MISC_DT, (a_ki, a_dt)
    lane = lax.broadcasted_iota(I32, (LANES, LANES), 1)
    ki_wi, dt = wt_ref[a_ki:a_ki + LANES, :].T, wt_ref[a_dt:a_dt + LANES, :].T
    vm_ref[:, whole:] = jnp.where(lane < MISC_DT, ki_wi, jnp.where(lane < MISC_DT + SSM_HEADS, dt, 0.0)).astype(BF16)
    z_ref[...] = col(o[6], o[7])
    xs_ref[...] = col(o[7], o[7] + CONV_CH // 2)
    bc_ref[...] = col(o[7] + CONV_CH // 2, o[8])
    g_ref[...] = col(o[9], o[10])


def _repack_w_in(wt):
    n, d = wt.shape
    widths = (QK_W, VM_W, D_INNER, CONV_CH // 2, CONV_CH // 2, 2 * D_MODEL)
    return pl.pallas_call(
        _repack_kernel,
        grid=(d // LANES,),
        in_specs=[pl.BlockSpec((n, LANES), lambda i: (0, i))],
        out_specs=[pl.BlockSpec((LANES, wd), lambda i: (i, 0)) for wd in widths],
        out_shape=[jax.ShapeDtypeStruct((d, wd), BF16) for wd in widths],
        compiler_params=_cparams("parallel"),
        name="repack_w_in",
    )(wt)


def _layer_params(l, w_in, g_q, g_k, g_norm_mix, g_norm_ffn, w_conv, b_conv, dt_bias, a_log, d_skip, g_ssm_norm,
                  w_branch_attn, w_branch_ssm, w_out, w_gate_up, w_down):
    w_qk, w_vm, w_z, w_xs, w_bc, w_g = _repack_w_in(w_in[l].T)
    head_of = jnp.arange(QK_W) // HEAD_DIM
    gsum = (head_of[:, None] == jnp.arange(LANES)[None, :]).astype(BF16)
    dtb = jnp.zeros((1, LANES), F32).at[0, MISC_DT:MISC_DT + SSM_HEADS].set(dt_bias[l])
    expand = ((jnp.arange(LANES) - MISC_DT)[:, None] == (jnp.arange(D_INNER) // SSM_HEAD_DIM)[None, :]).astype(BF16)
    a_log_lanes = jnp.zeros((1, LANES), F32).at[0, MISC_DT:MISC_DT + SSM_HEADS].set(a_log[l])
    return dict(
        w_qk=w_qk, w_vm=w_vm, w_z=w_z, w_xs=w_xs, w_bc=w_bc, w_g=w_g,
        gsum=jnp.tile(gsum, (2, 1)), gexp=jnp.tile(gsum.T, (2, 1)),
        qk_gain=jnp.concatenate([jnp.tile(g_q[l], N_HEADS), jnp.tile(g_k[l], N_KV_HEADS)]).reshape(1, QK_W),
        dtb=dtb, expand=jnp.tile(expand, (3, 1)),
        g_norm_mix=g_norm_mix[l], g_norm_ffn=g_norm_ffn[l],
        w_conv=w_conv[l], b_conv=b_conv[l].reshape(1, CONV_CH), a_log=a_log_lanes,
        dskip_full=jnp.repeat(d_skip[l], SSM_HEAD_DIM).reshape(1, D_INNER), g_ssm_norm=g_ssm_norm[l].reshape(1, D_INNER),
        w_ba=w_branch_attn[l].astype(BF16), w_bs=w_branch_ssm[l].astype(BF16), w_out=w_out[l].astype(BF16),
        w_gu=w_gate_up[l].astype(BF16), w_down=w_down[l].astype(BF16),
    )


def kernel(x_prompt, x_sample, cache_k, cache_v, cache_ki, state_conv, state_ssm, c_prompt, c_sample, w_ada, b_ada, g_norm_mix, g_norm_ffn, w_in, g_q, g_k, w_conv, b_conv, dt_bias, a_log, d_skip, g_ssm_norm, w_branch_attn, w_branch_ssm, w_out, w_gate_up, w_down):
    depth = w_in.shape[0]
    bp, bs = x_prompt.shape[0], x_sample.shape[0]
    past = cache_k.shape[2]
    y_p, y_s = x_prompt, x_sample
    c_all = jnp.concatenate([c_prompt, c_sample], axis=0)
    new_p = [[] for _ in range(5)]
    new_s = [[] for _ in range(5)]
    for l in range(depth):
        p = _layer_params(l, w_in, g_q, g_k, g_norm_mix, g_norm_ffn, w_conv, b_conv, dt_bias, a_log, d_skip,
                          g_ssm_norm, w_branch_attn, w_branch_ssm, w_out, w_gate_up, w_down)
        mod3 = _ada_mod(c_all, w_ada[l], b_ada[l]).reshape(bp + bs, 6, D_MODEL)
        zero_conv = jnp.zeros((bp, CONV_W - 1, CONV_CH), F32)
        zero_ssm = jnp.zeros((bp, SSM_HEADS, SSM_HEAD_DIM, D_STATE), F32)
        y_p, *st_p = _group_step(y_p, mod3, 0, None, zero_conv, zero_ssm, p)
        time_minor = lambda a: jnp.moveaxis(a, 1, -1)
        cache = (time_minor(cache_k[l]), time_minor(cache_v[l]), time_minor(cache_ki[l]))
        y_s, *st_s = _group_step(y_s, mod3, bp, cache, state_conv[l], state_ssm[l], p)
        for acc, a in zip(new_p, st_p):
            acc.append(a)
        for acc, a in zip(new_s, st_s):
            acc.append(a)
    return (y_p, y_s, *[jnp.stack(a) for a in new_p], *[jnp.stack(a) for a in new_s])
```

```python
import functools
import math

import jax
import jax.numpy as jnp
from jax import lax
from jax.experimental import pallas as pl
from jax.experimental.pallas import tpu as pltpu

F32, BF16, I32 = jnp.float32, jnp.bfloat16, jnp.int32

D_MODEL = 1024
CHUNK = 64
N_HEADS = 16
HEAD_DIM = 64
N_KV_HEADS = 4
Q_PER_KV = N_HEADS // N_KV_HEADS
IDX_HEADS = 8
IDX_DIM = 64
TOPK_MAX = 256
ROPE_THETA = 10000.0
D_INNER = 2 * D_MODEL
SSM_HEAD_DIM = 64
SSM_HEADS = D_INNER // SSM_HEAD_DIM
SSM_GROUPS = 8
HEADS_PER_GROUP = SSM_HEADS // SSM_GROUPS
GROUP_W = HEADS_PER_GROUP * SSM_HEAD_DIM
D_STATE = 128
CONV_W = 4
CONV_CH = D_INNER + 2 * SSM_GROUPS * D_STATE
D_FF = -(-8 * D_MODEL // (3 * 256)) * 256
EPS = 1e-6
QK_W = (N_HEADS + N_KV_HEADS) * HEAD_DIM
KV_W = N_KV_HEADS * HEAD_DIM
QI_W = IDX_HEADS * IDX_DIM

LANES = 128
MISC_WI = IDX_DIM
MISC_DT = IDX_DIM + IDX_HEADS
WI_SCALE = (IDX_HEADS ** -0.5) * (IDX_DIM ** -0.5)
INT_MIN = -(2 ** 31)
NEG_BIG = -1e30
COUNT_ROWS = 64
SSD_CHUNKS_PER_STEP = 8
ATTN_TQ = 256
ATTN_LOGITS_BYTES = 16 * 1024 * 1024
VMEM_LIMIT = 56 * 1024 * 1024


def _cparams(*sem):
    return pltpu.CompilerParams(dimension_semantics=sem, vmem_limit_bytes=VMEM_LIMIT)


def _silu(x):
    h = 0.5 * x
    return h + h * jnp.tanh(h)


def _split_bf16(x, n):
    pieces = []
    r = x
    for _ in range(n):
        p = r.astype(BF16)
        pieces.append(p)
        r = r - p.astype(F32)
    return pieces


def _dot(a, b):
    return jnp.dot(a, b, preferred_element_type=F32)


def _dot_nt(a, b):
    return lax.dot_general(a, b, (((1,), (1,)), ((), ())), preferred_element_type=F32)


def _dot_tn(a, b):
    return lax.dot_general(a, b, (((0,), (0,)), ((), ())), preferred_element_type=F32)


def _exact_dot(x, m_stacked, n):
    return _dot(jnp.concatenate(_split_bf16(x, n), axis=1), m_stacked)


def _exact_dot_left(m_tiled, x, n):
    return _dot(m_tiled, jnp.concatenate(_split_bf16(x, n), axis=0))


def _rotate_half(x):
    w = x.shape[-1]
    lane = lax.broadcasted_iota(I32, x.shape, x.ndim - 1)
    first = (lane % HEAD_DIM) < (HEAD_DIM // 2)
    return jnp.where(first, pltpu.roll(x, w - HEAD_DIM // 2, x.ndim - 1), pltpu.roll(x, HEAD_DIM // 2, x.ndim - 1))


def _rope(x, cos, sin):
    reps = x.shape[-1] // LANES
    if reps > 1:
        cos = jnp.tile(cos, (1, reps))
        sin = jnp.tile(sin, (1, reps))
    return x * cos + _rotate_half(x) * sin


def _mod_kernel(c_ref, w_ref, b_ref, o_ref):
    s = _silu(c_ref[...])
    o_ref[...] = _dot(s.astype(BF16), w_ref[...].astype(BF16)) + b_ref[...]


def _ada_mod(c_all, w_ada, b_ada):
    bt = c_all.shape[0]
    n = w_ada.shape[1]
    tn = D_MODEL
    return pl.pallas_call(
        _mod_kernel,
        grid=(n // tn,),
        in_specs=[
            pl.BlockSpec((bt, D_MODEL), lambda j: (0, 0)),
            pl.BlockSpec((D_MODEL, tn), lambda j: (0, j)),
            pl.BlockSpec((1, tn), lambda j: (0, j)),
        ],
        out_specs=pl.BlockSpec((bt, tn), lambda j: (0, j)),
        out_shape=jax.ShapeDtypeStruct((bt, n), F32),
        compiler_params=_cparams("parallel"),
        name="ada_mod",
    )(c_all, w_ada, b_ada.reshape(1, n))


def _modulated_norm(x, mod, gain, shift_idx, scale_idx):
    ms = jnp.mean(x * x, axis=-1, keepdims=True)
    xn = x * lax.rsqrt(ms + EPS)
    sh = mod[:, shift_idx:shift_idx + 1, :]
    sc = mod[:, scale_idx:scale_idx + 1, :]
    return xn * gain * (1.0 + sc) + sh


def _row_blocking(b, t):
    tt = min(t, 512)
    bb = max(1, min(b, 512 // tt))
    assert t % tt == 0 and b % bb == 0 and tt % CHUNK == 0, (b, t)
    return bb, tt


def _proj_act_kernel(h_ref, w_ref, o_ref, *, act):
    acc = _dot(h_ref[...], w_ref[...])
    if act == "silu":
        acc = _silu(acc)
    elif act == "sigmoid":
        acc = jax.nn.sigmoid(acc)
    o_ref[...] = acc.astype(o_ref.dtype)


def _proj_act(h2d, w, act, out_dtype, tm, tn):
    m, k = h2d.shape
    n = w.shape[1]
    return pl.pallas_call(
        functools.partial(_proj_act_kernel, act=act),
        grid=(n // tn, m // tm),
        in_specs=[
            pl.BlockSpec((tm, k), lambda j, i: (i, 0)),
            pl.BlockSpec((k, tn), lambda j, i: (0, j)),
        ],
        out_specs=pl.BlockSpec((tm, tn), lambda j, i: (i, j)),
        out_shape=jax.ShapeDtypeStruct((m, n), out_dtype),
        compiler_params=_cparams("parallel", "parallel"),
        name="proj_" + act,
    )(h2d, w)


SUBLANES = 8
CONV_SLAB = 256


def _proj_conv_kernel(h_ref, w_ref, cst_ref, wconv_ref, bconv_ref, o_ref, tail_ref, buf, *, bb, tt, tiles_per_seq):
    i = pl.program_id(1)
    tail = CONV_W - 1
    tn = w_ref.shape[1]

    @pl.when(i % tiles_per_seq == 0)
    def _():
        for s in range(bb):
            buf[s, 0:SUBLANES - tail, :] = jnp.zeros((SUBLANES - tail, tn), F32)
            buf[s, SUBLANES - tail:SUBLANES, :] = cst_ref[s]

    h = h_ref[...]
    for c in range(tn // CONV_SLAB):
        cs = slice(c * CONV_SLAB, (c + 1) * CONV_SLAB)
        acc = _dot(h, w_ref[:, cs])
        for s in range(bb):
            buf[s, SUBLANES:SUBLANES + tt, cs] = acc[s * tt:(s + 1) * tt]
            xb = buf[s, :, cs]
            xc = bconv_ref[:, cs] + wconv_ref[tail:tail + 1, cs] * xb[SUBLANES:]
            for jj in range(tail):
                shifted = pltpu.roll(xb, tail - jj, 0)[SUBLANES:]
                xc = xc + wconv_ref[jj:jj + 1, cs] * shifted
            o_ref[s * tt:(s + 1) * tt, cs] = _silu(xc).astype(o_ref.dtype)
            last = buf[s, tt:tt + SUBLANES, cs]
            tail_ref[s, :, cs] = last
            buf[s, 0:SUBLANES, cs] = last


def _proj_conv(h2d, w, conv_state, w_conv, b_conv, b, t, out_dtype):
    m, k = h2d.shape
    n = w.shape[1]
    bb, tt = _row_blocking(b, t)
    tm = bb * tt
    tiles_per_seq = t // tt
    tn = min(n, 2048)
    kern = functools.partial(_proj_conv_kernel, bb=bb, tt=tt, tiles_per_seq=tiles_per_seq)
    seq_block = lambda rows: pl.BlockSpec((bb, rows, tn), lambda j, i: (i // tiles_per_seq, 0, j))
    return pl.pallas_call(
        kern,
        grid=(n // tn, m // tm),
        in_specs=[
            pl.BlockSpec((tm, k), lambda j, i: (i, 0)),
            pl.BlockSpec((k, tn), lambda j, i: (0, j)),
            seq_block(CONV_W - 1),
            pl.BlockSpec((CONV_W, tn), lambda j, i: (0, j)),
            pl.BlockSpec((1, tn), lambda j, i: (0, j)),
        ],
        out_specs=[pl.BlockSpec((tm, tn), lambda j, i: (i, j)), seq_block(SUBLANES)],
        out_shape=[jax.ShapeDtypeStruct((m, n), out_dtype), jax.ShapeDtypeStruct((b, SUBLANES, n), F32)],
        scratch_shapes=[pltpu.VMEM((bb, SUBLANES + tt, tn), F32)],
        compiler_params=_cparams("parallel", "arbitrary"),
        name="proj_conv",
    )(h2d, w, conv_state, w_conv, b_conv)


def _store_time_minor(ref, x):
    bb, tt = ref.shape[0], ref.shape[-1]
    for s in range(bb):
        ref[s] = x[s * tt:(s + 1) * tt, :].T.reshape(ref.shape[1:])


def _qk_kernel(x_ref, mod_ref, gmix_ref, w_ref, gsum_ref, gexp_ref, gain_ref, cos_ref, sin_ref, q_ref, k_ref, h_ref):
    bb, tt, d = x_ref.shape
    h = _modulated_norm(x_ref[...], mod_ref[...], gmix_ref[...], 0, 1).astype(h_ref.dtype).reshape(bb * tt, d)
    h_ref[...] = h
    acc = _dot(h, w_ref[...])
    ss = _exact_dot(acc * acc, gsum_ref[...], 2)
    rs = lax.rsqrt(ss * (1.0 / HEAD_DIM) + EPS)
    rs_full = _exact_dot(rs, gexp_ref[...], 2)
    xn = acc * rs_full * gain_ref[...]
    out = _rope(xn, cos_ref[...], sin_ref[...])
    nq = N_HEADS * HEAD_DIM
    q_ref[...] = (out[:, :nq] * (HEAD_DIM ** -0.5)).astype(q_ref.dtype)
    _store_time_minor(k_ref, out[:, nq:])


def _vm_kernel(h_ref, w_ref, cos_ref, sin_ref, dtb_ref, v_ref, qi_ref, misc_ref, kit_ref):
    acc = _dot(h_ref[...], w_ref[...])
    cos, sin = cos_ref[...], sin_ref[...]
    _store_time_minor(v_ref, acc[:, :KV_W])
    qi_ref[...] = _rope(acc[:, KV_W:KV_W + QI_W], cos, sin).astype(qi_ref.dtype)
    m = acc[:, KV_W + QI_W:]
    lane = lax.broadcasted_iota(I32, m.shape, 1)
    roped = _rope(m, cos, sin)
    dt = jax.nn.softplus(m + dtb_ref[...])
    misc_ref[...] = jnp.where(lane < MISC_WI, roped,
                              jnp.where(lane < MISC_DT, m * WI_SCALE,
                                        jnp.where(lane < MISC_DT + SSM_HEADS, dt, 0.0)))
    _store_time_minor(kit_ref, roped[:, :IDX_DIM])


def _table_spec(tab_rows, tm):
    nblk = tab_rows // tm
    return pl.BlockSpec((tm, LANES), lambda i: (i % nblk, 0))


def _time_minor_spec(b, t, *mid):
    bb, tt = _row_blocking(b, t)
    per_seq = t // tt
    zeros = (0,) * len(mid)
    return pl.BlockSpec((bb, *mid, tt), lambda i: (i // per_seq, *zeros, i % per_seq))


def _proj_qk(x, mod3, mod_off, g_mix, w_qk, gsum, gexp, gain, cos_tab, sin_tab):
    b, t, k = x.shape
    m = b * t
    bb, tt = _row_blocking(b, t)
    tm = bb * tt
    per_seq = t // tt
    assert mod_off % bb == 0, (mod_off, bb)
    off = mod_off // bb
    nq = N_HEADS * HEAD_DIM
    const = lambda i: (0, 0)
    return pl.pallas_call(
        _qk_kernel,
        grid=(m // tm,),
        in_specs=[
            pl.BlockSpec((bb, tt, k), lambda i: (i // per_seq, i % per_seq, 0)),
            pl.BlockSpec((bb, 6, k), lambda i: (i // per_seq + off, 0, 0)),
            pl.BlockSpec((1, 1, k), lambda i: (0, 0, 0)),
            pl.BlockSpec((k, QK_W), const),
            pl.BlockSpec((2 * QK_W, LANES), const),
            pl.BlockSpec((2 * LANES, QK_W), const),
            pl.BlockSpec((1, QK_W), const),
            _table_spec(cos_tab.shape[0], tm),
            _table_spec(sin_tab.shape[0], tm),
        ],
        out_specs=[pl.BlockSpec((tm, nq), lambda i: (i, 0)), _time_minor_spec(b, t, N_KV_HEADS, HEAD_DIM),
                   pl.BlockSpec((tm, k), lambda i: (i, 0))],
        out_shape=[jax.ShapeDtypeStruct((m, nq), BF16), jax.ShapeDtypeStruct((b, N_KV_HEADS, HEAD_DIM, t), F32),
                   jax.ShapeDtypeStruct((m, k), BF16)],
        compiler_params=_cparams("parallel"),
        name="proj_qk",
    )(x, mod3, g_mix.reshape(1, 1, k), w_qk, gsum, gexp, gain, cos_tab, sin_tab)


def _proj_vm(h2d, w_vm, cos_tab, sin_tab, dtb, b, t):
    m, k = h2d.shape
    tm = math.prod(_row_blocking(b, t))
    wn = w_vm.shape[1]
    const = lambda i: (0, 0)
    return pl.pallas_call(
        _vm_kernel,
        grid=(m // tm,),
        in_specs=[
            pl.BlockSpec((tm, k), lambda i: (i, 0)),
            pl.BlockSpec((k, wn), const),
            _table_spec(cos_tab.shape[0], tm),
            _table_spec(sin_tab.shape[0], tm),
            pl.BlockSpec((1, LANES), const),
        ],
        out_specs=[_time_minor_spec(b, t, N_KV_HEADS, HEAD_DIM), pl.BlockSpec((tm, QI_W), lambda i: (i, 0)),
                   pl.BlockSpec((tm, LANES), lambda i: (i, 0)), _time_minor_spec(b, t, IDX_DIM)],
        out_shape=[jax.ShapeDtypeStruct((b, N_KV_HEADS, HEAD_DIM, t), F32), jax.ShapeDtypeStruct((m, QI_W), BF16),
                   jax.ShapeDtypeStruct((m, LANES), F32), jax.ShapeDtypeStruct((b, IDX_DIM, t), F32)],
        compiler_params=_cparams("parallel"),
        name="proj_vm",
    )(h2d, w_vm, cos_tab, sin_tab, dtb)


def _attn_kernel(*refs, tq, kb, gp, topk, past, t_new, has_cache):
    n_in = 9 if has_cache else 6
    q_ref, qi_ref, wi_ref, k_ref, v_ref, kit_ref = refs[:6]
    o_ref = refs[n_in]
    kbf, vbf, kibf, sc_scr, keep_scr, qis, qs, mb_scr, acc_scr, s_scr = refs[n_in + 1:n_in + 11]
    if has_cache:
        ck_ref, cv_ref, cki_ref = refs[6:9]
        ckb, cvb, ckib = refs[n_in + 11:]
    j = pl.program_id(1)
    kbn = min(kb, t_new)
    n_cache_blocks = past // kb

    def ones_row(n):
        return jnp.where(lax.broadcasted_iota(I32, (HEAD_DIM, n), 0) == 0, 1.0, 0.0).astype(BF16)

    def stage(k_src, v_src, ki_src, k_dst, v_dst, ki_dst, n):
        for g in range(N_KV_HEADS):
            k_dst[g] = k_src[0, g].astype(BF16)
            v_dst[g, 0:HEAD_DIM, :] = v_src[0, g].astype(BF16)
            v_dst[g, HEAD_DIM:2 * HEAD_DIM, :] = ones_row(n)
        ki_dst[...] = ki_src[0].T.astype(BF16)

    @pl.when(j == 0)
    def _():
        stage(k_ref, v_ref, kit_ref, kbf, vbf, kibf, t_new)
        if has_cache:
            stage(ck_ref, cv_ref, cki_ref, ckb, cvb, ckib, past)

    for h in range(IDX_HEADS):
        qis[h * tq:(h + 1) * tq, :] = qi_ref[0, :, h * IDX_DIM:(h + 1) * IDX_DIM]
    for h in range(N_HEADS):
        qs[h * tq:(h + 1) * tq, :] = q_ref[0, :, h * HEAD_DIM:(h + 1) * HEAD_DIM]
    wi_t = wi_ref[0, :, MISC_WI:MISC_WI + IDX_HEADS].T
    hpv = max(1, LANES // tq)
    wi_wide = [jnp.concatenate([wi_t[h + p:h + p + 1, :] for p in range(hpv)], axis=1) if hpv > 1 else wi_t[h:h + 1, :]
               for h in range(0, IDX_HEADS, hpv)]

    qpos = past + j * tq + lax.broadcasted_iota(I32, (1, tq), 1)
    limit = (qpos // CHUNK + 1) * CHUNK
    n_new_blocks = (j * tq + tq + kbn - 1) // kbn

    def by_pairs(fn, n_blocks, width, init):
        def pair(i, c):
            off = pl.multiple_of(i * (2 * width), 2 * width)
            return fn(pl.multiple_of(off + width, width), fn(off, c))
        c = lax.fori_loop(0, n_blocks // 2, pair, init)
        if isinstance(n_blocks, int):
            return fn((n_blocks - 1) * width, c) if n_blocks % 2 else c
        last = pl.multiple_of((n_blocks - 1) * width, width)
        return lax.cond(n_blocks % 2 == 1, lambda c: fn(last, c), lambda c: c, c)

    def over_cache(fn, init):
        if not has_cache:
            return init
        return by_pairs(fn, n_cache_blocks, kb, init)

    def over_new(fn, init):
        if t_new <= kb:
            return fn(0, init)
        return by_pairs(fn, n_new_blocks, kbn, init)

    def lanes_at(off):
        return past + off if isinstance(off, int) else pl.multiple_of(past + off, LANES)

    def score_block(ki_blk, kpos0, width):
        lg = _dot_nt(ki_blk, qis[...])
        wide = jnp.zeros((width, hpv * tq), F32)
        for i, w_row in enumerate(wi_wide):
            wide = wide + w_row * jnp.maximum(lg[:, i * hpv * tq:(i + 1) * hpv * tq], 0.0)
        sc = wide[:, :tq]
        for p in range(1, hpv):
            sc = sc + wide[:, p * tq:(p + 1) * tq]
        kpos = kpos0 + lax.broadcasted_iota(I32, (width, 1), 0)
        sc_scr[pl.ds(kpos0, width), :] = jnp.where(kpos < limit, sc, -jnp.inf)

    def p1c(off, c):
        score_block(ckib[pl.ds(off, kb), :], off, kb)
        return c

    def p1n(off, c):
        score_block(kibf[pl.ds(off, kbn), :], lanes_at(off), kbn)
        return c

    over_cache(p1c, 0)
    over_new(p1n, 0)

    def key_to_float(c):
        return pltpu.bitcast(jnp.where(c >= 0, c, c ^ 0x7FFFFFFF), F32)

    def count(cmp, cand):
        def cnt(row_off, width, acc):
            hit = jnp.where(cmp(sc_scr[pl.ds(row_off, width), :], cand), 1.0, 0.0)
            for r in range(width // COUNT_ROWS):
                acc = acc + hit[r * COUNT_ROWS:(r + 1) * COUNT_ROWS]
            return acc
        acc = jnp.zeros((COUNT_ROWS, tq), F32)
        acc = over_cache(lambda off, a: cnt(off, kb, a), acc)
        acc = over_new(lambda off, a: cnt(lanes_at(off), kbn, a), acc)
        return jnp.sum(acc, axis=0, keepdims=True)

    def bit_step(it, prefix):
        bit = jnp.left_shift(jnp.int32(1), 31 - it)
        cand = key_to_float((prefix | bit) ^ INT_MIN)
        return jnp.where(count(jnp.greater_equal, cand) >= float(topk), prefix | bit, prefix)

    prefix = lax.fori_loop(0, 32, bit_step, jnp.zeros((1, tq), I32))
    thr = key_to_float(prefix ^ INT_MIN)
    flt_max = float(jnp.finfo(F32).max)
    thr = jnp.where(thr >= -flt_max, thr, -flt_max)

    n_ge = count(jnp.greater_equal, thr)
    has_ties = jnp.max(n_ge) > float(topk)

    def keep_ranked(row_off, width, need, seen):
        sc = sc_scr[pl.ds(row_off, width), :]
        tie = sc == thr
        tie_f = jnp.where(tie, 1.0, 0.0)
        below = lax.broadcasted_iota(I32, (width, width), 1) <= lax.broadcasted_iota(I32, (width, width), 0)
        rank = seen + _dot(jnp.where(below, 1.0, 0.0).astype(BF16), tie_f.astype(BF16))
        keep = (sc > thr) | (tie & (rank <= need))
        keep_scr[pl.ds(row_off, width), :] = jnp.where(keep, 1.0, 0.0).astype(BF16)
        return seen + jnp.sum(tie_f, axis=0, keepdims=True)

    def keep_all_ties(row_off, width, c):
        sc = sc_scr[pl.ds(row_off, width), :]
        keep_scr[pl.ds(row_off, width), :] = jnp.where(sc >= thr, 1.0, 0.0).astype(BF16)
        return c

    def with_ties():
        need = float(topk) - count(jnp.greater, thr)
        seen = over_cache(lambda off, s: keep_ranked(off, kb, need, s), jnp.zeros((1, tq), F32))
        over_new(lambda off, s: keep_ranked(lanes_at(off), kbn, need, s), seen)

    def without_ties():
        over_cache(lambda off, c: keep_all_ties(off, kb, c), 0)
        over_new(lambda off, c: keep_all_ties(lanes_at(off), kbn, c), 0)

    lax.cond(has_ties, with_ties, without_ties)

    rows = Q_PER_KV * tq

    eye = jnp.where(lax.broadcasted_iota(I32, (tq, tq), 0) == lax.broadcasted_iota(I32, (tq, tq), 1), 1.0, 0.0).astype(BF16)

    def pass_a(g0, k_of, lane_off, width):
        keep_q = _dot_nt(eye, keep_scr[pl.ds(lane_off, width), :])
        b = jnp.tile(jnp.where(keep_q > 0.5, 0.0, NEG_BIG), (Q_PER_KV, 1))
        for gi in range(gp):
            g = g0 + gi
            s = _dot(qs[g * rows:(g + 1) * rows, :], k_of(g)) + b
            s_scr[gi, :, pl.ds(lane_off, width)] = s
            m = mb_scr[gi]
            if width % LANES:
                m = jnp.maximum(m, jnp.max(s, axis=1, keepdims=True))
            else:
                for c in range(width // LANES):
                    m = jnp.maximum(m, s[:, c * LANES:(c + 1) * LANES])
            mb_scr[gi] = m

    def pass_b(g0, v_of, lane_off, width):
        for gi in range(gp):
            m = mb_scr[gi]
            m = m[:, :width] if width < LANES else jnp.tile(m, (1, width // LANES))
            p = jnp.exp(s_scr[gi, :, pl.ds(lane_off, width)] - m)
            acc_scr[gi] += _dot_nt(p.astype(BF16), v_of(g0 + gi))

    def cache_k(off):
        return lambda g: ckb[g, :, pl.ds(off, kb)]

    def cache_v(off):
        return lambda g: cvb[g, :, pl.ds(off, kb)]

    def new_k(off):
        return lambda g: kbf[g, :, pl.ds(off, kbn)]

    def new_v(off):
        return lambda g: vbf[g, :, pl.ds(off, kbn)]

    def run(fn, cache_args, new_args):
        def on_cache(off, c):
            fn(*[a(off) for a in cache_args], off, kb)
            return c

        def on_new(off, c):
            fn(*[a(off) for a in new_args], lanes_at(off), kbn)
            return c

        over_cache(on_cache, 0)
        over_new(on_new, 0)

    for g0 in range(0, N_KV_HEADS, gp):
        mb_scr[...] = jnp.full(mb_scr.shape, NEG_BIG, F32)
        run(functools.partial(pass_a, g0), [cache_k], [new_k])
        for gi in range(gp):
            mb_scr[gi] = jnp.broadcast_to(jnp.max(mb_scr[gi], axis=1, keepdims=True), (rows, LANES))
        acc_scr[...] = jnp.zeros(acc_scr.shape, F32)
        run(functools.partial(pass_b, g0), [cache_v], [new_v])

        for gi in range(gp):
            a = acc_scr[gi]
            out = a[:, :HEAD_DIM] / a[:, HEAD_DIM:HEAD_DIM + 1]
            for r in range(Q_PER_KV):
                hh = (g0 + gi) * Q_PER_KV + r
                o_ref[0, :, hh * HEAD_DIM:(hh + 1) * HEAD_DIM] = out[r * tq:(r + 1) * tq].astype(o_ref.dtype)


def _attention(q, qi, misc, kt, vt, kit, cache=None):
    b, t, _ = q.shape
    has_cache = cache is not None
    past = cache[2].shape[2] if has_cache else 0
    n_keys = past + t
    topk = min(TOPK_MAX, n_keys // 4)
    tq = min(t, ATTN_TQ)
    kb = 256
    rows = Q_PER_KV * tq
    key_w = past + -(-t // LANES) * LANES
    gp = N_KV_HEADS
    while gp > 1 and gp * rows * key_w * 4 > ATTN_LOGITS_BYTES:
        gp //= 2
    qtile = lambda w: pl.BlockSpec((1, tq, w), lambda i, j: (i, j, 0))
    heads = lambda n: pl.BlockSpec((1, N_KV_HEADS, HEAD_DIM, n), lambda i, j: (i, 0, 0, 0))
    idx = lambda n: pl.BlockSpec((1, IDX_DIM, n), lambda i, j: (i, 0, 0))
    in_specs = [qtile(N_HEADS * HEAD_DIM), qtile(QI_W), qtile(LANES), heads(t), heads(t), idx(t)]
    args = [q, qi, misc, kt, vt, kit]
    if has_cache:
        in_specs += [heads(past), heads(past), idx(past)]
        args += list(cache)
    kern = functools.partial(_attn_kernel, tq=tq, kb=kb, gp=gp, topk=topk, past=past, t_new=t, has_cache=has_cache)
    return pl.pallas_call(
        kern,
        grid=(b, t // tq),
        in_specs=in_specs,
        out_specs=qtile(N_HEADS * HEAD_DIM),
        out_shape=jax.ShapeDtypeStruct((b, t, N_HEADS * HEAD_DIM), BF16),
        scratch_shapes=[
            pltpu.VMEM((N_KV_HEADS, HEAD_DIM, t), BF16),
            pltpu.VMEM((N_KV_HEADS, 2 * HEAD_DIM, t), BF16),
            pltpu.VMEM((t, IDX_DIM), BF16),
            pltpu.VMEM((n_keys, tq), F32),
            pltpu.VMEM((n_keys, tq), BF16),
            pltpu.VMEM((IDX_HEADS * tq, IDX_DIM), BF16),
            pltpu.VMEM((N_HEADS * tq, HEAD_DIM), BF16),
            pltpu.VMEM((gp, rows, LANES), F32),
            pltpu.VMEM((gp, rows, 2 * HEAD_DIM), F32),
            pltpu.VMEM((gp, rows, key_w), F32),
        ] + ([
            pltpu.VMEM((N_KV_HEADS, HEAD_DIM, past), BF16),
            pltpu.VMEM((N_KV_HEADS, 2 * HEAD_DIM, past), BF16),
            pltpu.VMEM((past, IDX_DIM), BF16),
        ] if has_cache else []),
        compiler_params=_cparams("parallel", "arbitrary"),
        name="attn_cache" if has_cache else "attn_prompt",
    )(*args)


def _ssd_kernel(xs_ref, bc_ref, misc_ref, zs_ref, h0_ref, alog_ref, dskip_ref, gnorm_ref, expand_ref, y_ref, hout_ref):
    c = pl.program_id(1)
    L = CHUNK

    @pl.when(c == 0)
    def _():
        hout_ref[...] = h0_ref[...]

    a = -jnp.exp(alog_ref[...])
    ri = lax.broadcasted_iota(I32, (L, 3 * L), 0)
    ci = lax.broadcasted_iota(I32, (L, 3 * L), 1) % L
    tri3 = jnp.where(ri >= ci, 1.0, 0.0).astype(BF16)
    expand3 = expand_ref[...]
    row_i = lax.broadcasted_iota(I32, (L, D_INNER), 0)
    lane_j = lax.broadcasted_iota(I32, (L, D_INNER), 1) % SSM_HEAD_DIM
    causal = row_i >= lane_j
    blk_r = lax.broadcasted_iota(I32, (GROUP_W, GROUP_W), 0) // SSM_HEAD_DIM
    blk_c = lax.broadcasted_iota(I32, (GROUP_W, GROUP_W), 1) // SSM_HEAD_DIM
    same_head = blk_r == blk_c

    for sub in range(xs_ref.shape[1] // L):
        _ssd_chunk(slice(sub * L, (sub + 1) * L), xs_ref, bc_ref, misc_ref, zs_ref, dskip_ref, gnorm_ref, y_ref, hout_ref,
                   a, tri3, expand3, causal, same_head)


def _heads_over_time(x):
    L = x.shape[0]
    xt = x.T
    xtt = jnp.concatenate([xt, xt], axis=1)
    first = lax.broadcasted_iota(I32, (L, 2 * L), 1) < L
    cols = []
    for h in range(0, SSM_HEADS, 2):
        r = MISC_DT + h
        cols.append(jnp.where(first, jnp.broadcast_to(xtt[r:r + 1, :], (L, 2 * L)),
                              jnp.broadcast_to(xtt[r + 1:r + 2, :], (L, 2 * L))))
    return jnp.concatenate(cols, axis=1)


def _ssd_chunk(rows, xs_ref, bc_ref, misc_ref, zs_ref, dskip_ref, gnorm_ref, y_ref, hout_ref,
               a, tri3, expand3, causal, same_head):
    L = CHUNK
    dt = misc_ref[0, rows, :]
    acs = _exact_dot_left(tri3, dt * a, 3)
    col_acs = _exact_dot(acs, expand3, 3)
    col_dt = _exact_dot(dt, expand3[:2 * LANES], 2)
    row_acs = _heads_over_time(acs)
    row_dt = _heads_over_time(dt)
    a_last = acs[L - 1:L, MISC_DT:MISC_DT + SSM_HEADS]

    groups = range(SSM_GROUPS)
    gsl = [slice(g * GROUP_W, (g + 1) * GROUP_W) for g in groups]
    bgs = [bc_ref[0, rows, g * D_STATE:(g + 1) * D_STATE] for g in groups]
    cgs = [bc_ref[0, rows, (SSM_GROUPS + g) * D_STATE:(SSM_GROUPS + g + 1) * D_STATE] for g in groups]
    xgs = [xs_ref[0, rows, gs] for gs in gsl]
    hprevs = [hout_ref[0, g * HEADS_PER_GROUP:(g + 1) * HEADS_PER_GROUP].reshape(GROUP_W, D_STATE) for g in groups]
    cbs = [_dot_nt(cgs[g], jnp.tile(bgs[g], (HEADS_PER_GROUP, 1))) for g in groups]
    offs = [_dot_nt(cgs[g], hprevs[g].astype(BF16)) for g in groups]
    sts = []
    for g in groups:
        ce = col_acs[:, gsl[g]]
        w_state = jnp.exp(ce[L - 1:L, :] - ce) * col_dt[:, gsl[g]]
        sts.append(_dot_tn((xgs[g] * w_state).astype(BF16), bgs[g]))
    y_diags = []
    for g in groups:
        gs = gsl[g]
        mm = cbs[g] * jnp.exp(jnp.where(causal[:, gs], col_acs[:, gs] - row_acs[:, gs], -jnp.inf)) * row_dt[:, gs]
        xbd = jnp.where(same_head, jnp.tile(xgs[g].astype(BF16), (HEADS_PER_GROUP, 1)), 0.0).astype(BF16)
        y_diags.append(_dot(mm.astype(BF16), xbd))
    for g in groups:
        for r in range(HEADS_PER_GROUP):
            hh = g * HEADS_PER_GROUP + r
            decay = jnp.exp(a_last[:, hh:hh + 1])
            rs = slice(r * SSM_HEAD_DIM, (r + 1) * SSM_HEAD_DIM)
            hout_ref[0, hh] = decay * hprevs[g][rs] + sts[g][rs]
    for g in groups:
        gs = gsl[g]
        y_off = jnp.exp(col_acs[:, gs]) * offs[g]
        yt = (y_diags[g] + y_off + dskip_ref[:, gs] * xgs[g]) * zs_ref[0, rows, gs].astype(F32)
        ms = jnp.mean(yt * yt, axis=-1, keepdims=True)
        y_ref[0, rows, gs] = (yt * lax.rsqrt(ms + EPS) * gnorm_ref[:, gs]).astype(y_ref.dtype)


def _ssd(xs, bc, misc, zs, h0, a_log, dskip_full, g_norm, expand):
    b, t, _ = xs.shape
    rows = min(t, SSD_CHUNKS_PER_STEP * CHUNK)
    nc = t // rows
    chunk = lambda w: pl.BlockSpec((1, rows, w), lambda i, c: (i, c, 0))
    const2 = lambda r, w: pl.BlockSpec((r, w), lambda i, c: (0, 0))
    state = pl.BlockSpec((1, SSM_HEADS, SSM_HEAD_DIM, D_STATE), lambda i, c: (i, 0, 0, 0))
    return pl.pallas_call(
        _ssd_kernel,
        grid=(b, nc),
        in_specs=[
            chunk(D_INNER), chunk(2 * SSM_GROUPS * D_STATE), chunk(LANES), chunk(D_INNER), state,
            const2(1, LANES), const2(1, D_INNER), const2(1, D_INNER), const2(3 * LANES, D_INNER),
        ],
        out_specs=[chunk(D_INNER), state],
        out_shape=[jax.ShapeDtypeStruct((b, t, D_INNER), BF16),
                   jax.ShapeDtypeStruct((b, SSM_HEADS, SSM_HEAD_DIM, D_STATE), F32)],
        compiler_params=_cparams("parallel", "arbitrary"),
        name="ssd",
    )(xs, bc, misc, zs, h0, a_log, dskip_full, g_norm, expand)


def _merge_kernel(o_ref, y_ref, g_ref, x_ref, mod_ref, gain_ref, wa_ref, ws_ref, wo_ref, x1_ref, h2_ref):
    bb, tt, d = x_ref.shape
    rows = bb * tt
    o = o_ref[...].reshape(rows, -1)
    y = y_ref[...].reshape(rows, -1)
    gates = g_ref[...].reshape(rows, -1).astype(F32)
    mixed = gates[:, :d] * _dot(o, wa_ref[...]) + gates[:, d:] * _dot(y, ws_ref[...])
    out = _dot(mixed.astype(BF16), wo_ref[...]).reshape(bb, tt, d)
    mod = mod_ref[...]
    x1 = x_ref[...] + mod[:, 2:3, :] * out
    x1_ref[...] = x1
    h2_ref[...] = _modulated_norm(x1, mod, gain_ref[...], 3, 4).astype(h2_ref.dtype)


def _merge(o_attn, y, gates, x, mod3, mod_off, gain, wa, ws, wo):
    b, t, d = x.shape
    bb, tt = _row_blocking(b, t)
    assert mod_off % bb == 0, (mod_off, bb)
    off = mod_off // bb
    tile = lambda w: pl.BlockSpec((bb, tt, w), lambda i, j: (i, j, 0))
    const = lambda r, w: pl.BlockSpec((r, w), lambda i, j: (0, 0))
    return pl.pallas_call(
        _merge_kernel,
        grid=(b // bb, t // tt),
        in_specs=[
            tile(N_HEADS * HEAD_DIM), tile(D_INNER), tile(2 * D_MODEL), tile(d),
            pl.BlockSpec((bb, 6, d), lambda i, j: (i + off, 0, 0)),
            pl.BlockSpec((1, 1, d), lambda i, j: (0, 0, 0)),
            const(N_HEADS * HEAD_DIM, d), const(D_INNER, d), const(d, d),
        ],
        out_specs=[tile(d), tile(d)],
        out_shape=[jax.ShapeDtypeStruct((b, t, d), F32), jax.ShapeDtypeStruct((b, t, d), BF16)],
        compiler_params=_cparams("parallel", "parallel"),
        name="merge",
    )(o_attn, y, gates, x, mod3, gain.reshape(1, 1, d), wa, ws, wo)


def _ffn_kernel(h2_ref, x1_ref, mod_ref, wg_ref, wu_ref, wd_ref, o_ref, acc_ref):
    f = pl.program_id(2)
    bb, tt, d = x1_ref.shape
    h2 = h2_ref[...].reshape(bb * tt, d)
    act = _silu(_dot(h2, wg_ref[...])) * _dot(h2, wu_ref[...])
    part = _dot(act.astype(BF16), wd_ref[...])

    @pl.when(f == 0)
    def _():
        acc_ref[...] = part

    @pl.when(f > 0)
    def _():
        acc_ref[...] += part

    @pl.when(f == pl.num_programs(2) - 1)
    def _():
        o_ref[...] = x1_ref[...] + mod_ref[...][:, 5:6, :] * acc_ref[...].reshape(bb, tt, d)


def _ffn(h2, x1, mod3, mod_off, w_gu, w_down):
    b, t, d = x1.shape
    bb, tt = _row_blocking(b, t)
    assert mod_off % bb == 0, (mod_off, bb)
    off = mod_off // bb
    nf = 2
    tf = D_FF // nf
    tile = pl.BlockSpec((bb, tt, d), lambda i, j, f: (i, j, 0))
    return pl.pallas_call(
        _ffn_kernel,
        grid=(b // bb, t // tt, nf),
        in_specs=[
            tile, tile,
            pl.BlockSpec((bb, 6, d), lambda i, j, f: (i + off, 0, 0)),
            pl.BlockSpec((d, tf), lambda i, j, f: (0, f)),
            pl.BlockSpec((d, tf), lambda i, j, f: (0, nf + f)),
            pl.BlockSpec((tf, d), lambda i, j, f: (f, 0)),
        ],
        out_specs=tile,
        out_shape=jax.ShapeDtypeStruct((b, t, d), F32),
        scratch_shapes=[pltpu.VMEM((bb * tt, d), F32)],
        compiler_params=_cparams("parallel", "parallel", "arbitrary"),
        name="ffn",
    )(h2, x1, mod3, w_gu, w_gu, w_down)


def _rope_tables(t, past, tm):
    half = HEAD_DIM // 2
    inv = ROPE_THETA ** (-jnp.arange(half, dtype=F32) / half)
    ang = (past + jnp.arange(t)).astype(F32)[:, None] * inv[None, :]
    cos, sin = jnp.cos(ang), jnp.sin(ang)
    cos_t = jnp.concatenate([cos, cos, cos, cos], axis=1)
    sin_t = jnp.concatenate([-sin, sin, -sin, sin], axis=1)
    if t < tm:
        cos_t, sin_t = jnp.tile(cos_t, (tm // t, 1)), jnp.tile(sin_t, (tm // t, 1))
    return cos_t, sin_t


def _group_step(x, mod3, mod_off, cache, conv_state, ssm_state, p):
    b, t, d = x.shape
    m = b * t
    tm = min(512, m)
    past = cache[2].shape[2] if cache is not None else 0
    cos_t, sin_t = _rope_tables(t, past, tm)
    q, kt, h = _proj_qk(x, mod3, mod_off, p["g_norm_mix"], p["w_qk"], p["gsum"], p["gexp"], p["qk_gain"], cos_t, sin_t)
    vt, qi, misc, kit = _proj_vm(h, p["w_vm"], cos_t, sin_t, p["dtb"], b, t)
    zs = _proj_act(h, p["w_z"], "silu", BF16, tm, D_INNER)
    gates = _proj_act(h, p["w_g"], "sigmoid", BF16, tm, 2 * D_MODEL)
    half = CONV_CH // 2
    xs, tail_x = _proj_conv(h, p["w_xs"], conv_state[:, :, :half], p["w_conv"][:, :half],
                            p["b_conv"][:, :half], b, t, F32)
    bc, tail_bc = _proj_conv(h, p["w_bc"], conv_state[:, :, half:], p["w_conv"][:, half:],
                             p["b_conv"][:, half:], b, t, BF16)

    r3 = lambda a: a.reshape(b, t, a.shape[-1])
    o_attn = _attention(r3(q), r3(qi), r3(misc), kt, vt, kit, cache)
    y, h_last = _ssd(r3(xs), r3(bc), r3(misc), r3(zs), ssm_state, p["a_log"], p["dskip_full"], p["g_ssm_norm"],
                     p["expand"])
    x1, h2 = _merge(o_attn, y, r3(gates), x, mod3, mod_off, p["g_norm_ffn"], p["w_ba"], p["w_bs"], p["w_out"])
    out = _ffn(h2, x1, mod3, mod_off, p["w_gu"], p["w_down"])
    conv_new = jnp.concatenate([tail_x, tail_bc], axis=2)[:, -(CONV_W - 1):]
    token_major = lambda a: jnp.moveaxis(a, -1, 1)
    return out, token_major(kt), token_major(vt), token_major(kit), conv_new, h_last


IN_SIZES = (N_HEADS * HEAD_DIM, KV_W, KV_W, QI_W, IDX_DIM, IDX_HEADS, D_INNER, CONV_CH, SSM_HEADS, 2 * D_MODEL)
IN_OFFS = tuple(sum(IN_SIZES[:i]) for i in range(len(IN_SIZES) + 1))
VM_W = KV_W + QI_W + LANES


def _repack_kernel(wt_ref, qk_ref, vm_ref, z_ref, xs_ref, bc_ref, g_ref):
    o = IN_OFFS
    col = lambda a, b: wt_ref[a:b, :].T.astype(BF16)
    qk_ref[...] = col(o[0], o[2])
    whole = (o[4] - o[2]) // LANES * LANES
    vm_ref[:, 0:whole] = col(o[2], o[2] + whole)
    a_ki, a_dt = o[2] + whole, (o[8] - MISC_DT) // LANES * LANES
    assert a_ki == o[4] and o[8] - a_dt == MISC_DT, (a_ki, a_dt)
    lane = lax.broadcasted_iota(I32, (LANES, LANES), 1)
    ki_wi, dt = wt_ref[a_ki:a_ki + LANES, :].T, wt_ref[a_dt:a_dt + LANES, :].T
    vm_ref[:, whole:] = jnp.where(lane < MISC_DT, ki_wi, jnp.where(lane < MISC_DT + SSM_HEADS, dt, 0.0)).astype(BF16)
    z_ref[...] = col(o[6], o[7])
    xs_ref[...] = col(o[7], o[7] + CONV_CH // 2)
    bc_ref[...] = col(o[7] + CONV_CH // 2, o[8])
    g_ref[...] = col(o[9], o[10])


def _repack_w_in(wt):
    n, d = wt.shape
    widths = (QK_W, VM_W, D_INNER, CONV_CH // 2, CONV_CH // 2, 2 * D_MODEL)
    return pl.pallas_call(
        _repack_kernel,
        grid=(d // LANES,),
        in_specs=[pl.BlockSpec((n, LANES), lambda i: (0, i))],
        out_specs=[pl.BlockSpec((LANES, wd), lambda i: (i, 0)) for wd in widths],
        out_shape=[jax.ShapeDtypeStruct((d, wd), BF16) for wd in widths],
        compiler_params=_cparams("parallel"),
        name="repack_w_in",
    )(wt)


def _layer_params(l, w_in, g_q, g_k, g_norm_mix, g_norm_ffn, w_conv, b_conv, dt_bias, a_log, d_skip, g_ssm_norm,
                  w_branch_attn, w_branch_ssm, w_out, w_gate_up, w_down):
    w_qk, w_vm, w_z, w_xs, w_bc, w_g = _repack_w_in(w_in[l].T)
    head_of = jnp.arange(QK_W) // HEAD_DIM
    gsum = (head_of[:, None] == jnp.arange(LANES)[None, :]).astype(BF16)
    dtb = jnp.zeros((1, LANES), F32).at[0, MISC_DT:MISC_DT + SSM_HEADS].set(dt_bias[l])
    expand = ((jnp.arange(LANES) - MISC_DT)[:, None] == (jnp.arange(D_INNER) // SSM_HEAD_DIM)[None, :]).astype(BF16)
    a_log_lanes = jnp.zeros((1, LANES), F32).at[0, MISC_DT:MISC_DT + SSM_HEADS].set(a_log[l])
    return dict(
        w_qk=w_qk, w_vm=w_vm, w_z=w_z, w_xs=w_xs, w_bc=w_bc, w_g=w_g,
        gsum=jnp.tile(gsum, (2, 1)), gexp=jnp.tile(gsum.T, (2, 1)),
        qk_gain=jnp.concatenate([jnp.tile(g_q[l], N_HEADS), jnp.tile(g_k[l], N_KV_HEADS)]).reshape(1, QK_W),
        dtb=dtb, expand=jnp.tile(expand, (3, 1)),
        g_norm_mix=g_norm_mix[l], g_norm_ffn=g_norm_ffn[l],
        w_conv=w_conv[l], b_conv=b_conv[l].reshape(1, CONV_CH), a_log=a_log_lanes,
        dskip_full=jnp.repeat(d_skip[l], SSM_HEAD_DIM).reshape(1, D_INNER), g_ssm_norm=g_ssm_norm[l].reshape(1, D_INNER),
        w_ba=w_branch_attn[l].astype(BF16), w_bs=w_branch_ssm[l].astype(BF16), w_out=w_out[l].astype(BF16),
        w_gu=w_gate_up[l].astype(BF16), w_down=w_down[l].astype(BF16),
    )


def kernel(x_prompt, x_sample, cache_k, cache_v, cache_ki, state_conv, state_ssm, c_prompt, c_sample, w_ada, b_ada, g_norm_mix, g_norm_ffn, w_in, g_q, g_k, w_conv, b_conv, dt_bias, a_log, d_skip, g_ssm_norm, w_branch_attn, w_branch_ssm, w_out, w_gate_up, w_down):
    depth = w_in.shape[0]
    bp, bs = x_prompt.shape[0], x_sample.shape[0]
    past = cache_k.shape[2]
    y_p, y_s = x_prompt, x_sample
    c_all = jnp.concatenate([c_prompt, c_sample], axis=0)
    new_p = [[] for _ in range(5)]
    new_s = [[] for _ in range(5)]
    for l in range(depth):
        p = _layer_params(l, w_in, g_q, g_k, g_norm_mix, g_norm_ffn, w_conv, b_conv, dt_bias, a_log, d_skip,
                          g_ssm_norm, w_branch_attn, w_branch_ssm, w_out, w_gate_up, w_down)
        mod3 = _ada_mod(c_all, w_ada[l], b_ada[l]).reshape(bp + bs, 6, D_MODEL)
        zero_conv = jnp.zeros((bp, CONV_W - 1, CONV_CH), F32)
        zero_ssm = jnp.zeros((bp, SSM_HEADS, SSM_HEAD_DIM, D_STATE), F32)
        y_p, *st_p = _group_step(y_p, mod3, 0, None, zero_conv, zero_ssm, p)
        time_minor = lambda a: jnp.moveaxis(a, 1, -1)
        cache = (time_minor(cache_k[l]), time_minor(cache_v[l]), time_minor(cache_ki[l]))
        y_s, *st_s = _group_step(y_s, mod3, bp, cache, state_conv[l], state_ssm[l], p)
        for acc, a in zip(new_p, st_p):
            acc.append(a)
        for acc, a in zip(new_s, st_s):
            acc.append(a)
    return (y_p, y_s, *[jnp.stack(a) for a in new_p], *[jnp.stack(a) for a in new_s])
```

```python
import functools
import math

import jax
import jax.numpy as jnp
from jax import lax
from jax.experimental import pallas as pl
from jax.experimental.pallas import tpu as pltpu

F32, BF16, I32 = jnp.float32, jnp.bfloat16, jnp.int32

D_MODEL = 1024
CHUNK = 64
N_HEADS = 16
HEAD_DIM = 64
N_KV_HEADS = 4
Q_PER_KV = N_HEADS // N_KV_HEADS
IDX_HEADS = 8
IDX_DIM = 64
TOPK_MAX = 256
ROPE_THETA = 10000.0
D_INNER = 2 * D_MODEL
SSM_HEAD_DIM = 64
SSM_HEADS = D_INNER // SSM_HEAD_DIM
SSM_GROUPS = 8
HEADS_PER_GROUP = SSM_HEADS // SSM_GROUPS
GROUP_W = HEADS_PER_GROUP * SSM_HEAD_DIM
D_STATE = 128
CONV_W = 4
CONV_CH = D_INNER + 2 * SSM_GROUPS * D_STATE
D_FF = -(-8 * D_MODEL // (3 * 256)) * 256
EPS = 1e-6
QK_W = (N_HEADS + N_KV_HEADS) * HEAD_DIM
KV_W = N_KV_HEADS * HEAD_DIM
QI_W = IDX_HEADS * IDX_DIM

LANES = 128
MISC_WI = IDX_DIM
MISC_DT = IDX_DIM + IDX_HEADS
WI_SCALE = (IDX_HEADS ** -0.5) * (IDX_DIM ** -0.5)
INT_MIN = -(2 ** 31)
NEG_BIG = -1e30
COUNT_ROWS = 64
SSD_CHUNKS_PER_STEP = 8
ATTN_TQ = 256
ATTN_LOGITS_BYTES = 16 * 1024 * 1024
VMEM_LIMIT = 56 * 1024 * 1024


def _cparams(*sem):
    return pltpu.CompilerParams(dimension_semantics=sem, vmem_limit_bytes=VMEM_LIMIT)


def _silu(x):
    h = 0.5 * x
    return h + h * jnp.tanh(h)


def _split_bf16(x, n):
    pieces = []
    r = x
    for _ in range(n):
        p = r.astype(BF16)
        pieces.append(p)
        r = r - p.astype(F32)
    return pieces


def _dot(a, b):
    return jnp.dot(a, b, preferred_element_type=F32)


def _dot_nt(a, b):
    return lax.dot_general(a, b, (((1,), (1,)), ((), ())), preferred_element_type=F32)


def _dot_tn(a, b):
    return lax.dot_general(a, b, (((0,), (0,)), ((), ())), preferred_element_type=F32)


def _exact_dot(x, m_stacked, n):
    return _dot(jnp.concatenate(_split_bf16(x, n), axis=1), m_stacked)


def _exact_dot_left(m_tiled, x, n):
    return _dot(m_tiled, jnp.concatenate(_split_bf16(x, n), axis=0))


def _rotate_half(x):
    w = x.shape[-1]
    lane = lax.broadcasted_iota(I32, x.shape, x.ndim - 1)
    first = (lane % HEAD_DIM) < (HEAD_DIM // 2)
    return jnp.where(first, pltpu.roll(x, w - HEAD_DIM // 2, x.ndim - 1), pltpu.roll(x, HEAD_DIM // 2, x.ndim - 1))


def _rope(x, cos, sin):
    reps = x.shape[-1] // LANES
    if reps > 1:
        cos = jnp.tile(cos, (1, reps))
        sin = jnp.tile(sin, (1, reps))
    return x * cos + _rotate_half(x) * sin


def _mod_kernel(c_ref, w_ref, b_ref, o_ref):
    s = _silu(c_ref[...])
    o_ref[...] = _dot(s.astype(BF16), w_ref[...].astype(BF16)) + b_ref[...]


def _ada_mod(c_all, w_ada, b_ada):
    bt = c_all.shape[0]
    n = w_ada.shape[1]
    tn = D_MODEL
    return pl.pallas_call(
        _mod_kernel,
        grid=(n // tn,),
        in_specs=[
            pl.BlockSpec((bt, D_MODEL), lambda j: (0, 0)),
            pl.BlockSpec((D_MODEL, tn), lambda j: (0, j)),
            pl.BlockSpec((1, tn), lambda j: (0, j)),
        ],
        out_specs=pl.BlockSpec((bt, tn), lambda j: (0, j)),
        out_shape=jax.ShapeDtypeStruct((bt, n), F32),
        compiler_params=_cparams("parallel"),
        name="ada_mod",
    )(c_all, w_ada, b_ada.reshape(1, n))


def _modulated_norm(x, mod, gain, shift_idx, scale_idx):
    ms = jnp.mean(x * x, axis=-1, keepdims=True)
    xn = x * lax.rsqrt(ms + EPS)
    sh = mod[:, shift_idx:shift_idx + 1, :]
    sc = mod[:, scale_idx:scale_idx + 1, :]
    return xn * gain * (1.0 + sc) + sh


def _row_blocking(b, t):
    tt = min(t, 512)
    bb = max(1, min(b, 512 // tt))
    assert t % tt == 0 and b % bb == 0 and tt % CHUNK == 0, (b, t)
    return bb, tt


def _proj_act_kernel(h_ref, w_ref, o_ref, *, act):
    acc = _dot(h_ref[...], w_ref[...])
    if act == "silu":
        acc = _silu(acc)
    elif act == "sigmoid":
        acc = jax.nn.sigmoid(acc)
    o_ref[...] = acc.astype(o_ref.dtype)


def _proj_act(h2d, w, act, out_dtype, tm, tn):
    m, k = h2d.shape
    n = w.shape[1]
    return pl.pallas_call(
        functools.partial(_proj_act_kernel, act=act),
        grid=(n // tn, m // tm),
        in_specs=[
            pl.BlockSpec((tm, k), lambda j, i: (i, 0)),
            pl.BlockSpec((k, tn), lambda j, i: (0, j)),
        ],
        out_specs=pl.BlockSpec((tm, tn), lambda j, i: (i, j)),
        out_shape=jax.ShapeDtypeStruct((m, n), out_dtype),
        compiler_params=_cparams("parallel", "parallel"),
        name="proj_" + act,
    )(h2d, w)


SUBLANES = 8
CONV_SLAB = 256


def _proj_conv_kernel(h_ref, w_ref, cst_ref, wconv_ref, bconv_ref, o_ref, tail_ref, buf, *, bb, tt, tiles_per_seq):
    i = pl.program_id(1)
    tail = CONV_W - 1
    tn = w_ref.shape[1]

    @pl.when(i % tiles_per_seq == 0)
    def _():
        for s in range(bb):
            buf[s, 0:SUBLANES - tail, :] = jnp.zeros((SUBLANES - tail, tn), F32)
            buf[s, SUBLANES - tail:SUBLANES, :] = cst_ref[s]

    h = h_ref[...]
    for c in range(tn // CONV_SLAB):
        cs = slice(c * CONV_SLAB, (c + 1) * CONV_SLAB)
        acc = _dot(h, w_ref[:, cs])
        for s in range(bb):
            buf[s, SUBLANES:SUBLANES + tt, cs] = acc[s * tt:(s + 1) * tt]
            xb = buf[s, :, cs]
            xc = bconv_ref[:, cs] + wconv_ref[tail:tail + 1, cs] * xb[SUBLANES:]
            for jj in range(tail):
                shifted = pltpu.roll(xb, tail - jj, 0)[SUBLANES:]
                xc = xc + wconv_ref[jj:jj + 1, cs] * shifted
            o_ref[s * tt:(s + 1) * tt, cs] = _silu(xc).astype(o_ref.dtype)
            last = buf[s, tt:tt + SUBLANES, cs]
            tail_ref[s, :, cs] = last
            buf[s, 0:SUBLANES, cs] = last


def _proj_conv(h2d, w, conv_state, w_conv, b_conv, b, t, out_dtype):
    m, k = h2d.shape
    n = w.shape[1]
    bb, tt = _row_blocking(b, t)
    tm = bb * tt
    tiles_per_seq = t // tt
    tn = min(n, 2048)
    kern = functools.partial(_proj_conv_kernel, bb=bb, tt=tt, tiles_per_seq=tiles_per_seq)
    seq_block = lambda rows: pl.BlockSpec((bb, rows, tn), lambda j, i: (i // tiles_per_seq, 0, j))
    return pl.pallas_call(
        kern,
        grid=(n // tn, m // tm),
        in_specs=[
            pl.BlockSpec((tm, k), lambda j, i: (i, 0)),
            pl.BlockSpec((k, tn), lambda j, i: (0, j)),
            seq_block(CONV_W - 1),
            pl.BlockSpec((CONV_W, tn), lambda j, i: (0, j)),
            pl.BlockSpec((1, tn), lambda j, i: (0, j)),
        ],
        out_specs=[pl.BlockSpec((tm, tn), lambda j, i: (i, j)), seq_block(SUBLANES)],
        out_shape=[jax.ShapeDtypeStruct((m, n), out_dtype), jax.ShapeDtypeStruct((b, SUBLANES, n), F32)],
        scratch_shapes=[pltpu.VMEM((bb, SUBLANES + tt, tn), F32)],
        compiler_params=_cparams("parallel", "arbitrary"),
        name="proj_conv",
    )(h2d, w, conv_state, w_conv, b_conv)


def _store_time_minor(ref, x):
    bb, tt = ref.shape[0], ref.shape[-1]
    for s in range(bb):
        ref[s] = x[s * tt:(s + 1) * tt, :].T.reshape(ref.shape[1:])


def _qk_kernel(x_ref, mod_ref, gmix_ref, w_ref, gsum_ref, gexp_ref, gain_ref, cos_ref, sin_ref, q_ref, k_ref, h_ref):
    bb, tt, d = x_ref.shape
    h = _modulated_norm(x_ref[...], mod_ref[...], gmix_ref[...], 0, 1).astype(h_ref.dtype).reshape(bb * tt, d)
    h_ref[...] = h
    acc = _dot(h, w_ref[...])
    ss = _exact_dot(acc * acc, gsum_ref[...], 2)
    rs = lax.rsqrt(ss * (1.0 / HEAD_DIM) + EPS)
    rs_full = _exact_dot(rs, gexp_ref[...], 2)
    xn = acc * rs_full * gain_ref[...]
    out = _rope(xn, cos_ref[...], sin_ref[...])
    nq = N_HEADS * HEAD_DIM
    q_ref[...] = (out[:, :nq] * (HEAD_DIM ** -0.5)).astype(q_ref.dtype)
    _store_time_minor(k_ref, out[:, nq:])


def _vm_kernel(h_ref, w_ref, cos_ref, sin_ref, dtb_ref, v_ref, qi_ref, misc_ref, kit_ref):
    acc = _dot(h_ref[...], w_ref[...])
    cos, sin = cos_ref[...], sin_ref[...]
    _store_time_minor(v_ref, acc[:, :KV_W])
    qi_ref[...] = _rope(acc[:, KV_W:KV_W + QI_W], cos, sin).astype(qi_ref.dtype)
    m = acc[:, KV_W + QI_W:]
    lane = lax.broadcasted_iota(I32, m.shape, 1)
    roped = _rope(m, cos, sin)
    dt = jax.nn.softplus(m + dtb_ref[...])
    misc_ref[...] = jnp.where(lane < MISC_WI, roped,
                              jnp.where(lane < MISC_DT, m * WI_SCALE,
                                        jnp.where(lane < MISC_DT + SSM_HEADS, dt, 0.0)))
    _store_time_minor(kit_ref, roped[:, :IDX_DIM])


def _table_spec(tab_rows, tm):
    nblk = tab_rows // tm
    return pl.BlockSpec((tm, LANES), lambda i: (i % nblk, 0))


def _time_minor_spec(b, t, *mid):
    bb, tt = _row_blocking(b, t)
    per_seq = t // tt
    zeros = (0,) * len(mid)
    return pl.BlockSpec((bb, *mid, tt), lambda i: (i // per_seq, *zeros, i % per_seq))


def _proj_qk(x, mod3, mod_off, g_mix, w_qk, gsum, gexp, gain, cos_tab, sin_tab):
    b, t, k = x.shape
    m = b * t
    bb, tt = _row_blocking(b, t)
    tm = bb * tt
    per_seq = t // tt
    assert mod_off % bb == 0, (mod_off, bb)
    off = mod_off // bb
    nq = N_HEADS * HEAD_DIM
    const = lambda i: (0, 0)
    return pl.pallas_call(
        _qk_kernel,
        grid=(m // tm,),
        in_specs=[
            pl.BlockSpec((bb, tt, k), lambda i: (i // per_seq, i % per_seq, 0)),
            pl.BlockSpec((bb, 6, k), lambda i: (i // per_seq + off, 0, 0)),
            pl.BlockSpec((1, 1, k), lambda i: (0, 0, 0)),
            pl.BlockSpec((k, QK_W), const),
            pl.BlockSpec((2 * QK_W, LANES), const),
            pl.BlockSpec((2 * LANES, QK_W), const),
            pl.BlockSpec((1, QK_W), const),
            _table_spec(cos_tab.shape[0], tm),
            _table_spec(sin_tab.shape[0], tm),
        ],
        out_specs=[pl.BlockSpec((tm, nq), lambda i: (i, 0)), _time_minor_spec(b, t, N_KV_HEADS, HEAD_DIM),
                   pl.BlockSpec((tm, k), lambda i: (i, 0))],
        out_shape=[jax.ShapeDtypeStruct((m, nq), BF16), jax.ShapeDtypeStruct((b, N_KV_HEADS, HEAD_DIM, t), F32),
                   jax.ShapeDtypeStruct((m, k), BF16)],
        compiler_params=_cparams("parallel"),
        name="proj_qk",
    )(x, mod3, g_mix.reshape(1, 1, k), w_qk, gsum, gexp, gain, cos_tab, sin_tab)


def _proj_vm(h2d, w_vm, cos_tab, sin_tab, dtb, b, t):
    m, k = h2d.shape
    tm = math.prod(_row_blocking(b, t))
    wn = w_vm.shape[1]
    const = lambda i: (0, 0)
    return pl.pallas_call(
        _vm_kernel,
        grid=(m // tm,),
        in_specs=[
            pl.BlockSpec((tm, k), lambda i: (i, 0)),
            pl.BlockSpec((k, wn), const),
            _table_spec(cos_tab.shape[0], tm),
            _table_spec(sin_tab.shape[0], tm),
            pl.BlockSpec((1, LANES), const),
        ],
        out_specs=[_time_minor_spec(b, t, N_KV_HEADS, HEAD_DIM), pl.BlockSpec((tm, QI_W), lambda i: (i, 0)),
                   pl.BlockSpec((tm, LANES), lambda i: (i, 0)), _time_minor_spec(b, t, IDX_DIM)],
        out_shape=[jax.ShapeDtypeStruct((b, N_KV_HEADS, HEAD_DIM, t), F32), jax.ShapeDtypeStruct((m, QI_W), BF16),
                   jax.ShapeDtypeStruct((m, LANES), F32), jax.ShapeDtypeStruct((b, IDX_DIM, t), F32)],
        compiler_params=_cparams("parallel"),
        name="proj_vm",
    )(h2d, w_vm, cos_tab, sin_tab, dtb)


def _attn_kernel(*refs, tq, kb, gp, topk, past, t_new, has_cache):
    n_in = 9 if has_cache else 6
    q_ref, qi_ref, wi_ref, k_ref, v_ref, kit_ref = refs[:6]
    o_ref = refs[n_in]
    kbf, vbf, kibf, sc_scr, keep_scr, qis, qs, mb_scr, acc_scr, s_scr = refs[n_in + 1:n_in + 11]
    if has_cache:
        ck_ref, cv_ref, cki_ref = refs[6:9]
        ckb, cvb, ckib = refs[n_in + 11:]
    j = pl.program_id(1)
    kbn = min(kb, t_new)
    n_cache_blocks = past // kb

    def ones_row(n):
        return jnp.where(lax.broadcasted_iota(I32, (HEAD_DIM, n), 0) == 0, 1.0, 0.0).astype(BF16)

    def stage(k_src, v_src, ki_src, k_dst, v_dst, ki_dst, n):
        for g in range(N_KV_HEADS):
            k_dst[g] = k_src[0, g].astype(BF16)
            v_dst[g, 0:HEAD_DIM, :] = v_src[0, g].astype(BF16)
            v_dst[g, HEAD_DIM:2 * HEAD_DIM, :] = ones_row(n)
        ki_dst[...] = ki_src[0].T.astype(BF16)

    @pl.when(j == 0)
    def _():
        stage(k_ref, v_ref, kit_ref, kbf, vbf, kibf, t_new)
        if has_cache:
            stage(ck_ref, cv_ref, cki_ref, ckb, cvb, ckib, past)

    for h in range(IDX_HEADS):
        qis[h * tq:(h + 1) * tq, :] = qi_ref[0, :, h * IDX_DIM:(h + 1) * IDX_DIM]
    for h in range(N_HEADS):
        qs[h * tq:(h + 1) * tq, :] = q_ref[0, :, h * HEAD_DIM:(h + 1) * HEAD_DIM]
    wi_t = wi_ref[0, :, MISC_WI:MISC_WI + IDX_HEADS].T
    hpv = max(1, LANES // tq)
    wi_wide = [jnp.concatenate([wi_t[h + p:h + p + 1, :] for p in range(hpv)], axis=1) if hpv > 1 else wi_t[h:h + 1, :]
               for h in range(0, IDX_HEADS, hpv)]

    qpos = past + j * tq + lax.broadcasted_iota(I32, (1, tq), 1)
    limit = (qpos // CHUNK + 1) * CHUNK
    n_new_blocks = (j * tq + tq + kbn - 1) // kbn

    def by_pairs(fn, n_blocks, width, init):
        def pair(i, c):
            off = pl.multiple_of(i * (2 * width), 2 * width)
            return fn(pl.multiple_of(off + width, width), fn(off, c))
        c = lax.fori_loop(0, n_blocks // 2, pair, init)
        if isinstance(n_blocks, int):
            return fn((n_blocks - 1) * width, c) if n_blocks % 2 else c
        last = pl.multiple_of((n_blocks - 1) * width, width)
        return lax.cond(n_blocks % 2 == 1, lambda c: fn(last, c), lambda c: c, c)

    def over_cache(fn, init):
        if not has_cache:
            return init
        return by_pairs(fn, n_cache_blocks, kb, init)

    def over_new(fn, init):
        if t_new <= kb:
            return fn(0, init)
        return by_pairs(fn, n_new_blocks, kbn, init)

    def lanes_at(off):
        return past + off if isinstance(off, int) else pl.multiple_of(past + off, LANES)

    def score_block(ki_blk, kpos0, width):
        lg = _dot_nt(ki_blk, qis[...])
        wide = jnp.zeros((width, hpv * tq), F32)
        for i, w_row in enumerate(wi_wide):
            wide = wide + w_row * jnp.maximum(lg[:, i * hpv * tq:(i + 1) * hpv * tq], 0.0)
        sc = wide[:, :tq]
        for p in range(1, hpv):
            sc = sc + wide[:, p * tq:(p + 1) * tq]
        kpos = kpos0 + lax.broadcasted_iota(I32, (width, 1), 0)
        sc_scr[pl.ds(kpos0, width), :] = jnp.where(kpos < limit, sc, -jnp.inf)

    def p1c(off, c):
        score_block(ckib[pl.ds(off, kb), :], off, kb)
        return c

    def p1n(off, c):
        score_block(kibf[pl.ds(off, kbn), :], lanes_at(off), kbn)
        return c

    over_cache(p1c, 0)
    over_new(p1n, 0)

    def key_to_float(c):
        return pltpu.bitcast(jnp.where(c >= 0, c, c ^ 0x7FFFFFFF), F32)

    def count(cmp, cand):
        def cnt(row_off, width, acc):
            hit = jnp.where(cmp(sc_scr[pl.ds(row_off, width), :], cand), 1.0, 0.0)
            for r in range(width // COUNT_ROWS):
                acc = acc + hit[r * COUNT_ROWS:(r + 1) * COUNT_ROWS]
            return acc
        acc = jnp.zeros((COUNT_ROWS, tq), F32)
        acc = over_cache(lambda off, a: cnt(off, kb, a), acc)
        acc = over_new(lambda off, a: cnt(lanes_at(off), kbn, a), acc)
        return jnp.sum(acc, axis=0, keepdims=True)

    def bit_step(it, prefix):
        bit = jnp.left_shift(jnp.int32(1), 31 - it)
        cand = key_to_float((prefix | bit) ^ INT_MIN)
        return jnp.where(count(jnp.greater_equal, cand) >= float(topk), prefix | bit, prefix)

    prefix = lax.fori_loop(0, 32, bit_step, jnp.zeros((1, tq), I32))
    thr = key_to_float(prefix ^ INT_MIN)
    flt_max = float(jnp.finfo(F32).max)
    thr = jnp.where(thr >= -flt_max, thr, -flt_max)

    n_ge = count(jnp.greater_equal, thr)
    has_ties = jnp.max(n_ge) > float(topk)

    def keep_ranked(row_off, width, need, seen):
        sc = sc_scr[pl.ds(row_off, width), :]
        tie = sc == thr
        tie_f = jnp.where(tie, 1.0, 0.0)
        below = lax.broadcasted_iota(I32, (width, width), 1) <= lax.broadcasted_iota(I32, (width, width), 0)
        rank = seen + _dot(jnp.where(below, 1.0, 0.0).astype(BF16), tie_f.astype(BF16))
        keep = (sc > thr) | (tie & (rank <= need))
        keep_scr[pl.ds(row_off, width), :] = jnp.where(keep, 1.0, 0.0).astype(BF16)
        return seen + jnp.sum(tie_f, axis=0, keepdims=True)

    def keep_all_ties(row_off, width, c):
        sc = sc_scr[pl.ds(row_off, width), :]
        keep_scr[pl.ds(row_off, width), :] = jnp.where(sc >= thr, 1.0, 0.0).astype(BF16)
        return c

    def with_ties():
        need = float(topk) - count(jnp.greater, thr)
        seen = over_cache(lambda off, s: keep_ranked(off, kb, need, s), jnp.zeros((1, tq), F32))
        over_new(lambda off, s: keep_ranked(lanes_at(off), kbn, need, s), seen)

    def without_ties():
        over_cache(lambda off, c: keep_all_ties(off, kb, c), 0)
        over_new(lambda off, c: keep_all_ties(lanes_at(off), kbn, c), 0)

    lax.cond(has_ties, with_ties, without_ties)

    rows = Q_PER_KV * tq

    eye = jnp.where(lax.broadcasted_iota(I32, (tq, tq), 0) == lax.broadcasted_iota(I32, (tq, tq), 1), 1.0, 0.0).astype(BF16)

    def pass_a(g0, k_of, lane_off, width):
        keep_q = _dot_nt(eye, keep_scr[pl.ds(lane_off, width), :])
        b = jnp.tile(jnp.where(keep_q > 0.5, 0.0, NEG_BIG), (Q_PER_KV, 1))
        for gi in range(gp):
            g = g0 + gi
            s = _dot(qs[g * rows:(g + 1) * rows, :], k_of(g)) + b
            s_scr[gi, :, pl.ds(lane_off, width)] = s
            m = mb_scr[gi]
            if width % LANES:
                m = jnp.maximum(m, jnp.max(s, axis=1, keepdims=True))
            else:
                for c in range(width // LANES):
                    m = jnp.maximum(m, s[:, c * LANES:(c + 1) * LANES])
            mb_scr[gi] = m

    def pass_b(g0, v_of, lane_off, width):
        for gi in range(gp):
            m = mb_scr[gi]
            m = m[:, :width] if width < LANES else jnp.tile(m, (1, width // LANES))
            p = jnp.exp(s_scr[gi, :, pl.ds(lane_off, width)] - m)
            acc_scr[gi] += _dot_nt(p.astype(BF16), v_of(g0 + gi))

    def cache_k(off):
        return lambda g: ckb[g, :, pl.ds(off, kb)]

    def cache_v(off):
        return lambda g: cvb[g, :, pl.ds(off, kb)]

    def new_k(off):
        return lambda g: kbf[g, :, pl.ds(off, kbn)]

    def new_v(off):
        return lambda g: vbf[g, :, pl.ds(off, kbn)]

    def run(fn, cache_args, new_args):
        def on_cache(off, c):
            fn(*[a(off) for a in cache_args], off, kb)
            return c

        def on_new(off, c):
            fn(*[a(off) for a in new_args], lanes_at(off), kbn)
            return c

        over_cache(on_cache, 0)
        over_new(on_new, 0)

    for g0 in range(0, N_KV_HEADS, gp):
        mb_scr[...] = jnp.full(mb_scr.shape, NEG_BIG, F32)
        run(functools.partial(pass_a, g0), [cache_k], [new_k])
        for gi in range(gp):
            mb_scr[gi] = jnp.broadcast_to(jnp.max(mb_scr[gi], axis=1, keepdims=True), (rows, LANES))
        acc_scr[...] = jnp.zeros(acc_scr.shape, F32)
        run(functools.partial(pass_b, g0), [cache_v], [new_v])

        for gi in range(gp):
            a = acc_scr[gi]
            out = a[:, :HEAD_DIM] / a[:, HEAD_DIM:HEAD_DIM + 1]
            for r in range(Q_PER_KV):
                hh = (g0 + gi) * Q_PER_KV + r
                o_ref[0, :, hh * HEAD_DIM:(hh + 1) * HEAD_DIM] = out[r * tq:(r + 1) * tq].astype(o_ref.dtype)


def _attention(q, qi, misc, kt, vt, kit, cache=None):
    b, t, _ = q.shape
    has_cache = cache is not None
    past = cache[2].shape[2] if has_cache else 0
    n_keys = past + t
    topk = min(TOPK_MAX, n_keys // 4)
    tq = min(t, ATTN_TQ)
    kb = 256
    rows = Q_PER_KV * tq
    key_w = past + -(-t // LANES) * LANES
    gp = N_KV_HEADS
    while gp > 1 and gp * rows * key_w * 4 > ATTN_LOGITS_BYTES:
        gp //= 2
    qtile = lambda w: pl.BlockSpec((1, tq, w), lambda i, j: (i, j, 0))
    heads = lambda n: pl.BlockSpec((1, N_KV_HEADS, HEAD_DIM, n), lambda i, j: (i, 0, 0, 0))
    idx = lambda n: pl.BlockSpec((1, IDX_DIM, n), lambda i, j: (i, 0, 0))
    in_specs = [qtile(N_HEADS * HEAD_DIM), qtile(QI_W), qtile(LANES), heads(t), heads(t), idx(t)]
    args = [q, qi, misc, kt, vt, kit]
    if has_cache:
        in_specs += [heads(past), heads(past), idx(past)]
        args += list(cache)
    kern = functools.partial(_attn_kernel, tq=tq, kb=kb, gp=gp, topk=topk, past=past, t_new=t, has_cache=has_cache)
    return pl.pallas_call(
        kern,
        grid=(b, t // tq),
        in_specs=in_specs,
        out_specs=qtile(N_HEADS * HEAD_DIM),
        out_shape=jax.ShapeDtypeStruct((b, t, N_HEADS * HEAD_DIM), BF16),
        scratch_shapes=[
            pltpu.VMEM((N_KV_HEADS, HEAD_DIM, t), BF16),
            pltpu.VMEM((N_KV_HEADS, 2 * HEAD_DIM, t), BF16),
            pltpu.VMEM((t, IDX_DIM), BF16),
            pltpu.VMEM((n_keys, tq), F32),
            pltpu.VMEM((n_keys, tq), BF16),
            pltpu.VMEM((IDX_HEADS * tq, IDX_DIM), BF16),
            pltpu.VMEM((N_HEADS * tq, HEAD_DIM), BF16),
            pltpu.VMEM((gp, rows, LANES), F32),
            pltpu.VMEM((gp, rows, 2 * HEAD_DIM), F32),
            pltpu.VMEM((gp, rows, key_w), F32),
        ] + ([
            pltpu.VMEM((N_KV_HEADS, HEAD_DIM, past), BF16),
            pltpu.VMEM((N_KV_HEADS, 2 * HEAD_DIM, past), BF16),
            pltpu.VMEM((past, IDX_DIM), BF16),
        ] if has_cache else []),
        compiler_params=_cparams("parallel", "arbitrary"),
        name="attn_cache" if has_cache else "attn_prompt",
    )(*args)


def _ssd_kernel(xs_ref, bc_ref, misc_ref, zs_ref, h0_ref, alog_ref, dskip_ref, gnorm_ref, expand_ref, y_ref, hout_ref):
    c = pl.program_id(1)
    L = CHUNK

    @pl.when(c == 0)
    def _():
        hout_ref[...] = h0_ref[...]

    a = -jnp.exp(alog_ref[...])
    ri = lax.broadcasted_iota(I32, (L, 3 * L), 0)
    ci = lax.broadcasted_iota(I32, (L, 3 * L), 1) % L
    tri3 = jnp.where(ri >= ci, 1.0, 0.0).astype(BF16)
    expand3 = expand_ref[...]
    row_i = lax.broadcasted_iota(I32, (L, D_INNER), 0)
    lane_j = lax.broadcasted_iota(I32, (L, D_INNER), 1) % SSM_HEAD_DIM
    causal = row_i >= lane_j
    blk_r = lax.broadcasted_iota(I32, (GROUP_W, GROUP_W), 0) // SSM_HEAD_DIM
    blk_c = lax.broadcasted_iota(I32, (GROUP_W, GROUP_W), 1) // SSM_HEAD_DIM
    same_head = blk_r == blk_c

    for sub in range(xs_ref.shape[1] // L):
        _ssd_chunk(slice(sub * L, (sub + 1) * L), xs_ref, bc_ref, misc_ref, zs_ref, dskip_ref, gnorm_ref, y_ref, hout_ref,
                   a, tri3, expand3, causal, same_head)


def _heads_over_time(x):
    L = x.shape[0]
    xt = x.T
    xtt = jnp.concatenate([xt, xt], axis=1)
    first = lax.broadcasted_iota(I32, (L, 2 * L), 1) < L
    cols = []
    for h in range(0, SSM_HEADS, 2):
        r = MISC_DT + h
        cols.append(jnp.where(first, jnp.broadcast_to(xtt[r:r + 1, :], (L, 2 * L)),
                              jnp.broadcast_to(xtt[r + 1:r + 2, :], (L, 2 * L))))
    return jnp.concatenate(cols, axis=1)


def _ssd_chunk(rows, xs_ref, bc_ref, misc_ref, zs_ref, dskip_ref, gnorm_ref, y_ref, hout_ref,
               a, tri3, expand3, causal, same_head):
    L = CHUNK
    dt = misc_ref[0, rows, :]
    acs = _exact_dot_left(tri3, dt * a, 3)
    col_acs = _exact_dot(acs, expand3, 3)
    col_dt = _exact_dot(dt, expand3[:2 * LANES], 2)
    row_acs = _heads_over_time(acs)
    row_dt = _heads_over_time(dt)
    a_last = acs[L - 1:L, MISC_DT:MISC_DT + SSM_HEADS]

    groups = range(SSM_GROUPS)
    gsl = [slice(g * GROUP_W, (g + 1) * GROUP_W) for g in groups]
    bgs = [bc_ref[0, rows, g * D_STATE:(g + 1) * D_STATE] for g in groups]
    cgs = [bc_ref[0, rows, (SSM_GROUPS + g) * D_STATE:(SSM_GROUPS + g + 1) * D_STATE] for g in groups]
    xgs = [xs_ref[0, rows, gs] for gs in gsl]
    hprevs = [hout_ref[0, g * HEADS_PER_GROUP:(g + 1) * HEADS_PER_GROUP].reshape(GROUP_W, D_STATE) for g in groups]
    cbs = [_dot_nt(cgs[g], jnp.tile(bgs[g], (HEADS_PER_GROUP, 1))) for g in groups]
    offs = [_dot_nt(cgs[g], hprevs[g].astype(BF16)) for g in groups]
    sts = []
    for g in groups:
        ce = col_acs[:, gsl[g]]
        w_state = jnp.exp(ce[L - 1:L, :] - ce) * col_dt[:, gsl[g]]
        sts.append(_dot_tn((xgs[g] * w_state).astype(BF16), bgs[g]))
    y_diags = []
    for g in groups:
        gs = gsl[g]
        mm = cbs[g] * jnp.exp(jnp.where(causal[:, gs], col_acs[:, gs] - row_acs[:, gs], -jnp.inf)) * row_dt[:, gs]
        xbd = jnp.where(same_head, jnp.tile(xgs[g].astype(BF16), (HEADS_PER_GROUP, 1)), 0.0).astype(BF16)
        y_diags.append(_dot(mm.astype(BF16), xbd))
    for g in groups:
        for r in range(HEADS_PER_GROUP):
            hh = g * HEADS_PER_GROUP + r
            decay = jnp.exp(a_last[:, hh:hh + 1])
            rs = slice(r * SSM_HEAD_DIM, (r + 1) * SSM_HEAD_DIM)
            hout_ref[0, hh] = decay * hprevs[g][rs] + sts[g][rs]
    for g in groups:
        gs = gsl[g]
        y_off = jnp.exp(col_acs[:, gs]) * offs[g]
        yt = (y_diags[g] + y_off + dskip_ref[:, gs] * xgs[g]) * zs_ref[0, rows, gs].astype(F32)
        ms = jnp.mean(yt * yt, axis=-1, keepdims=True)
        y_ref[0, rows, gs] = (yt * lax.rsqrt(ms + EPS) * gnorm_ref[:, gs]).astype(y_ref.dtype)


def _ssd(xs, bc, misc, zs, h0, a_log, dskip_full, g_norm, expand):
    b, t, _ = xs.shape
    rows = min(t, SSD_CHUNKS_PER_STEP * CHUNK)
    nc = t // rows
    chunk = lambda w: pl.BlockSpec((1, rows, w), lambda i, c: (i, c, 0))
    const2 = lambda r, w: pl.BlockSpec((r, w), lambda i, c: (0, 0))
    state = pl.BlockSpec((1, SSM_HEADS, SSM_HEAD_DIM, D_STATE), lambda i, c: (i, 0, 0, 0))
    shared_h0 = h0.shape[0] == 1
    state_in = pl.BlockSpec((1, SSM_HEADS, SSM_HEAD_DIM, D_STATE), lambda i, c: (0, 0, 0, 0)) if shared_h0 else state
    return pl.pallas_call(
        _ssd_kernel,
        grid=(b, nc),
        in_specs=[
            chunk(D_INNER), chunk(2 * SSM_GROUPS * D_STATE), chunk(LANES), chunk(D_INNER), state_in,
            const2(1, LANES), const2(1, D_INNER), const2(1, D_INNER), const2(3 * LANES, D_INNER),
        ],
        out_specs=[chunk(D_INNER), state],
        out_shape=[jax.ShapeDtypeStruct((b, t, D_INNER), BF16),
                   jax.ShapeDtypeStruct((b, SSM_HEADS, SSM_HEAD_DIM, D_STATE), F32)],
        compiler_params=_cparams("parallel", "arbitrary"),
        name="ssd",
    )(xs, bc, misc, zs, h0, a_log, dskip_full, g_norm, expand)


def _merge_kernel(o_ref, y_ref, g_ref, x_ref, mod_ref, gain_ref, wa_ref, ws_ref, wo_ref, x1_ref, h2_ref):
    bb, tt, d = x_ref.shape
    rows = bb * tt
    o = o_ref[...].reshape(rows, -1)
    y = y_ref[...].reshape(rows, -1)
    gates = g_ref[...].reshape(rows, -1).astype(F32)
    mixed = gates[:, :d] * _dot(o, wa_ref[...]) + gates[:, d:] * _dot(y, ws_ref[...])
    out = _dot(mixed.astype(BF16), wo_ref[...]).reshape(bb, tt, d)
    mod = mod_ref[...]
    x1 = x_ref[...] + mod[:, 2:3, :] * out
    x1_ref[...] = x1
    h2_ref[...] = _modulated_norm(x1, mod, gain_ref[...], 3, 4).astype(h2_ref.dtype)


def _merge(o_attn, y, gates, x, mod3, mod_off, gain, wa, ws, wo):
    b, t, d = x.shape
    bb, tt = _row_blocking(b, t)
    assert mod_off % bb == 0, (mod_off, bb)
    off = mod_off // bb
    tile = lambda w: pl.BlockSpec((bb, tt, w), lambda i, j: (i, j, 0))
    const = lambda r, w: pl.BlockSpec((r, w), lambda i, j: (0, 0))
    return pl.pallas_call(
        _merge_kernel,
        grid=(b // bb, t // tt),
        in_specs=[
            tile(N_HEADS * HEAD_DIM), tile(D_INNER), tile(2 * D_MODEL), tile(d),
            pl.BlockSpec((bb, 6, d), lambda i, j: (i + off, 0, 0)),
            pl.BlockSpec((1, 1, d), lambda i, j: (0, 0, 0)),
            const(N_HEADS * HEAD_DIM, d), const(D_INNER, d), const(d, d),
        ],
        out_specs=[tile(d), tile(d)],
        out_shape=[jax.ShapeDtypeStruct((b, t, d), F32), jax.ShapeDtypeStruct((b, t, d), BF16)],
        compiler_params=_cparams("parallel", "parallel"),
        name="merge",
    )(o_attn, y, gates, x, mod3, gain.reshape(1, 1, d), wa, ws, wo)


def _ffn_kernel(h2_ref, x1_ref, mod_ref, wg_ref, wu_ref, wd_ref, o_ref, acc_ref):
    f = pl.program_id(2)
    bb, tt, d = x1_ref.shape
    h2 = h2_ref[...].reshape(bb * tt, d)
    act = _silu(_dot(h2, wg_ref[...])) * _dot(h2, wu_ref[...])
    part = _dot(act.astype(BF16), wd_ref[...])

    @pl.when(f == 0)
    def _():
        acc_ref[...] = part

    @pl.when(f > 0)
    def _():
        acc_ref[...] += part

    @pl.when(f == pl.num_programs(2) - 1)
    def _():
        o_ref[...] = x1_ref[...] + mod_ref[...][:, 5:6, :] * acc_ref[...].reshape(bb, tt, d)


def _ffn(h2, x1, mod3, mod_off, w_gu, w_down):
    b, t, d = x1.shape
    bb, tt = _row_blocking(b, t)
    assert mod_off % bb == 0, (mod_off, bb)
    off = mod_off // bb
    nf = 2
    tf = D_FF // nf
    tile = pl.BlockSpec((bb, tt, d), lambda i, j, f: (i, j, 0))
    return pl.pallas_call(
        _ffn_kernel,
        grid=(b // bb, t // tt, nf),
        in_specs=[
            tile, tile,
            pl.BlockSpec((bb, 6, d), lambda i, j, f: (i + off, 0, 0)),
            pl.BlockSpec((d, tf), lambda i, j, f: (0, f)),
            pl.BlockSpec((d, tf), lambda i, j, f: (0, nf + f)),
            pl.BlockSpec((tf, d), lambda i, j, f: (f, 0)),
        ],
        out_specs=tile,
        out_shape=jax.ShapeDtypeStruct((b, t, d), F32),
        scratch_shapes=[pltpu.VMEM((bb * tt, d), F32)],
        compiler_params=_cparams("parallel", "parallel", "arbitrary"),
        name="ffn",
    )(h2, x1, mod3, w_gu, w_gu, w_down)


def _rope_tables(t, past, tm):
    half = HEAD_DIM // 2
    inv = ROPE_THETA ** (-jnp.arange(half, dtype=F32) / half)
    ang = (past + jnp.arange(t)).astype(F32)[:, None] * inv[None, :]
    cos, sin = jnp.cos(ang), jnp.sin(ang)
    cos_t = jnp.concatenate([cos, cos, cos, cos], axis=1)
    sin_t = jnp.concatenate([-sin, sin, -sin, sin], axis=1)
    if t < tm:
        cos_t, sin_t = jnp.tile(cos_t, (tm // t, 1)), jnp.tile(sin_t, (tm // t, 1))
    return cos_t, sin_t


def _group_step(x, mod3, mod_off, cache, conv_state, ssm_state, p):
    b, t, d = x.shape
    m = b * t
    tm = min(512, m)
    past = cache[2].shape[2] if cache is not None else 0
    cos_t, sin_t = _rope_tables(t, past, tm)
    q, kt, h = _proj_qk(x, mod3, mod_off, p["g_norm_mix"], p["w_qk"], p["gsum"], p["gexp"], p["qk_gain"], cos_t, sin_t)
    vt, qi, misc, kit = _proj_vm(h, p["w_vm"], cos_t, sin_t, p["dtb"], b, t)
    zs = _proj_act(h, p["w_z"], "silu", BF16, tm, D_INNER)
    gates = _proj_act(h, p["w_g"], "sigmoid", BF16, tm, 2 * D_MODEL)
    half = CONV_CH // 2
    xs, tail_x = _proj_conv(h, p["w_xs"], conv_state[:, :, :half], p["w_conv"][:, :half],
                            p["b_conv"][:, :half], b, t, F32)
    bc, tail_bc = _proj_conv(h, p["w_bc"], conv_state[:, :, half:], p["w_conv"][:, half:],
                             p["b_conv"][:, half:], b, t, BF16)

    r3 = lambda a: a.reshape(b, t, a.shape[-1])
    o_attn = _attention(r3(q), r3(qi), r3(misc), kt, vt, kit, cache)
    y, h_last = _ssd(r3(xs), r3(bc), r3(misc), r3(zs), ssm_state, p["a_log"], p["dskip_full"], p["g_ssm_norm"],
                     p["expand"])
    x1, h2 = _merge(o_attn, y, r3(gates), x, mod3, mod_off, p["g_norm_ffn"], p["w_ba"], p["w_bs"], p["w_out"])
    out = _ffn(h2, x1, mod3, mod_off, p["w_gu"], p["w_down"])
    conv_new = jnp.concatenate([tail_x, tail_bc], axis=2)[:, -(CONV_W - 1):]
    token_major = lambda a: jnp.moveaxis(a, -1, 1)
    return out, token_major(kt), token_major(vt), token_major(kit), conv_new, h_last


IN_SIZES = (N_HEADS * HEAD_DIM, KV_W, KV_W, QI_W, IDX_DIM, IDX_HEADS, D_INNER, CONV_CH, SSM_HEADS, 2 * D_MODEL)
IN_OFFS = tuple(sum(IN_SIZES[:i]) for i in range(len(IN_SIZES) + 1))
VM_W = KV_W + QI_W + LANES


def _repack_kernel(wt_ref, qk_ref, vm_ref, z_ref, xs_ref, bc_ref, g_ref):
    o = IN_OFFS
    col = lambda a, b: wt_ref[a:b, :].T.astype(BF16)
    qk_ref[...] = col(o[0], o[2])
    whole = (o[4] - o[2]) // LANES * LANES
    vm_ref[:, 0:whole] = col(o[2], o[2] + whole)
    a_ki, a_dt = o[2] + whole, (o[8] - MISC_DT) // LANES * LANES
    assert a_ki == o[4] and o[8] - a_dt == MISC_DT, (a_ki, a_dt)
    lane = lax.broadcasted_iota(I32, (LANES, LANES), 1)
    ki_wi, dt = wt_ref[a_ki:a_ki + LANES, :].T, wt_ref[a_dt:a_dt + LANES, :].T
    vm_ref[:, whole:] = jnp.where(lane < MISC_DT, ki_wi, jnp.where(lane < MISC_DT + SSM_HEADS, dt, 0.0)).astype(BF16)
    z_ref[...] = col(o[6], o[7])
    xs_ref[...] = col(o[7], o[7] + CONV_CH // 2)
    bc_ref[...] = col(o[7] + CONV_CH // 2, o[8])
    g_ref[...] = col(o[9], o[10])


def _repack_w_in(wt):
    n, d = wt.shape
    widths = (QK_W, VM_W, D_INNER, CONV_CH // 2, CONV_CH // 2, 2 * D_MODEL)
    return pl.pallas_call(
        _repack_kernel,
        grid=(d // LANES,),
        in_specs=[pl.BlockSpec((n, LANES), lambda i: (0, i))],
        out_specs=[pl.BlockSpec((LANES, wd), lambda i: (i, 0)) for wd in widths],
        out_shape=[jax.ShapeDtypeStruct((d, wd), BF16) for wd in widths],
        compiler_params=_cparams("parallel"),
        name="repack_w_in",
    )(wt)


def _layer_params(l, w_in, g_q, g_k, g_norm_mix, g_norm_ffn, w_conv, b_conv, dt_bias, a_log, d_skip, g_ssm_norm,
                  w_branch_attn, w_branch_ssm, w_out, w_gate_up, w_down):
    w_qk, w_vm, w_z, w_xs, w_bc, w_g = _repack_w_in(w_in[l].T)
    head_of = jnp.arange(QK_W) // HEAD_DIM
    gsum = (head_of[:, None] == jnp.arange(LANES)[None, :]).astype(BF16)
    dtb = jnp.zeros((1, LANES), F32).at[0, MISC_DT:MISC_DT + SSM_HEADS].set(dt_bias[l])
    expand = ((jnp.arange(LANES) - MISC_DT)[:, None] == (jnp.arange(D_INNER) // SSM_HEAD_DIM)[None, :]).astype(BF16)
    a_log_lanes = jnp.zeros((1, LANES), F32).at[0, MISC_DT:MISC_DT + SSM_HEADS].set(a_log[l])
    return dict(
        w_qk=w_qk, w_vm=w_vm, w_z=w_z, w_xs=w_xs, w_bc=w_bc, w_g=w_g,
        gsum=jnp.tile(gsum, (2, 1)), gexp=jnp.tile(gsum.T, (2, 1)),
        qk_gain=jnp.concatenate([jnp.tile(g_q[l], N_HEADS), jnp.tile(g_k[l], N_KV_HEADS)]).reshape(1, QK_W),
        dtb=dtb, expand=jnp.tile(expand, (3, 1)),
        g_norm_mix=g_norm_mix[l], g_norm_ffn=g_norm_ffn[l],
        w_conv=w_conv[l], b_conv=b_conv[l].reshape(1, CONV_CH), a_log=a_log_lanes,
        dskip_full=jnp.repeat(d_skip[l], SSM_HEAD_DIM).reshape(1, D_INNER), g_ssm_norm=g_ssm_norm[l].reshape(1, D_INNER),
        w_ba=w_branch_attn[l].astype(BF16), w_bs=w_branch_ssm[l].astype(BF16), w_out=w_out[l].astype(BF16),
        w_gu=w_gate_up[l].astype(BF16), w_down=w_down[l].astype(BF16),
    )


def kernel(x_prompt, x_sample, cache_k, cache_v, cache_ki, state_conv, state_ssm, c_prompt, c_sample, w_ada, b_ada, g_norm_mix, g_norm_ffn, w_in, g_q, g_k, w_conv, b_conv, dt_bias, a_log, d_skip, g_ssm_norm, w_branch_attn, w_branch_ssm, w_out, w_gate_up, w_down):
    depth = w_in.shape[0]
    bp, bs = x_prompt.shape[0], x_sample.shape[0]
    past = cache_k.shape[2]
    y_p, y_s = x_prompt, x_sample
    c_all = jnp.concatenate([c_prompt, c_sample], axis=0)
    new_p = [[] for _ in range(5)]
    new_s = [[] for _ in range(5)]
    for l in range(depth):
        p = _layer_params(l, w_in, g_q, g_k, g_norm_mix, g_norm_ffn, w_conv, b_conv, dt_bias, a_log, d_skip,
                          g_ssm_norm, w_branch_attn, w_branch_ssm, w_out, w_gate_up, w_down)
        mod3 = _ada_mod(c_all, w_ada[l], b_ada[l]).reshape(bp + bs, 6, D_MODEL)
        zero_conv = jnp.zeros((bp, CONV_W - 1, CONV_CH), F32)
        zero_ssm = jnp.zeros((1, SSM_HEADS, SSM_HEAD_DIM, D_STATE), F32)
        y_p, *st_p = _group_step(y_p, mod3, 0, None, zero_conv, zero_ssm, p)
        time_minor = lambda a: jnp.moveaxis(a, 1, -1)
        cache = (time_minor(cache_k[l]), time_minor(cache_v[l]), time_minor(cache_ki[l]))
        y_s, *st_s = _group_step(y_s, mod3, bp, cache, state_conv[l], state_ssm[l], p)
        for acc, a in zip(new_p, st_p):
            acc.append(a)
        for acc, a in zip(new_s, st_s):
            acc.append(a)
    return (y_p, y_s, *[jnp.stack(a) for a in new_p], *[jnp.stack(a) for a in new_s])
```
